```python
import math
import jax, jax.numpy as jnp
from jax import lax
import numpy as np

D_MODEL = 1024
BATCH = 2
SEQ = 8192
DEPTH = 1

D_MIX = D_MODEL
HG_HEADS = 4
HG_DK = 128
HG_DV = 128
HG_WIDTH = HG_HEADS * HG_DV
HG_CHUNK = 64
AT_HEADS = 8
AT_HEAD_DIM = 64
AT_WIDTH = AT_HEADS * AT_HEAD_DIM
DILATED_PATTERNS = ((128, 1), (512, 4), (2048, 16))
ATT_BLOCK = 128
ROPE_THETA = 10000.0
NORM_EPS = 1e-6
IN_SPLITS = (HG_HEADS * HG_DK, HG_HEADS * HG_DK, HG_WIDTH, HG_WIDTH,
             AT_WIDTH, AT_WIDTH, AT_WIDTH, AT_WIDTH)
IN_COLS = sum(IN_SPLITS)

kernel_name = "hgrn2_dilated_attn_parallel_hybrid"


def rms_norm(x, w):
    xf = x.astype(jnp.float32)
    y = xf * lax.rsqrt(jnp.mean(xf * xf, axis=-1, keepdims=True) + NORM_EPS)
    return y * w.astype(jnp.float32)


def rope(x, positions):
    half = x.shape[-1] // 2
    inv_freq = 1.0 / (ROPE_THETA ** (jnp.arange(half, dtype=jnp.float32) / half))
    ang = positions.astype(jnp.float32)[:, None] * inv_freq[None, :]
    cos = jnp.cos(ang)[:, None, :]
    sin = jnp.sin(ang)[:, None, :]
    x1, x2 = x[..., :half], x[..., half:]
    return jnp.concatenate([x1 * cos - x2 * sin, x2 * cos + x1 * sin], axis=-1)


def hgrn2_mixer(q, f_logit, i, lb):
    B, S, H, DK = q.shape
    DV = i.shape[-1]
    C = HG_CHUNK
    nC = S // C
    log_f = jnp.logaddexp(jnp.log(lb), jnp.log1p(-lb) + jax.nn.log_sigmoid(f_logit))
    k = -jnp.expm1(log_f)
    q = jax.nn.silu(q)

    def chunks(t):
        return t.reshape(B, nC, C, H, t.shape[-1]).transpose(1, 0, 3, 2, 4)

    causal = (jnp.arange(C)[:, None] >= jnp.arange(C)[None, :])[:, :, None]

    def step(state, inp):
        qc, kc, vc, gc = inp
        b = jnp.cumsum(gc, axis=2)
        o_inter = jnp.einsum('bhck,bhkv->bhcv', qc * jnp.exp(b), state)
        diff = b[:, :, :, None, :] - b[:, :, None, :, :]
        decay = jnp.exp(jnp.where(causal, diff, -jnp.inf))
        scores = jnp.einsum('bhtk,bhtsk,bhsk->bhts', qc, decay, kc)
        o_intra = jnp.einsum('bhts,bhsv->bhtv', scores, vc)
        b_last = b[:, :, -1:, :]
        k_dec = kc * jnp.exp(b_last - b)
        new_state = jnp.exp(b_last[:, :, 0, :])[..., None] * state + \
            jnp.einsum('bhsk,bhsv->bhkv', k_dec, vc)
        return new_state, o_inter + o_intra

    state0 = jnp.zeros((B, H, DK, DV), jnp.float32)
    _, out = lax.scan(step, state0, (chunks(q), chunks(k), chunks(i), chunks(log_f)))
    return out.transpose(1, 0, 3, 2, 4).reshape(B, S, H, DV)


def dilated_window_attention(q, k, v, window, dilation):
    B, S, H, Dh = q.shape
    span = window // dilation
    assert span <= ATT_BLOCK
    BLK = ATT_BLOCK
    L = S // dilation
    nb = -(-L // BLK)
    Lp = nb * BLK

    def to_sub(t):
        t = t.reshape(B, L, dilation, H, Dh).transpose(0, 2, 3, 1, 4)
        return jnp.pad(t, ((0, 0), (0, 0), (0, 0), (0, Lp - L), (0, 0)))

    def band(t):
        tp = jnp.pad(t, ((0, 0), (0, 0), (0, 0), (BLK, 0), (0, 0)))
        tp = tp.reshape(B, dilation, H, nb + 1, BLK, Dh)
        return jnp.concatenate([tp[:, :, :, :-1], tp[:, :, :, 1:]], axis=-2)

    qb = to_sub(q).reshape(B, dilation, H, nb, BLK, Dh)
    kb = band(to_sub(k))
    vb = band(to_sub(v))
    s = jnp.einsum('bdhnqc,bdhnkc->bdhnqk', qb, kb) * (Dh ** -0.5)
    qi = jnp.arange(BLK)[:, None]
    kj = jnp.arange(2 * BLK)[None, :]
    dist = qi + BLK - kj
    key_idx = jnp.arange(nb)[:, None, None] * BLK + kj[None] - BLK
    valid = (dist >= 0)[None] & (dist <= span)[None] & (key_idx >= 0)
    s = jnp.where(valid, s, -jnp.inf)
    m = jnp.max(s, axis=-1, keepdims=True)
    p = jnp.exp(s - m)
    l = jnp.sum(p, axis=-1, keepdims=True)
    o = jnp.einsum('bdhnqk,bdhnkc->bdhnqc', p, vb) / l
    lse = (m + jnp.log(l))[..., 0]
    o = o.reshape(B, dilation, H, Lp, Dh)[:, :, :, :L].transpose(0, 3, 1, 2, 4).reshape(B, S, H, Dh)
    lse = lse.reshape(B, dilation, H, Lp)[..., :L].transpose(0, 3, 1, 2).reshape(B, S, H)
    return o, lse


def setup_inputs(seed: int = 0) -> dict:
    key = jax.random.key(seed)
    ks = jax.random.split(key, 8)
    x = jax.random.normal(ks[0], (BATCH, SEQ, D_MODEL), jnp.float32)
    norm_w = 1.0 + 0.02 * jax.random.normal(ks[1], (DEPTH, D_MODEL), jnp.float32)
    w_in = jax.random.normal(ks[2], (DEPTH, D_MODEL, IN_COLS), jnp.float32) * D_MODEL ** -0.5
    hgrn_lb_logits = 0.5 * jax.random.normal(ks[3], (DEPTH + 1, HG_HEADS * HG_DK), jnp.float32)
    hg_norm_w = 1.0 + 0.02 * jax.random.normal(ks[4], (DEPTH, HG_HEADS * HG_DV), jnp.float32)
    w_out = jax.random.normal(ks[5], (DEPTH, D_MIX, D_MODEL), jnp.float32) * D_MIX ** -0.5
    final_norm_w = 1.0 + 0.02 * jax.random.normal(ks[6], (D_MODEL,), jnp.float32)
    return {"x": x, "norm_w": norm_w, "w_in": w_in, "hgrn_lb_logits": hgrn_lb_logits,
            "hg_norm_w": hg_norm_w, "w_out": w_out, "final_norm_w": final_norm_w}


def reference(x, norm_w, w_in, hgrn_lb_logits, hg_norm_w, w_out, final_norm_w):
    B, S, _ = x.shape
    positions = jnp.arange(S, dtype=jnp.int32)
    lb_table = jnp.cumsum(jax.nn.softmax(hgrn_lb_logits.astype(jnp.float32), axis=0), axis=0)
    split_points = list(np.cumsum(IN_SPLITS)[:-1])
    h = x
    for layer in range(DEPTH):
        u = rms_norm(h, norm_w[layer])
        proj = jnp.einsum('bsd,dc->bsc', u, w_in[layer].astype(jnp.float32))
        hg_q, hg_f, hg_i, hg_z, at_q, at_k, at_v, at_z = jnp.split(proj, split_points, axis=-1)

        lb = jnp.clip(lb_table[layer], 1e-6, 1.0 - 1e-6).reshape(HG_HEADS, HG_DK)
        o_hg = hgrn2_mixer(hg_q.reshape(B, S, HG_HEADS, HG_DK),
                           hg_f.reshape(B, S, HG_HEADS, HG_DK),
                           hg_i.reshape(B, S, HG_HEADS, HG_DV), lb)
        g = hg_norm_w[layer].astype(jnp.float32).reshape(HG_HEADS, HG_DV)
        o_hg = o_hg * lax.rsqrt(jnp.mean(o_hg * o_hg, axis=-1, keepdims=True) + NORM_EPS) * g
        o_hg = o_hg.reshape(B, S, HG_WIDTH) * jax.nn.silu(hg_z)

        q = rope(at_q.reshape(B, S, AT_HEADS, AT_HEAD_DIM), positions)
        k = rope(at_k.reshape(B, S, AT_HEADS, AT_HEAD_DIM), positions)
        v = at_v.reshape(B, S, AT_HEADS, AT_HEAD_DIM)
        outs, lses = [], []
        for window, dilation in DILATED_PATTERNS:
            o_i, lse_i = dilated_window_attention(q, k, v, window, dilation)
            outs.append(o_i)
            lses.append(lse_i)
        mix_w = jax.nn.softmax(jnp.stack(lses, axis=0), axis=0)[..., None]
        o_at = jnp.sum(mix_w * jnp.stack(outs, axis=0), axis=0)
        o_at = o_at.reshape(B, S, AT_WIDTH) * jax.nn.silu(at_z)

        mixed = jnp.concatenate([o_hg, o_at], axis=-1)
        y = jnp.einsum('bsc,cd->bsd', mixed, w_out[layer].astype(jnp.float32))
        h = h + y.astype(h.dtype)
    return rms_norm(h, final_norm_w).astype(x.dtype)
```

```python
import functools

import numpy as np
import jax
import jax.numpy as jnp
from jax import lax
from jax.experimental import pallas as pl
from jax.experimental.pallas import tpu as pltpu

F32 = jnp.float32
BF16 = jnp.bfloat16

NORM_EPS = 1e-6
ROPE_THETA = 10000.0
LANES = 128

HG_HEADS = 4
HG_DK = 128
HG_CHUNK = 64
AT_HEAD_DIM = 64
SECTION = 512
DILATIONS = (1, 4, 16)
ATT_BLOCK = 128
ATT_TILE = ATT_BLOCK * max(DILATIONS)
NEG_BIG = -1e30

PROJ_ROWS = 512
HG_ROWS = 1024
VMEM_LIMIT = 48 * 1024 * 1024


def _proj_kernel(x_ref, nw_ref, w_ref, lbl_ref, cos_ref, sina_ref, sinb_ref,
                 qs_ref, g_ref, kk_ref, vi_ref, zs_ref,
                 aq_ref, ak_ref, av_ref, az_ref):
    x = x_ref[...]
    ms = jnp.mean(x * x, axis=-1, keepdims=True)
    u = (x * lax.rsqrt(ms + NORM_EPS) * nw_ref[...]).astype(BF16)

    def section(j):
        return jnp.dot(u, w_ref[:, j * SECTION:(j + 1) * SECTION],
                       preferred_element_type=F32)

    def silu(p):
        return p * jax.nn.sigmoid(p)

    lg = lbl_ref[...]
    e = jnp.exp(lg - jnp.max(lg, axis=0, keepdims=True))
    lb = e[0:1, :] / jnp.sum(e, axis=0, keepdims=True)
    lb = jnp.clip(lb, 1e-6, 1.0 - 1e-6)

    qs_ref[...] = silu(section(0)).astype(BF16)

    sig = jax.nn.sigmoid(section(1))
    g_ref[...] = jnp.log(lb + (1.0 - lb) * sig)
    kk_ref[...] = ((1.0 - lb) * (1.0 - sig)).astype(BF16)

    vi_ref[...] = section(2).astype(BF16)
    zs_ref[...] = silu(section(3)).astype(BF16)

    cos = cos_ref[...]
    sina = sina_ref[...]
    sinb = sinb_ref[...]

    def rope_store(p, dst_ref, scale):
        for c in range(SECTION // LANES):
            xc = p[:, c * LANES:(c + 1) * LANES]
            r = (xc * cos + pltpu.roll(xc, LANES - 32, 1) * sina
                 + pltpu.roll(xc, 32, 1) * sinb)
            dst_ref[:, c * LANES:(c + 1) * LANES] = r * scale

    rope_store(section(4), aq_ref, AT_HEAD_DIM ** -0.5)
    rope_store(section(5), ak_ref, 1.0)
    av_ref[...] = section(6)
    az_ref[...] = silu(section(7)).astype(BF16)


def _rope_tables(seq):
    half = AT_HEAD_DIM // 2
    inv_freq = 1.0 / (ROPE_THETA ** (jnp.arange(half, dtype=F32) / half))
    ang = jnp.arange(seq, dtype=F32)[:, None] * inv_freq[None, :]
    cos = jnp.cos(ang)
    sin = jnp.sin(ang)
    zero = jnp.zeros_like(sin)
    reps = LANES // AT_HEAD_DIM
    cos_t = jnp.tile(jnp.concatenate([cos, cos], axis=1), (1, reps))
    sina_t = jnp.tile(jnp.concatenate([-sin, zero], axis=1), (1, reps))
    sinb_t = jnp.tile(jnp.concatenate([zero, sin], axis=1), (1, reps))
    return cos_t, sina_t, sinb_t


def _in_projection(x2, norm_w, w_in_bf, lb_logits, seq):
    rows, d_model = x2.shape
    in_cols = w_in_bf.shape[1]
    assert in_cols == 8 * SECTION and rows % PROJ_ROWS == 0 and seq % PROJ_ROWS == 0
    cos_t, sina_t, sinb_t = _rope_tables(seq)
    seq_tiles = seq // PROJ_ROWS
    row_spec = lambda w: pl.BlockSpec((PROJ_ROWS, w), lambda i: (i, 0))
    tab_spec = pl.BlockSpec((PROJ_ROWS, LANES), lambda i: (i % seq_tiles, 0))
    full = lambda a: pl.BlockSpec(a.shape, lambda i: (0, 0))
    out_dtypes = (BF16, F32, BF16, BF16, BF16, F32, F32, F32, BF16)
    return pl.pallas_call(
        _proj_kernel,
        grid=(rows // PROJ_ROWS,),
        in_specs=[row_spec(d_model), full(norm_w), full(w_in_bf), full(lb_logits),
                  tab_spec, tab_spec, tab_spec],
        out_specs=[row_spec(SECTION)] * len(out_dtypes),
        out_shape=[jax.ShapeDtypeStruct((rows, SECTION), dt) for dt in out_dtypes],
        compiler_params=pltpu.CompilerParams(
            dimension_semantics=("parallel",), vmem_limit_bytes=VMEM_LIMIT),
        name="in_projection",
    )(x2, norm_w, w_in_bf, lb_logits, cos_t, sina_t, sinb_t)


HG_LEVELS = (8, 16, 32)


def _hgrn_constants():
    c = HG_CHUNK
    t = np.arange(c)
    tri = (t[:, None] >= t[None, :]).astype(np.float32)
    first = (t // 8) * 8
    rows = [tri, tri - 0.5 * (tri[first] + tri[first + 7])]
    for lb in HG_LEVELS:
        ref = (t // (2 * lb)) * (2 * lb) + lb - 1
        rows.append(tri - tri[ref])
    rows.append(tri[c - 1][None, :] - tri)
    coef = np.concatenate(rows, axis=0)
    ts, ss = t[:, None], t[None, :]
    level = np.full((c, c), len(HG_LEVELS) + 1, np.int32)
    level[(ts // 8 == ss // 8) & (ts >= ss)] = 0
    for i, lb in enumerate(HG_LEVELS):
        m = ((ts // (2 * lb) == ss // (2 * lb)) & ((ts // lb) % 2 == 1)
             & ((ss // lb) % 2 == 0))
        level[m] = i + 1
    return jnp.asarray(coef, BF16), jnp.asarray(level)


def _dot_nt(a, b):
    return lax.dot_general(a, b, (((1,), (1,)), ((), ())), preferred_element_type=F32)


def _hgrn_kernel(coef_ref, level_ref, qs_ref, g_ref, kk_ref, vi_ref, zs_ref, gn_ref,
                 o_ref, state_ref):
    @pl.when(pl.program_id(2) == 0)
    def _():
        state_ref[...] = jnp.zeros_like(state_ref)

    c = HG_CHUNK
    coef = coef_ref[...]
    level = level_ref[...]
    gn = gn_ref[...]

    def chunk(ci, carry):
        rows = pl.ds(pl.multiple_of(ci * c, c), c)
        q = qs_ref[0, rows, :].astype(F32)
        g = g_ref[0, rows, :]
        k = kk_ref[0, rows, :].astype(F32)
        v = vi_ref[0, rows, :]
        z = zs_ref[0, rows, :].astype(F32)

        g_hi = g.astype(BF16)
        g_lo = (g - g_hi.astype(F32)).astype(BF16)
        ex = jnp.dot(coef, jnp.concatenate([g_hi, g_lo], axis=1),
                     preferred_element_type=F32)
        ex = ex[:, :HG_DK] + ex[:, HG_DK:]
        e_b = ex[0:c]
        e_diag = ex[c:2 * c]
        e_last = ex[5 * c:6 * c]

        state = state_ref[...]
        o = _dot_nt((q * jnp.exp(e_b)).astype(BF16), state.astype(BF16))

        a = _dot_nt((q * jnp.exp(e_diag)).astype(BF16),
                    (k * jnp.exp(-e_diag)).astype(BF16))
        scores = jnp.where(level == 0, a, 0.0)
        for i in range(len(HG_LEVELS)):
            e_l = ex[(2 + i) * c:(3 + i) * c]
            a = _dot_nt((q * jnp.exp(jnp.minimum(e_l, 0.0))).astype(BF16),
                        (k * jnp.exp(jnp.minimum(-e_l, 0.0))).astype(BF16))
            scores = jnp.where(level == i + 1, a, scores)
        o = o + jnp.dot(scores.astype(BF16), v, preferred_element_type=F32)

        k_dec = (k * jnp.exp(e_last)).astype(BF16)
        v_t = v.astype(F32).T.astype(BF16)
        state_ref[...] = (state * jnp.exp(e_b[c - 1:c, :])
                          + jnp.dot(v_t, k_dec, preferred_element_type=F32))

        ms = jnp.mean(o * o, axis=-1, keepdims=True)
        o_ref[0, rows, :] = (o * lax.rsqrt(ms + NORM_EPS) * gn * z).astype(BF16)
        return carry

    lax.fori_loop(0, HG_ROWS // c, chunk, 0)


def _hgrn(qs, g, kk, vi, zs, hg_norm_w):
    b, s, w = qs.shape
    assert w == HG_HEADS * HG_DK and s % HG_ROWS == 0
    coef, level = _hgrn_constants()
    head_spec = pl.BlockSpec((1, HG_ROWS, HG_DK), lambda bi, h, t: (bi, t, h))
    const = lambda a: pl.BlockSpec(a.shape, lambda bi, h, t: (0, 0))
    return pl.pallas_call(
        _hgrn_kernel,
        grid=(b, HG_HEADS, s // HG_ROWS),
        in_specs=[const(coef), const(level)] + [head_spec] * 5
                 + [pl.BlockSpec((1, HG_DK), lambda bi, h, t: (0, h))],
        out_specs=head_spec,
        out_shape=jax.ShapeDtypeStruct((b, s, w), BF16),
        scratch_shapes=[pltpu.VMEM((HG_DK, HG_DK), F32)],
        compiler_params=pltpu.CompilerParams(
            dimension_semantics=("parallel", "parallel", "arbitrary"),
            vmem_limit_bytes=VMEM_LIMIT),
        name="hgrn2",
    )(coef, level, qs, g, kk, vi, zs, hg_norm_w)


def _attn_bias():
    qi = np.arange(ATT_BLOCK)[:, None]
    kj = np.arange(2 * ATT_BLOCK)[None, :]
    dist = ATT_BLOCK + qi - kj
    valid = (dist >= 0) & (dist <= ATT_BLOCK)
    b0 = np.where(valid, 0.0, NEG_BIG)
    b1 = np.where(valid & (kj >= ATT_BLOCK), 0.0, NEG_BIG)
    return jnp.asarray(np.stack([b0, b1]), F32)


def _attn_kernel(bias_ref, aq_ref, ak_ref, av_ref, az_ref, o_ref,
                 kbuf, vbuf, m0_ref, m1_ref, l0_ref, l1_ref, acc_ref):
    ta = ATT_TILE
    blk = ATT_BLOCK
    tile = pl.program_id(2)

    @pl.when(tile == 0)
    def _():
        kbuf[0:ta, :] = jnp.zeros((ta, LANES), F32)
        vbuf[0:ta, :] = jnp.zeros((ta, LANES), F32)

    @pl.when(tile > 0)
    def _():
        kbuf[0:ta, :] = kbuf[ta:2 * ta, :]
        vbuf[0:ta, :] = vbuf[ta:2 * ta, :]

    kbuf[ta:2 * ta, :] = ak_ref[0]
    vbuf[ta:2 * ta, :] = av_ref[0]
    m0_ref[...] = jnp.full((ta, LANES), NEG_BIG, F32)
    m1_ref[...] = jnp.full((ta, LANES), NEG_BIG, F32)
    l0_ref[...] = jnp.zeros((ta, LANES), F32)
    l1_ref[...] = jnp.zeros((ta, LANES), F32)
    acc_ref[...] = jnp.zeros((ta, LANES), F32)

    first_head = lax.broadcasted_iota(jnp.int32, (blk, LANES), 1) < AT_HEAD_DIM

    def run_pattern(d):
        blocks_per_phase = ta // (blk * d)

        def block(i, carry):
            phase = i // blocks_per_phase
            j = i % blocks_per_phase
            q_start = phase + d * blk * j
            if d == 1:
                q_start = pl.multiple_of(q_start, blk)
                q_rows = pl.ds(q_start, blk)
                k_rows = pl.ds(pl.multiple_of(ta + q_start - blk, blk), 2 * blk)
            else:
                q_rows = pl.ds(q_start, blk, stride=d)
                k_rows = pl.ds(ta + q_start - d * blk, 2 * blk, stride=d)
            hide_first = jnp.logical_and(tile == 0, j == 0).astype(jnp.int32)
            bias = bias_ref[hide_first]

            q = aq_ref[0, q_rows, :]
            k = kbuf[k_rows, :].astype(BF16)
            v = vbuf[k_rows, :].astype(BF16)
            acc_old = acc_ref[q_rows, :]

            def head(qh, m_ref, l_ref):
                s = _dot_nt(qh.astype(BF16), k) + bias
                m_old = m_ref[q_rows, :]
                m_new = jnp.maximum(m_old, jnp.max(s, axis=-1, keepdims=True))
                alpha = jnp.exp(m_old - m_new)
                p = jnp.exp(s - jnp.concatenate([m_new, m_new], axis=1))
                l_ref[q_rows, :] = (alpha * l_ref[q_rows, :]
                                    + jnp.sum(p, axis=-1, keepdims=True))
                m_ref[q_rows, :] = m_new
                pv = jnp.dot(p.astype(BF16), v, preferred_element_type=F32)
                return alpha * acc_old + pv

            a0 = head(jnp.where(first_head, q, 0.0), m0_ref, l0_ref)
            a1 = head(jnp.where(first_head, 0.0, q), m1_ref, l1_ref)
            acc_ref[q_rows, :] = jnp.where(first_head, a0, a1)
            return carry

        lax.fori_loop(0, ta // blk, block, 0)

    for d in DILATIONS:
        run_pattern(d)

    fin_rows = 256
    first_head_f = lax.broadcasted_iota(jnp.int32, (fin_rows, LANES), 1) < AT_HEAD_DIM

    def finalize(ci, carry):
        rows = pl.ds(pl.multiple_of(ci * fin_rows, fin_rows), fin_rows)
        inv_l = jnp.where(first_head_f, 1.0 / l0_ref[rows, :], 1.0 / l1_ref[rows, :])
        z = az_ref[0, rows, :].astype(F32)
        o_ref[0, rows, :] = (acc_ref[rows, :] * inv_l * z).astype(BF16)
        return carry

    lax.fori_loop(0, ta // fin_rows, finalize, 0)


def _attention(aq, ak, av, az):
    b, s, w = aq.shape
    assert s % ATT_TILE == 0 and w % LANES == 0
    bias = _attn_bias()
    pair_spec = pl.BlockSpec((1, ATT_TILE, LANES), lambda bi, h, t: (bi, t, h))
    stat = pltpu.VMEM((ATT_TILE, LANES), F32)
    hist = pltpu.VMEM((2 * ATT_TILE, LANES), F32)
    return pl.pallas_call(
        _attn_kernel,
        grid=(b, w // LANES, s // ATT_TILE),
        in_specs=[pl.BlockSpec(bias.shape, lambda bi, h, t: (0, 0, 0))] + [pair_spec] * 4,
        out_specs=pair_spec,
        out_shape=jax.ShapeDtypeStruct((b, s, w), BF16),
        scratch_shapes=[hist, hist, stat, stat, stat, stat, stat],
        compiler_params=pltpu.CompilerParams(
            dimension_semantics=("parallel", "parallel", "arbitrary"),
            vmem_limit_bytes=VMEM_LIMIT),
        name="dilated_attention",
    )(bias, aq, ak, av, az)


def _out_kernel(x_ref, ohg_ref, oat_ref, w_ref, fw_ref, o_ref):
    half = ohg_ref.shape[1]
    y = jnp.dot(ohg_ref[...], w_ref[0:half, :], preferred_element_type=F32)
    y = y + jnp.dot(oat_ref[...], w_ref[half:2 * half, :], preferred_element_type=F32)
    h = x_ref[...] + y
    ms = jnp.mean(h * h, axis=-1, keepdims=True)
    o_ref[...] = h * lax.rsqrt(ms + NORM_EPS) * fw_ref[...]


def _out_projection(x2, ohg, oat, w_out_bf, final_norm_w):
    rows, d_model = x2.shape
    half = ohg.shape[1]
    row_spec = lambda w: pl.BlockSpec((PROJ_ROWS, w), lambda i: (i, 0))
    full = lambda a: pl.BlockSpec(a.shape, lambda i: (0, 0))
    return pl.pallas_call(
        _out_kernel,
        grid=(rows // PROJ_ROWS,),
        in_specs=[row_spec(d_model), row_spec(half), row_spec(half),
                  full(w_out_bf), full(final_norm_w)],
        out_specs=row_spec(d_model),
        out_shape=jax.ShapeDtypeStruct((rows, d_model), F32),
        compiler_params=pltpu.CompilerParams(
            dimension_semantics=("parallel",), vmem_limit_bytes=VMEM_LIMIT),
        name="out_projection",
    )(x2, ohg, oat, w_out_bf, final_norm_w)


def kernel(x, norm_w, w_in, hgrn_lb_logits, hg_norm_w, w_out, final_norm_w):
    b, s, d_model = x.shape
    assert norm_w.shape[0] == 1 and w_in.shape[0] == 1 and w_out.shape[0] == 1
    x2 = x.reshape(b * s, d_model)
    qs, g, kk, vi, zs, aq, ak, av, az = _in_projection(
        x2, norm_w, w_in[0].astype(BF16), hgrn_lb_logits, s)
    to3 = lambda a: a.reshape(b, s, a.shape[-1])
    ohg = _hgrn(to3(qs), to3(g), to3(kk), to3(vi), to3(zs), hg_norm_w)
    oat = _attention(to3(aq), to3(ak), to3(av), to3(az))
    out = _out_projection(x2, ohg.reshape(b * s, -1), oat.reshape(b * s, -1),
                          w_out[0].astype(BF16), final_norm_w.reshape(1, d_model))
    return out.reshape(b, s, d_model)
```

```python
import functools

import numpy as np
import jax
import jax.numpy as jnp
from jax import lax
from jax.experimental import pallas as pl
from jax.experimental.pallas import tpu as pltpu

F32 = jnp.float32
BF16 = jnp.bfloat16

NORM_EPS = 1e-6
ROPE_THETA = 10000.0
LANES = 128

HG_HEADS = 4
HG_DK = 128
HG_CHUNK = 64
AT_HEAD_DIM = 64
SECTION = 512
DILATIONS = (1, 4, 16)
ATT_BLOCK = 128
ATT_TILE = ATT_BLOCK * max(DILATIONS)
NEG_BIG = -1e30

PROJ_ROWS = 512
HG_ROWS = 1024
HG_GROUP = 4
ATT_GROUP = 4
LOG2E = 1.4426950408889634
VMEM_LIMIT = 48 * 1024 * 1024


def _proj_kernel(x_ref, nw_ref, w_ref, lbl_ref, cos_ref, sina_ref, sinb_ref,
                 qs_ref, g_ref, kk_ref, vi_ref, zs_ref,
                 aq_ref, ak_ref, av_ref, az_ref):
    x = x_ref[...]
    ms = jnp.mean(x * x, axis=-1, keepdims=True)
    u = (x * lax.rsqrt(ms + NORM_EPS) * nw_ref[...]).astype(BF16)

    def section(j):
        return jnp.dot(u, w_ref[:, j * SECTION:(j + 1) * SECTION],
                       preferred_element_type=F32)

    def silu(p):
        return p * jax.nn.sigmoid(p)

    lg = lbl_ref[...]
    e = jnp.exp(lg - jnp.max(lg, axis=0, keepdims=True))
    lb = e[0:1, :] / jnp.sum(e, axis=0, keepdims=True)
    lb = jnp.clip(lb, 1e-6, 1.0 - 1e-6)

    qs_ref[...] = silu(section(0)).astype(BF16)

    sig = jax.nn.sigmoid(section(1))
    g_ref[...] = jnp.log(lb + (1.0 - lb) * sig)
    kk_ref[...] = ((1.0 - lb) * (1.0 - sig)).astype(BF16)

    vi_ref[...] = section(2).astype(BF16)
    zs_ref[...] = silu(section(3)).astype(BF16)

    cos = cos_ref[...]
    sina = sina_ref[...]
    sinb = sinb_ref[...]

    def rope_store(p, dst_ref, scale):
        for c in range(SECTION // LANES):
            xc = p[:, c * LANES:(c + 1) * LANES]
            r = (xc * cos + pltpu.roll(xc, LANES - 32, 1) * sina
                 + pltpu.roll(xc, 32, 1) * sinb)
            dst_ref[:, c * LANES:(c + 1) * LANES] = r * scale

    rope_store(section(4), aq_ref, AT_HEAD_DIM ** -0.5 * LOG2E)
    rope_store(section(5), ak_ref, 1.0)
    av_ref[...] = section(6)
    az_ref[...] = silu(section(7)).astype(BF16)


def _rope_tables(seq):
    half = AT_HEAD_DIM // 2
    inv_freq = 1.0 / (ROPE_THETA ** (jnp.arange(half, dtype=F32) / half))
    ang = jnp.arange(seq, dtype=F32)[:, None] * inv_freq[None, :]
    cos = jnp.cos(ang)
    sin = jnp.sin(ang)
    zero = jnp.zeros_like(sin)
    reps = LANES // AT_HEAD_DIM
    cos_t = jnp.tile(jnp.concatenate([cos, cos], axis=1), (1, reps))
    sina_t = jnp.tile(jnp.concatenate([-sin, zero], axis=1), (1, reps))
    sinb_t = jnp.tile(jnp.concatenate([zero, sin], axis=1), (1, reps))
    return cos_t, sina_t, sinb_t


def _in_projection(x2, norm_w, w_in_bf, lb_logits, seq):
    rows, d_model = x2.shape
    in_cols = w_in_bf.shape[1]
    assert in_cols == 8 * SECTION and rows % PROJ_ROWS == 0 and seq % PROJ_ROWS == 0
    cos_t, sina_t, sinb_t = _rope_tables(seq)
    seq_tiles = seq // PROJ_ROWS
    row_spec = lambda w: pl.BlockSpec((PROJ_ROWS, w), lambda i: (i, 0))
    tab_spec = pl.BlockSpec((PROJ_ROWS, LANES), lambda i: (i % seq_tiles, 0))
    full = lambda a: pl.BlockSpec(a.shape, lambda i: (0, 0))
    out_dtypes = (BF16, F32, BF16, BF16, BF16, F32, F32, F32, BF16)
    return pl.pallas_call(
        _proj_kernel,
        grid=(rows // PROJ_ROWS,),
        in_specs=[row_spec(d_model), full(norm_w), full(w_in_bf), full(lb_logits),
                  tab_spec, tab_spec, tab_spec],
        out_specs=[row_spec(SECTION)] * len(out_dtypes),
        out_shape=[jax.ShapeDtypeStruct((rows, SECTION), dt) for dt in out_dtypes],
        compiler_params=pltpu.CompilerParams(
            dimension_semantics=("parallel",), vmem_limit_bytes=VMEM_LIMIT),
        name="in_projection",
    )(x2, norm_w, w_in_bf, lb_logits, cos_t, sina_t, sinb_t)


HG_LEVELS = (8, 16, 32)


def _hgrn_constants():
    c = HG_CHUNK
    t = np.arange(c)
    tri = (t[:, None] >= t[None, :]).astype(np.float32)
    first = (t // 8) * 8
    rows = [tri, tri - 0.5 * (tri[first] + tri[first + 7])]
    for lb in HG_LEVELS:
        ref = (t // (2 * lb)) * (2 * lb) + lb - 1
        rows.append(tri - tri[ref])
    coef = np.concatenate(rows, axis=0)
    ts, ss = t[:, None], t[None, :]
    level = np.full((c, c), len(HG_LEVELS) + 1, np.int32)
    level[(ts // 8 == ss // 8) & (ts >= ss)] = 0
    for i, lb in enumerate(HG_LEVELS):
        m = ((ts // (2 * lb) == ss // (2 * lb)) & ((ts // lb) % 2 == 1)
             & ((ss // lb) % 2 == 0))
        level[m] = i + 1
    return jnp.asarray(coef, BF16), jnp.asarray(level)


def _dot_nt(a, b):
    return lax.dot_general(a, b, (((1,), (1,)), ((), ())), preferred_element_type=F32)


def _hgrn_kernel(coef_ref, level_ref, qs_ref, g_ref, kk_ref, vi_ref, zs_ref, gn_ref,
                 o_ref, state_ref):
    @pl.when(pl.program_id(2) == 0)
    def _():
        state_ref[...] = jnp.zeros_like(state_ref)

    c = HG_CHUNK
    coef = coef_ref[...]
    level = level_ref[...]
    gn = gn_ref[...]

    def chunk_local(rows):
        q = qs_ref[0, rows, :].astype(F32)
        g = g_ref[0, rows, :]
        k = kk_ref[0, rows, :].astype(F32)
        v = vi_ref[0, rows, :]

        g_hi = g.astype(BF16)
        g_lo = (g - g_hi.astype(F32)).astype(BF16)
        ex = jnp.dot(coef, jnp.concatenate([g_hi, g_lo], axis=1),
                     preferred_element_type=F32)
        ex = ex[:, :HG_DK] + ex[:, HG_DK:]
        e_b = ex[0:c]
        e_diag = ex[c:2 * c]
        b_last = e_b[c - 1:c, :]

        a = _dot_nt((q * jnp.exp(e_diag)).astype(BF16),
                    (k * jnp.exp(-e_diag)).astype(BF16))
        scores = jnp.where(level == 0, a, 0.0)
        for i in range(len(HG_LEVELS)):
            e_l = ex[(2 + i) * c:(3 + i) * c]
            a = _dot_nt((q * jnp.exp(jnp.minimum(e_l, 0.0))).astype(BF16),
                        (k * jnp.exp(jnp.minimum(-e_l, 0.0))).astype(BF16))
            scores = jnp.where(level == i + 1, a, scores)
        o_intra = jnp.dot(scores.astype(BF16), v, preferred_element_type=F32)

        k_dec = (k * jnp.exp(b_last - e_b)).astype(BF16)
        v_t = v.astype(F32).T.astype(BF16)
        kv_t = jnp.dot(v_t, k_dec, preferred_element_type=F32)
        q_dec = (q * jnp.exp(e_b)).astype(BF16)
        return q_dec, o_intra, kv_t, jnp.exp(b_last)

    def group(gi, carry):
        row_sets = [pl.ds(pl.multiple_of((gi * HG_GROUP + j) * c, c), c)
                    for j in range(HG_GROUP)]
        local = [chunk_local(rows) for rows in row_sets]
        state = state_ref[...]
        for rows, (q_dec, o_intra, kv_t, decay) in zip(row_sets, local):
            o = o_intra + _dot_nt(q_dec, state.astype(BF16))
            state = state * decay + kv_t
            z = zs_ref[0, rows, :].astype(F32)
            ms = jnp.mean(o * o, axis=-1, keepdims=True)
            o_ref[0, rows, :] = (o * lax.rsqrt(ms + NORM_EPS) * gn * z).astype(BF16)
        state_ref[...] = state
        return carry

    lax.fori_loop(0, HG_ROWS // (c * HG_GROUP), group, 0)


def _hgrn(qs, g, kk, vi, zs, hg_norm_w):
    b, s, w = qs.shape
    assert w == HG_HEADS * HG_DK and s % HG_ROWS == 0
    coef, level = _hgrn_constants()
    head_spec = pl.BlockSpec((1, HG_ROWS, HG_DK), lambda bi, h, t: (bi, t, h))
    const = lambda a: pl.BlockSpec(a.shape, lambda bi, h, t: (0, 0))
    return pl.pallas_call(
        _hgrn_kernel,
        grid=(b, HG_HEADS, s // HG_ROWS),
        in_specs=[const(coef), const(level)] + [head_spec] * 5
                 + [pl.BlockSpec((1, HG_DK), lambda bi, h, t: (0, h))],
        out_specs=head_spec,
        out_shape=jax.ShapeDtypeStruct((b, s, w), BF16),
        scratch_shapes=[pltpu.VMEM((HG_DK, HG_DK), F32)],
        compiler_params=pltpu.CompilerParams(
            dimension_semantics=("parallel", "parallel", "arbitrary"),
            vmem_limit_bytes=VMEM_LIMIT),
        name="hgrn2",
    )(coef, level, qs, g, kk, vi, zs, hg_norm_w)


def _attn_bias():
    qi = np.arange(ATT_BLOCK)[:, None]
    kj = np.arange(2 * ATT_BLOCK)[None, :]
    dist = ATT_BLOCK + qi - kj
    valid = (dist >= 0) & (dist <= ATT_BLOCK)
    b0 = np.where(valid, 0.0, NEG_BIG)
    b1 = np.where(valid & (kj >= ATT_BLOCK), 0.0, NEG_BIG)
    return jnp.asarray(np.stack([b0, b1]), F32)


def _attn_kernel(bias_ref, aq_ref, ak_ref, av_ref, az_ref, o_ref,
                 kbuf, vbuf, m_ref, l_ref, acc_ref):
    ta = ATT_TILE
    blk = ATT_BLOCK
    tile = pl.program_id(2)

    @pl.when(tile == 0)
    def _():
        kbuf[0:ta, :] = jnp.zeros((ta, LANES), F32)
        vbuf[0:ta, :] = jnp.zeros((ta, LANES), F32)

    @pl.when(tile > 0)
    def _():
        kbuf[0:ta, :] = kbuf[ta:2 * ta, :]
        vbuf[0:ta, :] = vbuf[ta:2 * ta, :]

    kbuf[ta:2 * ta, :] = ak_ref[0]
    vbuf[ta:2 * ta, :] = av_ref[0]

    first_head = lax.broadcasted_iota(jnp.int32, (blk, LANES), 1) < AT_HEAD_DIM

    def run_pattern(pi, d):
        blocks_per_phase = ta // (blk * d)

        def one_block(i):
            phase = i // blocks_per_phase
            j = i % blocks_per_phase
            q_start = phase + d * blk * j
            if d == 1:
                q_start = pl.multiple_of(q_start, blk)
                q_rows = pl.ds(q_start, blk)
                k_rows = pl.ds(pl.multiple_of(ta + q_start - blk, blk), 2 * blk)
            else:
                q_rows = pl.ds(q_start, blk, stride=d)
                k_rows = pl.ds(ta + q_start - d * blk, 2 * blk, stride=d)
            hide_first = jnp.logical_and(tile == 0, j == 0).astype(jnp.int32)
            bias = bias_ref[hide_first]

            q = aq_ref[0, q_rows, :]
            k = kbuf[k_rows, :].astype(BF16)
            v = vbuf[k_rows, :].astype(BF16)

            q2 = jnp.concatenate([jnp.where(first_head, q, 0.0),
                                  jnp.where(first_head, 0.0, q)], axis=0).astype(BF16)
            s = _dot_nt(q2, k) + jnp.concatenate([bias, bias], axis=0)
            m = jnp.max(s, axis=-1, keepdims=True)
            p = jnp.exp2(s - m)
            l = jnp.sum(p, axis=-1, keepdims=True)
            pv = jnp.dot(p.astype(BF16), v, preferred_element_type=F32)

            m_ref[pi, q_rows, :] = jnp.where(first_head, m[:blk], m[blk:])
            l_ref[pi, q_rows, :] = jnp.where(first_head, l[:blk], l[blk:])
            acc_ref[pi, q_rows, :] = jnp.where(first_head, pv[:blk], pv[blk:])

        def group(gi, carry):
            for g in range(ATT_GROUP):
                one_block(gi * ATT_GROUP + g)
            return carry

        lax.fori_loop(0, ta // (blk * ATT_GROUP), group, 0)

    for pi, d in enumerate(DILATIONS):
        run_pattern(pi, d)

    fin_rows = 256
    n_pat = len(DILATIONS)

    def finalize(ci, carry):
        rows = pl.ds(pl.multiple_of(ci * fin_rows, fin_rows), fin_rows)
        ms = [m_ref[pi, rows, :] for pi in range(n_pat)]
        m_all = functools.reduce(jnp.maximum, ms)
        ws = [jnp.exp2(m - m_all) for m in ms]
        l_all = sum(w * l_ref[pi, rows, :] for pi, w in enumerate(ws))
        acc = sum(w * acc_ref[pi, rows, :] for pi, w in enumerate(ws))
        z = az_ref[0, rows, :].astype(F32)
        o_ref[0, rows, :] = (acc / l_all * z).astype(BF16)
        return carry

    lax.fori_loop(0, ta // fin_rows, finalize, 0)


def _attention(aq, ak, av, az):
    b, s, w = aq.shape
    assert s % ATT_TILE == 0 and w % LANES == 0
    bias = _attn_bias()
    pair_spec = pl.BlockSpec((1, ATT_TILE, LANES), lambda bi, h, t: (bi, t, h))
    stat = pltpu.VMEM((len(DILATIONS), ATT_TILE, LANES), F32)
    hist = pltpu.VMEM((2 * ATT_TILE, LANES), F32)
    return pl.pallas_call(
        _attn_kernel,
        grid=(b, w // LANES, s // ATT_TILE),
        in_specs=[pl.BlockSpec(bias.shape, lambda bi, h, t: (0, 0, 0))] + [pair_spec] * 4,
        out_specs=pair_spec,
        out_shape=jax.ShapeDtypeStruct((b, s, w), BF16),
        scratch_shapes=[hist, hist, stat, stat, stat],
        compiler_params=pltpu.CompilerParams(
            dimension_semantics=("parallel", "parallel", "arbitrary"),
            vmem_limit_bytes=VMEM_LIMIT),
        name="dilated_attention",
    )(bias, aq, ak, av, az)


def _out_kernel(x_ref, ohg_ref, oat_ref, w_ref, fw_ref, o_ref):
    half = ohg_ref.shape[1]
    y = jnp.dot(ohg_ref[...], w_ref[0:half, :], preferred_element_type=F32)
    y = y + jnp.dot(oat_ref[...], w_ref[half:2 * half, :], preferred_element_type=F32)
    h = x_ref[...] + y
    ms = jnp.mean(h * h, axis=-1, keepdims=True)
    o_ref[...] = h * lax.rsqrt(ms + NORM_EPS) * fw_ref[...]


def _out_projection(x2, ohg, oat, w_out_bf, final_norm_w):
    rows, d_model = x2.shape
    half = ohg.shape[1]
    row_spec = lambda w: pl.BlockSpec((PROJ_ROWS, w), lambda i: (i, 0))
    full = lambda a: pl.BlockSpec(a.shape, lambda i: (0, 0))
    return pl.pallas_call(
        _out_kernel,
        grid=(rows // PROJ_ROWS,),
        in_specs=[row_spec(d_model), row_spec(half), row_spec(half),
                  full(w_out_bf), full(final_norm_w)],
        out_specs=row_spec(d_model),
        out_shape=jax.ShapeDtypeStruct((rows, d_model), F32),
        compiler_params=pltpu.CompilerParams(
            dimension_semantics=("parallel",), vmem_limit_bytes=VMEM_LIMIT),
        name="out_projection",
    )(x2, ohg, oat, w_out_bf, final_norm_w)


def kernel(x, norm_w, w_in, hgrn_lb_logits, hg_norm_w, w_out, final_norm_w):
    b, s, d_model = x.shape
    assert norm_w.shape[0] == 1 and w_in.shape[0] == 1 and w_out.shape[0] == 1
    x2 = x.reshape(b * s, d_model)
    qs, g, kk, vi, zs, aq, ak, av, az = _in_projection(
        x2, norm_w, w_in[0].astype(BF16), hgrn_lb_logits, s)
    to3 = lambda a: a.reshape(b, s, a.shape[-1])
    ohg = _hgrn(to3(qs), to3(g), to3(kk), to3(vi), to3(zs), hg_norm_w)
    oat = _attention(to3(aq), to3(ak), to3(av), to3(az))
    out = _out_projection(x2, ohg.reshape(b * s, -1), oat.reshape(b * s, -1),
                          w_out[0].astype(BF16), final_norm_w.reshape(1, d_model))
    return out.reshape(b, s, d_model)
```

```python
import functools

import numpy as np
import jax
import jax.numpy as jnp
from jax import lax
from jax.experimental import pallas as pl
from jax.experimental.pallas import tpu as pltpu

F32 = jnp.float32
BF16 = jnp.bfloat16

NORM_EPS = 1e-6
ROPE_THETA = 10000.0
LANES = 128

HG_HEADS = 4
HG_DK = 128
HG_CHUNK = 64
AT_HEAD_DIM = 64
SECTION = 512
DILATIONS = (1, 4, 16)
ATT_BLOCK = 128
ATT_TILE = ATT_BLOCK * max(DILATIONS)
NEG_BIG = -1e30

PROJ_ROWS = 512
HG_ROWS = 1024
HG_GROUP = 4
ATT_GROUP = 16
ATT_PERM = 4
LOG2E = 1.4426950408889634
VMEM_LIMIT = 48 * 1024 * 1024


def _proj_kernel(x_ref, nw_ref, w_ref, lbl_ref, cos_ref, sina_ref, sinb_ref,
                 qs_ref, g_ref, kk_ref, vi_ref, zs_ref,
                 aq_ref, ak_ref, av_ref, az_ref):
    x = x_ref[...]
    ms = jnp.mean(x * x, axis=-1, keepdims=True)
    u = (x * lax.rsqrt(ms + NORM_EPS) * nw_ref[...]).astype(BF16)

    def section(j):
        return jnp.dot(u, w_ref[:, j * SECTION:(j + 1) * SECTION],
                       preferred_element_type=F32)

    def silu(p):
        return p * jax.nn.sigmoid(p)

    lg = lbl_ref[...]
    e = jnp.exp(lg - jnp.max(lg, axis=0, keepdims=True))
    lb = e[0:1, :] / jnp.sum(e, axis=0, keepdims=True)
    lb = jnp.clip(lb, 1e-6, 1.0 - 1e-6)

    qs_ref[...] = silu(section(0)).astype(BF16)

    sig = jax.nn.sigmoid(section(1))
    g_ref[...] = jnp.log(lb + (1.0 - lb) * sig)
    kk_ref[...] = ((1.0 - lb) * (1.0 - sig)).astype(BF16)

    vi_ref[...] = section(2).astype(BF16)
    zs_ref[...] = silu(section(3)).astype(BF16)

    cos = cos_ref[...]
    sina = sina_ref[...]
    sinb = sinb_ref[...]

    def rope_store(p, dst_ref, scale):
        for c in range(SECTION // LANES):
            xc = p[:, c * LANES:(c + 1) * LANES]
            r = (xc * cos + pltpu.roll(xc, LANES - 32, 1) * sina
                 + pltpu.roll(xc, 32, 1) * sinb)
            dst_ref[:, c * LANES:(c + 1) * LANES] = r * scale

    rope_store(section(4), aq_ref, AT_HEAD_DIM ** -0.5 * LOG2E)
    rope_store(section(5), ak_ref, 1.0)
    av_ref[...] = section(6)
    az_ref[...] = silu(section(7)).astype(BF16)


def _rope_tables(seq):
    half = AT_HEAD_DIM // 2
    inv_freq = 1.0 / (ROPE_THETA ** (jnp.arange(half, dtype=F32) / half))
    ang = jnp.arange(seq, dtype=F32)[:, None] * inv_freq[None, :]
    cos = jnp.cos(ang)
    sin = jnp.sin(ang)
    zero = jnp.zeros_like(sin)
    reps = LANES // AT_HEAD_DIM
    cos_t = jnp.tile(jnp.concatenate([cos, cos], axis=1), (1, reps))
    sina_t = jnp.tile(jnp.concatenate([-sin, zero], axis=1), (1, reps))
    sinb_t = jnp.tile(jnp.concatenate([zero, sin], axis=1), (1, reps))
    return cos_t, sina_t, sinb_t


def _in_projection(x2, norm_w, w_in_bf, lb_logits, seq):
    rows, d_model = x2.shape
    in_cols = w_in_bf.shape[1]
    assert in_cols == 8 * SECTION and rows % PROJ_ROWS == 0 and seq % PROJ_ROWS == 0
    cos_t, sina_t, sinb_t = _rope_tables(seq)
    seq_tiles = seq // PROJ_ROWS
    row_spec = lambda w: pl.BlockSpec((PROJ_ROWS, w), lambda i: (i, 0))
    tab_spec = pl.BlockSpec((PROJ_ROWS, LANES), lambda i: (i % seq_tiles, 0))
    full = lambda a: pl.BlockSpec(a.shape, lambda i: (0, 0))
    out_dtypes = (BF16, F32, BF16, BF16, BF16, F32, F32, F32, BF16)
    return pl.pallas_call(
        _proj_kernel,
        grid=(rows // PROJ_ROWS,),
        in_specs=[row_spec(d_model), full(norm_w), full(w_in_bf), full(lb_logits),
                  tab_spec, tab_spec, tab_spec],
        out_specs=[row_spec(SECTION)] * len(out_dtypes),
        out_shape=[jax.ShapeDtypeStruct((rows, SECTION), dt) for dt in out_dtypes],
        compiler_params=pltpu.CompilerParams(
            dimension_semantics=("parallel",), vmem_limit_bytes=VMEM_LIMIT),
        name="in_projection",
    )(x2, norm_w, w_in_bf, lb_logits, cos_t, sina_t, sinb_t)


HG_LEVELS = (8, 16, 32)


def _hgrn_constants():
    c = HG_CHUNK
    t = np.arange(c)
    tri = (t[:, None] >= t[None, :]).astype(np.float32)
    first = (t // 8) * 8
    rows = [tri, tri - 0.5 * (tri[first] + tri[first + 7])]
    for lb in HG_LEVELS:
        ref = (t // (2 * lb)) * (2 * lb) + lb - 1
        rows.append(tri - tri[ref])
    coef = np.concatenate(rows, axis=0)
    ts, ss = t[:, None], t[None, :]
    level = np.full((c, c), len(HG_LEVELS) + 1, np.int32)
    level[(ts // 8 == ss // 8) & (ts >= ss)] = 0
    for i, lb in enumerate(HG_LEVELS):
        m = ((ts // (2 * lb) == ss // (2 * lb)) & ((ts // lb) % 2 == 1)
             & ((ss // lb) % 2 == 0))
        level[m] = i + 1
    return jnp.asarray(coef, BF16), jnp.asarray(level)


def _dot_nt(a, b):
    return lax.dot_general(a, b, (((1,), (1,)), ((), ())), preferred_element_type=F32)


def _hgrn_kernel(coef_ref, level_ref, qs_ref, g_ref, kk_ref, vi_ref, zs_ref, gn_ref,
                 o_ref, state_ref):
    @pl.when(pl.program_id(2) == 0)
    def _():
        state_ref[...] = jnp.zeros_like(state_ref)

    c = HG_CHUNK
    coef = coef_ref[...]
    level = level_ref[...]
    gn = gn_ref[...]

    def chunk_local(rows):
        q = qs_ref[0, rows, :].astype(F32)
        g = g_ref[0, rows, :]
        k = kk_ref[0, rows, :].astype(F32)
        v = vi_ref[0, rows, :]

        g_hi = g.astype(BF16)
        g_lo = (g - g_hi.astype(F32)).astype(BF16)
        ex = jnp.dot(coef, jnp.concatenate([g_hi, g_lo], axis=1),
                     preferred_element_type=F32)
        ex = ex[:, :HG_DK] + ex[:, HG_DK:]
        e_b = ex[0:c]
        e_diag = ex[c:2 * c]
        b_last = e_b[c - 1:c, :]

        a = _dot_nt((q * jnp.exp(e_diag)).astype(BF16),
                    (k * jnp.exp(-e_diag)).astype(BF16))
        scores = jnp.where(level == 0, a, 0.0)
        for i in range(len(HG_LEVELS)):
            e_l = ex[(2 + i) * c:(3 + i) * c]
            a = _dot_nt((q * jnp.exp(jnp.minimum(e_l, 0.0))).astype(BF16),
                        (k * jnp.exp(jnp.minimum(-e_l, 0.0))).astype(BF16))
            scores = jnp.where(level == i + 1, a, scores)
        o_intra = jnp.dot(scores.astype(BF16), v, preferred_element_type=F32)

        k_dec = (k * jnp.exp(b_last - e_b)).astype(BF16)
        v_t = v.astype(F32).T.astype(BF16)
        kv_t = jnp.dot(v_t, k_dec, preferred_element_type=F32)
        q_dec = (q * jnp.exp(e_b)).astype(BF16)
        return q_dec, o_intra, kv_t, jnp.exp(b_last)

    def group(gi, carry):
        row_sets = [pl.ds(pl.multiple_of((gi * HG_GROUP + j) * c, c), c)
                    for j in range(HG_GROUP)]
        local = [chunk_local(rows) for rows in row_sets]
        state = state_ref[...]
        for rows, (q_dec, o_intra, kv_t, decay) in zip(row_sets, local):
            o = o_intra + _dot_nt(q_dec, state.astype(BF16))
            state = state * decay + kv_t
            z = zs_ref[0, rows, :].astype(F32)
            ms = jnp.mean(o * o, axis=-1, keepdims=True)
            o_ref[0, rows, :] = (o * lax.rsqrt(ms + NORM_EPS) * gn * z).astype(BF16)
        state_ref[...] = state
        return carry

    lax.fori_loop(0, HG_ROWS // (c * HG_GROUP), group, 0)


def _hgrn(qs, g, kk, vi, zs, hg_norm_w):
    b, s, w = qs.shape
    assert w == HG_HEADS * HG_DK and s % HG_ROWS == 0
    coef, level = _hgrn_constants()
    head_spec = pl.BlockSpec((1, HG_ROWS, HG_DK), lambda bi, h, t: (bi, t, h))
    const = lambda a: pl.BlockSpec(a.shape, lambda bi, h, t: (0, 0))
    return pl.pallas_call(
        _hgrn_kernel,
        grid=(b, HG_HEADS, s // HG_ROWS),
        in_specs=[const(coef), const(level)] + [head_spec] * 5
                 + [pl.BlockSpec((1, HG_DK), lambda bi, h, t: (0, h))],
        out_specs=head_spec,
        out_shape=jax.ShapeDtypeStruct((b, s, w), BF16),
        scratch_shapes=[pltpu.VMEM((HG_DK, HG_DK), F32)],
        compiler_params=pltpu.CompilerParams(
            dimension_semantics=("parallel", "parallel", "arbitrary"),
            vmem_limit_bytes=VMEM_LIMIT),
        name="hgrn2",
    )(coef, level, qs, g, kk, vi, zs, hg_norm_w)


def _attn_bias():
    qi = np.arange(ATT_BLOCK)[:, None]
    kj = np.arange(2 * ATT_BLOCK)[None, :]
    dist = ATT_BLOCK + qi - kj
    valid = (dist >= 0) & (dist <= ATT_BLOCK)
    b0 = np.where(valid, 0.0, NEG_BIG)
    b1 = np.where(valid & (kj >= ATT_BLOCK), 0.0, NEG_BIG)
    return jnp.asarray(np.stack([b0, b1]), F32)


def _attn_kernel(bias_ref, aq_ref, ak_ref, av_ref, az_ref, o_ref,
                 knat, vnat, kprm, vprm, qprm, m_ref, l_ref, acc_ref, onat):
    ta = ATT_TILE
    blk = ATT_BLOCK
    nph = ATT_PERM
    reg = ta // nph
    tile = pl.program_id(2)

    @pl.when(tile == 0)
    def _():
        knat[0:blk, :] = jnp.zeros((blk, LANES), F32)
        vnat[0:blk, :] = jnp.zeros((blk, LANES), F32)
        kprm[:, 0:reg, :] = jnp.zeros((nph, reg, LANES), F32)
        vprm[:, 0:reg, :] = jnp.zeros((nph, reg, LANES), F32)

    @pl.when(tile > 0)
    def _():
        knat[0:blk, :] = knat[ta:ta + blk, :]
        vnat[0:blk, :] = vnat[ta:ta + blk, :]
        kprm[:, 0:reg, :] = kprm[:, reg:2 * reg, :]
        vprm[:, 0:reg, :] = vprm[:, reg:2 * reg, :]

    knat[blk:blk + ta, :] = ak_ref[0]
    vnat[blk:blk + ta, :] = av_ref[0]
    for r in range(nph):
        kprm[r, reg:2 * reg, :] = ak_ref[0, pl.ds(r, reg, stride=nph), :]
        vprm[r, reg:2 * reg, :] = av_ref[0, pl.ds(r, reg, stride=nph), :]
        qprm[r] = aq_ref[0, pl.ds(r, reg, stride=nph), :]

    first_head = lax.broadcasted_iota(jnp.int32, (blk, LANES), 1) < AT_HEAD_DIM
    ones = jnp.ones((2 * blk, LANES), BF16)
    first_tile = (tile == 0).astype(jnp.int32)

    def block_stats(q, k, v, bias):
        k = k.astype(BF16)
        v = v.astype(BF16)
        q2 = jnp.concatenate([jnp.where(first_head, q, 0.0),
                              jnp.where(first_head, 0.0, q)], axis=0).astype(BF16)
        s = _dot_nt(q2, k) + jnp.concatenate([bias, bias], axis=0)
        m = jnp.max(s, axis=-1, keepdims=True)
        p = jnp.exp2(s - m).astype(BF16)
        pv = jnp.dot(p, jnp.concatenate([v, ones], axis=1), preferred_element_type=F32)
        return (jnp.where(first_head, m[:blk], m[blk:]),
                jnp.where(first_head, pv[:blk, LANES:], pv[blk:, LANES:]),
                jnp.where(first_head, pv[:blk, :LANES], pv[blk:, :LANES]))

    def store_stats(pi, rows, stats):
        m_ref[pi, rows, :], l_ref[pi, rows, :], acc_ref[pi, rows, :] = stats

    def dilation_1(gi, carry):
        for g in range(ATT_GROUP):
            j = gi * ATT_GROUP + g
            q_rows = pl.ds(pl.multiple_of(j * blk, blk), blk)
            k_rows = pl.ds(pl.multiple_of(j * blk, blk), 2 * blk)
            bias = bias_ref[jnp.where(j == 0, first_tile, 0)]
            store_stats(0, q_rows, block_stats(
                aq_ref[0, q_rows, :], knat[k_rows, :], vnat[k_rows, :], bias))
        return carry

    def dilation_4(gi, carry):
        for g in range(ATT_GROUP):
            r = gi * (ATT_GROUP // nph) + g // nph
            j = g % nph
            k_rows = pl.ds(reg + (j - 1) * blk, 2 * blk)
            bias = bias_ref[first_tile] if j == 0 else bias_ref[0]
            stats = block_stats(qprm[r, j * blk:(j + 1) * blk, :],
                                kprm[r, k_rows, :], vprm[r, k_rows, :], bias)
            store_stats(1, pl.ds(pl.multiple_of(r * reg + j * blk, blk), blk), stats)
        return carry

    def dilation_16(gi, carry):
        bias = bias_ref[first_tile]
        for g in range(ATT_GROUP):
            r = gi * (ATT_GROUP // nph) + g // nph
            r16 = g % nph
            k_rows = pl.ds(r16, 2 * blk, stride=nph)
            stats = block_stats(qprm[r, pl.ds(r16, blk, stride=nph), :],
                                kprm[r, k_rows, :], vprm[r, k_rows, :], bias)
            store_stats(2, pl.ds(r * reg + r16, blk, stride=nph), stats)
        return carry

    trips = ta // (blk * ATT_GROUP)
    lax.fori_loop(0, trips, dilation_1, 0)
    lax.fori_loop(0, trips, dilation_4, 0)
    lax.fori_loop(0, trips, dilation_16, 0)

    fin_rows = 256

    def merge(ci, carry):
        r = ci // (reg // fin_rows)
        i0 = (ci % (reg // fin_rows)) * fin_rows
        nat_rows = pl.ds(r + nph * i0, fin_rows, stride=nph)
        prm_rows = pl.ds(pl.multiple_of(ci * fin_rows, fin_rows), fin_rows)
        rows = (nat_rows, prm_rows, prm_rows)
        ms = [m_ref[pi, rw, :] for pi, rw in enumerate(rows)]
        m_all = functools.reduce(jnp.maximum, ms)
        ws = [jnp.exp2(m - m_all) for m in ms]
        l_all = sum(w * l_ref[pi, rw, :] for (pi, rw), w in zip(enumerate(rows), ws))
        acc = sum(w * acc_ref[pi, rw, :] for (pi, rw), w in zip(enumerate(rows), ws))
        onat[nat_rows, :] = acc / l_all
        return carry

    lax.fori_loop(0, ta // fin_rows, merge, 0)

    def gate(ci, carry):
        rows = pl.ds(pl.multiple_of(ci * fin_rows, fin_rows), fin_rows)
        o_ref[0, rows, :] = (onat[rows, :] * az_ref[0, rows, :].astype(F32)).astype(BF16)
        return carry

    lax.fori_loop(0, ta // fin_rows, gate, 0)


def _attention(aq, ak, av, az):
    b, s, w = aq.shape
    assert s % ATT_TILE == 0 and w % LANES == 0
    assert DILATIONS == (1, ATT_PERM, ATT_PERM * ATT_PERM) and ATT_GROUP % ATT_PERM == 0
    bias = _attn_bias()
    reg = ATT_TILE // ATT_PERM
    pair_spec = pl.BlockSpec((1, ATT_TILE, LANES), lambda bi, h, t: (bi, t, h))
    nat = pltpu.VMEM((ATT_BLOCK + ATT_TILE, LANES), F32)
    prm = pltpu.VMEM((ATT_PERM, 2 * reg, LANES), F32)
    stat = pltpu.VMEM((len(DILATIONS), ATT_TILE, LANES), F32)
    return pl.pallas_call(
        _attn_kernel,
        grid=(b, w // LANES, s // ATT_TILE),
        in_specs=[pl.BlockSpec(bias.shape, lambda bi, h, t: (0, 0, 0))] + [pair_spec] * 4,
        out_specs=pair_spec,
        out_shape=jax.ShapeDtypeStruct((b, s, w), BF16),
        scratch_shapes=[nat, nat, prm, prm, pltpu.VMEM((ATT_PERM, reg, LANES), F32),
                        stat, stat, stat, pltpu.VMEM((ATT_TILE, LANES), F32)],
        compiler_params=pltpu.CompilerParams(
            dimension_semantics=("parallel", "parallel", "arbitrary"),
            vmem_limit_bytes=VMEM_LIMIT),
        name="dilated_attention",
    )(bias, aq, ak, av, az)


def _out_kernel(x_ref, ohg_ref, oat_ref, w_ref, fw_ref, o_ref):
    half = ohg_ref.shape[1]
    y = jnp.dot(ohg_ref[...], w_ref[0:half, :], preferred_element_type=F32)
    y = y + jnp.dot(oat_ref[...], w_ref[half:2 * half, :], preferred_element_type=F32)
    h = x_ref[...] + y
    ms = jnp.mean(h * h, axis=-1, keepdims=True)
    o_ref[...] = h * lax.rsqrt(ms + NORM_EPS) * fw_ref[...]


def _out_projection(x2, ohg, oat, w_out_bf, final_norm_w):
    rows, d_model = x2.shape
    half = ohg.shape[1]
    row_spec = lambda w: pl.BlockSpec((PROJ_ROWS, w), lambda i: (i, 0))
    full = lambda a: pl.BlockSpec(a.shape, lambda i: (0, 0))
    return pl.pallas_call(
        _out_kernel,
        grid=(rows // PROJ_ROWS,),
        in_specs=[row_spec(d_model), row_spec(half), row_spec(half),
                  full(w_out_bf), full(final_norm_w)],
        out_specs=row_spec(d_model),
        out_shape=jax.ShapeDtypeStruct((rows, d_model), F32),
        compiler_params=pltpu.CompilerParams(
            dimension_semantics=("parallel",), vmem_limit_bytes=VMEM_LIMIT),
        name="out_projection",
    )(x2, ohg, oat, w_out_bf, final_norm_w)


def kernel(x, norm_w, w_in, hgrn_lb_logits, hg_norm_w, w_out, final_norm_w):
    b, s, d_model = x.shape
    assert norm_w.shape[0] == 1 and w_in.shape[0] == 1 and w_out.shape[0] == 1
    x2 = x.reshape(b * s, d_model)
    qs, g, kk, vi, zs, aq, ak, av, az = _in_projection(
        x2, norm_w, w_in[0].astype(BF16), hgrn_lb_logits, s)
    to3 = lambda a: a.reshape(b, s, a.shape[-1])
    ohg = _hgrn(to3(qs), to3(g), to3(kk), to3(vi), to3(zs), hg_norm_w)
    oat = _attention(to3(aq), to3(ak), to3(av), to3(az))
    out = _out_projection(x2, ohg.reshape(b * s, -1), oat.reshape(b * s, -1),
                          w_out[0].astype(BF16), final_norm_w.reshape(1, d_model))
    return out.reshape(b, s, d_model)
```

```python
import functools

import numpy as np
import jax
import jax.numpy as jnp
from jax import lax
from jax.experimental import pallas as pl
from jax.experimental.pallas import tpu as pltpu

F32 = jnp.float32
BF16 = jnp.bfloat16

NORM_EPS = 1e-6
ROPE_THETA = 10000.0
LANES = 128

HG_HEADS = 4
HG_DK = 128
HG_CHUNK = 64
AT_HEAD_DIM = 64
SECTION = 512
DILATIONS = (1, 4, 16)
ATT_BLOCK = 128
ATT_TILE = ATT_BLOCK * max(DILATIONS)
NEG_BIG = -1e30

PROJ_ROWS = 512
HG_ROWS = 1024
ATT_GROUP = 16
ATT_PERM = 4
LOG2E = 1.4426950408889634
VMEM_LIMIT = 48 * 1024 * 1024


def _proj_kernel(x_ref, nw_ref, w_ref, lbl_ref, cos_ref, sina_ref, sinb_ref,
                 qs_ref, g_ref, kk_ref, vi_ref, zs_ref,
                 aq_ref, ak_ref, av_ref, az_ref):
    x = x_ref[...]
    ms = jnp.mean(x * x, axis=-1, keepdims=True)
    u = (x * lax.rsqrt(ms + NORM_EPS) * nw_ref[...]).astype(BF16)

    def section(j):
        return jnp.dot(u, w_ref[:, j * SECTION:(j + 1) * SECTION],
                       preferred_element_type=F32)

    def silu(p):
        return p * jax.nn.sigmoid(p)

    lg = lbl_ref[...]
    e = jnp.exp(lg - jnp.max(lg, axis=0, keepdims=True))
    lb = e[0:1, :] / jnp.sum(e, axis=0, keepdims=True)
    lb = jnp.clip(lb, 1e-6, 1.0 - 1e-6)

    qs_ref[...] = silu(section(0)).astype(BF16)

    sig = jax.nn.sigmoid(section(1))
    g2 = jnp.log2(lb + (1.0 - lb) * sig)
    g2_hi = g2.astype(BF16)
    g2_lo = (g2 - g2_hi.astype(F32)).astype(BF16)
    for h in range(HG_HEADS):
        g_ref[:, 2 * h * HG_DK:(2 * h + 1) * HG_DK] = g2_hi[:, h * HG_DK:(h + 1) * HG_DK]
        g_ref[:, (2 * h + 1) * HG_DK:(2 * h + 2) * HG_DK] = g2_lo[:, h * HG_DK:(h + 1) * HG_DK]
    kk_ref[...] = ((1.0 - lb) * (1.0 - sig)).astype(BF16)

    vi_ref[...] = section(2).astype(BF16)
    zs_ref[...] = silu(section(3)).astype(BF16)

    cos = cos_ref[...]
    sina = sina_ref[...]
    sinb = sinb_ref[...]

    def rope_store(p, dst_ref, scale):
        for c in range(SECTION // LANES):
            xc = p[:, c * LANES:(c + 1) * LANES]
            r = (xc * cos + pltpu.roll(xc, LANES - 32, 1) * sina
                 + pltpu.roll(xc, 32, 1) * sinb)
            dst_ref[:, c * LANES:(c + 1) * LANES] = r * scale

    rope_store(section(4), aq_ref, AT_HEAD_DIM ** -0.5 * LOG2E)
    rope_store(section(5), ak_ref, 1.0)
    av_ref[...] = section(6)
    az_ref[...] = silu(section(7)).astype(BF16)


def _rope_tables(seq):
    half = AT_HEAD_DIM // 2
    inv_freq = 1.0 / (ROPE_THETA ** (jnp.arange(half, dtype=F32) / half))
    ang = jnp.arange(seq, dtype=F32)[:, None] * inv_freq[None, :]
    cos = jnp.cos(ang)
    sin = jnp.sin(ang)
    zero = jnp.zeros_like(sin)
    reps = LANES // AT_HEAD_DIM
    cos_t = jnp.tile(jnp.concatenate([cos, cos], axis=1), (1, reps))
    sina_t = jnp.tile(jnp.concatenate([-sin, zero], axis=1), (1, reps))
    sinb_t = jnp.tile(jnp.concatenate([zero, sin], axis=1), (1, reps))
    return cos_t, sina_t, sinb_t


def _in_projection(x2, norm_w, w_in_bf, lb_logits, seq):
    rows, d_model = x2.shape
    in_cols = w_in_bf.shape[1]
    assert in_cols == 8 * SECTION and rows % PROJ_ROWS == 0 and seq % PROJ_ROWS == 0
    cos_t, sina_t, sinb_t = _rope_tables(seq)
    seq_tiles = seq // PROJ_ROWS
    row_spec = lambda w: pl.BlockSpec((PROJ_ROWS, w), lambda i: (i, 0))
    tab_spec = pl.BlockSpec((PROJ_ROWS, LANES), lambda i: (i % seq_tiles, 0))
    full = lambda a: pl.BlockSpec(a.shape, lambda i: (0, 0))
    out_dtypes = (BF16, BF16, BF16, BF16, BF16, F32, F32, F32, BF16)
    out_widths = (SECTION, 2 * SECTION) + (SECTION,) * 7
    return pl.pallas_call(
        _proj_kernel,
        grid=(rows // PROJ_ROWS,),
        in_specs=[row_spec(d_model), full(norm_w), full(w_in_bf), full(lb_logits),
                  tab_spec, tab_spec, tab_spec],
        out_specs=[row_spec(w) for w in out_widths],
        out_shape=[jax.ShapeDtypeStruct((rows, w), dt)
                   for w, dt in zip(out_widths, out_dtypes)],
        compiler_params=pltpu.CompilerParams(
            dimension_semantics=("parallel",), vmem_limit_bytes=VMEM_LIMIT),
        name="in_projection",
    )(x2, norm_w, w_in_bf, lb_logits, cos_t, sina_t, sinb_t)


HG_LEVELS = (8, 16, 32)


def _hgrn_constants():
    c = HG_CHUNK
    t = np.arange(c)
    tri = (t[:, None] >= t[None, :]).astype(np.float32)
    first = (t // 8) * 8
    half = np.concatenate([tri, tri - 0.5 * (tri[first] + tri[first + 7])], axis=0)
    coef = np.concatenate([half, half], axis=1)
    ts, ss = t[:, None], t[None, :]
    level = np.full((c, c), len(HG_LEVELS) + 1, np.int32)
    level[(ts // 8 == ss // 8) & (ts >= ss)] = 0
    for i, lb in enumerate(HG_LEVELS):
        m = ((ts // (2 * lb) == ss // (2 * lb)) & ((ts // lb) % 2 == 1)
             & ((ss // lb) % 2 == 0))
        level[m] = i + 1
    return jnp.asarray(coef, BF16), jnp.asarray(level)


def _dot_nt(a, b):
    return lax.dot_general(a, b, (((1,), (1,)), ((), ())), preferred_element_type=F32)


def _dot_tn(a, b):
    return lax.dot_general(a, b, (((0,), (0,)), ((), ())), preferred_element_type=F32)


def _hgrn_kernel(coef_ref, level_ref, qs_ref, g_ref, kk_ref, vi_ref, zs_ref, gn_ref,
                 o_ref, state_ref, qd_ref, oi_ref, kv_ref, dec_ref):
    @pl.when(pl.program_id(2) == 0)
    def _():
        state_ref[...] = jnp.zeros_like(state_ref)

    c = HG_CHUNK
    n_chunks = HG_ROWS // c
    coef = coef_ref[...]
    level = level_ref[...]
    gn = gn_ref[...]

    chunk_rows = [pl.ds(ci * c, c) for ci in range(n_chunks)]

    def cum_decay(rows):
        g2 = g_ref[0, rows, :]
        return jnp.dot(coef, jnp.concatenate([g2[:, :HG_DK], g2[:, HG_DK:]], axis=0),
                       preferred_element_type=F32)

    def level_operands(rows, be):
        q = qs_ref[0, rows, :]
        k = kk_ref[0, rows, :]
        b = be[0:c]
        e_diag = be[c:2 * c]
        ops = [(q * jnp.exp2(e_diag).astype(BF16), k * jnp.exp2(-e_diag).astype(BF16))]
        for lb in HG_LEVELS:
            ref = jnp.concatenate(
                [jnp.broadcast_to(b[p + lb - 1:p + lb, :], (2 * lb, HG_DK))
                 for p in range(0, c, 2 * lb)], axis=0)
            w = jnp.exp2(-jnp.abs(b - ref)).astype(BF16)
            ops.append((q * w, k * w))
        b_last = b[c - 1:c, :]
        return ops, k * jnp.exp2(b_last - b).astype(BF16), q * jnp.exp2(b).astype(BF16), \
            jnp.exp2(b_last)

    def intra_scores(ops):
        scores = jnp.where(level == 0, _dot_nt(*ops[0]), 0.0)
        for i in range(len(HG_LEVELS)):
            scores = jnp.where(level == i + 1, _dot_nt(*ops[i + 1]), scores)
        return scores.astype(BF16)

    stage1 = [cum_decay(rows) for rows in chunk_rows]
    stage2 = [level_operands(rows, be) for rows, be in zip(chunk_rows, stage1)]
    stage3 = [intra_scores(ops) for ops, _, _, _ in stage2]
    for ci, rows in enumerate(chunk_rows):
        _, k_dec, q_dec, decay = stage2[ci]
        v = vi_ref[0, rows, :]
        oi_ref[rows, :] = jnp.dot(stage3[ci], v, preferred_element_type=F32)
        kv_ref[ci] = _dot_tn(v, k_dec)
        qd_ref[rows, :] = q_dec
        dec_ref[ci] = jnp.broadcast_to(decay, dec_ref.shape[1:])

    state = state_ref[...]
    for ci in range(n_chunks):
        rows = pl.ds(ci * c, c)
        o = oi_ref[rows, :] + _dot_nt(qd_ref[rows, :], state.astype(BF16))
        state = state * dec_ref[ci, 0:1, :] + kv_ref[ci]
        z = zs_ref[0, rows, :].astype(F32)
        ms = jnp.mean(o * o, axis=-1, keepdims=True)
        o_ref[0, rows, :] = (o * lax.rsqrt(ms + NORM_EPS) * gn * z).astype(BF16)
    state_ref[...] = state


def _hgrn(qs, g2, kk, vi, zs, hg_norm_w):
    b, s, w = qs.shape
    assert w == HG_HEADS * HG_DK and s % HG_ROWS == 0 and g2.shape[-1] == 2 * w
    coef, level = _hgrn_constants()
    head_spec = pl.BlockSpec((1, HG_ROWS, HG_DK), lambda bi, h, t: (bi, t, h))
    pair_spec = pl.BlockSpec((1, HG_ROWS, 2 * HG_DK), lambda bi, h, t: (bi, t, h))
    const = lambda a: pl.BlockSpec(a.shape, lambda bi, h, t: (0, 0))
    return pl.pallas_call(
        _hgrn_kernel,
        grid=(b, HG_HEADS, s // HG_ROWS),
        in_specs=[const(coef), const(level), head_spec, pair_spec, head_spec, head_spec,
                  head_spec, pl.BlockSpec((1, HG_DK), lambda bi, h, t: (0, h))],
        out_specs=head_spec,
        out_shape=jax.ShapeDtypeStruct((b, s, w), BF16),
        scratch_shapes=[pltpu.VMEM((HG_DK, HG_DK), F32),
                        pltpu.VMEM((HG_ROWS, HG_DK), BF16),
                        pltpu.VMEM((HG_ROWS, HG_DK), F32),
                        pltpu.VMEM((HG_ROWS // HG_CHUNK, HG_DK, HG_DK), F32),
                        pltpu.VMEM((HG_ROWS // HG_CHUNK, 8, HG_DK), F32)],
        compiler_params=pltpu.CompilerParams(
            dimension_semantics=("parallel", "parallel", "arbitrary"),
            vmem_limit_bytes=VMEM_LIMIT),
        name="hgrn2",
    )(coef, level, qs, g2, kk, vi, zs, hg_norm_w)


def _attn_bias():
    qi = np.arange(ATT_BLOCK)[:, None]
    kj = np.arange(2 * ATT_BLOCK)[None, :]
    dist = ATT_BLOCK + qi - kj
    valid = (dist >= 0) & (dist <= ATT_BLOCK)
    b0 = np.where(valid, 0.0, NEG_BIG)
    b1 = np.where(valid & (kj >= ATT_BLOCK), 0.0, NEG_BIG)
    return jnp.asarray(np.stack([b0, b1]), F32)


def _attn_kernel(bias_ref, aq_ref, ak_ref, av_ref, az_ref, o_ref,
                 knat, vnat, kprm, vprm, qprm, m_ref, l_ref, acc_ref, onat):
    ta = ATT_TILE
    blk = ATT_BLOCK
    nph = ATT_PERM
    reg = ta // nph
    tile = pl.program_id(2)

    @pl.when(tile == 0)
    def _():
        knat[0:blk, :] = jnp.zeros((blk, LANES), F32)
        vnat[0:blk, :] = jnp.zeros((blk, LANES), F32)
        kprm[:, 0:reg, :] = jnp.zeros((nph, reg, LANES), F32)
        vprm[:, 0:reg, :] = jnp.zeros((nph, reg, LANES), F32)

    @pl.when(tile > 0)
    def _():
        knat[0:blk, :] = knat[ta:ta + blk, :]
        vnat[0:blk, :] = vnat[ta:ta + blk, :]
        kprm[:, 0:reg, :] = kprm[:, reg:2 * reg, :]
        vprm[:, 0:reg, :] = vprm[:, reg:2 * reg, :]

    knat[blk:blk + ta, :] = ak_ref[0]
    vnat[blk:blk + ta, :] = av_ref[0]
    for r in range(nph):
        kprm[r, reg:2 * reg, :] = ak_ref[0, pl.ds(r, reg, stride=nph), :]
        vprm[r, reg:2 * reg, :] = av_ref[0, pl.ds(r, reg, stride=nph), :]
        qprm[r] = aq_ref[0, pl.ds(r, reg, stride=nph), :]

    first_head = lax.broadcasted_iota(jnp.int32, (blk, LANES), 1) < AT_HEAD_DIM
    ones = jnp.ones((2 * blk, LANES), BF16)
    first_tile = (tile == 0).astype(jnp.int32)

    def block_stats(q, k, v, bias):
        k = k.astype(BF16)
        v = v.astype(BF16)
        q2 = jnp.concatenate([jnp.where(first_head, q, 0.0),
                              jnp.where(first_head, 0.0, q)], axis=0).astype(BF16)
        s = _dot_nt(q2, k) + jnp.concatenate([bias, bias], axis=0)
        m = jnp.max(s, axis=-1, keepdims=True)
        p = jnp.exp2(s - m).astype(BF16)
        pv = jnp.dot(p, jnp.concatenate([v, ones], axis=1), preferred_element_type=F32)
        return (jnp.where(first_head, m[:blk], m[blk:]),
                jnp.where(first_head, pv[:blk, LANES:], pv[blk:, LANES:]),
                jnp.where(first_head, pv[:blk, :LANES], pv[blk:, :LANES]))

    def store_stats(pi, rows, stats):
        m_ref[pi, rows, :], l_ref[pi, rows, :], acc_ref[pi, rows, :] = stats

    def dilation_1(gi, carry):
        for g in range(ATT_GROUP):
            j = gi * ATT_GROUP + g
            q_rows = pl.ds(pl.multiple_of(j * blk, blk), blk)
            k_rows = pl.ds(pl.multiple_of(j * blk, blk), 2 * blk)
            bias = bias_ref[jnp.where(j == 0, first_tile, 0)]
            store_stats(0, q_rows, block_stats(
                aq_ref[0, q_rows, :], knat[k_rows, :], vnat[k_rows, :], bias))
        return carry

    def dilation_4(gi, carry):
        for g in range(ATT_GROUP):
            r = gi * (ATT_GROUP // nph) + g // nph
            j = g % nph
            k_rows = pl.ds(reg + (j - 1) * blk, 2 * blk)
            bias = bias_ref[first_tile] if j == 0 else bias_ref[0]
            stats = block_stats(qprm[r, j * blk:(j + 1) * blk, :],
                                kprm[r, k_rows, :], vprm[r, k_rows, :], bias)
            store_stats(1, pl.ds(pl.multiple_of(r * reg + j * blk, blk), blk), stats)
        return carry

    def dilation_16(gi, carry):
        bias = bias_ref[first_tile]
        for g in range(ATT_GROUP):
            r = gi * (ATT_GROUP // nph) + g // nph
            r16 = g % nph
            k_rows = pl.ds(r16, 2 * blk, stride=nph)
            stats = block_stats(qprm[r, pl.ds(r16, blk, stride=nph), :],
                                kprm[r, k_rows, :], vprm[r, k_rows, :], bias)
            store_stats(2, pl.ds(r * reg + r16, blk, stride=nph), stats)
        return carry

    trips = ta // (blk * ATT_GROUP)
    lax.fori_loop(0, trips, dilation_1, 0)
    lax.fori_loop(0, trips, dilation_4, 0)
    lax.fori_loop(0, trips, dilation_16, 0)

    fin_rows = 256

    def merge(ci, carry):
        r = ci // (reg // fin_rows)
        i0 = (ci % (reg // fin_rows)) * fin_rows
        nat_rows = pl.ds(r + nph * i0, fin_rows, stride=nph)
        prm_rows = pl.ds(pl.multiple_of(ci * fin_rows, fin_rows), fin_rows)
        rows = (nat_rows, prm_rows, prm_rows)
        ms = [m_ref[pi, rw, :] for pi, rw in enumerate(rows)]
        m_all = functools.reduce(jnp.maximum, ms)
        ws = [jnp.exp2(m - m_all) for m in ms]
        l_all = sum(w * l_ref[pi, rw, :] for (pi, rw), w in zip(enumerate(rows), ws))
        acc = sum(w * acc_ref[pi, rw, :] for (pi, rw), w in zip(enumerate(rows), ws))
        onat[nat_rows, :] = acc / l_all
        return carry

    lax.fori_loop(0, ta // fin_rows, merge, 0)

    def gate(ci, carry):
        rows = pl.ds(pl.multiple_of(ci * fin_rows, fin_rows), fin_rows)
        o_ref[0, rows, :] = (onat[rows, :] * az_ref[0, rows, :].astype(F32)).astype(BF16)
        return carry

    lax.fori_loop(0, ta // fin_rows, gate, 0)


def _attention(aq, ak, av, az):
    b, s, w = aq.shape
    assert s % ATT_TILE == 0 and w % LANES == 0
    assert DILATIONS == (1, ATT_PERM, ATT_PERM * ATT_PERM) and ATT_GROUP % ATT_PERM == 0
    bias = _attn_bias()
    reg = ATT_TILE // ATT_PERM
    pair_spec = pl.BlockSpec((1, ATT_TILE, LANES), lambda bi, h, t: (bi, t, h))
    nat = pltpu.VMEM((ATT_BLOCK + ATT_TILE, LANES), F32)
    prm = pltpu.VMEM((ATT_PERM, 2 * reg, LANES), F32)
    stat = pltpu.VMEM((len(DILATIONS), ATT_TILE, LANES), F32)
    return pl.pallas_call(
        _attn_kernel,
        grid=(b, w // LANES, s // ATT_TILE),
        in_specs=[pl.BlockSpec(bias.shape, lambda bi, h, t: (0, 0, 0))] + [pair_spec] * 4,
        out_specs=pair_spec,
        out_shape=jax.ShapeDtypeStruct((b, s, w), BF16),
        scratch_shapes=[nat, nat, prm, prm, pltpu.VMEM((ATT_PERM, reg, LANES), F32),
                        stat, stat, stat, pltpu.VMEM((ATT_TILE, LANES), F32)],
        compiler_params=pltpu.CompilerParams(
            dimension_semantics=("parallel", "parallel", "arbitrary"),
            vmem_limit_bytes=VMEM_LIMIT),
        name="dilated_attention",
    )(bias, aq, ak, av, az)


def _out_kernel(x_ref, ohg_ref, oat_ref, w_ref, fw_ref, o_ref):
    half = ohg_ref.shape[1]
    y = jnp.dot(ohg_ref[...], w_ref[0:half, :], preferred_element_type=F32)
    y = y + jnp.dot(oat_ref[...], w_ref[half:2 * half, :], preferred_element_type=F32)
    h = x_ref[...] + y
    ms = jnp.mean(h * h, axis=-1, keepdims=True)
    o_ref[...] = h * lax.rsqrt(ms + NORM_EPS) * fw_ref[...]


def _out_projection(x2, ohg, oat, w_out_bf, final_norm_w):
    rows, d_model = x2.shape
    half = ohg.shape[1]
    row_spec = lambda w: pl.BlockSpec((PROJ_ROWS, w), lambda i: (i, 0))
    full = lambda a: pl.BlockSpec(a.shape, lambda i: (0, 0))
    return pl.pallas_call(
        _out_kernel,
        grid=(rows // PROJ_ROWS,),
        in_specs=[row_spec(d_model), row_spec(half), row_spec(half),
                  full(w_out_bf), full(final_norm_w)],
        out_specs=row_spec(d_model),
        out_shape=jax.ShapeDtypeStruct((rows, d_model), F32),
        compiler_params=pltpu.CompilerParams(
            dimension_semantics=("parallel",), vmem_limit_bytes=VMEM_LIMIT),
        name="out_projection",
    )(x2, ohg, oat, w_out_bf, final_norm_w)


def kernel(x, norm_w, w_in, hgrn_lb_logits, hg_norm_w, w_out, final_norm_w):
    b, s, d_model = x.shape
    assert norm_w.shape[0] == 1 and w_in.shape[0] == 1 and w_out.shape[0] == 1
    x2 = x.reshape(b * s, d_model)
    qs, g, kk, vi, zs, aq, ak, av, az = _in_projection(
        x2, norm_w, w_in[0].astype(BF16), hgrn_lb_logits, s)
    to3 = lambda a: a.reshape(b, s, a.shape[-1])
    ohg = _hgrn(to3(qs), to3(g), to3(kk), to3(vi), to3(zs), hg_norm_w)
    oat = _attention(to3(aq), to3(ak), to3(av), to3(az))
    out = _out_projection(x2, ohg.reshape(b * s, -1), oat.reshape(b * s, -1),
                          w_out[0].astype(BF16), final_norm_w.reshape(1, d_model))
    return out.reshape(b, s, d_model)
```

```python
import functools

import numpy as np
import jax
import jax.numpy as jnp
from jax import lax
from jax.experimental import pallas as pl
from jax.experimental.pallas import tpu as pltpu

F32 = jnp.float32
BF16 = jnp.bfloat16

NORM_EPS = 1e-6
ROPE_THETA = 10000.0
LANES = 128

HG_HEADS = 4
HG_DK = 128
HG_CHUNK = 64
AT_HEAD_DIM = 64
SECTION = 512
DILATIONS = (1, 4, 16)
ATT_BLOCK = 128
ATT_TILE = ATT_BLOCK * max(DILATIONS)
NEG_BIG = -1e30

PROJ_ROWS = 512
HG_ROWS = 1024
ATT_PERM = 4
ATT_MERGE_ROWS = 256
LOG2E = 1.4426950408889634
VMEM_LIMIT = 48 * 1024 * 1024


def _proj_kernel(x_ref, nw_ref, w_ref, lbl_ref, cos_ref, sina_ref, sinb_ref,
                 qs_ref, g_ref, kk_ref, vi_ref, zs_ref,
                 aq_ref, ak_ref, av_ref, az_ref):
    x = x_ref[...]
    ms = jnp.mean(x * x, axis=-1, keepdims=True)
    u = (x * lax.rsqrt(ms + NORM_EPS) * nw_ref[...]).astype(BF16)

    def section(j):
        return jnp.dot(u, w_ref[:, j * SECTION:(j + 1) * SECTION],
                       preferred_element_type=F32)

    def silu(p):
        return p * jax.nn.sigmoid(p)

    lg = lbl_ref[...]
    e = jnp.exp(lg - jnp.max(lg, axis=0, keepdims=True))
    lb = e[0:1, :] / jnp.sum(e, axis=0, keepdims=True)
    lb = jnp.clip(lb, 1e-6, 1.0 - 1e-6)

    qs_ref[...] = silu(section(0)).astype(BF16)

    sig = jax.nn.sigmoid(section(1))
    g2 = jnp.log2(lb + (1.0 - lb) * sig)
    g2_hi = g2.astype(BF16)
    g2_lo = (g2 - g2_hi.astype(F32)).astype(BF16)
    for h in range(HG_HEADS):
        g_ref[:, 2 * h * HG_DK:(2 * h + 1) * HG_DK] = g2_hi[:, h * HG_DK:(h + 1) * HG_DK]
        g_ref[:, (2 * h + 1) * HG_DK:(2 * h + 2) * HG_DK] = g2_lo[:, h * HG_DK:(h + 1) * HG_DK]
    kk_ref[...] = ((1.0 - lb) * (1.0 - sig)).astype(BF16)

    vi_ref[...] = section(2).astype(BF16)
    zs_ref[...] = silu(section(3)).astype(BF16)

    cos = cos_ref[...]
    sina = sina_ref[...]
    sinb = sinb_ref[...]

    def rope_store(p, dst_ref, scale):
        for c in range(SECTION // LANES):
            xc = p[:, c * LANES:(c + 1) * LANES]
            r = (xc * cos + pltpu.roll(xc, LANES - 32, 1) * sina
                 + pltpu.roll(xc, 32, 1) * sinb)
            dst_ref[:, c * LANES:(c + 1) * LANES] = r * scale

    rope_store(section(4), aq_ref, AT_HEAD_DIM ** -0.5 * LOG2E)
    rope_store(section(5), ak_ref, 1.0)
    av_ref[...] = section(6)
    az_ref[...] = silu(section(7)).astype(BF16)


@functools.lru_cache(maxsize=None)
def _rope_tables(seq):
    half = AT_HEAD_DIM // 2
    inv_freq = 1.0 / (ROPE_THETA ** (np.arange(half, dtype=np.float64) / half))
    ang = np.arange(seq, dtype=np.float64)[:, None] * inv_freq[None, :]
    cos = np.cos(ang)
    sin = np.sin(ang)
    zero = np.zeros_like(sin)
    reps = LANES // AT_HEAD_DIM
    cos_t = np.tile(np.concatenate([cos, cos], axis=1), (1, reps))
    sina_t = np.tile(np.concatenate([-sin, zero], axis=1), (1, reps))
    sinb_t = np.tile(np.concatenate([zero, sin], axis=1), (1, reps))
    return tuple(np.asarray(t, np.float32) for t in (cos_t, sina_t, sinb_t))


def _in_projection(x2, norm_w, w_in_bf, lb_logits, seq):
    rows, d_model = x2.shape
    in_cols = w_in_bf.shape[1]
    assert in_cols == 8 * SECTION and rows % PROJ_ROWS == 0 and seq % PROJ_ROWS == 0
    cos_t, sina_t, sinb_t = _rope_tables(seq)
    seq_tiles = seq // PROJ_ROWS
    row_spec = lambda w: pl.BlockSpec((PROJ_ROWS, w), lambda i: (i, 0))
    tab_spec = pl.BlockSpec((PROJ_ROWS, LANES), lambda i: (i % seq_tiles, 0))
    full = lambda a: pl.BlockSpec(a.shape, lambda i: (0, 0))
    out_dtypes = (BF16, BF16, BF16, BF16, BF16, F32, F32, F32, BF16)
    out_widths = (SECTION, 2 * SECTION) + (SECTION,) * 7
    return pl.pallas_call(
        _proj_kernel,
        grid=(rows // PROJ_ROWS,),
        in_specs=[row_spec(d_model), full(norm_w), full(w_in_bf), full(lb_logits),
                  tab_spec, tab_spec, tab_spec],
        out_specs=[row_spec(w) for w in out_widths],
        out_shape=[jax.ShapeDtypeStruct((rows, w), dt)
                   for w, dt in zip(out_widths, out_dtypes)],
        compiler_params=pltpu.CompilerParams(
            dimension_semantics=("parallel",), vmem_limit_bytes=VMEM_LIMIT),
        name="in_projection",
    )(x2, norm_w, w_in_bf, lb_logits, cos_t, sina_t, sinb_t)


HG_LEVELS = (8, 16, 32)


def _hgrn_constants():
    c = HG_CHUNK
    t = np.arange(c)
    tri = (t[:, None] >= t[None, :]).astype(np.float32)
    first = (t // 8) * 8
    half = np.concatenate([tri, tri - 0.5 * (tri[first] + tri[first + 7])], axis=0)
    coef = np.concatenate([half, half], axis=1)
    ts, ss = t[:, None], t[None, :]
    level = np.full((c, c), len(HG_LEVELS) + 1, np.int32)
    level[(ts // 8 == ss // 8) & (ts >= ss)] = 0
    for i, lb in enumerate(HG_LEVELS):
        m = ((ts // (2 * lb) == ss // (2 * lb)) & ((ts // lb) % 2 == 1)
             & ((ss // lb) % 2 == 0))
        level[m] = i + 1
    return jnp.asarray(coef, BF16), jnp.asarray(level)


def _dot_nt(a, b):
    return lax.dot_general(a, b, (((1,), (1,)), ((), ())), preferred_element_type=F32)


def _dot_tn(a, b):
    return lax.dot_general(a, b, (((0,), (0,)), ((), ())), preferred_element_type=F32)


def _hgrn_kernel(coef_ref, level_ref, qs_ref, g_ref, kk_ref, vi_ref, zs_ref, gn_ref,
                 o_ref, state_ref, qd_ref, oi_ref, kv_ref, dec_ref):
    @pl.when(pl.program_id(2) == 0)
    def _():
        state_ref[...] = jnp.zeros_like(state_ref)

    c = HG_CHUNK
    n_chunks = HG_ROWS // c
    coef = coef_ref[...]
    level = level_ref[...]
    gn = gn_ref[...]

    chunk_rows = [pl.ds(ci * c, c) for ci in range(n_chunks)]

    def cum_decay(rows):
        g2 = g_ref[0, rows, :]
        return jnp.dot(coef, jnp.concatenate([g2[:, :HG_DK], g2[:, HG_DK:]], axis=0),
                       preferred_element_type=F32)

    def level_operands(rows, be):
        q = qs_ref[0, rows, :]
        k = kk_ref[0, rows, :]
        b = be[0:c]
        e_diag = be[c:2 * c]
        ops = [(q * jnp.exp2(e_diag).astype(BF16), k * jnp.exp2(-e_diag).astype(BF16))]
        for lb in HG_LEVELS:
            ref = jnp.concatenate(
                [jnp.broadcast_to(b[p + lb - 1:p + lb, :], (2 * lb, HG_DK))
                 for p in range(0, c, 2 * lb)], axis=0)
            w = jnp.exp2(-jnp.abs(b - ref)).astype(BF16)
            ops.append((q * w, k * w))
        b_last = b[c - 1:c, :]
        return ops, k * jnp.exp2(b_last - b).astype(BF16), q * jnp.exp2(b).astype(BF16), \
            jnp.exp2(b_last)

    def intra_scores(ops):
        scores = jnp.where(level == 0, _dot_nt(*ops[0]), 0.0)
        for i in range(len(HG_LEVELS)):
            scores = jnp.where(level == i + 1, _dot_nt(*ops[i + 1]), scores)
        return scores.astype(BF16)

    stage1 = [cum_decay(rows) for rows in chunk_rows]
    stage2 = [level_operands(rows, be) for rows, be in zip(chunk_rows, stage1)]
    stage3 = [intra_scores(ops) for ops, _, _, _ in stage2]
    for ci, rows in enumerate(chunk_rows):
        _, k_dec, q_dec, decay = stage2[ci]
        v = vi_ref[0, rows, :]
        oi_ref[rows, :] = jnp.dot(stage3[ci], v, preferred_element_type=F32)
        kv_ref[ci] = _dot_tn(v, k_dec)
        qd_ref[rows, :] = q_dec
        dec_ref[ci] = jnp.broadcast_to(decay, dec_ref.shape[1:])

    state = state_ref[...]
    for ci in range(n_chunks):
        rows = pl.ds(ci * c, c)
        o = oi_ref[rows, :] + _dot_nt(qd_ref[rows, :], state.astype(BF16))
        state = state * dec_ref[ci, 0:1, :] + kv_ref[ci]
        z = zs_ref[0, rows, :].astype(F32)
        ms = jnp.mean(o * o, axis=-1, keepdims=True)
        o_ref[0, rows, :] = (o * lax.rsqrt(ms + NORM_EPS) * gn * z).astype(BF16)
    state_ref[...] = state


def _hgrn(qs, g2, kk, vi, zs, hg_norm_w):
    b, s, w = qs.shape
    assert w == HG_HEADS * HG_DK and s % HG_ROWS == 0 and g2.shape[-1] == 2 * w
    coef, level = _hgrn_constants()
    head_spec = pl.BlockSpec((1, HG_ROWS, HG_DK), lambda bi, h, t: (bi, t, h))
    pair_spec = pl.BlockSpec((1, HG_ROWS, 2 * HG_DK), lambda bi, h, t: (bi, t, h))
    const = lambda a: pl.BlockSpec(a.shape, lambda bi, h, t: (0, 0))
    return pl.pallas_call(
        _hgrn_kernel,
        grid=(b, HG_HEADS, s // HG_ROWS),
        in_specs=[const(coef), const(level), head_spec, pair_spec, head_spec, head_spec,
                  head_spec, pl.BlockSpec((1, HG_DK), lambda bi, h, t: (0, h))],
        out_specs=head_spec,
        out_shape=jax.ShapeDtypeStruct((b, s, w), BF16),
        scratch_shapes=[pltpu.VMEM((HG_DK, HG_DK), F32),
                        pltpu.VMEM((HG_ROWS, HG_DK), BF16),
                        pltpu.VMEM((HG_ROWS, HG_DK), F32),
                        pltpu.VMEM((HG_ROWS // HG_CHUNK, HG_DK, HG_DK), F32),
                        pltpu.VMEM((HG_ROWS // HG_CHUNK, 8, HG_DK), F32)],
        compiler_params=pltpu.CompilerParams(
            dimension_semantics=("parallel", "parallel", "arbitrary"),
            vmem_limit_bytes=VMEM_LIMIT),
        name="hgrn2",
    )(coef, level, qs, g2, kk, vi, zs, hg_norm_w)


def _attn_bias():
    qi = np.arange(ATT_BLOCK)[:, None]
    kj = np.arange(2 * ATT_BLOCK)[None, :]
    dist = ATT_BLOCK + qi - kj
    valid = (dist >= 0) & (dist <= ATT_BLOCK)
    b0 = np.where(valid, 0.0, NEG_BIG)
    b1 = np.where(valid & (kj >= ATT_BLOCK), 0.0, NEG_BIG)
    return jnp.asarray(np.stack([b0, b1]), F32)


def _attn_kernel(bias_ref, aq_ref, ak_ref, av_ref, az_ref, o_ref,
                 khist, vhist, kprm, vprm, qprm, m_ref, l_ref, acc_ref, onat):
    ta = ATT_TILE
    blk = ATT_BLOCK
    nph = ATT_PERM
    reg = ta // nph
    tile = pl.program_id(2)
    cur = tile % 2
    prev = 1 - cur

    @pl.when(tile == 0)
    def _():
        khist[...] = jnp.zeros_like(khist)
        vhist[...] = jnp.zeros_like(vhist)
        kprm[1] = jnp.zeros((nph, reg, LANES), F32)
        vprm[1] = jnp.zeros((nph, reg, LANES), F32)

    for r in range(nph):
        kprm[cur, r] = ak_ref[0, pl.ds(r, reg, stride=nph), :]
        vprm[cur, r] = av_ref[0, pl.ds(r, reg, stride=nph), :]
        qprm[r] = aq_ref[0, pl.ds(r, reg, stride=nph), :]

    first_head = lax.broadcasted_iota(jnp.int32, (blk, LANES), 1) < AT_HEAD_DIM
    ones = jnp.ones((2 * blk, LANES), BF16)
    bias_any = bias_ref[0]
    bias_first = bias_ref[(tile == 0).astype(jnp.int32)]

    def block_stats(q, k, v, bias):
        k = k.astype(BF16)
        v = v.astype(BF16)
        q2 = jnp.concatenate([jnp.where(first_head, q, 0.0),
                              jnp.where(first_head, 0.0, q)], axis=0).astype(BF16)
        s = _dot_nt(q2, k) + jnp.concatenate([bias, bias], axis=0)
        m = jnp.max(s, axis=-1, keepdims=True)
        p = jnp.exp2(s - m).astype(BF16)
        pv = jnp.dot(p, jnp.concatenate([v, ones], axis=1), preferred_element_type=F32)
        return (jnp.where(first_head, m[:blk], m[blk:]),
                jnp.where(first_head, pv[:blk, LANES:], pv[blk:, LANES:]),
                jnp.where(first_head, pv[:blk, :LANES], pv[blk:, :LANES]))

    def store_stats(pi, rows, stats):
        m_ref[pi, rows, :], l_ref[pi, rows, :], acc_ref[pi, rows, :] = stats

    def window(first_ref, first_rows, ref, rows):
        return jnp.concatenate([first_ref[first_rows], ref[rows]], axis=0)

    for j in range(ta // blk):
        q_rows = pl.ds(j * blk, blk)
        if j == 0:
            k = window(khist, (slice(None),) * 2, ak_ref, (0, q_rows))
            v = window(vhist, (slice(None),) * 2, av_ref, (0, q_rows))
        else:
            k_rows = pl.ds((j - 1) * blk, 2 * blk)
            k, v = ak_ref[0, k_rows, :], av_ref[0, k_rows, :]
        store_stats(0, q_rows, block_stats(aq_ref[0, q_rows, :], k, v,
                                           bias_first if j == 0 else bias_any))

    khist[...] = ak_ref[0, ta - blk:ta, :]
    vhist[...] = av_ref[0, ta - blk:ta, :]

    for r in range(nph):
        for j in range(reg // blk):
            q_rows = pl.ds(j * blk, blk)
            if j == 0:
                last = pl.ds(reg - blk, blk)
                k = window(kprm, (prev, r, last), kprm, (cur, r, q_rows))
                v = window(vprm, (prev, r, last), vprm, (cur, r, q_rows))
            else:
                k_rows = pl.ds((j - 1) * blk, 2 * blk)
                k, v = kprm[cur, r, k_rows, :], vprm[cur, r, k_rows, :]
            stats = block_stats(qprm[r, q_rows, :], k, v, bias_first if j == 0 else bias_any)
            store_stats(1, pl.ds(r * reg + j * blk, blk), stats)
        for r16 in range(nph):
            rows = pl.ds(r16, blk, stride=nph)
            k = window(kprm, (prev, r, rows), kprm, (cur, r, rows))
            v = window(vprm, (prev, r, rows), vprm, (cur, r, rows))
            stats = block_stats(qprm[r, rows, :], k, v, bias_first)
            store_stats(2, pl.ds(r * reg + r16, blk, stride=nph), stats)

    def merge(ci, carry):
        pieces = reg // ATT_MERGE_ROWS
        r = ci // pieces
        i0 = (ci % pieces) * ATT_MERGE_ROWS
        nat_rows = pl.ds(r + nph * i0, ATT_MERGE_ROWS, stride=nph)
        prm_rows = pl.ds(pl.multiple_of(ci * ATT_MERGE_ROWS, ATT_MERGE_ROWS), ATT_MERGE_ROWS)
        rows = (nat_rows, prm_rows, prm_rows)
        ms = [m_ref[pi, rw, :] for pi, rw in enumerate(rows)]
        m_all = functools.reduce(jnp.maximum, ms)
        ws = [jnp.exp2(m - m_all) for m in ms]
        l_all = sum(w * l_ref[pi, rw, :] for (pi, rw), w in zip(enumerate(rows), ws))
        acc = sum(w * acc_ref[pi, rw, :] for (pi, rw), w in zip(enumerate(rows), ws))
        onat[nat_rows, :] = acc / l_all
        return carry

    lax.fori_loop(0, ta // ATT_MERGE_ROWS, merge, 0)

    def gate(ci, carry):
        rows = pl.ds(pl.multiple_of(ci * ATT_MERGE_ROWS, ATT_MERGE_ROWS), ATT_MERGE_ROWS)
        o_ref[0, rows, :] = (onat[rows, :] * az_ref[0, rows, :].astype(F32)).astype(BF16)
        return carry

    lax.fori_loop(0, ta // ATT_MERGE_ROWS, gate, 0)


def _attention(aq, ak, av, az):
    b, s, w = aq.shape
    assert s % ATT_TILE == 0 and w % LANES == 0
    assert DILATIONS == (1, ATT_PERM, ATT_PERM * ATT_PERM)
    bias = _attn_bias()
    reg = ATT_TILE // ATT_PERM
    pair_spec = pl.BlockSpec((1, ATT_TILE, LANES), lambda bi, h, t: (bi, t, h))
    hist = pltpu.VMEM((ATT_BLOCK, LANES), F32)
    prm = pltpu.VMEM((2, ATT_PERM, reg, LANES), F32)
    stat = pltpu.VMEM((len(DILATIONS), ATT_TILE, LANES), F32)
    return pl.pallas_call(
        _attn_kernel,
        grid=(b, w // LANES, s // ATT_TILE),
        in_specs=[pl.BlockSpec(bias.shape, lambda bi, h, t: (0, 0, 0))] + [pair_spec] * 4,
        out_specs=pair_spec,
        out_shape=jax.ShapeDtypeStruct((b, s, w), BF16),
        scratch_shapes=[hist, hist, prm, prm, pltpu.VMEM((ATT_PERM, reg, LANES), F32),
                        stat, stat, stat, pltpu.VMEM((ATT_TILE, LANES), F32)],
        compiler_params=pltpu.CompilerParams(
            dimension_semantics=("parallel", "parallel", "arbitrary"),
            vmem_limit_bytes=VMEM_LIMIT),
        name="dilated_attention",
    )(bias, aq, ak, av, az)


def _out_kernel(x_ref, ohg_ref, oat_ref, w_ref, fw_ref, o_ref):
    half = ohg_ref.shape[1]
    y = jnp.dot(ohg_ref[...], w_ref[0:half, :], preferred_element_type=F32)
    y = y + jnp.dot(oat_ref[...], w_ref[half:2 * half, :], preferred_element_type=F32)
    h = x_ref[...] + y
    ms = jnp.mean(h * h, axis=-1, keepdims=True)
    o_ref[...] = h * lax.rsqrt(ms + NORM_EPS) * fw_ref[...]


def _out_projection(x2, ohg, oat, w_out_bf, final_norm_w):
    rows, d_model = x2.shape
    half = ohg.shape[1]
    row_spec = lambda w: pl.BlockSpec((PROJ_ROWS, w), lambda i: (i, 0))
    full = lambda a: pl.BlockSpec(a.shape, lambda i: (0, 0))
    return pl.pallas_call(
        _out_kernel,
        grid=(rows // PROJ_ROWS,),
        in_specs=[row_spec(d_model), row_spec(half), row_spec(half),
                  full(w_out_bf), full(final_norm_w)],
        out_specs=row_spec(d_model),
        out_shape=jax.ShapeDtypeStruct((rows, d_model), F32),
        compiler_params=pltpu.CompilerParams(
            dimension_semantics=("parallel",), vmem_limit_bytes=VMEM_LIMIT),
        name="out_projection",
    )(x2, ohg, oat, w_out_bf, final_norm_w)


def kernel(x, norm_w, w_in, hgrn_lb_logits, hg_norm_w, w_out, final_norm_w):
    b, s, d_model = x.shape
    assert norm_w.shape[0] == 1 and w_in.shape[0] == 1 and w_out.shape[0] == 1
    x2 = x.reshape(b * s, d_model)
    qs, g, kk, vi, zs, aq, ak, av, az = _in_projection(
        x2, norm_w, w_in[0].astype(BF16), hgrn_lb_logits, s)
    to3 = lambda a: a.reshape(b, s, a.shape[-1])
    ohg = _hgrn(to3(qs), to3(g), to3(kk), to3(vi), to3(zs), hg_norm_w)
    oat = _attention(to3(aq), to3(ak), to3(av), to3(az))
    out = _out_projection(x2, ohg.reshape(b * s, -1), oat.reshape(b * s, -1),
                          w_out[0].astype(BF16), final_norm_w.reshape(1, d_model))
    return out.reshape(b, s, d_model)
```

```python
import functools

import numpy as np
import jax
import jax.numpy as jnp
from jax import lax
from jax.experimental import pallas as pl
from jax.experimental.pallas import tpu as pltpu

F32 = jnp.float32
BF16 = jnp.bfloat16

NORM_EPS = 1e-6
ROPE_THETA = 10000.0
LANES = 128

HG_HEADS = 4
HG_DK = 128
HG_CHUNK = 64
AT_HEAD_DIM = 64
SECTION = 512
DILATIONS = (1, 4, 16)
ATT_BLOCK = 128
ATT_TILE = ATT_BLOCK * max(DILATIONS)
NEG_BIG = -1e30

PROJ_ROWS = 512
OUT_ROWS = 1024
HG_ROWS = 1024
ATT_PERM = 4
ATT_MERGE_ROWS = 256
LOG2E = 1.4426950408889634
VMEM_LIMIT = 48 * 1024 * 1024


def _proj_kernel(x_ref, nw_ref, w_ref, lbl_ref, cos_ref, sina_ref, sinb_ref,
                 qs_ref, g_ref, kk_ref, vi_ref, zs_ref,
                 aq_ref, ak_ref, av_ref, az_ref):
    x = x_ref[...]
    ms = jnp.mean(x * x, axis=-1, keepdims=True)
    u = (x * lax.rsqrt(ms + NORM_EPS) * nw_ref[...]).astype(BF16)

    def section(j):
        return jnp.dot(u, w_ref[:, j * SECTION:(j + 1) * SECTION],
                       preferred_element_type=F32)

    def sigmoid(p):
        return 0.5 * jnp.tanh(0.5 * p) + 0.5

    def silu(p):
        return p * sigmoid(p)

    lg = lbl_ref[...]
    e = jnp.exp(lg - jnp.max(lg, axis=0, keepdims=True))
    lb = e[0:1, :] / jnp.sum(e, axis=0, keepdims=True)
    lb = jnp.clip(lb, 1e-6, 1.0 - 1e-6)

    def store_q(p):
        qs_ref[...] = silu(p).astype(BF16)

    def store_f(p):
        sig = sigmoid(p)
        g2 = jnp.log2(lb + (1.0 - lb) * sig)
        g2_hi = g2.astype(BF16)
        g2_lo = (g2 - g2_hi.astype(F32)).astype(BF16)
        for h in range(HG_HEADS):
            g_ref[:, 2 * h * HG_DK:(2 * h + 1) * HG_DK] = g2_hi[:, h * HG_DK:(h + 1) * HG_DK]
            g_ref[:, (2 * h + 1) * HG_DK:(2 * h + 2) * HG_DK] = g2_lo[:, h * HG_DK:(h + 1) * HG_DK]
        kk_ref[...] = ((1.0 - lb) * (1.0 - sig)).astype(BF16)

    def store_i(p):
        vi_ref[...] = p.astype(BF16)

    def store_z(dst_ref, p):
        dst_ref[...] = silu(p).astype(BF16)

    def store_rope(dst_ref, scale, p):
        for c in range(SECTION // LANES):
            xc = p[:, c * LANES:(c + 1) * LANES]
            r = (xc * cos_ref[...] + pltpu.roll(xc, LANES - 32, 1) * sina_ref[...]
                 + pltpu.roll(xc, 32, 1) * sinb_ref[...])
            dst_ref[:, c * LANES:(c + 1) * LANES] = r * scale

    def store_v(p):
        av_ref[...] = p

    plan = [(0, store_q), (1, store_f), (3, functools.partial(store_z, zs_ref)),
            (7, functools.partial(store_z, az_ref)),
            (4, functools.partial(store_rope, aq_ref, AT_HEAD_DIM ** -0.5 * LOG2E)),
            (5, functools.partial(store_rope, ak_ref, 1.0)), (2, store_i), (6, store_v)]
    pending = None
    for j, finish in plan:
        p = section(j)
        if pending is not None:
            pending()
        pending = functools.partial(finish, p)
    pending()


@functools.lru_cache(maxsize=None)
def _rope_tables(seq):
    half = AT_HEAD_DIM // 2
    inv_freq = 1.0 / (ROPE_THETA ** (np.arange(half, dtype=np.float64) / half))
    ang = np.arange(seq, dtype=np.float64)[:, None] * inv_freq[None, :]
    cos = np.cos(ang)
    sin = np.sin(ang)
    zero = np.zeros_like(sin)
    reps = LANES // AT_HEAD_DIM
    cos_t = np.tile(np.concatenate([cos, cos], axis=1), (1, reps))
    sina_t = np.tile(np.concatenate([-sin, zero], axis=1), (1, reps))
    sinb_t = np.tile(np.concatenate([zero, sin], axis=1), (1, reps))
    return tuple(np.asarray(t, np.float32) for t in (cos_t, sina_t, sinb_t))


def _in_projection(x2, norm_w, w_in_bf, lb_logits, seq):
    rows, d_model = x2.shape
    in_cols = w_in_bf.shape[1]
    assert in_cols == 8 * SECTION and rows % PROJ_ROWS == 0 and seq % PROJ_ROWS == 0
    cos_t, sina_t, sinb_t = _rope_tables(seq)
    seq_tiles = seq // PROJ_ROWS
    row_spec = lambda w: pl.BlockSpec((PROJ_ROWS, w), lambda i: (i, 0))
    tab_spec = pl.BlockSpec((PROJ_ROWS, LANES), lambda i: (i % seq_tiles, 0))
    full = lambda a: pl.BlockSpec(a.shape, lambda i: (0, 0))
    out_dtypes = (BF16, BF16, BF16, BF16, BF16, F32, F32, F32, BF16)
    out_widths = (SECTION, 2 * SECTION) + (SECTION,) * 7
    return pl.pallas_call(
        _proj_kernel,
        grid=(rows // PROJ_ROWS,),
        in_specs=[row_spec(d_model), full(norm_w), full(w_in_bf), full(lb_logits),
                  tab_spec, tab_spec, tab_spec],
        out_specs=[row_spec(w) for w in out_widths],
        out_shape=[jax.ShapeDtypeStruct((rows, w), dt)
                   for w, dt in zip(out_widths, out_dtypes)],
        compiler_params=pltpu.CompilerParams(
            dimension_semantics=("parallel",), vmem_limit_bytes=VMEM_LIMIT),
        name="in_projection",
    )(x2, norm_w, w_in_bf, lb_logits, cos_t, sina_t, sinb_t)


HG_LEVELS = (8, 16, 32)


def _hgrn_constants():
    c = HG_CHUNK
    t = np.arange(c)
    tri = (t[:, None] >= t[None, :]).astype(np.float32)
    first = (t // 8) * 8
    half = np.concatenate([tri, tri - 0.5 * (tri[first] + tri[first + 7])], axis=0)
    coef = np.concatenate([half, half], axis=1)
    ts, ss = t[:, None], t[None, :]
    level = np.full((c, c), len(HG_LEVELS) + 1, np.int32)
    level[(ts // 8 == ss // 8) & (ts >= ss)] = 0
    for i, lb in enumerate(HG_LEVELS):
        m = ((ts // (2 * lb) == ss // (2 * lb)) & ((ts // lb) % 2 == 1)
             & ((ss // lb) % 2 == 0))
        level[m] = i + 1
    return jnp.asarray(coef, BF16), jnp.asarray(level)


def _dot_nt(a, b):
    return lax.dot_general(a, b, (((1,), (1,)), ((), ())), preferred_element_type=F32)


def _dot_tn(a, b):
    return lax.dot_general(a, b, (((0,), (0,)), ((), ())), preferred_element_type=F32)


def _hgrn_kernel(coef_ref, level_ref, qs_ref, g_ref, kk_ref, vi_ref, zs_ref, gn_ref,
                 o_ref, state_ref, qd_ref, oi_ref, kv_ref, dec_ref):
    @pl.when(pl.program_id(2) == 0)
    def _():
        state_ref[...] = jnp.zeros_like(state_ref)

    c = HG_CHUNK
    n_chunks = HG_ROWS // c
    coef = coef_ref[...]
    level = level_ref[...]
    gn = gn_ref[...]

    chunk_rows = [pl.ds(ci * c, c) for ci in range(n_chunks)]

    def cum_decay(rows):
        g2 = g_ref[0, rows, :]
        return jnp.dot(coef, jnp.concatenate([g2[:, :HG_DK], g2[:, HG_DK:]], axis=0),
                       preferred_element_type=F32)

    def level_operands(rows, be):
        q = qs_ref[0, rows, :]
        k = kk_ref[0, rows, :]
        b = be[0:c]
        e_diag = be[c:2 * c]
        ops = [(q * jnp.exp2(e_diag).astype(BF16), k * jnp.exp2(-e_diag).astype(BF16))]
        for lb in HG_LEVELS:
            ref = jnp.concatenate(
                [jnp.broadcast_to(b[p + lb - 1:p + lb, :], (2 * lb, HG_DK))
                 for p in range(0, c, 2 * lb)], axis=0)
            w = jnp.exp2(-jnp.abs(b - ref)).astype(BF16)
            ops.append((q * w, k * w))
        b_last = b[c - 1:c, :]
        return ops, k * jnp.exp2(b_last - b).astype(BF16), q * jnp.exp2(b).astype(BF16), \
            jnp.exp2(b_last)

    def intra_scores(ops):
        scores = jnp.where(level == 0, _dot_nt(*ops[0]), 0.0)
        for i in range(len(HG_LEVELS)):
            scores = jnp.where(level == i + 1, _dot_nt(*ops[i + 1]), scores)
        return scores.astype(BF16)

    stage1 = [cum_decay(rows) for rows in chunk_rows]
    stage2 = [level_operands(rows, be) for rows, be in zip(chunk_rows, stage1)]
    stage3 = [intra_scores(ops) for ops, _, _, _ in stage2]
    for ci, rows in enumerate(chunk_rows):
        _, k_dec, q_dec, decay = stage2[ci]
        v = vi_ref[0, rows, :]
        oi_ref[rows, :] = jnp.dot(stage3[ci], v, preferred_element_type=F32)
        kv_ref[ci] = _dot_tn(v, k_dec)
        qd_ref[rows, :] = q_dec
        dec_ref[ci] = jnp.broadcast_to(decay, dec_ref.shape[1:])

    state = state_ref[...]
    for ci in range(n_chunks):
        rows = pl.ds(ci * c, c)
        o = oi_ref[rows, :] + _dot_nt(qd_ref[rows, :], state.astype(BF16))
        state = state * dec_ref[ci, 0:1, :] + kv_ref[ci]
        z = zs_ref[0, rows, :].astype(F32)
        ms = jnp.mean(o * o, axis=-1, keepdims=True)
        o_ref[0, rows, :] = (o * lax.rsqrt(ms + NORM_EPS) * gn * z).astype(BF16)
    state_ref[...] = state


def _hgrn(qs, g2, kk, vi, zs, hg_norm_w):
    b, s, w = qs.shape
    assert w == HG_HEADS * HG_DK and s % HG_ROWS == 0 and g2.shape[-1] == 2 * w
    coef, level = _hgrn_constants()
    head_spec = pl.BlockSpec((1, HG_ROWS, HG_DK), lambda bi, h, t: (bi, t, h))
    pair_spec = pl.BlockSpec((1, HG_ROWS, 2 * HG_DK), lambda bi, h, t: (bi, t, h))
    const = lambda a: pl.BlockSpec(a.shape, lambda bi, h, t: (0, 0))
    return pl.pallas_call(
        _hgrn_kernel,
        grid=(b, HG_HEADS, s // HG_ROWS),
        in_specs=[const(coef), const(level), head_spec, pair_spec, head_spec, head_spec,
                  head_spec, pl.BlockSpec((1, HG_DK), lambda bi, h, t: (0, h))],
        out_specs=head_spec,
        out_shape=jax.ShapeDtypeStruct((b, s, w), BF16),
        scratch_shapes=[pltpu.VMEM((HG_DK, HG_DK), F32),
                        pltpu.VMEM((HG_ROWS, HG_DK), BF16),
                        pltpu.VMEM((HG_ROWS, HG_DK), F32),
                        pltpu.VMEM((HG_ROWS // HG_CHUNK, HG_DK, HG_DK), F32),
                        pltpu.VMEM((HG_ROWS // HG_CHUNK, 8, HG_DK), F32)],
        compiler_params=pltpu.CompilerParams(
            dimension_semantics=("parallel", "parallel", "arbitrary"),
            vmem_limit_bytes=VMEM_LIMIT),
        name="hgrn2",
    )(coef, level, qs, g2, kk, vi, zs, hg_norm_w)


def _attn_bias():
    qi = np.arange(ATT_BLOCK)[:, None]
    kj = np.arange(2 * ATT_BLOCK)[None, :]
    dist = ATT_BLOCK + qi - kj
    valid = (dist >= 0) & (dist <= ATT_BLOCK)
    b0 = np.where(valid, 0.0, NEG_BIG)
    b1 = np.where(valid & (kj >= ATT_BLOCK), 0.0, NEG_BIG)
    return jnp.asarray(np.stack([b0, b1]), F32)


def _attn_kernel(bias_ref, aq_ref, ak_ref, av_ref, az_ref, o_ref,
                 khist, vhist, kprm, vprm, qprm, m_ref, l_ref, acc_ref, onat):
    ta = ATT_TILE
    blk = ATT_BLOCK
    nph = ATT_PERM
    reg = ta // nph
    tile = pl.program_id(2)
    cur = tile % 2
    prev = 1 - cur

    @pl.when(tile == 0)
    def _():
        khist[...] = jnp.zeros_like(khist)
        vhist[...] = jnp.zeros_like(vhist)
        kprm[1] = jnp.zeros((nph, reg, LANES), F32)
        vprm[1] = jnp.zeros((nph, reg, LANES), F32)

    for r in range(nph):
        kprm[cur, r] = ak_ref[0, pl.ds(r, reg, stride=nph), :]
        vprm[cur, r] = av_ref[0, pl.ds(r, reg, stride=nph), :]
        qprm[r] = aq_ref[0, pl.ds(r, reg, stride=nph), :]

    first_head = lax.broadcasted_iota(jnp.int32, (blk, LANES), 1) < AT_HEAD_DIM
    ones = jnp.ones((2 * blk, LANES), BF16)
    bias_any = bias_ref[0]
    bias_first = bias_ref[(tile == 0).astype(jnp.int32)]

    def block_stats(q, k, v, bias):
        k = k.astype(BF16)
        v = v.astype(BF16)
        q2 = jnp.concatenate([jnp.where(first_head, q, 0.0),
                              jnp.where(first_head, 0.0, q)], axis=0).astype(BF16)
        s = _dot_nt(q2, k) + jnp.concatenate([bias, bias], axis=0)
        m = jnp.max(s, axis=-1, keepdims=True)
        p = jnp.exp2(s - m).astype(BF16)
        pv = jnp.dot(p, jnp.concatenate([v, ones], axis=1), preferred_element_type=F32)
        return (jnp.where(first_head, m[:blk], m[blk:]),
                jnp.where(first_head, pv[:blk, LANES:], pv[blk:, LANES:]),
                jnp.where(first_head, pv[:blk, :LANES], pv[blk:, :LANES]))

    def store_stats(pi, rows, stats):
        m_ref[pi, rows, :], l_ref[pi, rows, :], acc_ref[pi, rows, :] = stats

    def window(first_ref, first_rows, ref, rows):
        return jnp.concatenate([first_ref[first_rows], ref[rows]], axis=0)

    for j in range(ta // blk):
        q_rows = pl.ds(j * blk, blk)
        if j == 0:
            k = window(khist, (slice(None),) * 2, ak_ref, (0, q_rows))
            v = window(vhist, (slice(None),) * 2, av_ref, (0, q_rows))
        else:
            k_rows = pl.ds((j - 1) * blk, 2 * blk)
            k, v = ak_ref[0, k_rows, :], av_ref[0, k_rows, :]
        store_stats(0, q_rows, block_stats(aq_ref[0, q_rows, :], k, v,
                                           bias_first if j == 0 else bias_any))

    khist[...] = ak_ref[0, ta - blk:ta, :]
    vhist[...] = av_ref[0, ta - blk:ta, :]

    for r in range(nph):
        for j in range(reg // blk):
            q_rows = pl.ds(j * blk, blk)
            if j == 0:
                last = pl.ds(reg - blk, blk)
                k = window(kprm, (prev, r, last), kprm, (cur, r, q_rows))
                v = window(vprm, (prev, r, last), vprm, (cur, r, q_rows))
            else:
                k_rows = pl.ds((j - 1) * blk, 2 * blk)
                k, v = kprm[cur, r, k_rows, :], vprm[cur, r, k_rows, :]
            stats = block_stats(qprm[r, q_rows, :], k, v, bias_first if j == 0 else bias_any)
            store_stats(1, pl.ds(r * reg + j * blk, blk), stats)
        for r16 in range(nph):
            rows = pl.ds(r16, blk, stride=nph)
            k = window(kprm, (prev, r, rows), kprm, (cur, r, rows))
            v = window(vprm, (prev, r, rows), vprm, (cur, r, rows))
            stats = block_stats(qprm[r, rows, :], k, v, bias_first)
            store_stats(2, pl.ds(r * reg + r16, blk, stride=nph), stats)

    def merge(ci, carry):
        pieces = reg // ATT_MERGE_ROWS
        r = ci // pieces
        i0 = (ci % pieces) * ATT_MERGE_ROWS
        nat_rows = pl.ds(r + nph * i0, ATT_MERGE_ROWS, stride=nph)
        prm_rows = pl.ds(pl.multiple_of(ci * ATT_MERGE_ROWS, ATT_MERGE_ROWS), ATT_MERGE_ROWS)
        rows = (nat_rows, prm_rows, prm_rows)
        ms = [m_ref[pi, rw, :] for pi, rw in enumerate(rows)]
        m_all = functools.reduce(jnp.maximum, ms)
        ws = [jnp.exp2(m - m_all) for m in ms]
        l_all = sum(w * l_ref[pi, rw, :] for (pi, rw), w in zip(enumerate(rows), ws))
        acc = sum(w * acc_ref[pi, rw, :] for (pi, rw), w in zip(enumerate(rows), ws))
        onat[nat_rows, :] = acc / l_all
        return carry

    lax.fori_loop(0, ta // ATT_MERGE_ROWS, merge, 0)

    def gate(ci, carry):
        rows = pl.ds(pl.multiple_of(ci * ATT_MERGE_ROWS, ATT_MERGE_ROWS), ATT_MERGE_ROWS)
        o_ref[0, rows, :] = (onat[rows, :] * az_ref[0, rows, :].astype(F32)).astype(BF16)
        return carry

    lax.fori_loop(0, ta // ATT_MERGE_ROWS, gate, 0)


def _attention(aq, ak, av, az):
    b, s, w = aq.shape
    assert s % ATT_TILE == 0 and w % LANES == 0
    assert DILATIONS == (1, ATT_PERM, ATT_PERM * ATT_PERM)
    bias = _attn_bias()
    reg = ATT_TILE // ATT_PERM
    pair_spec = pl.BlockSpec((1, ATT_TILE, LANES), lambda bi, h, t: (bi, t, h))
    hist = pltpu.VMEM((ATT_BLOCK, LANES), F32)
    prm = pltpu.VMEM((2, ATT_PERM, reg, LANES), F32)
    stat = pltpu.VMEM((len(DILATIONS), ATT_TILE, LANES), F32)
    return pl.pallas_call(
        _attn_kernel,
        grid=(b, w // LANES, s // ATT_TILE),
        in_specs=[pl.BlockSpec(bias.shape, lambda bi, h, t: (0, 0, 0))] + [pair_spec] * 4,
        out_specs=pair_spec,
        out_shape=jax.ShapeDtypeStruct((b, s, w), BF16),
        scratch_shapes=[hist, hist, prm, prm, pltpu.VMEM((ATT_PERM, reg, LANES), F32),
                        stat, stat, stat, pltpu.VMEM((ATT_TILE, LANES), F32)],
        compiler_params=pltpu.CompilerParams(
            dimension_semantics=("parallel", "parallel", "arbitrary"),
            vmem_limit_bytes=VMEM_LIMIT),
        name="dilated_attention",
    )(bias, aq, ak, av, az)


def _out_kernel(x_ref, ohg_ref, oat_ref, w_ref, fw_ref, o_ref):
    half = ohg_ref.shape[1]
    y = jnp.dot(ohg_ref[...], w_ref[0:half, :], preferred_element_type=F32)
    y = y + jnp.dot(oat_ref[...], w_ref[half:2 * half, :], preferred_element_type=F32)
    h = x_ref[...] + y
    ms = jnp.mean(h * h, axis=-1, keepdims=True)
    o_ref[...] = h * lax.rsqrt(ms + NORM_EPS) * fw_ref[...]


def _out_projection(x2, ohg, oat, w_out_bf, final_norm_w):
    rows, d_model = x2.shape
    half = ohg.shape[1]
    assert rows % OUT_ROWS == 0
    row_spec = lambda w: pl.BlockSpec((OUT_ROWS, w), lambda i: (i, 0))
    full = lambda a: pl.BlockSpec(a.shape, lambda i: (0, 0))
    return pl.pallas_call(
        _out_kernel,
        grid=(rows // OUT_ROWS,),
        in_specs=[row_spec(d_model), row_spec(half), row_spec(half),
                  full(w_out_bf), full(final_norm_w)],
        out_specs=row_spec(d_model),
        out_shape=jax.ShapeDtypeStruct((rows, d_model), F32),
        compiler_params=pltpu.CompilerParams(
            dimension_semantics=("parallel",), vmem_limit_bytes=VMEM_LIMIT),
        name="out_projection",
    )(x2, ohg, oat, w_out_bf, final_norm_w)


def kernel(x, norm_w, w_in, hgrn_lb_logits, hg_norm_w, w_out, final_norm_w):
    b, s, d_model = x.shape
    assert norm_w.shape[0] == 1 and w_in.shape[0] == 1 and w_out.shape[0] == 1
    x2 = x.reshape(b * s, d_model)
    qs, g, kk, vi, zs, aq, ak, av, az = _in_projection(
        x2, norm_w, w_in[0].astype(BF16), hgrn_lb_logits, s)
    to3 = lambda a: a.reshape(b, s, a.shape[-1])
    ohg = _hgrn(to3(qs), to3(g), to3(kk), to3(vi), to3(zs), hg_norm_w)
    oat = _attention(to3(aq), to3(ak), to3(av), to3(az))
    out = _out_projection(x2, ohg.reshape(b * s, -1), oat.reshape(b * s, -1),
                          w_out[0].astype(BF16), final_norm_w.reshape(1, d_model))
    return out.reshape(b, s, d_model)
```

```python
import functools

import numpy as np
import jax
import jax.numpy as jnp
from jax import lax
from jax.experimental import pallas as pl
from jax.experimental.pallas import tpu as pltpu

F32 = jnp.float32
BF16 = jnp.bfloat16

NORM_EPS = 1e-6
ROPE_THETA = 10000.0
LANES = 128

HG_HEADS = 4
HG_DK = 128
HG_CHUNK = 64
AT_HEAD_DIM = 64
SECTION = 512
DILATIONS = (1, 4, 16)
ATT_BLOCK = 128
ATT_TILE = ATT_BLOCK * max(DILATIONS)
NEG_BIG = -1e30

PROJ_ROWS = 512
HG_ROWS = 512
ATT_PERM = 4
ATT_MERGE_ROWS = 256
LOG2E = 1.4426950408889634
VMEM_LIMIT = 48 * 1024 * 1024


def _proj_kernel(x_ref, nw_ref, w_ref, lbl_ref, cos_ref, sina_ref, sinb_ref,
                 qs_ref, g_ref, kk_ref, vi_ref, zs_ref,
                 aq_ref, ak_ref, av_ref, az_ref):
    x = x_ref[...]
    ms = jnp.mean(x * x, axis=-1, keepdims=True)
    u = (x * lax.rsqrt(ms + NORM_EPS) * nw_ref[...]).astype(BF16)

    def section(j):
        return jnp.dot(u, w_ref[:, j * SECTION:(j + 1) * SECTION],
                       preferred_element_type=F32)

    def sigmoid(p):
        return 0.5 * jnp.tanh(0.5 * p) + 0.5

    def silu(p):
        return p * sigmoid(p)

    lg = lbl_ref[...]
    e = jnp.exp(lg - jnp.max(lg, axis=0, keepdims=True))
    lb = e[0:1, :] / jnp.sum(e, axis=0, keepdims=True)
    lb = jnp.clip(lb, 1e-6, 1.0 - 1e-6)

    def store_q(p):
        qs_ref[...] = silu(p).astype(BF16)

    def store_f(p):
        sig = sigmoid(p)
        g2 = jnp.log2(lb + (1.0 - lb) * sig)
        g2_hi = g2.astype(BF16)
        g2_lo = (g2 - g2_hi.astype(F32)).astype(BF16)
        for h in range(HG_HEADS):
            g_ref[:, 2 * h * HG_DK:(2 * h + 1) * HG_DK] = g2_hi[:, h * HG_DK:(h + 1) * HG_DK]
            g_ref[:, (2 * h + 1) * HG_DK:(2 * h + 2) * HG_DK] = g2_lo[:, h * HG_DK:(h + 1) * HG_DK]
        kk_ref[...] = ((1.0 - lb) * (1.0 - sig)).astype(BF16)

    def store_i(p):
        vi_ref[...] = p.astype(BF16)

    def store_z(dst_ref, p):
        dst_ref[...] = silu(p).astype(BF16)

    def store_rope(dst_ref, scale, p):
        for c in range(SECTION // LANES):
            xc = p[:, c * LANES:(c + 1) * LANES]
            r = (xc * cos_ref[...] + pltpu.roll(xc, LANES - 32, 1) * sina_ref[...]
                 + pltpu.roll(xc, 32, 1) * sinb_ref[...])
            dst_ref[:, c * LANES:(c + 1) * LANES] = r * scale

    def store_v(p):
        av_ref[...] = p

    plan = [(0, store_q), (1, store_f), (3, functools.partial(store_z, zs_ref)),
            (7, functools.partial(store_z, az_ref)),
            (4, functools.partial(store_rope, aq_ref, AT_HEAD_DIM ** -0.5 * LOG2E)),
            (5, functools.partial(store_rope, ak_ref, 1.0)), (2, store_i), (6, store_v)]
    pending = None
    for j, finish in plan:
        p = section(j)
        if pending is not None:
            pending()
        pending = functools.partial(finish, p)
    pending()


@functools.lru_cache(maxsize=None)
def _rope_tables(seq):
    half = AT_HEAD_DIM // 2
    inv_freq = 1.0 / (ROPE_THETA ** (np.arange(half, dtype=np.float64) / half))
    ang = np.arange(seq, dtype=np.float64)[:, None] * inv_freq[None, :]
    cos = np.cos(ang)
    sin = np.sin(ang)
    zero = np.zeros_like(sin)
    reps = LANES // AT_HEAD_DIM
    cos_t = np.tile(np.concatenate([cos, cos], axis=1), (1, reps))
    sina_t = np.tile(np.concatenate([-sin, zero], axis=1), (1, reps))
    sinb_t = np.tile(np.concatenate([zero, sin], axis=1), (1, reps))
    return tuple(np.asarray(t, np.float32) for t in (cos_t, sina_t, sinb_t))


def _in_projection(x2, norm_w, w_in_bf, lb_logits, seq):
    rows, d_model = x2.shape
    in_cols = w_in_bf.shape[1]
    assert in_cols == 8 * SECTION and rows % PROJ_ROWS == 0 and seq % PROJ_ROWS == 0
    cos_t, sina_t, sinb_t = _rope_tables(seq)
    seq_tiles = seq // PROJ_ROWS
    row_spec = lambda w: pl.BlockSpec((PROJ_ROWS, w), lambda i: (i, 0))
    tab_spec = pl.BlockSpec((PROJ_ROWS, LANES), lambda i: (i % seq_tiles, 0))
    full = lambda a: pl.BlockSpec(a.shape, lambda i: (0, 0))
    out_dtypes = (BF16, BF16, BF16, BF16, BF16, F32, F32, F32, BF16)
    out_widths = (SECTION, 2 * SECTION) + (SECTION,) * 7
    return pl.pallas_call(
        _proj_kernel,
        grid=(rows // PROJ_ROWS,),
        in_specs=[row_spec(d_model), full(norm_w), full(w_in_bf), full(lb_logits),
                  tab_spec, tab_spec, tab_spec],
        out_specs=[row_spec(w) for w in out_widths],
        out_shape=[jax.ShapeDtypeStruct((rows, w), dt)
                   for w, dt in zip(out_widths, out_dtypes)],
        compiler_params=pltpu.CompilerParams(
            dimension_semantics=("parallel",), vmem_limit_bytes=VMEM_LIMIT),
        name="in_projection",
    )(x2, norm_w, w_in_bf, lb_logits, cos_t, sina_t, sinb_t)


HG_LEVELS = (8, 16, 32)


def _hgrn_constants():
    c = HG_CHUNK
    t = np.arange(c)
    tri = (t[:, None] >= t[None, :]).astype(np.float32)
    first = (t // 8) * 8
    half = np.concatenate([tri, tri - 0.5 * (tri[first] + tri[first + 7])], axis=0)
    coef = np.concatenate([half, half], axis=1)
    ts, ss = t[:, None], t[None, :]
    level = np.full((c, c), len(HG_LEVELS) + 1, np.int32)
    level[(ts // 8 == ss // 8) & (ts >= ss)] = 0
    for i, lb in enumerate(HG_LEVELS):
        m = ((ts // (2 * lb) == ss // (2 * lb)) & ((ts // lb) % 2 == 1)
             & ((ss // lb) % 2 == 0))
        level[m] = i + 1
    return jnp.asarray(coef, BF16), jnp.asarray(level)


def _dot_nt(a, b):
    return lax.dot_general(a, b, (((1,), (1,)), ((), ())), preferred_element_type=F32)


def _dot_tn(a, b):
    return lax.dot_general(a, b, (((0,), (0,)), ((), ())), preferred_element_type=F32)


def _hgrn_out_kernel(coef_ref, level_ref, qs_ref, g_ref, kk_ref, vi_ref, zs_ref, gn_ref,
                     x_ref, oat_ref, w_ref, fw_ref, o_ref,
                     state_ref, qd_ref, oi_ref, kv_ref, dec_ref, ohg_ref):
    @pl.when(pl.program_id(1) == 0)
    def _():
        state_ref[...] = jnp.zeros_like(state_ref)

    c = HG_CHUNK
    n_chunks = HG_ROWS // c
    half = HG_HEADS * HG_DK
    coef = coef_ref[...]
    level = level_ref[...]
    units = [(h, ci) for h in range(HG_HEADS) for ci in range(n_chunks)]

    def rows_of(ci):
        return pl.ds(ci * c, c)

    def cols_of(h, width=HG_DK):
        return pl.ds(h * width, width)

    def cum_decay(h, ci):
        g2 = g_ref[0, rows_of(ci), cols_of(h, 2 * HG_DK)]
        return jnp.dot(coef, jnp.concatenate([g2[:, :HG_DK], g2[:, HG_DK:]], axis=0),
                       preferred_element_type=F32)

    def level_operands(h, ci, be):
        q = qs_ref[0, rows_of(ci), cols_of(h)]
        k = kk_ref[0, rows_of(ci), cols_of(h)]
        b = be[0:c]
        e_diag = be[c:2 * c]
        ops = [(q * jnp.exp2(e_diag).astype(BF16), k * jnp.exp2(-e_diag).astype(BF16))]
        for lb in HG_LEVELS:
            ref = jnp.concatenate(
                [jnp.broadcast_to(b[p + lb - 1:p + lb, :], (2 * lb, HG_DK))
                 for p in range(0, c, 2 * lb)], axis=0)
            w = jnp.exp2(-jnp.abs(b - ref)).astype(BF16)
            ops.append((q * w, k * w))
        b_last = b[c - 1:c, :]
        return ops, k * jnp.exp2(b_last - b).astype(BF16), q * jnp.exp2(b).astype(BF16), \
            jnp.exp2(b_last)

    def intra_scores(ops):
        scores = jnp.where(level == 0, _dot_nt(*ops[0]), 0.0)
        for i in range(len(HG_LEVELS)):
            scores = jnp.where(level == i + 1, _dot_nt(*ops[i + 1]), scores)
        return scores.astype(BF16)

    stage1 = [cum_decay(h, ci) for h, ci in units]
    stage2 = [level_operands(h, ci, be) for (h, ci), be in zip(units, stage1)]

    o_ref[0] = x_ref[0] + jnp.dot(oat_ref[0], w_ref[half:2 * half, :],
                                  preferred_element_type=F32)

    stage3 = [intra_scores(ops) for ops, _, _, _ in stage2]
    for (h, ci), (_, k_dec, q_dec, decay), scores in zip(units, stage2, stage3):
        v = vi_ref[0, rows_of(ci), cols_of(h)]
        oi_ref[rows_of(ci), cols_of(h)] = jnp.dot(scores, v, preferred_element_type=F32)
        kv_ref[h, ci] = _dot_tn(v, k_dec)
        qd_ref[rows_of(ci), cols_of(h)] = q_dec
        dec_ref[h, ci] = jnp.broadcast_to(decay, dec_ref.shape[2:])

    states = [state_ref[h] for h in range(HG_HEADS)]
    for ci in range(n_chunks):
        for h in range(HG_HEADS):
            rows, cols = rows_of(ci), cols_of(h)
            o = oi_ref[rows, cols] + _dot_nt(qd_ref[rows, cols], states[h].astype(BF16))
            states[h] = states[h] * dec_ref[h, ci, 0:1, :] + kv_ref[h, ci]
            z = zs_ref[0, rows, cols].astype(F32)
            ms = jnp.mean(o * o, axis=-1, keepdims=True)
            ohg_ref[rows, cols] = (o * lax.rsqrt(ms + NORM_EPS) * gn_ref[:, cols] * z
                                   ).astype(BF16)
    for h in range(HG_HEADS):
        state_ref[h] = states[h]

    hres = o_ref[0] + jnp.dot(ohg_ref[...], w_ref[0:half, :], preferred_element_type=F32)
    ms = jnp.mean(hres * hres, axis=-1, keepdims=True)
    o_ref[0] = hres * lax.rsqrt(ms + NORM_EPS) * fw_ref[...]


def _hgrn_out(qs, g2, kk, vi, zs, hg_norm_w, x, oat, w_out_bf, final_norm_w):
    b, s, w = qs.shape
    d_model = x.shape[-1]
    assert w == HG_HEADS * HG_DK and s % HG_ROWS == 0 and g2.shape[-1] == 2 * w
    assert w_out_bf.shape == (w + oat.shape[-1], d_model)
    coef, level = _hgrn_constants()
    n_chunks = HG_ROWS // HG_CHUNK
    tile = lambda width: pl.BlockSpec((1, HG_ROWS, width), lambda bi, t: (bi, t, 0))
    const = lambda a: pl.BlockSpec(a.shape, lambda bi, t: (0,) * a.ndim)
    return pl.pallas_call(
        _hgrn_out_kernel,
        grid=(b, s // HG_ROWS),
        in_specs=[const(coef), const(level), tile(w), tile(2 * w), tile(w), tile(w), tile(w),
                  const(hg_norm_w), tile(d_model), tile(oat.shape[-1]), const(w_out_bf),
                  const(final_norm_w)],
        out_specs=tile(d_model),
        out_shape=jax.ShapeDtypeStruct((b, s, d_model), F32),
        scratch_shapes=[pltpu.VMEM((HG_HEADS, HG_DK, HG_DK), F32),
                        pltpu.VMEM((HG_ROWS, w), BF16),
                        pltpu.VMEM((HG_ROWS, w), F32),
                        pltpu.VMEM((HG_HEADS, n_chunks, HG_DK, HG_DK), F32),
                        pltpu.VMEM((HG_HEADS, n_chunks, 8, HG_DK), F32),
                        pltpu.VMEM((HG_ROWS, w), BF16)],
        compiler_params=pltpu.CompilerParams(
            dimension_semantics=("parallel", "arbitrary"),
            vmem_limit_bytes=VMEM_LIMIT),
        name="hgrn2_out_projection",
    )(coef, level, qs, g2, kk, vi, zs, hg_norm_w, x, oat, w_out_bf, final_norm_w)


def _attn_bias():
    qi = np.arange(ATT_BLOCK)[:, None]
    kj = np.arange(2 * ATT_BLOCK)[None, :]
    dist = ATT_BLOCK + qi - kj
    valid = (dist >= 0) & (dist <= ATT_BLOCK)
    b0 = np.where(valid, 0.0, NEG_BIG)
    b1 = np.where(valid & (kj >= ATT_BLOCK), 0.0, NEG_BIG)
    return jnp.asarray(np.stack([b0, b1]), F32)


def _attn_kernel(bias_ref, aq_ref, ak_ref, av_ref, az_ref, o_ref,
                 khist, vhist, kprm, vprm, qprm, m_ref, l_ref, acc_ref, onat):
    ta = ATT_TILE
    blk = ATT_BLOCK
    nph = ATT_PERM
    reg = ta // nph
    tile = pl.program_id(2)
    cur = tile % 2
    prev = 1 - cur

    @pl.when(tile == 0)
    def _():
        khist[...] = jnp.zeros_like(khist)
        vhist[...] = jnp.zeros_like(vhist)
        kprm[1] = jnp.zeros((nph, reg, LANES), F32)
        vprm[1] = jnp.zeros((nph, reg, LANES), F32)

    for r in range(nph):
        kprm[cur, r] = ak_ref[0, pl.ds(r, reg, stride=nph), :]
        vprm[cur, r] = av_ref[0, pl.ds(r, reg, stride=nph), :]
        qprm[r] = aq_ref[0, pl.ds(r, reg, stride=nph), :]

    first_head = lax.broadcasted_iota(jnp.int32, (blk, LANES), 1) < AT_HEAD_DIM
    ones = jnp.ones((2 * blk, LANES), BF16)
    bias_any = bias_ref[0]
    bias_first = bias_ref[(tile == 0).astype(jnp.int32)]

    def block_stats(q, k, v, bias):
        k = k.astype(BF16)
        v = v.astype(BF16)
        q2 = jnp.concatenate([jnp.where(first_head, q, 0.0),
                              jnp.where(first_head, 0.0, q)], axis=0).astype(BF16)
        s = _dot_nt(q2, k) + jnp.concatenate([bias, bias], axis=0)
        m = jnp.max(s, axis=-1, keepdims=True)
        p = jnp.exp2(s - m).astype(BF16)
        pv = jnp.dot(p, jnp.concatenate([v, ones], axis=1), preferred_element_type=F32)
        return (jnp.where(first_head, m[:blk], m[blk:]),
                jnp.where(first_head, pv[:blk, LANES:], pv[blk:, LANES:]),
                jnp.where(first_head, pv[:blk, :LANES], pv[blk:, :LANES]))

    def store_stats(pi, rows, stats):
        m_ref[pi, rows, :], l_ref[pi, rows, :], acc_ref[pi, rows, :] = stats

    def window(first_ref, first_rows, ref, rows):
        return jnp.concatenate([first_ref[first_rows], ref[rows]], axis=0)

    for j in range(ta // blk):
        q_rows = pl.ds(j * blk, blk)
        if j == 0:
            k = window(khist, (slice(None),) * 2, ak_ref, (0, q_rows))
            v = window(vhist, (slice(None),) * 2, av_ref, (0, q_rows))
        else:
            k_rows = pl.ds((j - 1) * blk, 2 * blk)
            k, v = ak_ref[0, k_rows, :], av_ref[0, k_rows, :]
        store_stats(0, q_rows, block_stats(aq_ref[0, q_rows, :], k, v,
                                           bias_first if j == 0 else bias_any))

    khist[...] = ak_ref[0, ta - blk:ta, :]
    vhist[...] = av_ref[0, ta - blk:ta, :]

    for r in range(nph):
        for j in range(reg // blk):
            q_rows = pl.ds(j * blk, blk)
            if j == 0:
                last = pl.ds(reg - blk, blk)
                k = window(kprm, (prev, r, last), kprm, (cur, r, q_rows))
                v = window(vprm, (prev, r, last), vprm, (cur, r, q_rows))
            else:
                k_rows = pl.ds((j - 1) * blk, 2 * blk)
                k, v = kprm[cur, r, k_rows, :], vprm[cur, r, k_rows, :]
            stats = block_stats(qprm[r, q_rows, :], k, v, bias_first if j == 0 else bias_any)
            store_stats(1, pl.ds(r * reg + j * blk, blk), stats)
        for r16 in range(nph):
            rows = pl.ds(r16, blk, stride=nph)
            k = window(kprm, (prev, r, rows), kprm, (cur, r, rows))
            v = window(vprm, (prev, r, rows), vprm, (cur, r, rows))
            stats = block_stats(qprm[r, rows, :], k, v, bias_first)
            store_stats(2, pl.ds(r * reg + r16, blk, stride=nph), stats)

    def merge(ci, carry):
        pieces = reg // ATT_MERGE_ROWS
        r = ci // pieces
        i0 = (ci % pieces) * ATT_MERGE_ROWS
        nat_rows = pl.ds(r + nph * i0, ATT_MERGE_ROWS, stride=nph)
        prm_rows = pl.ds(pl.multiple_of(ci * ATT_MERGE_ROWS, ATT_MERGE_ROWS), ATT_MERGE_ROWS)
        rows = (nat_rows, prm_rows, prm_rows)
        ms = [m_ref[pi, rw, :] for pi, rw in enumerate(rows)]
        m_all = functools.reduce(jnp.maximum, ms)
        ws = [jnp.exp2(m - m_all) for m in ms]
        l_all = sum(w * l_ref[pi, rw, :] for (pi, rw), w in zip(enumerate(rows), ws))
        acc = sum(w * acc_ref[pi, rw, :] for (pi, rw), w in zip(enumerate(rows), ws))
        onat[nat_rows, :] = acc / l_all
        return carry

    lax.fori_loop(0, ta // ATT_MERGE_ROWS, merge, 0)

    def gate(ci, carry):
        rows = pl.ds(pl.multiple_of(ci * ATT_MERGE_ROWS, ATT_MERGE_ROWS), ATT_MERGE_ROWS)
        o_ref[0, rows, :] = (onat[rows, :] * az_ref[0, rows, :].astype(F32)).astype(BF16)
        return carry

    lax.fori_loop(0, ta // ATT_MERGE_ROWS, gate, 0)


def _attention(aq, ak, av, az):
    b, s, w = aq.shape
    assert s % ATT_TILE == 0 and w % LANES == 0
    assert DILATIONS == (1, ATT_PERM, ATT_PERM * ATT_PERM)
    bias = _attn_bias()
    reg = ATT_TILE // ATT_PERM
    pair_spec = pl.BlockSpec((1, ATT_TILE, LANES), lambda bi, h, t: (bi, t, h))
    hist = pltpu.VMEM((ATT_BLOCK, LANES), F32)
    prm = pltpu.VMEM((2, ATT_PERM, reg, LANES), F32)
    stat = pltpu.VMEM((len(DILATIONS), ATT_TILE, LANES), F32)
    return pl.pallas_call(
        _attn_kernel,
        grid=(b, w // LANES, s // ATT_TILE),
        in_specs=[pl.BlockSpec(bias.shape, lambda bi, h, t: (0, 0, 0))] + [pair_spec] * 4,
        out_specs=pair_spec,
        out_shape=jax.ShapeDtypeStruct((b, s, w), BF16),
        scratch_shapes=[hist, hist, prm, prm, pltpu.VMEM((ATT_PERM, reg, LANES), F32),
                        stat, stat, stat, pltpu.VMEM((ATT_TILE, LANES), F32)],
        compiler_params=pltpu.CompilerParams(
            dimension_semantics=("parallel", "parallel", "arbitrary"),
            vmem_limit_bytes=VMEM_LIMIT),
        name="dilated_attention",
    )(bias, aq, ak, av, az)


def kernel(x, norm_w, w_in, hgrn_lb_logits, hg_norm_w, w_out, final_norm_w):
    b, s, d_model = x.shape
    assert norm_w.shape[0] == 1 and w_in.shape[0] == 1 and w_out.shape[0] == 1
    qs, g2, kk, vi, zs, aq, ak, av, az = _in_projection(
        x.reshape(b * s, d_model), norm_w, w_in[0].astype(BF16), hgrn_lb_logits, s)
    to3 = lambda a: a.reshape(b, s, a.shape[-1])
    oat = _attention(to3(aq), to3(ak), to3(av), to3(az))
    return _hgrn_out(to3(qs), to3(g2), to3(kk), to3(vi), to3(zs), hg_norm_w, x, oat,
                     w_out[0].astype(BF16), final_norm_w.reshape(1, d_model))
```

```python
import functools

import numpy as np
import jax
import jax.numpy as jnp
from jax import lax
from jax.experimental import pallas as pl
from jax.experimental.pallas import tpu as pltpu

F32 = jnp.float32
BF16 = jnp.bfloat16

NORM_EPS = 1e-6
ROPE_THETA = 10000.0
LANES = 128

HG_HEADS = 4
HG_DK = 128
HG_CHUNK = 64
AT_HEAD_DIM = 64
SECTION = 512
DILATIONS = (1, 4, 16)
ATT_BLOCK = 128
ATT_TILE = ATT_BLOCK * max(DILATIONS)
NEG_BIG = -1e30

PROJ_ROWS = 1024
HG_ROWS = 512
HG_GROUP = 32
ATT_PERM = 4
ATT_MERGE_ROWS = 256
LOG2E = 1.4426950408889634
VMEM_LIMIT = 56 * 1024 * 1024


def _proj_kernel(x_ref, nw_ref, w_ref, lbl_ref, cos_ref, sina_ref, sinb_ref,
                 qs_ref, g_ref, kk_ref, vi_ref, zs_ref,
                 aq_ref, ak_ref, av_ref, az_ref):
    x = x_ref[...]
    ms = jnp.mean(x * x, axis=-1, keepdims=True)
    u = (x * lax.rsqrt(ms + NORM_EPS) * nw_ref[...]).astype(BF16)

    def section(j):
        return jnp.dot(u, w_ref[:, j * SECTION:(j + 1) * SECTION],
                       preferred_element_type=F32)

    def sigmoid(p):
        return 0.5 * jnp.tanh(0.5 * p) + 0.5

    def silu(p):
        return p * sigmoid(p)

    lg = lbl_ref[...]
    e = jnp.exp(lg - jnp.max(lg, axis=0, keepdims=True))
    lb = e[0:1, :] / jnp.sum(e, axis=0, keepdims=True)
    lb = jnp.clip(lb, 1e-6, 1.0 - 1e-6)

    def store_q(p):
        qs_ref[...] = silu(p).astype(BF16)

    f_mid = 0.5 * (1.0 + lb)
    f_amp = 0.5 * (1.0 - lb)

    def store_f(p):
        f = f_mid + f_amp * jnp.tanh(0.5 * p)
        g2 = jnp.log2(f)
        hi_bits = lax.bitcast_convert_type(g2, jnp.uint32) & jnp.uint32(0xFFFF0000)
        g2_hi = lax.bitcast_convert_type(hi_bits, F32)
        g2_lo = (g2 - g2_hi).astype(BF16)
        g2_hi = g2_hi.astype(BF16)
        for h in range(HG_HEADS):
            g_ref[:, 2 * h * HG_DK:(2 * h + 1) * HG_DK] = g2_hi[:, h * HG_DK:(h + 1) * HG_DK]
            g_ref[:, (2 * h + 1) * HG_DK:(2 * h + 2) * HG_DK] = g2_lo[:, h * HG_DK:(h + 1) * HG_DK]
        kk_ref[...] = (1.0 - f).astype(BF16)

    def store_i(p):
        vi_ref[...] = p.astype(BF16)

    def store_z(dst_ref, p):
        dst_ref[...] = silu(p).astype(BF16)

    def store_rope(dst_ref, scale, p):
        for c in range(SECTION // LANES):
            xc = p[:, c * LANES:(c + 1) * LANES]
            r = (xc * cos_ref[...] + pltpu.roll(xc, LANES - 32, 1) * sina_ref[...]
                 + pltpu.roll(xc, 32, 1) * sinb_ref[...])
            dst_ref[:, c * LANES:(c + 1) * LANES] = r * scale

    def store_v(p):
        av_ref[...] = p

    plan = [(0, store_q), (1, store_f), (3, functools.partial(store_z, zs_ref)),
            (7, functools.partial(store_z, az_ref)),
            (4, functools.partial(store_rope, aq_ref, AT_HEAD_DIM ** -0.5 * LOG2E)),
            (5, functools.partial(store_rope, ak_ref, 1.0)), (2, store_i), (6, store_v)]
    pending = None
    for j, finish in plan:
        p = section(j)
        if pending is not None:
            pending()
        pending = functools.partial(finish, p)
    pending()


@functools.lru_cache(maxsize=None)
def _rope_tables(seq):
    half = AT_HEAD_DIM // 2
    inv_freq = 1.0 / (ROPE_THETA ** (np.arange(half, dtype=np.float64) / half))
    ang = np.arange(seq, dtype=np.float64)[:, None] * inv_freq[None, :]
    cos = np.cos(ang)
    sin = np.sin(ang)
    zero = np.zeros_like(sin)
    reps = LANES // AT_HEAD_DIM
    cos_t = np.tile(np.concatenate([cos, cos], axis=1), (1, reps))
    sina_t = np.tile(np.concatenate([-sin, zero], axis=1), (1, reps))
    sinb_t = np.tile(np.concatenate([zero, sin], axis=1), (1, reps))
    return tuple(np.asarray(t, np.float32) for t in (cos_t, sina_t, sinb_t))


def _in_projection(x2, norm_w, w_in_bf, lb_logits, seq):
    rows, d_model = x2.shape
    in_cols = w_in_bf.shape[1]
    assert in_cols == 8 * SECTION and rows % PROJ_ROWS == 0 and seq % PROJ_ROWS == 0
    cos_t, sina_t, sinb_t = _rope_tables(seq)
    seq_tiles = seq // PROJ_ROWS
    row_spec = lambda w: pl.BlockSpec((PROJ_ROWS, w), lambda i: (i, 0))
    tab_spec = pl.BlockSpec((PROJ_ROWS, LANES), lambda i: (i % seq_tiles, 0))
    full = lambda a: pl.BlockSpec(a.shape, lambda i: (0, 0))
    out_dtypes = (BF16, BF16, BF16, BF16, BF16, F32, F32, F32, BF16)
    out_widths = (SECTION, 2 * SECTION) + (SECTION,) * 7
    return pl.pallas_call(
        _proj_kernel,
        grid=(rows // PROJ_ROWS,),
        in_specs=[row_spec(d_model), full(norm_w), full(w_in_bf), full(lb_logits),
                  tab_spec, tab_spec, tab_spec],
        out_specs=[row_spec(w) for w in out_widths],
        out_shape=[jax.ShapeDtypeStruct((rows, w), dt)
                   for w, dt in zip(out_widths, out_dtypes)],
        compiler_params=pltpu.CompilerParams(
            dimension_semantics=("parallel",), vmem_limit_bytes=VMEM_LIMIT),
        name="in_projection",
    )(x2, norm_w, w_in_bf, lb_logits, cos_t, sina_t, sinb_t)


HG_LEVELS = (8, 16, 32)


def _hgrn_constants():
    c = HG_CHUNK
    t = np.arange(c)
    tri = (t[:, None] >= t[None, :]).astype(np.float32)
    first = (t // 8) * 8
    half = np.concatenate([tri, tri - 0.5 * (tri[first] + tri[first + 7])], axis=0)
    coef = np.concatenate([half, half], axis=1)
    ts, ss = t[:, None], t[None, :]
    level = np.full((c, c), len(HG_LEVELS) + 1, np.int32)
    level[(ts // 8 == ss // 8) & (ts >= ss)] = 0
    for i, lb in enumerate(HG_LEVELS):
        m = ((ts // (2 * lb) == ss // (2 * lb)) & ((ts // lb) % 2 == 1)
             & ((ss // lb) % 2 == 0))
        level[m] = i + 1
    return jnp.asarray(coef, BF16), jnp.asarray(level)


def _dot_nt(a, b):
    return lax.dot_general(a, b, (((1,), (1,)), ((), ())), preferred_element_type=F32)


def _dot_tn(a, b):
    return lax.dot_general(a, b, (((0,), (0,)), ((), ())), preferred_element_type=F32)


def _neg_abs(x):
    bits = lax.bitcast_convert_type(x, jnp.uint32) | jnp.uint32(0x80000000)
    return lax.bitcast_convert_type(bits, F32)


def _hgrn_out_kernel(coef_ref, level_ref, qs_ref, g_ref, kk_ref, vi_ref, zs_ref, gn_ref,
                     x_ref, oat_ref, w_ref, fw_ref, o_ref,
                     state_ref, qd_ref, oi_ref, kv_ref, dec_ref, ohg_ref):
    @pl.when(pl.program_id(1) == 0)
    def _():
        state_ref[...] = jnp.zeros_like(state_ref)

    c = HG_CHUNK
    n_chunks = HG_ROWS // c
    half = HG_HEADS * HG_DK
    coef = coef_ref[...]
    level = level_ref[...]
    units = [(h, ci) for h in range(HG_HEADS) for ci in range(n_chunks)]

    def rows_of(ci):
        return pl.ds(ci * c, c)

    def cols_of(h, width=HG_DK):
        return pl.ds(h * width, width)

    def cum_decay(h, ci):
        g2 = g_ref[0, rows_of(ci), cols_of(h, 2 * HG_DK)]
        return jnp.dot(coef, jnp.concatenate([g2[:, :HG_DK], g2[:, HG_DK:]], axis=0),
                       preferred_element_type=F32)

    def level_operands(h, ci, be):
        q = qs_ref[0, rows_of(ci), cols_of(h)]
        k = kk_ref[0, rows_of(ci), cols_of(h)]
        scaled = lambda t, w: t * w.astype(BF16)
        b = be[0:c]
        e_diag = be[c:2 * c]
        ops = [(scaled(q, jnp.exp2(e_diag)), scaled(k, jnp.exp2(-e_diag)))]
        for lb in HG_LEVELS:
            ref = jnp.concatenate(
                [jnp.broadcast_to(b[p + lb - 1:p + lb, :], (2 * lb, HG_DK))
                 for p in range(0, c, 2 * lb)], axis=0)
            w = jnp.exp2(_neg_abs(b - ref))
            ops.append((scaled(q, w), scaled(k, w)))
        b_last = b[c - 1:c, :]
        return ops, scaled(k, jnp.exp2(b_last - b)), scaled(q, jnp.exp2(b)), jnp.exp2(b_last)

    def intra_scores(ops):
        scores = jnp.where(level == 0, _dot_nt(*ops[0]), 0.0)
        for i in range(len(HG_LEVELS)):
            scores = jnp.where(level == i + 1, _dot_nt(*ops[i + 1]), scores)
        return scores.astype(BF16)

    def elementwise_stages(group):
        stage1 = [cum_decay(h, ci) for h, ci in group]
        return [level_operands(h, ci, be) for (h, ci), be in zip(group, stage1)]

    def product_stages(group, stage2):
        stage3 = [intra_scores(ops) for ops, _, _, _ in stage2]
        for (h, ci), (_, k_dec, q_dec, decay), scores in zip(group, stage2, stage3):
            v = vi_ref[0, rows_of(ci), cols_of(h)]
            oi_ref[rows_of(ci), cols_of(h)] = jnp.dot(scores, v, preferred_element_type=F32)
            kv_ref[h, ci] = _dot_tn(v, k_dec)
            qd_ref[rows_of(ci), cols_of(h)] = q_dec
            dec_ref[h, ci] = jnp.broadcast_to(decay, dec_ref.shape[2:])

    groups = [units[i:i + HG_GROUP] for i in range(0, len(units), HG_GROUP)]
    ready = elementwise_stages(groups[0])
    o_ref[0] = x_ref[0] + jnp.dot(oat_ref[0], w_ref[half:2 * half, :],
                                  preferred_element_type=F32)
    for gi, group in enumerate(groups):
        upcoming = elementwise_stages(groups[gi + 1]) if gi + 1 < len(groups) else None
        product_stages(group, ready)
        ready = upcoming

    states = [state_ref[h] for h in range(HG_HEADS)]
    for ci in range(n_chunks):
        for h in range(HG_HEADS):
            rows, cols = rows_of(ci), cols_of(h)
            o = oi_ref[rows, cols] + _dot_nt(qd_ref[rows, cols], states[h].astype(BF16))
            states[h] = states[h] * dec_ref[h, ci, 0:1, :] + kv_ref[h, ci]
            z = zs_ref[0, rows, cols].astype(F32)
            ms = jnp.mean(o * o, axis=-1, keepdims=True)
            ohg_ref[rows, cols] = (o * lax.rsqrt(ms + NORM_EPS) * gn_ref[:, cols] * z
                                   ).astype(BF16)
    for h in range(HG_HEADS):
        state_ref[h] = states[h]

    hres = o_ref[0] + jnp.dot(ohg_ref[...], w_ref[0:half, :], preferred_element_type=F32)
    ms = jnp.mean(hres * hres, axis=-1, keepdims=True)
    o_ref[0] = hres * lax.rsqrt(ms + NORM_EPS) * fw_ref[...]


def _hgrn_out(qs, g2, kk, vi, zs, hg_norm_w, x, oat, w_out_bf, final_norm_w):
    b, s, w = qs.shape
    d_model = x.shape[-1]
    assert w == HG_HEADS * HG_DK and s % HG_ROWS == 0 and g2.shape[-1] == 2 * w
    assert w_out_bf.shape == (w + oat.shape[-1], d_model)
    coef, level = _hgrn_constants()
    n_chunks = HG_ROWS // HG_CHUNK
    tile = lambda width: pl.BlockSpec((1, HG_ROWS, width), lambda bi, t: (bi, t, 0))
    const = lambda a: pl.BlockSpec(a.shape, lambda bi, t: (0,) * a.ndim)
    return pl.pallas_call(
        _hgrn_out_kernel,
        grid=(b, s // HG_ROWS),
        in_specs=[const(coef), const(level), tile(w), tile(2 * w), tile(w), tile(w), tile(w),
                  const(hg_norm_w), tile(d_model), tile(oat.shape[-1]), const(w_out_bf),
                  const(final_norm_w)],
        out_specs=tile(d_model),
        out_shape=jax.ShapeDtypeStruct((b, s, d_model), F32),
        scratch_shapes=[pltpu.VMEM((HG_HEADS, HG_DK, HG_DK), F32),
                        pltpu.VMEM((HG_ROWS, w), BF16),
                        pltpu.VMEM((HG_ROWS, w), F32),
                        pltpu.VMEM((HG_HEADS, n_chunks, HG_DK, HG_DK), F32),
                        pltpu.VMEM((HG_HEADS, n_chunks, 8, HG_DK), F32),
                        pltpu.VMEM((HG_ROWS, w), BF16)],
        compiler_params=pltpu.CompilerParams(
            dimension_semantics=("parallel", "arbitrary"),
            vmem_limit_bytes=VMEM_LIMIT),
        name="hgrn2_out_projection",
    )(coef, level, qs, g2, kk, vi, zs, hg_norm_w, x, oat, w_out_bf, final_norm_w)


def _attn_bias():
    qi = np.arange(ATT_BLOCK)[:, None]
    kj = np.arange(2 * ATT_BLOCK)[None, :]
    dist = ATT_BLOCK + qi - kj
    valid = (dist >= 0) & (dist <= ATT_BLOCK)
    b0 = np.where(valid, 0.0, NEG_BIG)
    b1 = np.where(valid & (kj >= ATT_BLOCK), 0.0, NEG_BIG)
    return jnp.asarray(np.stack([b0, b1]), F32)


def _attn_kernel(bias_ref, aq_ref, ak_ref, av_ref, az_ref, o_ref,
                 khist, vhist, kprm, vprm, qprm, m_ref, l_ref, acc_ref, onat):
    ta = ATT_TILE
    blk = ATT_BLOCK
    nph = ATT_PERM
    reg = ta // nph
    tile = pl.program_id(2)
    cur = tile % 2
    prev = 1 - cur

    @pl.when(tile == 0)
    def _():
        khist[...] = jnp.zeros_like(khist)
        vhist[...] = jnp.zeros_like(vhist)
        kprm[1] = jnp.zeros((nph, reg, LANES), F32)
        vprm[1] = jnp.zeros((nph, reg, LANES), F32)

    for r in range(nph):
        kprm[cur, r] = ak_ref[0, pl.ds(r, reg, stride=nph), :]
        vprm[cur, r] = av_ref[0, pl.ds(r, reg, stride=nph), :]
        qprm[r] = aq_ref[0, pl.ds(r, reg, stride=nph), :]

    first_head = lax.broadcasted_iota(jnp.int32, (blk, LANES), 1) < AT_HEAD_DIM
    ones = jnp.ones((2 * blk, LANES), BF16)
    bias_any = bias_ref[0]
    bias_first = bias_ref[(tile == 0).astype(jnp.int32)]

    def block_stats(q, k, v, bias):
        k = k.astype(BF16)
        v = v.astype(BF16)
        q2 = jnp.concatenate([jnp.where(first_head, q, 0.0),
                              jnp.where(first_head, 0.0, q)], axis=0).astype(BF16)
        s = _dot_nt(q2, k) + jnp.concatenate([bias, bias], axis=0)
        m = jnp.max(s, axis=-1, keepdims=True)
        p = jnp.exp2(s - m).astype(BF16)
        pv = jnp.dot(p, jnp.concatenate([v, ones], axis=1), preferred_element_type=F32)
        return (jnp.where(first_head, m[:blk], m[blk:]),
                jnp.where(first_head, pv[:blk, LANES:], pv[blk:, LANES:]),
                jnp.where(first_head, pv[:blk, :LANES], pv[blk:, :LANES]))

    def store_stats(pi, rows, stats):
        m_ref[pi, rows, :], l_ref[pi, rows, :], acc_ref[pi, rows, :] = stats

    def window(first_ref, first_rows, ref, rows):
        return jnp.concatenate([first_ref[first_rows], ref[rows]], axis=0)

    for j in range(ta // blk):
        q_rows = pl.ds(j * blk, blk)
        if j == 0:
            k = window(khist, (slice(None),) * 2, ak_ref, (0, q_rows))
            v = window(vhist, (slice(None),) * 2, av_ref, (0, q_rows))
        else:
            k_rows = pl.ds((j - 1) * blk, 2 * blk)
            k, v = ak_ref[0, k_rows, :], av_ref[0, k_rows, :]
        store_stats(0, q_rows, block_stats(aq_ref[0, q_rows, :], k, v,
                                           bias_first if j == 0 else bias_any))

    khist[...] = ak_ref[0, ta - blk:ta, :]
    vhist[...] = av_ref[0, ta - blk:ta, :]

    for r in range(nph):
        for j in range(reg // blk):
            q_rows = pl.ds(j * blk, blk)
            if j == 0:
                last = pl.ds(reg - blk, blk)
                k = window(kprm, (prev, r, last), kprm, (cur, r, q_rows))
                v = window(vprm, (prev, r, last), vprm, (cur, r, q_rows))
            else:
                k_rows = pl.ds((j - 1) * blk, 2 * blk)
                k, v = kprm[cur, r, k_rows, :], vprm[cur, r, k_rows, :]
            stats = block_stats(qprm[r, q_rows, :], k, v, bias_first if j == 0 else bias_any)
            store_stats(1, pl.ds(r * reg + j * blk, blk), stats)
        for r16 in range(nph):
            rows = pl.ds(r16, blk, stride=nph)
            k = window(kprm, (prev, r, rows), kprm, (cur, r, rows))
            v = window(vprm, (prev, r, rows), vprm, (cur, r, rows))
            stats = block_stats(qprm[r, rows, :], k, v, bias_first)
            store_stats(2, pl.ds(r * reg + r16, blk, stride=nph), stats)

    def merge(ci, carry):
        pieces = reg // ATT_MERGE_ROWS
        r = ci // pieces
        i0 = (ci % pieces) * ATT_MERGE_ROWS
        nat_rows = pl.ds(r + nph * i0, ATT_MERGE_ROWS, stride=nph)
        prm_rows = pl.ds(pl.multiple_of(ci * ATT_MERGE_ROWS, ATT_MERGE_ROWS), ATT_MERGE_ROWS)
        rows = (nat_rows, prm_rows, prm_rows)
        ms = [m_ref[pi, rw, :] for pi, rw in enumerate(rows)]
        m_all = functools.reduce(jnp.maximum, ms)
        ws = [jnp.exp2(m - m_all) for m in ms]
        l_all = sum(w * l_ref[pi, rw, :] for (pi, rw), w in zip(enumerate(rows), ws))
        acc = sum(w * acc_ref[pi, rw, :] for (pi, rw), w in zip(enumerate(rows), ws))
        onat[nat_rows, :] = acc / l_all
        return carry

    lax.fori_loop(0, ta // ATT_MERGE_ROWS, merge, 0)

    def gate(ci, carry):
        rows = pl.ds(pl.multiple_of(ci * ATT_MERGE_ROWS, ATT_MERGE_ROWS), ATT_MERGE_ROWS)
        o_ref[0, rows, :] = (onat[rows, :] * az_ref[0, rows, :].astype(F32)).astype(BF16)
        return carry

    lax.fori_loop(0, ta // ATT_MERGE_ROWS, gate, 0)


def _attention(aq, ak, av, az):
    b, s, w = aq.shape
    assert s % ATT_TILE == 0 and w % LANES == 0
    assert DILATIONS == (1, ATT_PERM, ATT_PERM * ATT_PERM)
    bias = _attn_bias()
    reg = ATT_TILE // ATT_PERM
    pair_spec = pl.BlockSpec((1, ATT_TILE, LANES), lambda bi, h, t: (bi, t, h))
    hist = pltpu.VMEM((ATT_BLOCK, LANES), F32)
    prm = pltpu.VMEM((2, ATT_PERM, reg, LANES), F32)
    stat = pltpu.VMEM((len(DILATIONS), ATT_TILE, LANES), F32)
    return pl.pallas_call(
        _attn_kernel,
        grid=(b, w // LANES, s // ATT_TILE),
        in_specs=[pl.BlockSpec(bias.shape, lambda bi, h, t: (0, 0, 0))] + [pair_spec] * 4,
        out_specs=pair_spec,
        out_shape=jax.ShapeDtypeStruct((b, s, w), BF16),
        scratch_shapes=[hist, hist, prm, prm, pltpu.VMEM((ATT_PERM, reg, LANES), F32),
                        stat, stat, stat, pltpu.VMEM((ATT_TILE, LANES), F32)],
        compiler_params=pltpu.CompilerParams(
            dimension_semantics=("parallel", "parallel", "arbitrary"),
            vmem_limit_bytes=VMEM_LIMIT),
        name="dilated_attention",
    )(bias, aq, ak, av, az)


def kernel(x, norm_w, w_in, hgrn_lb_logits, hg_norm_w, w_out, final_norm_w):
    b, s, d_model = x.shape
    assert norm_w.shape[0] == 1 and w_in.shape[0] == 1 and w_out.shape[0] == 1
    qs, g2, kk, vi, zs, aq, ak, av, az = _in_projection(
        x.reshape(b * s, d_model), norm_w, w_in[0].astype(BF16), hgrn_lb_logits, s)
    to3 = lambda a: a.reshape(b, s, a.shape[-1])
    oat = _attention(to3(aq), to3(ak), to3(av), to3(az))
    return _hgrn_out(to3(qs), to3(g2), to3(kk), to3(vi), to3(zs), hg_norm_w, x, oat,
                     w_out[0].astype(BF16), final_norm_w.reshape(1, d_model))
```

```python
import functools

import numpy as np
import jax
import jax.numpy as jnp
from jax import lax
from jax.experimental import pallas as pl
from jax.experimental.pallas import tpu as pltpu

F32 = jnp.float32
BF16 = jnp.bfloat16

NORM_EPS = 1e-6
ROPE_THETA = 10000.0
LANES = 128

HG_HEADS = 4
HG_DK = 128
HG_CHUNK = 64
AT_HEAD_DIM = 64
SECTION = 512
DILATIONS = (1, 4, 16)
ATT_BLOCK = 128
ATT_TILE = ATT_BLOCK * max(DILATIONS)
NEG_BIG = -1e30

PROJ_ROWS = 1024
HG_ROWS = 512
ATT_PERM = 4
ATT_MERGE_ROWS = 256
LOG2E = 1.4426950408889634
VMEM_LIMIT = 56 * 1024 * 1024


def _normalized(x, gain):
    ms = jnp.mean(x * x, axis=-1, keepdims=True)
    return (x * lax.rsqrt(ms + NORM_EPS) * gain).astype(BF16)


def _silu(p):
    return p * (0.5 * jnp.tanh(0.5 * p) + 0.5)


def _project_sections(u, w_ref, plan):
    pending = None
    for j, finish in plan:
        p = jnp.dot(u, w_ref[:, j * SECTION:(j + 1) * SECTION], preferred_element_type=F32)
        if pending is not None:
            pending()
        pending = functools.partial(finish, p)
    pending()


def _dot_nt(a, b):
    return lax.dot_general(a, b, (((1,), (1,)), ((), ())), preferred_element_type=F32)


def _dot_tn(a, b):
    return lax.dot_general(a, b, (((0,), (0,)), ((), ())), preferred_element_type=F32)


def _neg_abs(x):
    bits = lax.bitcast_convert_type(x, jnp.uint32) | jnp.uint32(0x80000000)
    return lax.bitcast_convert_type(bits, F32)


def _attn_proj_kernel(x_ref, nw_ref, w_ref, cos_ref, sina_ref, sinb_ref,
                      aq_ref, ak_ref, av_ref, az_ref):
    u = _normalized(x_ref[...], nw_ref[...])

    def store_rope(dst_ref, scale, p):
        for c in range(SECTION // LANES):
            xc = p[:, c * LANES:(c + 1) * LANES]
            r = (xc * cos_ref[...] + pltpu.roll(xc, LANES - 32, 1) * sina_ref[...]
                 + pltpu.roll(xc, 32, 1) * sinb_ref[...])
            dst_ref[:, c * LANES:(c + 1) * LANES] = r * scale

    def store_z(p):
        az_ref[...] = _silu(p).astype(BF16)

    def store_v(p):
        av_ref[...] = p

    _project_sections(u, w_ref, [
        (3, store_z),
        (0, functools.partial(store_rope, aq_ref, AT_HEAD_DIM ** -0.5 * LOG2E)),
        (1, functools.partial(store_rope, ak_ref, 1.0)),
        (2, store_v)])


@functools.lru_cache(maxsize=None)
def _rope_tables(seq):
    half = AT_HEAD_DIM // 2
    inv_freq = 1.0 / (ROPE_THETA ** (np.arange(half, dtype=np.float64) / half))
    ang = np.arange(seq, dtype=np.float64)[:, None] * inv_freq[None, :]
    cos = np.cos(ang)
    sin = np.sin(ang)
    zero = np.zeros_like(sin)
    reps = LANES // AT_HEAD_DIM
    cos_t = np.tile(np.concatenate([cos, cos], axis=1), (1, reps))
    sina_t = np.tile(np.concatenate([-sin, zero], axis=1), (1, reps))
    sinb_t = np.tile(np.concatenate([zero, sin], axis=1), (1, reps))
    return tuple(np.asarray(t, np.float32) for t in (cos_t, sina_t, sinb_t))


def _attn_projection(x2, norm_w, w_at_bf, seq):
    rows, d_model = x2.shape
    assert w_at_bf.shape == (d_model, 4 * SECTION)
    assert rows % PROJ_ROWS == 0 and seq % PROJ_ROWS == 0
    tables = _rope_tables(seq)
    seq_tiles = seq // PROJ_ROWS
    row_spec = lambda w: pl.BlockSpec((PROJ_ROWS, w), lambda i: (i, 0))
    tab_spec = pl.BlockSpec((PROJ_ROWS, LANES), lambda i: (i % seq_tiles, 0))
    full = lambda a: pl.BlockSpec(a.shape, lambda i: (0, 0))
    out_dtypes = (F32, F32, F32, BF16)
    return pl.pallas_call(
        _attn_proj_kernel,
        grid=(rows // PROJ_ROWS,),
        in_specs=[row_spec(d_model), full(norm_w), full(w_at_bf)] + [tab_spec] * 3,
        out_specs=[row_spec(SECTION)] * len(out_dtypes),
        out_shape=[jax.ShapeDtypeStruct((rows, SECTION), dt) for dt in out_dtypes],
        compiler_params=pltpu.CompilerParams(
            dimension_semantics=("parallel",), vmem_limit_bytes=VMEM_LIMIT),
        name="attention_projection",
    )(x2, norm_w, w_at_bf, *tables)


def _attn_bias():
    qi = np.arange(ATT_BLOCK)[:, None]
    kj = np.arange(2 * ATT_BLOCK)[None, :]
    dist = ATT_BLOCK + qi - kj
    valid = (dist >= 0) & (dist <= ATT_BLOCK)
    b0 = np.where(valid, 0.0, NEG_BIG)
    b1 = np.where(valid & (kj >= ATT_BLOCK), 0.0, NEG_BIG)
    return jnp.asarray(np.stack([b0, b1]), F32)


def _attn_kernel(bias_ref, aq_ref, ak_ref, av_ref, az_ref, o_ref,
                 khist, vhist, kprm, vprm, qprm, m_ref, l_ref, acc_ref, onat):
    ta = ATT_TILE
    blk = ATT_BLOCK
    nph = ATT_PERM
    reg = ta // nph
    tile = pl.program_id(2)
    cur = tile % 2
    prev = 1 - cur

    @pl.when(tile == 0)
    def _():
        khist[...] = jnp.zeros_like(khist)
        vhist[...] = jnp.zeros_like(vhist)
        kprm[1] = jnp.zeros((nph, reg, LANES), F32)
        vprm[1] = jnp.zeros((nph, reg, LANES), F32)

    for r in range(nph):
        kprm[cur, r] = ak_ref[0, pl.ds(r, reg, stride=nph), :]
        vprm[cur, r] = av_ref[0, pl.ds(r, reg, stride=nph), :]
        qprm[r] = aq_ref[0, pl.ds(r, reg, stride=nph), :]

    first_head = lax.broadcasted_iota(jnp.int32, (blk, LANES), 1) < AT_HEAD_DIM
    ones = jnp.ones((2 * blk, LANES), BF16)
    bias_any = bias_ref[0]
    bias_first = bias_ref[(tile == 0).astype(jnp.int32)]

    def block_stats(q, k, v, bias):
        k = k.astype(BF16)
        v = v.astype(BF16)
        q2 = jnp.concatenate([jnp.where(first_head, q, 0.0),
                              jnp.where(first_head, 0.0, q)], axis=0).astype(BF16)
        s = _dot_nt(q2, k) + jnp.concatenate([bias, bias], axis=0)
        m = jnp.max(s, axis=-1, keepdims=True)
        p = jnp.exp2(s - m).astype(BF16)
        pv = jnp.dot(p, jnp.concatenate([v, ones], axis=1), preferred_element_type=F32)
        return (jnp.where(first_head, m[:blk], m[blk:]),
                jnp.where(first_head, pv[:blk, LANES:], pv[blk:, LANES:]),
                jnp.where(first_head, pv[:blk, :LANES], pv[blk:, :LANES]))

    def store_stats(pi, rows, stats):
        m_ref[pi, rows, :], l_ref[pi, rows, :], acc_ref[pi, rows, :] = stats

    def window(first_ref, first_rows, ref, rows):
        return jnp.concatenate([first_ref[first_rows], ref[rows]], axis=0)

    for j in range(ta // blk):
        q_rows = pl.ds(j * blk, blk)
        if j == 0:
            k = window(khist, (slice(None),) * 2, ak_ref, (0, q_rows))
            v = window(vhist, (slice(None),) * 2, av_ref, (0, q_rows))
        else:
            k_rows = pl.ds((j - 1) * blk, 2 * blk)
            k, v = ak_ref[0, k_rows, :], av_ref[0, k_rows, :]
        store_stats(0, q_rows, block_stats(aq_ref[0, q_rows, :], k, v,
                                           bias_first if j == 0 else bias_any))

    khist[...] = ak_ref[0, ta - blk:ta, :]
    vhist[...] = av_ref[0, ta - blk:ta, :]

    for r in range(nph):
        for j in range(reg // blk):
            q_rows = pl.ds(j * blk, blk)
            if j == 0:
                last = pl.ds(reg - blk, blk)
                k = window(kprm, (prev, r, last), kprm, (cur, r, q_rows))
                v = window(vprm, (prev, r, last), vprm, (cur, r, q_rows))
            else:
                k_rows = pl.ds((j - 1) * blk, 2 * blk)
                k, v = kprm[cur, r, k_rows, :], vprm[cur, r, k_rows, :]
            stats = block_stats(qprm[r, q_rows, :], k, v, bias_first if j == 0 else bias_any)
            store_stats(1, pl.ds(r * reg + j * blk, blk), stats)
        for r16 in range(nph):
            rows = pl.ds(r16, blk, stride=nph)
            k = window(kprm, (prev, r, rows), kprm, (cur, r, rows))
            v = window(vprm, (prev, r, rows), vprm, (cur, r, rows))
            stats = block_stats(qprm[r, rows, :], k, v, bias_first)
            store_stats(2, pl.ds(r * reg + r16, blk, stride=nph), stats)

    def merge(ci, carry):
        pieces = reg // ATT_MERGE_ROWS
        r = ci // pieces
        i0 = (ci % pieces) * ATT_MERGE_ROWS
        nat_rows = pl.ds(r + nph * i0, ATT_MERGE_ROWS, stride=nph)
        prm_rows = pl.ds(pl.multiple_of(ci * ATT_MERGE_ROWS, ATT_MERGE_ROWS), ATT_MERGE_ROWS)
        rows = (nat_rows, prm_rows, prm_rows)
        ms = [m_ref[pi, rw, :] for pi, rw in enumerate(rows)]
        m_all = functools.reduce(jnp.maximum, ms)
        ws = [jnp.exp2(m - m_all) for m in ms]
        l_all = sum(w * l_ref[pi, rw, :] for (pi, rw), w in zip(enumerate(rows), ws))
        acc = sum(w * acc_ref[pi, rw, :] for (pi, rw), w in zip(enumerate(rows), ws))
        onat[nat_rows, :] = acc / l_all
        return carry

    lax.fori_loop(0, ta // ATT_MERGE_ROWS, merge, 0)

    def gate(ci, carry):
        rows = pl.ds(pl.multiple_of(ci * ATT_MERGE_ROWS, ATT_MERGE_ROWS), ATT_MERGE_ROWS)
        o_ref[0, rows, :] = (onat[rows, :] * az_ref[0, rows, :].astype(F32)).astype(BF16)
        return carry

    lax.fori_loop(0, ta // ATT_MERGE_ROWS, gate, 0)


def _attention(aq, ak, av, az):
    b, s, w = aq.shape
    assert s % ATT_TILE == 0 and w % LANES == 0
    assert DILATIONS == (1, ATT_PERM, ATT_PERM * ATT_PERM)
    bias = _attn_bias()
    reg = ATT_TILE // ATT_PERM
    pair_spec = pl.BlockSpec((1, ATT_TILE, LANES), lambda bi, h, t: (bi, t, h))
    hist = pltpu.VMEM((ATT_BLOCK, LANES), F32)
    prm = pltpu.VMEM((2, ATT_PERM, reg, LANES), F32)
    stat = pltpu.VMEM((len(DILATIONS), ATT_TILE, LANES), F32)
    return pl.pallas_call(
        _attn_kernel,
        grid=(b, w // LANES, s // ATT_TILE),
        in_specs=[pl.BlockSpec(bias.shape, lambda bi, h, t: (0, 0, 0))] + [pair_spec] * 4,
        out_specs=pair_spec,
        out_shape=jax.ShapeDtypeStruct((b, s, w), BF16),
        scratch_shapes=[hist, hist, prm, prm, pltpu.VMEM((ATT_PERM, reg, LANES), F32),
                        stat, stat, stat, pltpu.VMEM((ATT_TILE, LANES), F32)],
        compiler_params=pltpu.CompilerParams(
            dimension_semantics=("parallel", "parallel", "arbitrary"),
            vmem_limit_bytes=VMEM_LIMIT),
        name="dilated_attention",
    )(bias, aq, ak, av, az)


HG_LEVELS = (8, 16, 32)


def _hgrn_constants():
    c = HG_CHUNK
    t = np.arange(c)
    tri = (t[:, None] >= t[None, :]).astype(np.float32)
    first = (t // 8) * 8
    half = np.concatenate([tri, tri - 0.5 * (tri[first] + tri[first + 7])], axis=0)
    coef = np.concatenate([half, half], axis=1)
    ts, ss = t[:, None], t[None, :]
    level = np.full((c, c), len(HG_LEVELS) + 1, np.int32)
    level[(ts // 8 == ss // 8) & (ts >= ss)] = 0
    for i, lb in enumerate(HG_LEVELS):
        m = ((ts // (2 * lb) == ss // (2 * lb)) & ((ts // lb) % 2 == 1)
             & ((ss // lb) % 2 == 0))
        level[m] = i + 1
    return jnp.asarray(coef, BF16), jnp.asarray(level)


def _hgrn_out_kernel(coef_ref, level_ref, x_ref, nw_ref, w_in_ref, lbl_ref, gn_ref,
                     oat_ref, w_ref, fw_ref, o_ref,
                     state_ref, qs_ref, g_ref, kk_ref, vi_ref, zs_ref,
                     qd_ref, oi_ref, kv_ref, dec_ref, ohg_ref):
    @pl.when(pl.program_id(1) == 0)
    def _():
        state_ref[...] = jnp.zeros_like(state_ref)

    lg = lbl_ref[...]
    ex = jnp.exp(lg - jnp.max(lg, axis=0, keepdims=True))
    lb = jnp.clip(ex[0:1, :] / jnp.sum(ex, axis=0, keepdims=True), 1e-6, 1.0 - 1e-6)
    f_mid = 0.5 * (1.0 + lb)
    f_amp = 0.5 * (1.0 - lb)

    def store_q(p):
        qs_ref[...] = _silu(p).astype(BF16)

    def store_f(p):
        f = f_mid + f_amp * jnp.tanh(0.5 * p)
        g2 = jnp.log2(f)
        hi_bits = lax.bitcast_convert_type(g2, jnp.uint32) & jnp.uint32(0xFFFF0000)
        g2_hi = lax.bitcast_convert_type(hi_bits, F32)
        g2_lo = (g2 - g2_hi).astype(BF16)
        g2_hi = g2_hi.astype(BF16)
        for h in range(HG_HEADS):
            g_ref[:, 2 * h * HG_DK:(2 * h + 1) * HG_DK] = g2_hi[:, h * HG_DK:(h + 1) * HG_DK]
            g_ref[:, (2 * h + 1) * HG_DK:(2 * h + 2) * HG_DK] = g2_lo[:, h * HG_DK:(h + 1) * HG_DK]
        kk_ref[...] = (1.0 - f).astype(BF16)

    def store_z(p):
        zs_ref[...] = _silu(p).astype(BF16)

    def store_i(p):
        vi_ref[...] = p.astype(BF16)

    _project_sections(_normalized(x_ref[0], nw_ref[...]), w_in_ref,
                      [(0, store_q), (1, store_f), (3, store_z), (2, store_i)])

    c = HG_CHUNK
    n_chunks = HG_ROWS // c
    half = HG_HEADS * HG_DK
    coef = coef_ref[...]
    level = level_ref[...]
    units = [(h, ci) for h in range(HG_HEADS) for ci in range(n_chunks)]

    def rows_of(ci):
        return pl.ds(ci * c, c)

    def cols_of(h, width=HG_DK):
        return pl.ds(h * width, width)

    def cum_decay(h, ci):
        g2 = g_ref[rows_of(ci), cols_of(h, 2 * HG_DK)]
        return jnp.dot(coef, jnp.concatenate([g2[:, :HG_DK], g2[:, HG_DK:]], axis=0),
                       preferred_element_type=F32)

    def level_operands(h, ci, be):
        q = qs_ref[rows_of(ci), cols_of(h)]
        k = kk_ref[rows_of(ci), cols_of(h)]
        scaled = lambda t, w: t * w.astype(BF16)
        b = be[0:c]
        e_diag = be[c:2 * c]
        ops = [(scaled(q, jnp.exp2(e_diag)), scaled(k, jnp.exp2(-e_diag)))]
        for lb_rows in HG_LEVELS:
            ref = jnp.concatenate(
                [jnp.broadcast_to(b[p + lb_rows - 1:p + lb_rows, :], (2 * lb_rows, HG_DK))
                 for p in range(0, c, 2 * lb_rows)], axis=0)
            w = jnp.exp2(_neg_abs(b - ref))
            ops.append((scaled(q, w), scaled(k, w)))
        b_last = b[c - 1:c, :]
        return ops, scaled(k, jnp.exp2(b_last - b)), scaled(q, jnp.exp2(b)), jnp.exp2(b_last)

    def intra_scores(ops):
        scores = jnp.where(level == 0, _dot_nt(*ops[0]), 0.0)
        for i in range(len(HG_LEVELS)):
            scores = jnp.where(level == i + 1, _dot_nt(*ops[i + 1]), scores)
        return scores.astype(BF16)

    stage1 = [cum_decay(h, ci) for h, ci in units]
    stage2 = [level_operands(h, ci, be) for (h, ci), be in zip(units, stage1)]

    o_ref[0] = x_ref[0] + jnp.dot(oat_ref[0], w_ref[half:2 * half, :],
                                  preferred_element_type=F32)

    stage3 = [intra_scores(ops) for ops, _, _, _ in stage2]
    for (h, ci), (_, k_dec, q_dec, decay), scores in zip(units, stage2, stage3):
        v = vi_ref[rows_of(ci), cols_of(h)]
        oi_ref[rows_of(ci), cols_of(h)] = jnp.dot(scores, v, preferred_element_type=F32)
        kv_ref[h, ci] = _dot_tn(v, k_dec)
        qd_ref[rows_of(ci), cols_of(h)] = q_dec
        dec_ref[h, ci] = jnp.broadcast_to(decay, dec_ref.shape[2:])

    states = [state_ref[h] for h in range(HG_HEADS)]
    for ci in range(n_chunks):
        for h in range(HG_HEADS):
            rows, cols = rows_of(ci), cols_of(h)
            o = oi_ref[rows, cols] + _dot_nt(qd_ref[rows, cols], states[h].astype(BF16))
            states[h] = states[h] * dec_ref[h, ci, 0:1, :] + kv_ref[h, ci]
            z = zs_ref[rows, cols].astype(F32)
            ms = jnp.mean(o * o, axis=-1, keepdims=True)
            ohg_ref[rows, cols] = (o * lax.rsqrt(ms + NORM_EPS) * gn_ref[:, cols] * z
                                   ).astype(BF16)
    for h in range(HG_HEADS):
        state_ref[h] = states[h]

    hres = o_ref[0] + jnp.dot(ohg_ref[...], w_ref[0:half, :], preferred_element_type=F32)
    ms = jnp.mean(hres * hres, axis=-1, keepdims=True)
    o_ref[0] = hres * lax.rsqrt(ms + NORM_EPS) * fw_ref[...]


def _hgrn_out(x, norm_w, w_hg_bf, lb_logits, hg_norm_w, oat, w_out_bf, final_norm_w):
    b, s, d_model = x.shape
    w = HG_HEADS * HG_DK
    assert s % HG_ROWS == 0 and w_hg_bf.shape == (d_model, 4 * SECTION) and w == SECTION
    assert w_out_bf.shape == (w + oat.shape[-1], d_model)
    coef, level = _hgrn_constants()
    n_chunks = HG_ROWS // HG_CHUNK
    tile = lambda width: pl.BlockSpec((1, HG_ROWS, width), lambda bi, t: (bi, t, 0))
    const = lambda a: pl.BlockSpec(a.shape, lambda bi, t: (0,) * a.ndim)
    act = lambda width: pltpu.VMEM((HG_ROWS, width), BF16)
    return pl.pallas_call(
        _hgrn_out_kernel,
        grid=(b, s // HG_ROWS),
        in_specs=[const(coef), const(level), tile(d_model), const(norm_w), const(w_hg_bf),
                  const(lb_logits), const(hg_norm_w), tile(oat.shape[-1]), const(w_out_bf),
                  const(final_norm_w)],
        out_specs=tile(d_model),
        out_shape=jax.ShapeDtypeStruct((b, s, d_model), F32),
        scratch_shapes=[pltpu.VMEM((HG_HEADS, HG_DK, HG_DK), F32),
                        act(w), act(2 * w), act(w), act(w), act(w),
                        act(w),
                        pltpu.VMEM((HG_ROWS, w), F32),
                        pltpu.VMEM((HG_HEADS, n_chunks, HG_DK, HG_DK), F32),
                        pltpu.VMEM((HG_HEADS, n_chunks, 8, HG_DK), F32),
                        act(w)],
        compiler_params=pltpu.CompilerParams(
            dimension_semantics=("parallel", "arbitrary"),
            vmem_limit_bytes=VMEM_LIMIT),
        name="hgrn2_branch_and_output",
    )(coef, level, x, norm_w, w_hg_bf, lb_logits, hg_norm_w, oat, w_out_bf, final_norm_w)


def kernel(x, norm_w, w_in, hgrn_lb_logits, hg_norm_w, w_out, final_norm_w):
    b, s, d_model = x.shape
    assert norm_w.shape[0] == 1 and w_in.shape[0] == 1 and w_out.shape[0] == 1
    n_hg = 4 * SECTION
    w_hg = w_in[0, :, :n_hg].astype(BF16)
    w_at = w_in[0, :, n_hg:].astype(BF16)
    aq, ak, av, az = _attn_projection(x.reshape(b * s, d_model), norm_w, w_at, s)
    to3 = lambda a: a.reshape(b, s, a.shape[-1])
    oat = _attention(to3(aq), to3(ak), to3(av), to3(az))
    return _hgrn_out(x, norm_w, w_hg, hgrn_lb_logits, hg_norm_w, oat,
                     w_out[0].astype(BF16), final_norm_w.reshape(1, d_model))
```

```python
import functools

import numpy as np
import jax
import jax.numpy as jnp
from jax import lax
from jax.experimental import pallas as pl
from jax.experimental.pallas import tpu as pltpu

F32 = jnp.float32
BF16 = jnp.bfloat16

NORM_EPS = 1e-6
ROPE_THETA = 10000.0
LANES = 128

HG_HEADS = 4
HG_DK = 128
HG_CHUNK = 64
AT_HEAD_DIM = 64
SECTION = 512
DILATIONS = (1, 4, 16)
ATT_BLOCK = 128
ATT_TILE = ATT_BLOCK * max(DILATIONS)
NEG_BIG = -1e30

PROJ_ROWS = 1024
HG_ROWS = 512
ATT_PERM = 4
ATT_MERGE_ROWS = 256
LOG2E = 1.4426950408889634
VMEM_LIMIT = 56 * 1024 * 1024


def _normalized(x, gain):
    ms = jnp.mean(x * x, axis=-1, keepdims=True)
    return (x * lax.rsqrt(ms + NORM_EPS) * gain).astype(BF16)


def _silu(p):
    return p * (0.5 * jnp.tanh(0.5 * p) + 0.5)


def _project_sections(u, w_ref, plan):
    pending = None
    for j, finish in plan:
        p = jnp.dot(u, w_ref[:, j * SECTION:(j + 1) * SECTION], preferred_element_type=F32)
        if pending is not None:
            pending()
        pending = functools.partial(finish, p)
    pending()


def _dot_nt(a, b):
    return lax.dot_general(a, b, (((1,), (1,)), ((), ())), preferred_element_type=F32)


def _dot_tn(a, b):
    return lax.dot_general(a, b, (((0,), (0,)), ((), ())), preferred_element_type=F32)


def _neg_abs(x):
    bits = lax.bitcast_convert_type(x, jnp.uint32) | jnp.uint32(0x80000000)
    return lax.bitcast_convert_type(bits, F32)


def _attn_proj_kernel(x_ref, nw_ref, w32_ref, cos_ref, sina_ref, sinb_ref,
                      aq_ref, ak_ref, av_ref, az_ref, w_ref):
    @pl.when(pl.program_id(0) == 0)
    def _():
        w_ref[...] = w32_ref[0].astype(BF16)

    u = _normalized(x_ref[...], nw_ref[...])

    def store_rope(dst_ref, scale, p):
        for c in range(SECTION // LANES):
            xc = p[:, c * LANES:(c + 1) * LANES]
            r = (xc * cos_ref[...] + pltpu.roll(xc, LANES - 32, 1) * sina_ref[...]
                 + pltpu.roll(xc, 32, 1) * sinb_ref[...])
            dst_ref[:, c * LANES:(c + 1) * LANES] = r * scale

    def store_z(p):
        az_ref[...] = _silu(p).astype(BF16)

    def store_v(p):
        av_ref[...] = p

    _project_sections(u, w_ref, [
        (3, store_z),
        (0, functools.partial(store_rope, aq_ref, AT_HEAD_DIM ** -0.5 * LOG2E)),
        (1, functools.partial(store_rope, ak_ref, 1.0)),
        (2, store_v)])


@functools.lru_cache(maxsize=None)
def _rope_tables(seq):
    half = AT_HEAD_DIM // 2
    inv_freq = 1.0 / (ROPE_THETA ** (np.arange(half, dtype=np.float64) / half))
    ang = np.arange(seq, dtype=np.float64)[:, None] * inv_freq[None, :]
    cos = np.cos(ang)
    sin = np.sin(ang)
    zero = np.zeros_like(sin)
    reps = LANES // AT_HEAD_DIM
    cos_t = np.tile(np.concatenate([cos, cos], axis=1), (1, reps))
    sina_t = np.tile(np.concatenate([-sin, zero], axis=1), (1, reps))
    sinb_t = np.tile(np.concatenate([zero, sin], axis=1), (1, reps))
    return tuple(np.asarray(t, np.float32) for t in (cos_t, sina_t, sinb_t))


def _branch_weight_spec(w_in, branch):
    assert w_in.shape[0] == 1 and w_in.shape[2] == 8 * SECTION
    return pl.BlockSpec((1, w_in.shape[1], 4 * SECTION),
                        lambda *_: (0, 0, branch), pipeline_mode=pl.Buffered(1))


def _attn_projection(x2, norm_w, w_in, seq):
    rows, d_model = x2.shape
    assert rows % PROJ_ROWS == 0 and seq % PROJ_ROWS == 0
    tables = _rope_tables(seq)
    seq_tiles = seq // PROJ_ROWS
    row_spec = lambda w: pl.BlockSpec((PROJ_ROWS, w), lambda i: (i, 0))
    tab_spec = pl.BlockSpec((PROJ_ROWS, LANES), lambda i: (i % seq_tiles, 0))
    full = lambda a: pl.BlockSpec(a.shape, lambda i: (0, 0))
    out_dtypes = (F32, F32, F32, BF16)
    return pl.pallas_call(
        _attn_proj_kernel,
        grid=(rows // PROJ_ROWS,),
        in_specs=[row_spec(d_model), full(norm_w), _branch_weight_spec(w_in, 1)]
                 + [tab_spec] * 3,
        out_specs=[row_spec(SECTION)] * len(out_dtypes),
        out_shape=[jax.ShapeDtypeStruct((rows, SECTION), dt) for dt in out_dtypes],
        scratch_shapes=[pltpu.VMEM((d_model, 4 * SECTION), BF16)],
        compiler_params=pltpu.CompilerParams(
            dimension_semantics=("arbitrary",), vmem_limit_bytes=VMEM_LIMIT),
        name="attention_projection",
    )(x2, norm_w, w_in, *tables)


def _attn_bias():
    qi = np.arange(ATT_BLOCK)[:, None]
    kj = np.arange(2 * ATT_BLOCK)[None, :]
    dist = ATT_BLOCK + qi - kj
    valid = (dist >= 0) & (dist <= ATT_BLOCK)
    b0 = np.where(valid, 0.0, NEG_BIG)
    b1 = np.where(valid & (kj >= ATT_BLOCK), 0.0, NEG_BIG)
    return jnp.asarray(np.stack([b0, b1]), F32)


def _attn_kernel(bias_ref, aq_ref, ak_ref, av_ref, az_ref, o_ref,
                 khist, vhist, kprm, vprm, qprm, m_ref, l_ref, acc_ref, onat):
    ta = ATT_TILE
    blk = ATT_BLOCK
    nph = ATT_PERM
    reg = ta // nph
    tile = pl.program_id(2)
    cur = tile % 2
    prev = 1 - cur

    @pl.when(tile == 0)
    def _():
        khist[...] = jnp.zeros_like(khist)
        vhist[...] = jnp.zeros_like(vhist)
        kprm[1] = jnp.zeros((nph, reg, LANES), F32)
        vprm[1] = jnp.zeros((nph, reg, LANES), F32)

    for r in range(nph):
        kprm[cur, r] = ak_ref[0, pl.ds(r, reg, stride=nph), :]
        vprm[cur, r] = av_ref[0, pl.ds(r, reg, stride=nph), :]
        qprm[r] = aq_ref[0, pl.ds(r, reg, stride=nph), :]

    first_head = lax.broadcasted_iota(jnp.int32, (blk, LANES), 1) < AT_HEAD_DIM
    ones = jnp.ones((2 * blk, LANES), BF16)
    bias_any = bias_ref[0]
    bias_first = bias_ref[(tile == 0).astype(jnp.int32)]

    def block_stats(q, k, v, bias):
        k = k.astype(BF16)
        v = v.astype(BF16)
        q2 = jnp.concatenate([jnp.where(first_head, q, 0.0),
                              jnp.where(first_head, 0.0, q)], axis=0).astype(BF16)
        s = _dot_nt(q2, k) + jnp.concatenate([bias, bias], axis=0)
        m = jnp.max(s, axis=-1, keepdims=True)
        p = jnp.exp2(s - m).astype(BF16)
        pv = jnp.dot(p, jnp.concatenate([v, ones], axis=1), preferred_element_type=F32)
        return (jnp.where(first_head, m[:blk], m[blk:]),
                jnp.where(first_head, pv[:blk, LANES:], pv[blk:, LANES:]),
                jnp.where(first_head, pv[:blk, :LANES], pv[blk:, :LANES]))

    def store_stats(pi, rows, stats):
        m_ref[pi, rows, :], l_ref[pi, rows, :], acc_ref[pi, rows, :] = stats

    def window(first_ref, first_rows, ref, rows):
        return jnp.concatenate([first_ref[first_rows], ref[rows]], axis=0)

    for j in range(ta // blk):
        q_rows = pl.ds(j * blk, blk)
        if j == 0:
            k = window(khist, (slice(None),) * 2, ak_ref, (0, q_rows))
            v = window(vhist, (slice(None),) * 2, av_ref, (0, q_rows))
        else:
            k_rows = pl.ds((j - 1) * blk, 2 * blk)
            k, v = ak_ref[0, k_rows, :], av_ref[0, k_rows, :]
        store_stats(0, q_rows, block_stats(aq_ref[0, q_rows, :], k, v,
                                           bias_first if j == 0 else bias_any))

    khist[...] = ak_ref[0, ta - blk:ta, :]
    vhist[...] = av_ref[0, ta - blk:ta, :]

    for r in range(nph):
        for j in range(reg // blk):
            q_rows = pl.ds(j * blk, blk)
            if j == 0:
                last = pl.ds(reg - blk, blk)
                k = window(kprm, (prev, r, last), kprm, (cur, r, q_rows))
                v = window(vprm, (prev, r, last), vprm, (cur, r, q_rows))
            else:
                k_rows = pl.ds((j - 1) * blk, 2 * blk)
                k, v = kprm[cur, r, k_rows, :], vprm[cur, r, k_rows, :]
            stats = block_stats(qprm[r, q_rows, :], k, v, bias_first if j == 0 else bias_any)
            store_stats(1, pl.ds(r * reg + j * blk, blk), stats)
        for r16 in range(nph):
            rows = pl.ds(r16, blk, stride=nph)
            k = window(kprm, (prev, r, rows), kprm, (cur, r, rows))
            v = window(vprm, (prev, r, rows), vprm, (cur, r, rows))
            stats = block_stats(qprm[r, rows, :], k, v, bias_first)
            store_stats(2, pl.ds(r * reg + r16, blk, stride=nph), stats)

    def merge(ci, carry):
        pieces = reg // ATT_MERGE_ROWS
        r = ci // pieces
        i0 = (ci % pieces) * ATT_MERGE_ROWS
        nat_rows = pl.ds(r + nph * i0, ATT_MERGE_ROWS, stride=nph)
        prm_rows = pl.ds(pl.multiple_of(ci * ATT_MERGE_ROWS, ATT_MERGE_ROWS), ATT_MERGE_ROWS)
        rows = (nat_rows, prm_rows, prm_rows)
        ms = [m_ref[pi, rw, :] for pi, rw in enumerate(rows)]
        m_all = functools.reduce(jnp.maximum, ms)
        ws = [jnp.exp2(m - m_all) for m in ms]
        l_all = sum(w * l_ref[pi, rw, :] for (pi, rw), w in zip(enumerate(rows), ws))
        acc = sum(w * acc_ref[pi, rw, :] for (pi, rw), w in zip(enumerate(rows), ws))
        onat[nat_rows, :] = acc / l_all
        return carry

    lax.fori_loop(0, ta // ATT_MERGE_ROWS, merge, 0)

    def gate(ci, carry):
        rows = pl.ds(pl.multiple_of(ci * ATT_MERGE_ROWS, ATT_MERGE_ROWS), ATT_MERGE_ROWS)
        o_ref[0, rows, :] = (onat[rows, :] * az_ref[0, rows, :].astype(F32)).astype(BF16)
        return carry

    lax.fori_loop(0, ta // ATT_MERGE_ROWS, gate, 0)


def _attention(aq, ak, av, az):
    b, s, w = aq.shape
    assert s % ATT_TILE == 0 and w % LANES == 0
    assert DILATIONS == (1, ATT_PERM, ATT_PERM * ATT_PERM)
    bias = _attn_bias()
    reg = ATT_TILE // ATT_PERM
    pair_spec = pl.BlockSpec((1, ATT_TILE, LANES), lambda bi, h, t: (bi, t, h))
    hist = pltpu.VMEM((ATT_BLOCK, LANES), F32)
    prm = pltpu.VMEM((2, ATT_PERM, reg, LANES), F32)
    stat = pltpu.VMEM((len(DILATIONS), ATT_TILE, LANES), F32)
    return pl.pallas_call(
        _attn_kernel,
        grid=(b, w // LANES, s // ATT_TILE),
        in_specs=[pl.BlockSpec(bias.shape, lambda bi, h, t: (0, 0, 0))] + [pair_spec] * 4,
        out_specs=pair_spec,
        out_shape=jax.ShapeDtypeStruct((b, s, w), BF16),
        scratch_shapes=[hist, hist, prm, prm, pltpu.VMEM((ATT_PERM, reg, LANES), F32),
                        stat, stat, stat, pltpu.VMEM((ATT_TILE, LANES), F32)],
        compiler_params=pltpu.CompilerParams(
            dimension_semantics=("parallel", "parallel", "arbitrary"),
            vmem_limit_bytes=VMEM_LIMIT),
        name="dilated_attention",
    )(bias, aq, ak, av, az)


HG_LEVELS = (8, 16, 32)


def _hgrn_constants():
    c = HG_CHUNK
    t = np.arange(c)
    tri = (t[:, None] >= t[None, :]).astype(np.float32)
    first = (t // 8) * 8
    half = np.concatenate([tri, tri - 0.5 * (tri[first] + tri[first + 7])], axis=0)
    coef = np.concatenate([half, half], axis=1)
    ts, ss = t[:, None], t[None, :]
    level = np.full((c, c), len(HG_LEVELS) + 1, np.int32)
    level[(ts // 8 == ss // 8) & (ts >= ss)] = 0
    for i, lb in enumerate(HG_LEVELS):
        m = ((ts // (2 * lb) == ss // (2 * lb)) & ((ts // lb) % 2 == 1)
             & ((ss // lb) % 2 == 0))
        level[m] = i + 1
    return jnp.asarray(coef, BF16), jnp.asarray(level)


def _hgrn_out_kernel(coef_ref, level_ref, x_ref, nw_ref, w_in32_ref, lbl_ref, gn_ref,
                     oat_ref, w32_ref, fw_ref, o_ref,
                     w_in_ref, w_ref, state_ref, qs_ref, g_ref, kk_ref, vi_ref, zs_ref,
                     qd_ref, oi_ref, kv_ref, dec_ref, ohg_ref):
    @pl.when(pl.program_id(1) == 0)
    def _():
        state_ref[...] = jnp.zeros_like(state_ref)
        w_in_ref[...] = w_in32_ref[0].astype(BF16)
        w_ref[...] = w32_ref[0].astype(BF16)

    lg = lbl_ref[...]
    ex = jnp.exp(lg - jnp.max(lg, axis=0, keepdims=True))
    lb = jnp.clip(ex[0:1, :] / jnp.sum(ex, axis=0, keepdims=True), 1e-6, 1.0 - 1e-6)
    f_mid = 0.5 * (1.0 + lb)
    f_amp = 0.5 * (1.0 - lb)

    def store_q(p):
        qs_ref[...] = _silu(p).astype(BF16)

    def store_f(p):
        f = f_mid + f_amp * jnp.tanh(0.5 * p)
        g2 = jnp.log2(f)
        hi_bits = lax.bitcast_convert_type(g2, jnp.uint32) & jnp.uint32(0xFFFF0000)
        g2_hi = lax.bitcast_convert_type(hi_bits, F32)
        g2_lo = (g2 - g2_hi).astype(BF16)
        g2_hi = g2_hi.astype(BF16)
        for h in range(HG_HEADS):
            g_ref[:, 2 * h * HG_DK:(2 * h + 1) * HG_DK] = g2_hi[:, h * HG_DK:(h + 1) * HG_DK]
            g_ref[:, (2 * h + 1) * HG_DK:(2 * h + 2) * HG_DK] = g2_lo[:, h * HG_DK:(h + 1) * HG_DK]
        kk_ref[...] = (1.0 - f).astype(BF16)

    def store_z(p):
        zs_ref[...] = _silu(p).astype(BF16)

    def store_i(p):
        vi_ref[...] = p.astype(BF16)

    _project_sections(_normalized(x_ref[0], nw_ref[...]), w_in_ref,
                      [(0, store_q), (1, store_f), (3, store_z), (2, store_i)])

    c = HG_CHUNK
    n_chunks = HG_ROWS // c
    half = HG_HEADS * HG_DK
    coef = coef_ref[...]
    level = level_ref[...]
    units = [(h, ci) for h in range(HG_HEADS) for ci in range(n_chunks)]

    def rows_of(ci):
        return pl.ds(ci * c, c)

    def cols_of(h, width=HG_DK):
        return pl.ds(h * width, width)

    def cum_decay(h, ci):
        g2 = g_ref[rows_of(ci), cols_of(h, 2 * HG_DK)]
        return jnp.dot(coef, jnp.concatenate([g2[:, :HG_DK], g2[:, HG_DK:]], axis=0),
                       preferred_element_type=F32)

    def level_operands(h, ci, be):
        q = qs_ref[rows_of(ci), cols_of(h)]
        k = kk_ref[rows_of(ci), cols_of(h)]
        scaled = lambda t, w: t * w.astype(BF16)
        b = be[0:c]
        e_diag = be[c:2 * c]
        ops = [(scaled(q, jnp.exp2(e_diag)), scaled(k, jnp.exp2(-e_diag)))]
        for lb_rows in HG_LEVELS:
            ref = jnp.concatenate(
                [jnp.broadcast_to(b[p + lb_rows - 1:p + lb_rows, :], (2 * lb_rows, HG_DK))
                 for p in range(0, c, 2 * lb_rows)], axis=0)
            w = jnp.exp2(_neg_abs(b - ref))
            ops.append((scaled(q, w), scaled(k, w)))
        b_last = b[c - 1:c, :]
        return ops, scaled(k, jnp.exp2(b_last - b)), scaled(q, jnp.exp2(b)), jnp.exp2(b_last)

    def intra_scores(ops):
        scores = jnp.where(level == 0, _dot_nt(*ops[0]), 0.0)
        for i in range(len(HG_LEVELS)):
            scores = jnp.where(level == i + 1, _dot_nt(*ops[i + 1]), scores)
        return scores.astype(BF16)

    stage1 = [cum_decay(h, ci) for h, ci in units]
    stage2 = [level_operands(h, ci, be) for (h, ci), be in zip(units, stage1)]

    o_ref[0] = x_ref[0] + jnp.dot(oat_ref[0], w_ref[half:2 * half, :],
                                  preferred_element_type=F32)

    stage3 = [intra_scores(ops) for ops, _, _, _ in stage2]
    for (h, ci), (_, k_dec, q_dec, decay), scores in zip(units, stage2, stage3):
        v = vi_ref[rows_of(ci), cols_of(h)]
        oi_ref[rows_of(ci), cols_of(h)] = jnp.dot(scores, v, preferred_element_type=F32)
        kv_ref[h, ci] = _dot_tn(v, k_dec)
        qd_ref[rows_of(ci), cols_of(h)] = q_dec
        dec_ref[h, ci] = jnp.broadcast_to(decay, dec_ref.shape[2:])

    states = [state_ref[h] for h in range(HG_HEADS)]
    for ci in range(n_chunks):
        for h in range(HG_HEADS):
            rows, cols = rows_of(ci), cols_of(h)
            o = oi_ref[rows, cols] + _dot_nt(qd_ref[rows, cols], states[h].astype(BF16))
            states[h] = states[h] * dec_ref[h, ci, 0:1, :] + kv_ref[h, ci]
            z = zs_ref[rows, cols].astype(F32)
            ms = jnp.mean(o * o, axis=-1, keepdims=True)
            ohg_ref[rows, cols] = (o * lax.rsqrt(ms + NORM_EPS) * gn_ref[:, cols] * z
                                   ).astype(BF16)
    for h in range(HG_HEADS):
        state_ref[h] = states[h]

    hres = o_ref[0] + jnp.dot(ohg_ref[...], w_ref[0:half, :], preferred_element_type=F32)
    ms = jnp.mean(hres * hres, axis=-1, keepdims=True)
    o_ref[0] = hres * lax.rsqrt(ms + NORM_EPS) * fw_ref[...]


def _hgrn_out(x, norm_w, w_in, lb_logits, hg_norm_w, oat, w_out, final_norm_w):
    b, s, d_model = x.shape
    w = HG_HEADS * HG_DK
    assert s % HG_ROWS == 0 and w == SECTION
    assert w_out.shape == (1, w + oat.shape[-1], d_model)
    coef, level = _hgrn_constants()
    n_chunks = HG_ROWS // HG_CHUNK
    tile = lambda width: pl.BlockSpec((1, HG_ROWS, width), lambda bi, t: (bi, t, 0))
    const = lambda a: pl.BlockSpec(a.shape, lambda bi, t: (0,) * a.ndim)
    act = lambda width: pltpu.VMEM((HG_ROWS, width), BF16)
    return pl.pallas_call(
        _hgrn_out_kernel,
        grid=(b, s // HG_ROWS),
        in_specs=[const(coef), const(level), tile(d_model), const(norm_w),
                  _branch_weight_spec(w_in, 0), const(lb_logits), const(hg_norm_w),
                  tile(oat.shape[-1]), const(w_out), const(final_norm_w)],
        out_specs=tile(d_model),
        out_shape=jax.ShapeDtypeStruct((b, s, d_model), F32),
        scratch_shapes=[pltpu.VMEM((d_model, 4 * SECTION), BF16),
                        pltpu.VMEM(w_out.shape[1:], BF16),
                        pltpu.VMEM((HG_HEADS, HG_DK, HG_DK), F32),
                        act(w), act(2 * w), act(w), act(w), act(w),
                        act(w),
                        pltpu.VMEM((HG_ROWS, w), F32),
                        pltpu.VMEM((HG_HEADS, n_chunks, HG_DK, HG_DK), F32),
                        pltpu.VMEM((HG_HEADS, n_chunks, 8, HG_DK), F32),
                        act(w)],
        compiler_params=pltpu.CompilerParams(
            dimension_semantics=("parallel", "arbitrary"),
            vmem_limit_bytes=VMEM_LIMIT),
        name="hgrn2_branch_and_output",
    )(coef, level, x, norm_w, w_in, lb_logits, hg_norm_w, oat, w_out, final_norm_w)


def kernel(x, norm_w, w_in, hgrn_lb_logits, hg_norm_w, w_out, final_norm_w):
    b, s, d_model = x.shape
    assert norm_w.shape[0] == 1 and w_in.shape[0] == 1 and w_out.shape[0] == 1
    aq, ak, av, az = _attn_projection(x.reshape(b * s, d_model), norm_w, w_in, s)
    to3 = lambda a: a.reshape(b, s, a.shape[-1])
    oat = _attention(to3(aq), to3(ak), to3(av), to3(az))
    return _hgrn_out(x, norm_w, w_in, hgrn_lb_logits, hg_norm_w, oat, w_out,
                     final_norm_w.reshape(1, d_model))
```

```python
import functools

import numpy as np
import jax
import jax.numpy as jnp
from jax import lax
from jax.experimental import pallas as pl
from jax.experimental.pallas import tpu as pltpu

F32 = jnp.float32
BF16 = jnp.bfloat16

NORM_EPS = 1e-6
ROPE_THETA = 10000.0
LANES = 128

HG_HEADS = 4
HG_DK = 128
HG_CHUNK = 64
AT_HEAD_DIM = 64
SECTION = 512
DILATIONS = (1, 4, 16)
ATT_BLOCK = 128
ATT_TILE = ATT_BLOCK * max(DILATIONS)
NEG_BIG = -1e30

PROJ_ROWS = 1024
HG_ROWS = 512
ATT_PERM = 4
ATT_MERGE_ROWS = 512
LOG2E = 1.4426950408889634
VMEM_LIMIT = 56 * 1024 * 1024


def _normalized(x, gain):
    ms = jnp.mean(x * x, axis=-1, keepdims=True)
    return (x * lax.rsqrt(ms + NORM_EPS) * gain).astype(BF16)


def _silu(p):
    return p * (0.5 * jnp.tanh(0.5 * p) + 0.5)


def _project_sections(u, w_ref, plan):
    pending = None
    for j, finish in plan:
        p = jnp.dot(u, w_ref[:, j * SECTION:(j + 1) * SECTION], preferred_element_type=F32)
        if pending is not None:
            pending()
        pending = functools.partial(finish, p)
    pending()


def _dot_nt(a, b):
    return lax.dot_general(a, b, (((1,), (1,)), ((), ())), preferred_element_type=F32)


def _dot_tn(a, b):
    return lax.dot_general(a, b, (((0,), (0,)), ((), ())), preferred_element_type=F32)


def _neg_abs(x):
    bits = lax.bitcast_convert_type(x, jnp.uint32) | jnp.uint32(0x80000000)
    return lax.bitcast_convert_type(bits, F32)


def _attn_proj_kernel(x_ref, nw_ref, w32_ref, cos_ref, sina_ref, sinb_ref,
                      aq_ref, ak_ref, av_ref, az_ref, w_ref):
    @pl.when(pl.program_id(0) == 0)
    def _():
        w_ref[...] = w32_ref[0].astype(BF16)

    u = _normalized(x_ref[...], nw_ref[...])

    def store_rope(dst_ref, scale, p):
        half = AT_HEAD_DIM // 2
        for c in range(SECTION // LANES):
            xc = p[:, c * LANES:(c + 1) * LANES]
            r = (xc * cos_ref[...] + pltpu.roll(xc, LANES - half, 1) * sina_ref[...]
                 + pltpu.roll(xc, half, 1) * sinb_ref[...])
            dst_ref[:, c * LANES:(c + 1) * LANES] = r * scale

    def store_z(p):
        az_ref[...] = _silu(p).astype(BF16)

    def store_v(p):
        av_ref[...] = p

    _project_sections(u, w_ref, [
        (3, store_z),
        (0, functools.partial(store_rope, aq_ref, AT_HEAD_DIM ** -0.5 * LOG2E)),
        (1, functools.partial(store_rope, ak_ref, 1.0)),
        (2, store_v)])


@functools.lru_cache(maxsize=None)
def _rope_tables(seq):
    half = AT_HEAD_DIM // 2
    inv_freq = 1.0 / (ROPE_THETA ** (np.arange(half, dtype=np.float64) / half))
    ang = np.arange(seq, dtype=np.float64)[:, None] * inv_freq[None, :]
    cos = np.cos(ang)
    sin = np.sin(ang)
    zero = np.zeros_like(sin)
    reps = LANES // AT_HEAD_DIM
    cos_t = np.tile(np.concatenate([cos, cos], axis=1), (1, reps))
    sina_t = np.tile(np.concatenate([-sin, zero], axis=1), (1, reps))
    sinb_t = np.tile(np.concatenate([zero, sin], axis=1), (1, reps))
    return tuple(np.asarray(t, np.float32) for t in (cos_t, sina_t, sinb_t))


def _branch_weight_spec(w_in, branch):
    assert w_in.shape[0] == 1 and w_in.shape[2] == 8 * SECTION
    return pl.BlockSpec((1, w_in.shape[1], 4 * SECTION),
                        lambda *_: (0, 0, branch), pipeline_mode=pl.Buffered(1))


def _attn_projection(x2, norm_w, w_in, seq):
    rows, d_model = x2.shape
    assert rows % PROJ_ROWS == 0 and seq % PROJ_ROWS == 0
    tables = _rope_tables(seq)
    seq_tiles = seq // PROJ_ROWS
    row_spec = lambda w: pl.BlockSpec((PROJ_ROWS, w), lambda i: (i, 0))
    tab_spec = pl.BlockSpec((PROJ_ROWS, LANES), lambda i: (i % seq_tiles, 0))
    full = lambda a: pl.BlockSpec(a.shape, lambda i: (0, 0))
    out_dtypes = (F32, F32, F32, BF16)
    return pl.pallas_call(
        _attn_proj_kernel,
        grid=(rows // PROJ_ROWS,),
        in_specs=[row_spec(d_model), full(norm_w), _branch_weight_spec(w_in, 1)]
                 + [tab_spec] * 3,
        out_specs=[row_spec(SECTION)] * len(out_dtypes),
        out_shape=[jax.ShapeDtypeStruct((rows, SECTION), dt) for dt in out_dtypes],
        scratch_shapes=[pltpu.VMEM((d_model, 4 * SECTION), BF16)],
        compiler_params=pltpu.CompilerParams(
            dimension_semantics=("arbitrary",), vmem_limit_bytes=VMEM_LIMIT),
        name="attention_projection",
    )(x2, norm_w, w_in, *tables)


def _attn_bias():
    qi = np.arange(ATT_BLOCK)[:, None]
    kj = np.arange(2 * ATT_BLOCK)[None, :]
    dist = ATT_BLOCK + qi - kj
    valid = (dist >= 0) & (dist <= ATT_BLOCK)
    b0 = np.where(valid, 0.0, NEG_BIG)
    b1 = np.where(valid & (kj >= ATT_BLOCK), 0.0, NEG_BIG)
    return jnp.asarray(np.stack([b0, b1]), F32)


def _attn_kernel(bias_ref, aq_ref, ak_ref, av_ref, az_ref, o_ref,
                 khist, vhist, kprm, vprm, qprm, m_ref, l_ref, acc_ref, onat):
    ta = ATT_TILE
    blk = ATT_BLOCK
    nph = ATT_PERM
    reg = ta // nph
    tile = pl.program_id(2)
    cur = tile % 2
    prev = 1 - cur

    @pl.when(tile == 0)
    def _():
        khist[...] = jnp.zeros_like(khist)
        vhist[...] = jnp.zeros_like(vhist)
        kprm[1] = jnp.zeros((nph, reg, LANES), F32)
        vprm[1] = jnp.zeros((nph, reg, LANES), F32)

    for r in range(nph):
        kprm[cur, r] = ak_ref[0, pl.ds(r, reg, stride=nph), :]
        vprm[cur, r] = av_ref[0, pl.ds(r, reg, stride=nph), :]
        qprm[r] = aq_ref[0, pl.ds(r, reg, stride=nph), :]

    first_head = lax.broadcasted_iota(jnp.int32, (blk, LANES), 1) < AT_HEAD_DIM
    ones = jnp.ones((2 * blk, LANES), BF16)
    bias_any = bias_ref[0]
    bias_first = bias_ref[(tile == 0).astype(jnp.int32)]

    def block_stats(q, k, v, bias):
        k = k.astype(BF16)
        v = v.astype(BF16)
        q2 = jnp.concatenate([jnp.where(first_head, q, 0.0),
                              jnp.where(first_head, 0.0, q)], axis=0).astype(BF16)
        s = _dot_nt(q2, k) + jnp.concatenate([bias, bias], axis=0)
        m = jnp.max(s, axis=-1, keepdims=True)
        p = jnp.exp2(s - m).astype(BF16)
        pv = jnp.dot(p, jnp.concatenate([v, ones], axis=1), preferred_element_type=F32)
        return (jnp.where(first_head, m[:blk], m[blk:]),
                jnp.where(first_head, pv[:blk, LANES:], pv[blk:, LANES:]),
                jnp.where(first_head, pv[:blk, :LANES], pv[blk:, :LANES]))

    def store_stats(pi, rows, stats):
        m_ref[pi, rows, :], l_ref[pi, rows, :], acc_ref[pi, rows, :] = stats

    def window(first_ref, first_rows, ref, rows):
        return jnp.concatenate([first_ref[first_rows], ref[rows]], axis=0)

    for j in range(ta // blk):
        q_rows = pl.ds(j * blk, blk)
        if j == 0:
            k = window(khist, (slice(None),) * 2, ak_ref, (0, q_rows))
            v = window(vhist, (slice(None),) * 2, av_ref, (0, q_rows))
        else:
            k_rows = pl.ds((j - 1) * blk, 2 * blk)
            k, v = ak_ref[0, k_rows, :], av_ref[0, k_rows, :]
        store_stats(0, q_rows, block_stats(aq_ref[0, q_rows, :], k, v,
                                           bias_first if j == 0 else bias_any))

    khist[...] = ak_ref[0, ta - blk:ta, :]
    vhist[...] = av_ref[0, ta - blk:ta, :]

    for r in range(nph):
        for j in range(reg // blk):
            q_rows = pl.ds(j * blk, blk)
            if j == 0:
                last = pl.ds(reg - blk, blk)
                k = window(kprm, (prev, r, last), kprm, (cur, r, q_rows))
                v = window(vprm, (prev, r, last), vprm, (cur, r, q_rows))
            else:
                k_rows = pl.ds((j - 1) * blk, 2 * blk)
                k, v = kprm[cur, r, k_rows, :], vprm[cur, r, k_rows, :]
            stats = block_stats(qprm[r, q_rows, :], k, v, bias_first if j == 0 else bias_any)
            store_stats(1, pl.ds(r * reg + j * blk, blk), stats)
        for r16 in range(nph):
            rows = pl.ds(r16, blk, stride=nph)
            k = window(kprm, (prev, r, rows), kprm, (cur, r, rows))
            v = window(vprm, (prev, r, rows), vprm, (cur, r, rows))
            stats = block_stats(qprm[r, rows, :], k, v, bias_first)
            store_stats(2, pl.ds(r * reg + r16, blk, stride=nph), stats)

    def merge(ci, carry):
        pieces = reg // ATT_MERGE_ROWS
        r = ci // pieces
        i0 = (ci % pieces) * ATT_MERGE_ROWS
        nat_rows = pl.ds(r + nph * i0, ATT_MERGE_ROWS, stride=nph)
        prm_rows = pl.ds(pl.multiple_of(ci * ATT_MERGE_ROWS, ATT_MERGE_ROWS), ATT_MERGE_ROWS)
        rows = (nat_rows, prm_rows, prm_rows)
        ms = [m_ref[pi, rw, :] for pi, rw in enumerate(rows)]
        m_all = functools.reduce(jnp.maximum, ms)
        ws = [jnp.exp2(m - m_all) for m in ms]
        l_all = sum(w * l_ref[pi, rw, :] for (pi, rw), w in zip(enumerate(rows), ws))
        acc = sum(w * acc_ref[pi, rw, :] for (pi, rw), w in zip(enumerate(rows), ws))
        onat[nat_rows, :] = acc / l_all
        return carry

    lax.fori_loop(0, ta // ATT_MERGE_ROWS, merge, 0)

    def gate(ci, carry):
        rows = pl.ds(pl.multiple_of(ci * ATT_MERGE_ROWS, ATT_MERGE_ROWS), ATT_MERGE_ROWS)
        o_ref[0, rows, :] = (onat[rows, :] * az_ref[0, rows, :].astype(F32)).astype(BF16)
        return carry

    lax.fori_loop(0, ta // ATT_MERGE_ROWS, gate, 0)


def _attention(aq, ak, av, az):
    b, s, w = aq.shape
    assert s % ATT_TILE == 0 and w % LANES == 0
    assert DILATIONS == (1, ATT_PERM, ATT_PERM * ATT_PERM)
    assert (ATT_TILE // ATT_PERM) % ATT_MERGE_ROWS == 0
    bias = _attn_bias()
    reg = ATT_TILE // ATT_PERM
    pair_spec = pl.BlockSpec((1, ATT_TILE, LANES), lambda bi, h, t: (bi, t, h))
    hist = pltpu.VMEM((ATT_BLOCK, LANES), F32)
    prm = pltpu.VMEM((2, ATT_PERM, reg, LANES), F32)
    stat = pltpu.VMEM((len(DILATIONS), ATT_TILE, LANES), F32)
    return pl.pallas_call(
        _attn_kernel,
        grid=(b, w // LANES, s // ATT_TILE),
        in_specs=[pl.BlockSpec(bias.shape, lambda bi, h, t: (0, 0, 0))] + [pair_spec] * 4,
        out_specs=pair_spec,
        out_shape=jax.ShapeDtypeStruct((b, s, w), BF16),
        scratch_shapes=[hist, hist, prm, prm, pltpu.VMEM((ATT_PERM, reg, LANES), F32),
                        stat, stat, stat, pltpu.VMEM((ATT_TILE, LANES), F32)],
        compiler_params=pltpu.CompilerParams(
            dimension_semantics=("parallel", "parallel", "arbitrary"),
            vmem_limit_bytes=VMEM_LIMIT),
        name="dilated_attention",
    )(bias, aq, ak, av, az)


HG_DIAG = 8
HG_LEVELS = (8, 16, 32)
SUBLANES = 8


def _hgrn_constants():
    c = HG_CHUNK
    t = np.arange(c)
    tri = (t[:, None] >= t[None, :]).astype(np.float32)
    first = (t // HG_DIAG) * HG_DIAG
    half = np.concatenate([tri, tri - 0.5 * (tri[first] + tri[first + HG_DIAG - 1])], axis=0)
    coef = np.concatenate([half, half], axis=1)
    ts, ss = t[:, None], t[None, :]
    level = np.full((c, c), len(HG_LEVELS) + 1, np.int32)
    level[(ts // HG_DIAG == ss // HG_DIAG) & (ts >= ss)] = 0
    for i, lb in enumerate(HG_LEVELS):
        m = ((ts // (2 * lb) == ss // (2 * lb)) & ((ts // lb) % 2 == 1)
             & ((ss // lb) % 2 == 0))
        level[m] = i + 1
    return jnp.asarray(coef, BF16), jnp.asarray(level)


def _hgrn_out_kernel(coef_ref, level_ref, x_ref, nw_ref, w_in32_ref, lbl_ref, gn_ref,
                     oat_ref, w32_ref, fw_ref, o_ref,
                     w_in_ref, w_ref, state_ref, qs_ref, g_ref, kk_ref, vi_ref, zs_ref,
                     qd_ref, oi_ref, kv_ref, dec_ref, ohg_ref):
    @pl.when(pl.program_id(1) == 0)
    def _():
        state_ref[...] = jnp.zeros_like(state_ref)
        w_in_ref[...] = w_in32_ref[0].astype(BF16)
        w_ref[...] = w32_ref[0].astype(BF16)

    lg = lbl_ref[...]
    ex = jnp.exp(lg - jnp.max(lg, axis=0, keepdims=True))
    lb = jnp.clip(ex[0:1, :] / jnp.sum(ex, axis=0, keepdims=True), 1e-6, 1.0 - 1e-6)
    f_mid = 0.5 * (1.0 + lb)
    f_amp = 0.5 * (1.0 - lb)

    def store_q(p):
        qs_ref[...] = _silu(p).astype(BF16)

    def store_f(p):
        f = f_mid + f_amp * jnp.tanh(0.5 * p)
        g2 = jnp.log2(f)
        hi_bits = lax.bitcast_convert_type(g2, jnp.uint32) & jnp.uint32(0xFFFF0000)
        g2_hi = lax.bitcast_convert_type(hi_bits, F32)
        g2_lo = (g2 - g2_hi).astype(BF16)
        g2_hi = g2_hi.astype(BF16)
        for h in range(HG_HEADS):
            g_ref[:, 2 * h * HG_DK:(2 * h + 1) * HG_DK] = g2_hi[:, h * HG_DK:(h + 1) * HG_DK]
            g_ref[:, (2 * h + 1) * HG_DK:(2 * h + 2) * HG_DK] = g2_lo[:, h * HG_DK:(h + 1) * HG_DK]
        kk_ref[...] = (1.0 - f).astype(BF16)

    def store_z(p):
        zs_ref[...] = _silu(p).astype(BF16)

    def store_i(p):
        vi_ref[...] = p.astype(BF16)

    _project_sections(_normalized(x_ref[0], nw_ref[...]), w_in_ref,
                      [(0, store_q), (1, store_f), (3, store_z), (2, store_i)])

    c = HG_CHUNK
    n_chunks = HG_ROWS // c
    half = HG_HEADS * HG_DK
    coef = coef_ref[...]
    level = level_ref[...]
    units = [(h, ci) for h in range(HG_HEADS) for ci in range(n_chunks)]

    def rows_of(ci):
        return pl.ds(ci * c, c)

    def cols_of(h, width=HG_DK):
        return pl.ds(h * width, width)

    def cum_decay(h, ci):
        g2 = g_ref[rows_of(ci), cols_of(h, 2 * HG_DK)]
        return jnp.dot(coef, jnp.concatenate([g2[:, :HG_DK], g2[:, HG_DK:]], axis=0),
                       preferred_element_type=F32)

    def level_operands(h, ci, be):
        q = qs_ref[rows_of(ci), cols_of(h)]
        k = kk_ref[rows_of(ci), cols_of(h)]
        scaled = lambda t, w: t * w.astype(BF16)
        b = be[0:c]
        e_diag = be[c:2 * c]
        ops = [(scaled(q, jnp.exp2(e_diag)), scaled(k, jnp.exp2(-e_diag)))]
        for lb_rows in HG_LEVELS:
            ref = jnp.concatenate(
                [jnp.broadcast_to(b[p + lb_rows - 1:p + lb_rows, :], (2 * lb_rows, HG_DK))
                 for p in range(0, c, 2 * lb_rows)], axis=0)
            w = jnp.exp2(_neg_abs(b - ref))
            ops.append((scaled(q, w), scaled(k, w)))
        b_last = b[c - 1:c, :]
        return ops, scaled(k, jnp.exp2(b_last - b)), scaled(q, jnp.exp2(b)), jnp.exp2(b_last)

    def intra_scores(ops):
        scores = jnp.where(level == 0, _dot_nt(*ops[0]), 0.0)
        for i in range(len(HG_LEVELS)):
            scores = jnp.where(level == i + 1, _dot_nt(*ops[i + 1]), scores)
        return scores.astype(BF16)

    stage1 = [cum_decay(h, ci) for h, ci in units]
    stage2 = [level_operands(h, ci, be) for (h, ci), be in zip(units, stage1)]

    o_ref[0] = x_ref[0] + jnp.dot(oat_ref[0], w_ref[half:2 * half, :],
                                  preferred_element_type=F32)

    stage3 = [intra_scores(ops) for ops, _, _, _ in stage2]
    for (h, ci), (_, k_dec, q_dec, decay), scores in zip(units, stage2, stage3):
        v = vi_ref[rows_of(ci), cols_of(h)]
        oi_ref[rows_of(ci), cols_of(h)] = jnp.dot(scores, v, preferred_element_type=F32)
        kv_ref[h, ci] = _dot_tn(v, k_dec)
        qd_ref[rows_of(ci), cols_of(h)] = q_dec
        dec_ref[h, ci] = jnp.broadcast_to(decay, dec_ref.shape[2:])

    states = [state_ref[h] for h in range(HG_HEADS)]
    for ci in range(n_chunks):
        for h in range(HG_HEADS):
            rows, cols = rows_of(ci), cols_of(h)
            o = oi_ref[rows, cols] + _dot_nt(qd_ref[rows, cols], states[h].astype(BF16))
            states[h] = states[h] * dec_ref[h, ci, 0:1, :] + kv_ref[h, ci]
            z = zs_ref[rows, cols].astype(F32)
            ms = jnp.mean(o * o, axis=-1, keepdims=True)
            ohg_ref[rows, cols] = (o * lax.rsqrt(ms + NORM_EPS) * gn_ref[:, cols] * z
                                   ).astype(BF16)
    for h in range(HG_HEADS):
        state_ref[h] = states[h]

    hres = o_ref[0] + jnp.dot(ohg_ref[...], w_ref[0:half, :], preferred_element_type=F32)
    ms = jnp.mean(hres * hres, axis=-1, keepdims=True)
    o_ref[0] = hres * lax.rsqrt(ms + NORM_EPS) * fw_ref[...]


def _hgrn_out(x, norm_w, w_in, lb_logits, hg_norm_w, oat, w_out, final_norm_w):
    b, s, d_model = x.shape
    w = HG_HEADS * HG_DK
    assert s % HG_ROWS == 0 and w == SECTION
    assert w_out.shape == (1, w + oat.shape[-1], d_model)
    coef, level = _hgrn_constants()
    n_chunks = HG_ROWS // HG_CHUNK
    tile = lambda width: pl.BlockSpec((1, HG_ROWS, width), lambda bi, t: (bi, t, 0))
    const = lambda a: pl.BlockSpec(a.shape, lambda bi, t: (0,) * a.ndim)
    act = lambda width: pltpu.VMEM((HG_ROWS, width), BF16)
    return pl.pallas_call(
        _hgrn_out_kernel,
        grid=(b, s // HG_ROWS),
        in_specs=[const(coef), const(level), tile(d_model), const(norm_w),
                  _branch_weight_spec(w_in, 0), const(lb_logits), const(hg_norm_w),
                  tile(oat.shape[-1]), const(w_out), const(final_norm_w)],
        out_specs=tile(d_model),
        out_shape=jax.ShapeDtypeStruct((b, s, d_model), F32),
        scratch_shapes=[pltpu.VMEM((d_model, 4 * SECTION), BF16),
                        pltpu.VMEM(w_out.shape[1:], BF16),
                        pltpu.VMEM((HG_HEADS, HG_DK, HG_DK), F32),
                        act(w), act(2 * w), act(w), act(w), act(w),
                        act(w),
                        pltpu.VMEM((HG_ROWS, w), F32),
                        pltpu.VMEM((HG_HEADS, n_chunks, HG_DK, HG_DK), F32),
                        pltpu.VMEM((HG_HEADS, n_chunks, SUBLANES, HG_DK), F32),
                        act(w)],
        compiler_params=pltpu.CompilerParams(
            dimension_semantics=("parallel", "arbitrary"),
            vmem_limit_bytes=VMEM_LIMIT),
        name="hgrn2_branch_and_output",
    )(coef, level, x, norm_w, w_in, lb_logits, hg_norm_w, oat, w_out, final_norm_w)


def kernel(x, norm_w, w_in, hgrn_lb_logits, hg_norm_w, w_out, final_norm_w):
    b, s, d_model = x.shape
    assert norm_w.shape[0] == 1 and w_in.shape[0] == 1 and w_out.shape[0] == 1
    aq, ak, av, az = _attn_projection(x.reshape(b * s, d_model), norm_w, w_in, s)
    to3 = lambda a: a.reshape(b, s, a.shape[-1])
    oat = _attention(to3(aq), to3(ak), to3(av), to3(az))
    return _hgrn_out(x, norm_w, w_in, hgrn_lb_logits, hg_norm_w, oat, w_out,
                     final_norm_w.reshape(1, d_model))
```

```python
import functools

import numpy as np
import jax
import jax.numpy as jnp
from jax import lax
from jax.experimental import pallas as pl
from jax.experimental.pallas import tpu as pltpu

F32 = jnp.float32
BF16 = jnp.bfloat16

NORM_EPS = 1e-6
ROPE_THETA = 10000.0
LANES = 128

HG_HEADS = 4
HG_DK = 128
HG_CHUNK = 64
AT_HEAD_DIM = 64
SECTION = 512
DILATIONS = (1, 4, 16)
ATT_BLOCK = 128
ATT_TILE = ATT_BLOCK * max(DILATIONS)
NEG_BIG = -1e30

PROJ_ROWS = 1024
HG_ROWS = 1024
ATT_PERM = 4
ATT_MERGE_ROWS = 512
LOG2E = 1.4426950408889634
VMEM_LIMIT = 56 * 1024 * 1024


def _normalized(x, gain):
    ms = jnp.mean(x * x, axis=-1, keepdims=True)
    return (x * lax.rsqrt(ms + NORM_EPS) * gain).astype(BF16)


def _silu(p):
    return p * (0.5 * jnp.tanh(0.5 * p) + 0.5)


def _project_sections(u, w_ref, plan):
    pending = None
    for j, finish in plan:
        p = jnp.dot(u, w_ref[:, j * SECTION:(j + 1) * SECTION], preferred_element_type=F32)
        if pending is not None:
            pending()
        pending = functools.partial(finish, p)
    pending()


def _dot_nt(a, b):
    return lax.dot_general(a, b, (((1,), (1,)), ((), ())), preferred_element_type=F32)


def _dot_tn(a, b):
    return lax.dot_general(a, b, (((0,), (0,)), ((), ())), preferred_element_type=F32)


def _neg_abs(x):
    bits = lax.bitcast_convert_type(x, jnp.uint32) | jnp.uint32(0x80000000)
    return lax.bitcast_convert_type(bits, F32)


def _attn_proj_kernel(x_ref, nw_ref, w32_ref, cos_ref, sina_ref, sinb_ref,
                      aq_ref, ak_ref, av_ref, az_ref, w_ref):
    @pl.when(pl.program_id(0) == 0)
    def _():
        w_ref[...] = w32_ref[0].astype(BF16)

    u = _normalized(x_ref[...], nw_ref[...])

    def store_rope(dst_ref, scale, p):
        half = AT_HEAD_DIM // 2
        for c in range(SECTION // LANES):
            xc = p[:, c * LANES:(c + 1) * LANES]
            r = (xc * cos_ref[...] + pltpu.roll(xc, LANES - half, 1) * sina_ref[...]
                 + pltpu.roll(xc, half, 1) * sinb_ref[...])
            dst_ref[:, c * LANES:(c + 1) * LANES] = r * scale

    def store_z(p):
        az_ref[...] = _silu(p).astype(BF16)

    def store_v(p):
        av_ref[...] = p

    _project_sections(u, w_ref, [
        (3, store_z),
        (0, functools.partial(store_rope, aq_ref, AT_HEAD_DIM ** -0.5 * LOG2E)),
        (1, functools.partial(store_rope, ak_ref, 1.0)),
        (2, store_v)])


@functools.lru_cache(maxsize=None)
def _rope_tables(seq):
    half = AT_HEAD_DIM // 2
    inv_freq = 1.0 / (ROPE_THETA ** (np.arange(half, dtype=np.float64) / half))
    ang = np.arange(seq, dtype=np.float64)[:, None] * inv_freq[None, :]
    cos = np.cos(ang)
    sin = np.sin(ang)
    zero = np.zeros_like(sin)
    reps = LANES // AT_HEAD_DIM
    cos_t = np.tile(np.concatenate([cos, cos], axis=1), (1, reps))
    sina_t = np.tile(np.concatenate([-sin, zero], axis=1), (1, reps))
    sinb_t = np.tile(np.concatenate([zero, sin], axis=1), (1, reps))
    return tuple(np.asarray(t, np.float32) for t in (cos_t, sina_t, sinb_t))


def _branch_weight_spec(w_in, branch):
    assert w_in.shape[0] == 1 and w_in.shape[2] == 8 * SECTION
    return pl.BlockSpec((1, w_in.shape[1], 4 * SECTION),
                        lambda *_: (0, 0, branch), pipeline_mode=pl.Buffered(1))


def _attn_projection(x2, norm_w, w_in, seq):
    rows, d_model = x2.shape
    assert rows % PROJ_ROWS == 0 and seq % PROJ_ROWS == 0
    tables = _rope_tables(seq)
    seq_tiles = seq // PROJ_ROWS
    row_spec = lambda w: pl.BlockSpec((PROJ_ROWS, w), lambda i: (i, 0))
    tab_spec = pl.BlockSpec((PROJ_ROWS, LANES), lambda i: (i % seq_tiles, 0))
    full = lambda a: pl.BlockSpec(a.shape, lambda i: (0, 0))
    out_dtypes = (F32, F32, F32, BF16)
    return pl.pallas_call(
        _attn_proj_kernel,
        grid=(rows // PROJ_ROWS,),
        in_specs=[row_spec(d_model), full(norm_w), _branch_weight_spec(w_in, 1)]
                 + [tab_spec] * 3,
        out_specs=[row_spec(SECTION)] * len(out_dtypes),
        out_shape=[jax.ShapeDtypeStruct((rows, SECTION), dt) for dt in out_dtypes],
        scratch_shapes=[pltpu.VMEM((d_model, 4 * SECTION), BF16)],
        compiler_params=pltpu.CompilerParams(
            dimension_semantics=("arbitrary",), vmem_limit_bytes=VMEM_LIMIT),
        name="attention_projection",
    )(x2, norm_w, w_in, *tables)


def _attn_bias():
    qi = np.arange(ATT_BLOCK)[:, None]
    kj = np.arange(2 * ATT_BLOCK)[None, :]
    dist = ATT_BLOCK + qi - kj
    valid = (dist >= 0) & (dist <= ATT_BLOCK)
    b0 = np.where(valid, 0.0, NEG_BIG)
    b1 = np.where(valid & (kj >= ATT_BLOCK), 0.0, NEG_BIG)
    return jnp.asarray(np.stack([b0, b1]), F32)


def _attn_kernel(bias_ref, aq_ref, ak_ref, av_ref, az_ref, o_ref,
                 khist, vhist, kprm, vprm, qprm, m_ref, l_ref, acc_ref, onat):
    ta = ATT_TILE
    blk = ATT_BLOCK
    nph = ATT_PERM
    reg = ta // nph
    tile = pl.program_id(2)
    cur = tile % 2
    prev = 1 - cur

    @pl.when(tile == 0)
    def _():
        khist[...] = jnp.zeros_like(khist)
        vhist[...] = jnp.zeros_like(vhist)
        kprm[1] = jnp.zeros((nph, reg, LANES), F32)
        vprm[1] = jnp.zeros((nph, reg, LANES), F32)

    for r in range(nph):
        kprm[cur, r] = ak_ref[0, pl.ds(r, reg, stride=nph), :]
        vprm[cur, r] = av_ref[0, pl.ds(r, reg, stride=nph), :]
        qprm[r] = aq_ref[0, pl.ds(r, reg, stride=nph), :]

    first_head = lax.broadcasted_iota(jnp.int32, (blk, LANES), 1) < AT_HEAD_DIM
    ones = jnp.ones((2 * blk, LANES), BF16)
    bias_any = bias_ref[0]
    bias_first = bias_ref[(tile == 0).astype(jnp.int32)]

    def block_stats(q, k, v, bias):
        k = k.astype(BF16)
        v = v.astype(BF16)
        q2 = jnp.concatenate([jnp.where(first_head, q, 0.0),
                              jnp.where(first_head, 0.0, q)], axis=0).astype(BF16)
        s = _dot_nt(q2, k) + jnp.concatenate([bias, bias], axis=0)
        m = jnp.max(s, axis=-1, keepdims=True)
        p = jnp.exp2(s - m).astype(BF16)
        pv = jnp.dot(p, jnp.concatenate([v, ones], axis=1), preferred_element_type=F32)
        return (jnp.where(first_head, m[:blk], m[blk:]),
                jnp.where(first_head, pv[:blk, LANES:], pv[blk:, LANES:]),
                jnp.where(first_head, pv[:blk, :LANES], pv[blk:, :LANES]))

    def store_stats(pi, rows, stats):
        m_ref[pi, rows, :], l_ref[pi, rows, :], acc_ref[pi, rows, :] = stats

    def window(first_ref, first_rows, ref, rows):
        return jnp.concatenate([first_ref[first_rows], ref[rows]], axis=0)

    for j in range(ta // blk):
        q_rows = pl.ds(j * blk, blk)
        if j == 0:
            k = window(khist, (slice(None),) * 2, ak_ref, (0, q_rows))
            v = window(vhist, (slice(None),) * 2, av_ref, (0, q_rows))
        else:
            k_rows = pl.ds((j - 1) * blk, 2 * blk)
            k, v = ak_ref[0, k_rows, :], av_ref[0, k_rows, :]
        store_stats(0, q_rows, block_stats(aq_ref[0, q_rows, :], k, v,
                                           bias_first if j == 0 else bias_any))

    khist[...] = ak_ref[0, ta - blk:ta, :]
    vhist[...] = av_ref[0, ta - blk:ta, :]

    for r in range(nph):
        for j in range(reg // blk):
            q_rows = pl.ds(j * blk, blk)
            if j == 0:
                last = pl.ds(reg - blk, blk)
                k = window(kprm, (prev, r, last), kprm, (cur, r, q_rows))
                v = window(vprm, (prev, r, last), vprm, (cur, r, q_rows))
            else:
                k_rows = pl.ds((j - 1) * blk, 2 * blk)
                k, v = kprm[cur, r, k_rows, :], vprm[cur, r, k_rows, :]
            stats = block_stats(qprm[r, q_rows, :], k, v, bias_first if j == 0 else bias_any)
            store_stats(1, pl.ds(r * reg + j * blk, blk), stats)
        for r16 in range(nph):
            rows = pl.ds(r16, blk, stride=nph)
            k = window(kprm, (prev, r, rows), kprm, (cur, r, rows))
            v = window(vprm, (prev, r, rows), vprm, (cur, r, rows))
            stats = block_stats(qprm[r, rows, :], k, v, bias_first)
            store_stats(2, pl.ds(r * reg + r16, blk, stride=nph), stats)

    def merge(ci, carry):
        pieces = reg // ATT_MERGE_ROWS
        r = ci // pieces
        i0 = (ci % pieces) * ATT_MERGE_ROWS
        nat_rows = pl.ds(r + nph * i0, ATT_MERGE_ROWS, stride=nph)
        prm_rows = pl.ds(pl.multiple_of(ci * ATT_MERGE_ROWS, ATT_MERGE_ROWS), ATT_MERGE_ROWS)
        rows = (nat_rows, prm_rows, prm_rows)
        ms = [m_ref[pi, rw, :] for pi, rw in enumerate(rows)]
        m_all = functools.reduce(jnp.maximum, ms)
        ws = [jnp.exp2(m - m_all) for m in ms]
        l_all = sum(w * l_ref[pi, rw, :] for (pi, rw), w in zip(enumerate(rows), ws))
        acc = sum(w * acc_ref[pi, rw, :] for (pi, rw), w in zip(enumerate(rows), ws))
        onat[nat_rows, :] = acc / l_all
        return carry

    lax.fori_loop(0, ta // ATT_MERGE_ROWS, merge, 0)

    def gate(ci, carry):
        rows = pl.ds(pl.multiple_of(ci * ATT_MERGE_ROWS, ATT_MERGE_ROWS), ATT_MERGE_ROWS)
        o_ref[0, rows, :] = (onat[rows, :] * az_ref[0, rows, :].astype(F32)).astype(BF16)
        return carry

    lax.fori_loop(0, ta // ATT_MERGE_ROWS, gate, 0)


def _attention(aq, ak, av, az):
    b, s, w = aq.shape
    assert s % ATT_TILE == 0 and w % LANES == 0
    assert DILATIONS == (1, ATT_PERM, ATT_PERM * ATT_PERM)
    assert (ATT_TILE // ATT_PERM) % ATT_MERGE_ROWS == 0
    bias = _attn_bias()
    reg = ATT_TILE // ATT_PERM
    pair_spec = pl.BlockSpec((1, ATT_TILE, LANES), lambda bi, h, t: (bi, t, h))
    hist = pltpu.VMEM((ATT_BLOCK, LANES), F32)
    prm = pltpu.VMEM((2, ATT_PERM, reg, LANES), F32)
    stat = pltpu.VMEM((len(DILATIONS), ATT_TILE, LANES), F32)
    return pl.pallas_call(
        _attn_kernel,
        grid=(b, w // LANES, s // ATT_TILE),
        in_specs=[pl.BlockSpec(bias.shape, lambda bi, h, t: (0, 0, 0))] + [pair_spec] * 4,
        out_specs=pair_spec,
        out_shape=jax.ShapeDtypeStruct((b, s, w), BF16),
        scratch_shapes=[hist, hist, prm, prm, pltpu.VMEM((ATT_PERM, reg, LANES), F32),
                        stat, stat, stat, pltpu.VMEM((ATT_TILE, LANES), F32)],
        compiler_params=pltpu.CompilerParams(
            dimension_semantics=("parallel", "parallel", "arbitrary"),
            vmem_limit_bytes=VMEM_LIMIT),
        name="dilated_attention",
    )(bias, aq, ak, av, az)


HG_DIAG = 8
HG_LEVELS = (8, 16, 32)
SUBLANES = 8


def _hgrn_constants():
    c = HG_CHUNK
    t = np.arange(c)
    tri = (t[:, None] >= t[None, :]).astype(np.float32)
    first = (t // HG_DIAG) * HG_DIAG
    half = np.concatenate([tri, tri - 0.5 * (tri[first] + tri[first + HG_DIAG - 1])], axis=0)
    coef = np.concatenate([half, half], axis=1)
    ts, ss = t[:, None], t[None, :]
    level = np.full((c, c), len(HG_LEVELS) + 1, np.int32)
    level[(ts // HG_DIAG == ss // HG_DIAG) & (ts >= ss)] = 0
    for i, lb in enumerate(HG_LEVELS):
        m = ((ts // (2 * lb) == ss // (2 * lb)) & ((ts // lb) % 2 == 1)
             & ((ss // lb) % 2 == 0))
        level[m] = i + 1
    return jnp.asarray(coef, BF16), jnp.asarray(level)


def _hgrn_out_kernel(coef_ref, level_ref, x_ref, nw_ref, w_in32_ref, lbl_ref, gn_ref,
                     oat_ref, w32_ref, fw_ref, o_ref,
                     w_in_ref, w_ref, state_ref, qs_ref, g_ref, kk_ref, vi_ref, zs_ref,
                     qd_ref, oi_ref, kv_ref, dec_ref, ohg_ref):
    @pl.when(pl.program_id(1) == 0)
    def _():
        state_ref[...] = jnp.zeros_like(state_ref)
        w_in_ref[...] = w_in32_ref[0].astype(BF16)
        w_ref[...] = w32_ref[0].astype(BF16)

    lg = lbl_ref[...]
    ex = jnp.exp(lg - jnp.max(lg, axis=0, keepdims=True))
    lb = jnp.clip(ex[0:1, :] / jnp.sum(ex, axis=0, keepdims=True), 1e-6, 1.0 - 1e-6)
    f_mid = 0.5 * (1.0 + lb)
    f_amp = 0.5 * (1.0 - lb)

    def store_q(p):
        qs_ref[...] = _silu(p).astype(BF16)

    def store_f(p):
        f = f_mid + f_amp * jnp.tanh(0.5 * p)
        g2 = jnp.log2(f)
        hi_bits = lax.bitcast_convert_type(g2, jnp.uint32) & jnp.uint32(0xFFFF0000)
        g2_hi = lax.bitcast_convert_type(hi_bits, F32)
        g2_lo = (g2 - g2_hi).astype(BF16)
        g2_hi = g2_hi.astype(BF16)
        for h in range(HG_HEADS):
            g_ref[:, 2 * h * HG_DK:(2 * h + 1) * HG_DK] = g2_hi[:, h * HG_DK:(h + 1) * HG_DK]
            g_ref[:, (2 * h + 1) * HG_DK:(2 * h + 2) * HG_DK] = g2_lo[:, h * HG_DK:(h + 1) * HG_DK]
        kk_ref[...] = (1.0 - f).astype(BF16)

    def store_z(p):
        zs_ref[...] = _silu(p).astype(BF16)

    def store_i(p):
        vi_ref[...] = p.astype(BF16)

    _project_sections(_normalized(x_ref[0], nw_ref[...]), w_in_ref,
                      [(0, store_q), (1, store_f), (3, store_z), (2, store_i)])

    c = HG_CHUNK
    n_chunks = HG_ROWS // c
    half = HG_HEADS * HG_DK
    coef = coef_ref[...]
    level = level_ref[...]
    units = [(h, ci) for h in range(HG_HEADS) for ci in range(n_chunks)]

    def rows_of(ci):
        return pl.ds(ci * c, c)

    def cols_of(h, width=HG_DK):
        return pl.ds(h * width, width)

    def cum_decay(h, ci):
        g2 = g_ref[rows_of(ci), cols_of(h, 2 * HG_DK)]
        return jnp.dot(coef, jnp.concatenate([g2[:, :HG_DK], g2[:, HG_DK:]], axis=0),
                       preferred_element_type=F32)

    def level_operands(h, ci, be):
        q = qs_ref[rows_of(ci), cols_of(h)]
        k = kk_ref[rows_of(ci), cols_of(h)]
        scaled = lambda t, w: t * w.astype(BF16)
        b = be[0:c]
        e_diag = be[c:2 * c]
        ops = [(scaled(q, jnp.exp2(e_diag)), scaled(k, jnp.exp2(-e_diag)))]
        for lb_rows in HG_LEVELS:
            ref = jnp.concatenate(
                [jnp.broadcast_to(b[p + lb_rows - 1:p + lb_rows, :], (2 * lb_rows, HG_DK))
                 for p in range(0, c, 2 * lb_rows)], axis=0)
            w = jnp.exp2(_neg_abs(b - ref))
            ops.append((scaled(q, w), scaled(k, w)))
        b_last = b[c - 1:c, :]
        return ops, scaled(k, jnp.exp2(b_last - b)), scaled(q, jnp.exp2(b)), jnp.exp2(b_last)

    def intra_scores(ops):
        scores = jnp.where(level == 0, _dot_nt(*ops[0]), 0.0)
        for i in range(len(HG_LEVELS)):
            scores = jnp.where(level == i + 1, _dot_nt(*ops[i + 1]), scores)
        return scores.astype(BF16)

    stage1 = [cum_decay(h, ci) for h, ci in units]
    stage2 = [level_operands(h, ci, be) for (h, ci), be in zip(units, stage1)]

    o_ref[0] = x_ref[0] + jnp.dot(oat_ref[0], w_ref[half:2 * half, :],
                                  preferred_element_type=F32)

    stage3 = [intra_scores(ops) for ops, _, _, _ in stage2]
    for (h, ci), (_, k_dec, q_dec, decay), scores in zip(units, stage2, stage3):
        v = vi_ref[rows_of(ci), cols_of(h)]
        oi_ref[rows_of(ci), cols_of(h)] = jnp.dot(scores, v, preferred_element_type=F32)
        kv_ref[h, ci] = _dot_tn(v, k_dec)
        qd_ref[rows_of(ci), cols_of(h)] = q_dec
        dec_ref[h, ci] = jnp.broadcast_to(decay, dec_ref.shape[2:])

    states = [state_ref[h] for h in range(HG_HEADS)]
    for ci in range(n_chunks):
        for h in range(HG_HEADS):
            rows, cols = rows_of(ci), cols_of(h)
            o = oi_ref[rows, cols] + _dot_nt(qd_ref[rows, cols], states[h].astype(BF16))
            states[h] = states[h] * dec_ref[h, ci, 0:1, :] + kv_ref[h, ci]
            z = zs_ref[rows, cols].astype(F32)
            ms = jnp.mean(o * o, axis=-1, keepdims=True)
            ohg_ref[rows, cols] = (o * lax.rsqrt(ms + NORM_EPS) * gn_ref[:, cols] * z
                                   ).astype(BF16)
    for h in range(HG_HEADS):
        state_ref[h] = states[h]

    hres = o_ref[0] + jnp.dot(ohg_ref[...], w_ref[0:half, :], preferred_element_type=F32)
    ms = jnp.mean(hres * hres, axis=-1, keepdims=True)
    o_ref[0] = hres * lax.rsqrt(ms + NORM_EPS) * fw_ref[...]


def _hgrn_out(x, norm_w, w_in, lb_logits, hg_norm_w, oat, w_out, final_norm_w):
    b, s, d_model = x.shape
    w = HG_HEADS * HG_DK
    assert s % HG_ROWS == 0 and w == SECTION
    assert w_out.shape == (1, w + oat.shape[-1], d_model)
    coef, level = _hgrn_constants()
    n_chunks = HG_ROWS // HG_CHUNK
    tile = lambda width: pl.BlockSpec((1, HG_ROWS, width), lambda bi, t: (bi, t, 0))
    const = lambda a: pl.BlockSpec(a.shape, lambda bi, t: (0,) * a.ndim)
    act = lambda width: pltpu.VMEM((HG_ROWS, width), BF16)
    return pl.pallas_call(
        _hgrn_out_kernel,
        grid=(b, s // HG_ROWS),
        in_specs=[const(coef), const(level), tile(d_model), const(norm_w),
                  _branch_weight_spec(w_in, 0), const(lb_logits), const(hg_norm_w),
                  tile(oat.shape[-1]), const(w_out), const(final_norm_w)],
        out_specs=tile(d_model),
        out_shape=jax.ShapeDtypeStruct((b, s, d_model), F32),
        scratch_shapes=[pltpu.VMEM((d_model, 4 * SECTION), BF16),
                        pltpu.VMEM(w_out.shape[1:], BF16),
                        pltpu.VMEM((HG_HEADS, HG_DK, HG_DK), F32),
                        act(w), act(2 * w), act(w), act(w), act(w),
                        act(w),
                        pltpu.VMEM((HG_ROWS, w), F32),
                        pltpu.VMEM((HG_HEADS, n_chunks, HG_DK, HG_DK), F32),
                        pltpu.VMEM((HG_HEADS, n_chunks, SUBLANES, HG_DK), F32),
                        act(w)],
        compiler_params=pltpu.CompilerParams(
            dimension_semantics=("parallel", "arbitrary"),
            vmem_limit_bytes=VMEM_LIMIT),
        name="hgrn2_branch_and_output",
    )(coef, level, x, norm_w, w_in, lb_logits, hg_norm_w, oat, w_out, final_norm_w)


def kernel(x, norm_w, w_in, hgrn_lb_logits, hg_norm_w, w_out, final_norm_w):
    b, s, d_model = x.shape
    assert norm_w.shape[0] == 1 and w_in.shape[0] == 1 and w_out.shape[0] == 1
    aq, ak, av, az = _attn_projection(x.reshape(b * s, d_model), norm_w, w_in, s)
    to3 = lambda a: a.reshape(b, s, a.shape[-1])
    oat = _attention(to3(aq), to3(ak), to3(av), to3(az))
    return _hgrn_out(x, norm_w, w_in, hgrn_lb_logits, hg_norm_w, oat, w_out,
                     final_norm_w.reshape(1, d_model))
```

```python
import functools

import numpy as np
import jax
import jax.numpy as jnp
from jax import lax
from jax.experimental import pallas as pl
from jax.experimental.pallas import tpu as pltpu

F32 = jnp.float32
BF16 = jnp.bfloat16

NORM_EPS = 1e-6
ROPE_THETA = 10000.0
LANES = 128

HG_HEADS = 4
HG_DK = 128
HG_CHUNK = 64
AT_HEAD_DIM = 64
SECTION = 512
DILATIONS = (1, 4, 16)
ATT_BLOCK = 128
ATT_TILE = 2 * ATT_BLOCK * max(DILATIONS)
NEG_BIG = -1e30

PROJ_ROWS = 1024
HG_ROWS = 512
ATT_PERM = 4
ATT_MERGE_ROWS = 512
LOG2E = 1.4426950408889634
VMEM_LIMIT = 56 * 1024 * 1024


def _normalized(x, gain):
    ms = jnp.mean(x * x, axis=-1, keepdims=True)
    return (x * lax.rsqrt(ms + NORM_EPS) * gain).astype(BF16)


def _silu(p):
    return p * (0.5 * jnp.tanh(0.5 * p) + 0.5)


def _project_sections(u, w_ref, plan):
    pending = None
    for j, finish in plan:
        p = jnp.dot(u, w_ref[:, j * SECTION:(j + 1) * SECTION], preferred_element_type=F32)
        if pending is not None:
            pending()
        pending = functools.partial(finish, p)
    pending()


def _dot_nt(a, b):
    return lax.dot_general(a, b, (((1,), (1,)), ((), ())), preferred_element_type=F32)


def _dot_tn(a, b):
    return lax.dot_general(a, b, (((0,), (0,)), ((), ())), preferred_element_type=F32)


def _neg_abs(x):
    bits = lax.bitcast_convert_type(x, jnp.uint32) | jnp.uint32(0x80000000)
    return lax.bitcast_convert_type(bits, F32)


def _attn_proj_kernel(x_ref, nw_ref, w32_ref, cos_ref, sina_ref, sinb_ref,
                      aq_ref, ak_ref, av_ref, az_ref, w_ref):
    @pl.when(pl.program_id(0) == 0)
    def _():
        w_ref[...] = w32_ref[0].astype(BF16)

    u = _normalized(x_ref[...], nw_ref[...])

    def store_rope(dst_ref, scale, p):
        half = AT_HEAD_DIM // 2
        for c in range(SECTION // LANES):
            xc = p[:, c * LANES:(c + 1) * LANES]
            r = (xc * cos_ref[...] + pltpu.roll(xc, LANES - half, 1) * sina_ref[...]
                 + pltpu.roll(xc, half, 1) * sinb_ref[...])
            dst_ref[:, c * LANES:(c + 1) * LANES] = r * scale

    def store_z(p):
        az_ref[...] = _silu(p).astype(BF16)

    def store_v(p):
        av_ref[...] = p

    _project_sections(u, w_ref, [
        (3, store_z),
        (0, functools.partial(store_rope, aq_ref, AT_HEAD_DIM ** -0.5 * LOG2E)),
        (1, functools.partial(store_rope, ak_ref, 1.0)),
        (2, store_v)])


@functools.lru_cache(maxsize=None)
def _rope_tables(seq):
    half = AT_HEAD_DIM // 2
    inv_freq = 1.0 / (ROPE_THETA ** (np.arange(half, dtype=np.float64) / half))
    ang = np.arange(seq, dtype=np.float64)[:, None] * inv_freq[None, :]
    cos = np.cos(ang)
    sin = np.sin(ang)
    zero = np.zeros_like(sin)
    reps = LANES // AT_HEAD_DIM
    cos_t = np.tile(np.concatenate([cos, cos], axis=1), (1, reps))
    sina_t = np.tile(np.concatenate([-sin, zero], axis=1), (1, reps))
    sinb_t = np.tile(np.concatenate([zero, sin], axis=1), (1, reps))
    return tuple(np.asarray(t, np.float32) for t in (cos_t, sina_t, sinb_t))


def _branch_weight_spec(w_in, branch):
    assert w_in.shape[0] == 1 and w_in.shape[2] == 8 * SECTION
    return pl.BlockSpec((1, w_in.shape[1], 4 * SECTION),
                        lambda *_: (0, 0, branch), pipeline_mode=pl.Buffered(1))


def _attn_projection(x2, norm_w, w_in, seq):
    rows, d_model = x2.shape
    assert rows % PROJ_ROWS == 0 and seq % PROJ_ROWS == 0
    tables = _rope_tables(seq)
    seq_tiles = seq // PROJ_ROWS
    row_spec = lambda w: pl.BlockSpec((PROJ_ROWS, w), lambda i: (i, 0))
    tab_spec = pl.BlockSpec((PROJ_ROWS, LANES), lambda i: (i % seq_tiles, 0))
    full = lambda a: pl.BlockSpec(a.shape, lambda i: (0, 0))
    out_dtypes = (F32, F32, F32, BF16)
    return pl.pallas_call(
        _attn_proj_kernel,
        grid=(rows // PROJ_ROWS,),
        in_specs=[row_spec(d_model), full(norm_w), _branch_weight_spec(w_in, 1)]
                 + [tab_spec] * 3,
        out_specs=[row_spec(SECTION)] * len(out_dtypes),
        out_shape=[jax.ShapeDtypeStruct((rows, SECTION), dt) for dt in out_dtypes],
        scratch_shapes=[pltpu.VMEM((d_model, 4 * SECTION), BF16)],
        compiler_params=pltpu.CompilerParams(
            dimension_semantics=("arbitrary",), vmem_limit_bytes=VMEM_LIMIT),
        name="attention_projection",
    )(x2, norm_w, w_in, *tables)


def _attn_bias():
    qi = np.arange(ATT_BLOCK)[:, None]
    kj = np.arange(2 * ATT_BLOCK)[None, :]
    dist = ATT_BLOCK + qi - kj
    valid = (dist >= 0) & (dist <= ATT_BLOCK)
    b0 = np.where(valid, 0.0, NEG_BIG)
    b1 = np.where(valid & (kj >= ATT_BLOCK), 0.0, NEG_BIG)
    return jnp.asarray(np.stack([b0, b1]), F32)


def _attn_kernel(bias_ref, aq_ref, ak_ref, av_ref, az_ref, o_ref,
                 khist, vhist, kprm, vprm, qprm, m_ref, l_ref, acc_ref, onat):
    ta = ATT_TILE
    blk = ATT_BLOCK
    nph = ATT_PERM
    reg = ta // nph
    tile = pl.program_id(2)
    cur = tile % 2
    prev = 1 - cur

    @pl.when(tile == 0)
    def _():
        khist[...] = jnp.zeros_like(khist)
        vhist[...] = jnp.zeros_like(vhist)
        kprm[1] = jnp.zeros((nph, reg, LANES), F32)
        vprm[1] = jnp.zeros((nph, reg, LANES), F32)

    for r in range(nph):
        kprm[cur, r] = ak_ref[0, pl.ds(r, reg, stride=nph), :]
        vprm[cur, r] = av_ref[0, pl.ds(r, reg, stride=nph), :]
        qprm[r] = aq_ref[0, pl.ds(r, reg, stride=nph), :]

    first_head = lax.broadcasted_iota(jnp.int32, (blk, LANES), 1) < AT_HEAD_DIM
    ones = jnp.ones((2 * blk, LANES), BF16)
    bias_any = bias_ref[0]
    bias_first = bias_ref[(tile == 0).astype(jnp.int32)]

    def block_stats(q, k, v, bias):
        k = k.astype(BF16)
        v = v.astype(BF16)
        q2 = jnp.concatenate([jnp.where(first_head, q, 0.0),
                              jnp.where(first_head, 0.0, q)], axis=0).astype(BF16)
        s = _dot_nt(q2, k) + jnp.concatenate([bias, bias], axis=0)
        m = jnp.max(s, axis=-1, keepdims=True)
        p = jnp.exp2(s - m).astype(BF16)
        pv = jnp.dot(p, jnp.concatenate([v, ones], axis=1), preferred_element_type=F32)
        return (jnp.where(first_head, m[:blk], m[blk:]),
                jnp.where(first_head, pv[:blk, LANES:], pv[blk:, LANES:]),
                jnp.where(first_head, pv[:blk, :LANES], pv[blk:, :LANES]))

    def store_stats(pi, rows, stats):
        m_ref[pi, rows, :], l_ref[pi, rows, :], acc_ref[pi, rows, :] = stats

    def window(first_ref, first_rows, ref, rows):
        return jnp.concatenate([first_ref[first_rows], ref[rows]], axis=0)

    for j in range(ta // blk):
        q_rows = pl.ds(j * blk, blk)
        if j == 0:
            k = window(khist, (slice(None),) * 2, ak_ref, (0, q_rows))
            v = window(vhist, (slice(None),) * 2, av_ref, (0, q_rows))
        else:
            k_rows = pl.ds((j - 1) * blk, 2 * blk)
            k, v = ak_ref[0, k_rows, :], av_ref[0, k_rows, :]
        store_stats(0, q_rows, block_stats(aq_ref[0, q_rows, :], k, v,
                                           bias_first if j == 0 else bias_any))

    khist[...] = ak_ref[0, ta - blk:ta, :]
    vhist[...] = av_ref[0, ta - blk:ta, :]

    for r in range(nph):
        for j in range(reg // blk):
            q_rows = pl.ds(j * blk, blk)
            if j == 0:
                last = pl.ds(reg - blk, blk)
                k = window(kprm, (prev, r, last), kprm, (cur, r, q_rows))
                v = window(vprm, (prev, r, last), vprm, (cur, r, q_rows))
            else:
                k_rows = pl.ds((j - 1) * blk, 2 * blk)
                k, v = kprm[cur, r, k_rows, :], vprm[cur, r, k_rows, :]
            stats = block_stats(qprm[r, q_rows, :], k, v, bias_first if j == 0 else bias_any)
            store_stats(1, pl.ds(r * reg + j * blk, blk), stats)
        sub = reg // nph
        for r16 in range(nph):
            for j in range(sub // blk):
                q_rows = pl.ds(r16 + nph * blk * j, blk, stride=nph)
                if j == 0:
                    tail = pl.ds(r16 + nph * (sub - blk), blk, stride=nph)
                    k = window(kprm, (prev, r, tail), kprm, (cur, r, q_rows))
                    v = window(vprm, (prev, r, tail), vprm, (cur, r, q_rows))
                else:
                    k_rows = pl.ds(r16 + nph * blk * (j - 1), 2 * blk, stride=nph)
                    k, v = kprm[cur, r, k_rows, :], vprm[cur, r, k_rows, :]
                stats = block_stats(qprm[r, q_rows, :], k, v,
                                    bias_first if j == 0 else bias_any)
                store_stats(2, pl.ds(r * reg + r16 + nph * blk * j, blk, stride=nph), stats)

    def merge(ci, carry):
        pieces = reg // ATT_MERGE_ROWS
        r = ci // pieces
        i0 = (ci % pieces) * ATT_MERGE_ROWS
        nat_rows = pl.ds(r + nph * i0, ATT_MERGE_ROWS, stride=nph)
        prm_rows = pl.ds(pl.multiple_of(ci * ATT_MERGE_ROWS, ATT_MERGE_ROWS), ATT_MERGE_ROWS)
        rows = (nat_rows, prm_rows, prm_rows)
        ms = [m_ref[pi, rw, :] for pi, rw in enumerate(rows)]
        m_all = functools.reduce(jnp.maximum, ms)
        ws = [jnp.exp2(m - m_all) for m in ms]
        l_all = sum(w * l_ref[pi, rw, :] for (pi, rw), w in zip(enumerate(rows), ws))
        acc = sum(w * acc_ref[pi, rw, :] for (pi, rw), w in zip(enumerate(rows), ws))
        onat[nat_rows, :] = acc / l_all
        return carry

    lax.fori_loop(0, ta // ATT_MERGE_ROWS, merge, 0)

    def gate(ci, carry):
        rows = pl.ds(pl.multiple_of(ci * ATT_MERGE_ROWS, ATT_MERGE_ROWS), ATT_MERGE_ROWS)
        o_ref[0, rows, :] = (onat[rows, :] * az_ref[0, rows, :].astype(F32)).astype(BF16)
        return carry

    lax.fori_loop(0, ta // ATT_MERGE_ROWS, gate, 0)


def _attention(aq, ak, av, az):
    b, s, w = aq.shape
    assert s % ATT_TILE == 0 and w % LANES == 0
    assert DILATIONS == (1, ATT_PERM, ATT_PERM * ATT_PERM)
    assert (ATT_TILE // ATT_PERM) % ATT_MERGE_ROWS == 0
    bias = _attn_bias()
    reg = ATT_TILE // ATT_PERM
    pair_spec = pl.BlockSpec((1, ATT_TILE, LANES), lambda bi, h, t: (bi, t, h))
    hist = pltpu.VMEM((ATT_BLOCK, LANES), F32)
    prm = pltpu.VMEM((2, ATT_PERM, reg, LANES), F32)
    stat = pltpu.VMEM((len(DILATIONS), ATT_TILE, LANES), F32)
    return pl.pallas_call(
        _attn_kernel,
        grid=(b, w // LANES, s // ATT_TILE),
        in_specs=[pl.BlockSpec(bias.shape, lambda bi, h, t: (0, 0, 0))] + [pair_spec] * 4,
        out_specs=pair_spec,
        out_shape=jax.ShapeDtypeStruct((b, s, w), BF16),
        scratch_shapes=[hist, hist, prm, prm, pltpu.VMEM((ATT_PERM, reg, LANES), F32),
                        stat, stat, stat, pltpu.VMEM((ATT_TILE, LANES), F32)],
        compiler_params=pltpu.CompilerParams(
            dimension_semantics=("parallel", "parallel", "arbitrary"),
            vmem_limit_bytes=VMEM_LIMIT),
        name="dilated_attention",
    )(bias, aq, ak, av, az)


HG_DIAG = 8
HG_LEVELS = (8, 16, 32)
SUBLANES = 8


def _hgrn_constants():
    c = HG_CHUNK
    t = np.arange(c)
    tri = (t[:, None] >= t[None, :]).astype(np.float32)
    first = (t // HG_DIAG) * HG_DIAG
    half = np.concatenate([tri, tri - 0.5 * (tri[first] + tri[first + HG_DIAG - 1])], axis=0)
    coef = np.concatenate([half, half], axis=1)
    ts, ss = t[:, None], t[None, :]
    level = np.full((c, c), len(HG_LEVELS) + 1, np.int32)
    level[(ts // HG_DIAG == ss // HG_DIAG) & (ts >= ss)] = 0
    for i, lb in enumerate(HG_LEVELS):
        m = ((ts // (2 * lb) == ss // (2 * lb)) & ((ts // lb) % 2 == 1)
             & ((ss // lb) % 2 == 0))
        level[m] = i + 1
    return jnp.asarray(coef, BF16), jnp.asarray(level)


def _hgrn_out_kernel(coef_ref, level_ref, x_ref, nw_ref, w_in32_ref, lbl_ref, gn_ref,
                     oat_ref, w32_ref, fw_ref, o_ref,
                     w_in_ref, w_ref, state_ref, qs_ref, g_ref, kk_ref, vi_ref, zs_ref,
                     qd_ref, oi_ref, kv_ref, dec_ref, ohg_ref):
    @pl.when(pl.program_id(1) == 0)
    def _():
        state_ref[...] = jnp.zeros_like(state_ref)
        w_in_ref[...] = w_in32_ref[0].astype(BF16)
        w_ref[...] = w32_ref[0].astype(BF16)

    lg = lbl_ref[...]
    ex = jnp.exp(lg - jnp.max(lg, axis=0, keepdims=True))
    lb = jnp.clip(ex[0:1, :] / jnp.sum(ex, axis=0, keepdims=True), 1e-6, 1.0 - 1e-6)
    f_mid = 0.5 * (1.0 + lb)
    f_amp = 0.5 * (1.0 - lb)

    def store_q(p):
        qs_ref[...] = _silu(p).astype(BF16)

    def store_f(p):
        f = f_mid + f_amp * jnp.tanh(0.5 * p)
        g2 = jnp.log2(f)
        hi_bits = lax.bitcast_convert_type(g2, jnp.uint32) & jnp.uint32(0xFFFF0000)
        g2_hi = lax.bitcast_convert_type(hi_bits, F32)
        g2_lo = (g2 - g2_hi).astype(BF16)
        g2_hi = g2_hi.astype(BF16)
        for h in range(HG_HEADS):
            g_ref[:, 2 * h * HG_DK:(2 * h + 1) * HG_DK] = g2_hi[:, h * HG_DK:(h + 1) * HG_DK]
            g_ref[:, (2 * h + 1) * HG_DK:(2 * h + 2) * HG_DK] = g2_lo[:, h * HG_DK:(h + 1) * HG_DK]
        kk_ref[...] = (1.0 - f).astype(BF16)

    def store_z(p):
        zs_ref[...] = _silu(p).astype(BF16)

    def store_i(p):
        vi_ref[...] = p.astype(BF16)

    _project_sections(_normalized(x_ref[0], nw_ref[...]), w_in_ref,
                      [(0, store_q), (1, store_f), (3, store_z), (2, store_i)])

    c = HG_CHUNK
    n_chunks = HG_ROWS // c
    half = HG_HEADS * HG_DK
    coef = coef_ref[...]
    level = level_ref[...]
    units = [(h, ci) for h in range(HG_HEADS) for ci in range(n_chunks)]

    def rows_of(ci):
        return pl.ds(ci * c, c)

    def cols_of(h, width=HG_DK):
        return pl.ds(h * width, width)

    def cum_decay(h, ci):
        g2 = g_ref[rows_of(ci), cols_of(h, 2 * HG_DK)]
        return jnp.dot(coef, jnp.concatenate([g2[:, :HG_DK], g2[:, HG_DK:]], axis=0),
                       preferred_element_type=F32)

    def level_operands(h, ci, be):
        q = qs_ref[rows_of(ci), cols_of(h)]
        k = kk_ref[rows_of(ci), cols_of(h)]
        scaled = lambda t, w: t * w.astype(BF16)
        b = be[0:c]
        e_diag = be[c:2 * c]
        ops = [(scaled(q, jnp.exp2(e_diag)), scaled(k, jnp.exp2(-e_diag)))]
        for lb_rows in HG_LEVELS:
            ref = jnp.concatenate(
                [jnp.broadcast_to(b[p + lb_rows - 1:p + lb_rows, :], (2 * lb_rows, HG_DK))
                 for p in range(0, c, 2 * lb_rows)], axis=0)
            w = jnp.exp2(_neg_abs(b - ref))
            ops.append((scaled(q, w), scaled(k, w)))
        b_last = b[c - 1:c, :]
        return ops, scaled(k, jnp.exp2(b_last - b)), scaled(q, jnp.exp2(b)), jnp.exp2(b_last)

    def intra_scores(ops):
        scores = jnp.where(level == 0, _dot_nt(*ops[0]), 0.0)
        for i in range(len(HG_LEVELS)):
            scores = jnp.where(level == i + 1, _dot_nt(*ops[i + 1]), scores)
        return scores.astype(BF16)

    stage1 = [cum_decay(h, ci) for h, ci in units]
    stage2 = [level_operands(h, ci, be) for (h, ci), be in zip(units, stage1)]

    o_ref[0] = x_ref[0] + jnp.dot(oat_ref[0], w_ref[half:2 * half, :],
                                  preferred_element_type=F32)

    stage3 = [intra_scores(ops) for ops, _, _, _ in stage2]
    for (h, ci), (_, k_dec, q_dec, decay), scores in zip(units, stage2, stage3):
        v = vi_ref[rows_of(ci), cols_of(h)]
        oi_ref[rows_of(ci), cols_of(h)] = jnp.dot(scores, v, preferred_element_type=F32)
        kv_ref[h, ci] = _dot_tn(v, k_dec)
        qd_ref[rows_of(ci), cols_of(h)] = q_dec
        dec_ref[h, ci] = jnp.broadcast_to(decay, dec_ref.shape[2:])

    states = [state_ref[h] for h in range(HG_HEADS)]
    for ci in range(n_chunks):
        for h in range(HG_HEADS):
            rows, cols = rows_of(ci), cols_of(h)
            o = oi_ref[rows, cols] + _dot_nt(qd_ref[rows, cols], states[h].astype(BF16))
            states[h] = states[h] * dec_ref[h, ci, 0:1, :] + kv_ref[h, ci]
            z = zs_ref[rows, cols].astype(F32)
            ms = jnp.mean(o * o, axis=-1, keepdims=True)
            ohg_ref[rows, cols] = (o * lax.rsqrt(ms + NORM_EPS) * gn_ref[:, cols] * z
                                   ).astype(BF16)
    for h in range(HG_HEADS):
        state_ref[h] = states[h]

    hres = o_ref[0] + jnp.dot(ohg_ref[...], w_ref[0:half, :], preferred_element_type=F32)
    ms = jnp.mean(hres * hres, axis=-1, keepdims=True)
    o_ref[0] = hres * lax.rsqrt(ms + NORM_EPS) * fw_ref[...]


def _hgrn_out(x, norm_w, w_in, lb_logits, hg_norm_w, oat, w_out, final_norm_w):
    b, s, d_model = x.shape
    w = HG_HEADS * HG_DK
    assert s % HG_ROWS == 0 and w == SECTION
    assert w_out.shape == (1, w + oat.shape[-1], d_model)
    coef, level = _hgrn_constants()
    n_chunks = HG_ROWS // HG_CHUNK
    tile = lambda width: pl.BlockSpec((1, HG_ROWS, width), lambda bi, t: (bi, t, 0))
    const = lambda a: pl.BlockSpec(a.shape, lambda bi, t: (0,) * a.ndim)
    act = lambda width: pltpu.VMEM((HG_ROWS, width), BF16)
    return pl.pallas_call(
        _hgrn_out_kernel,
        grid=(b, s // HG_ROWS),
        in_specs=[const(coef), const(level), tile(d_model), const(norm_w),
                  _branch_weight_spec(w_in, 0), const(lb_logits), const(hg_norm_w),
                  tile(oat.shape[-1]), const(w_out), const(final_norm_w)],
        out_specs=tile(d_model),
        out_shape=jax.ShapeDtypeStruct((b, s, d_model), F32),
        scratch_shapes=[pltpu.VMEM((d_model, 4 * SECTION), BF16),
                        pltpu.VMEM(w_out.shape[1:], BF16),
                        pltpu.VMEM((HG_HEADS, HG_DK, HG_DK), F32),
                        act(w), act(2 * w), act(w), act(w), act(w),
                        act(w),
                        pltpu.VMEM((HG_ROWS, w), F32),
                        pltpu.VMEM((HG_HEADS, n_chunks, HG_DK, HG_DK), F32),
                        pltpu.VMEM((HG_HEADS, n_chunks, SUBLANES, HG_DK), F32),
                        act(w)],
        compiler_params=pltpu.CompilerParams(
            dimension_semantics=("parallel", "arbitrary"),
            vmem_limit_bytes=VMEM_LIMIT),
        name="hgrn2_branch_and_output",
    )(coef, level, x, norm_w, w_in, lb_logits, hg_norm_w, oat, w_out, final_norm_w)


def kernel(x, norm_w, w_in, hgrn_lb_logits, hg_norm_w, w_out, final_norm_w):
    b, s, d_model = x.shape
    assert norm_w.shape[0] == 1 and w_in.shape[0] == 1 and w_out.shape[0] == 1
    aq, ak, av, az = _attn_projection(x.reshape(b * s, d_model), norm_w, w_in, s)
    to3 = lambda a: a.reshape(b, s, a.shape[-1])
    oat = _attention(to3(aq), to3(ak), to3(av), to3(az))
    return _hgrn_out(x, norm_w, w_in, hgrn_lb_logits, hg_norm_w, oat, w_out,
                     final_norm_w.reshape(1, d_model))
```

```python
import functools

import numpy as np
import jax
import jax.numpy as jnp
from jax import lax
from jax.experimental import pallas as pl
from jax.experimental.pallas import tpu as pltpu

F32 = jnp.float32
BF16 = jnp.bfloat16

NORM_EPS = 1e-6
ROPE_THETA = 10000.0
LANES = 128

HG_HEADS = 4
HG_DK = 128
HG_CHUNK = 64
AT_HEAD_DIM = 64
SECTION = 512
DILATIONS = (1, 4, 16)
ATT_BLOCK = 128
ATT_TILE = 2 * ATT_BLOCK * max(DILATIONS)
NEG_BIG = -1e30

PROJ_ROWS = 512
HG_ROWS = 512
ATT_PERM = 4
ATT_MERGE_ROWS = 512
LOG2E = 1.4426950408889634
VMEM_LIMIT = 56 * 1024 * 1024


def _normalized(x, gain):
    ms = jnp.mean(x * x, axis=-1, keepdims=True)
    return (x * lax.rsqrt(ms + NORM_EPS) * gain).astype(BF16)


def _silu(p):
    return p * (0.5 * jnp.tanh(0.5 * p) + 0.5)


def _project_sections(u, w_ref, plan):
    pending = None
    for j, finish in plan:
        p = jnp.dot(u, w_ref[:, j * SECTION:(j + 1) * SECTION], preferred_element_type=F32)
        if pending is not None:
            pending()
        pending = functools.partial(finish, p)
    pending()


def _dot_nt(a, b):
    return lax.dot_general(a, b, (((1,), (1,)), ((), ())), preferred_element_type=F32)


def _dot_tn(a, b):
    return lax.dot_general(a, b, (((0,), (0,)), ((), ())), preferred_element_type=F32)


def _neg_abs(x):
    bits = lax.bitcast_convert_type(x, jnp.uint32) | jnp.uint32(0x80000000)
    return lax.bitcast_convert_type(bits, F32)


def _attn_proj_kernel(x_ref, nw_ref, w32_ref, cos_ref, sina_ref, sinb_ref,
                      aq_ref, ak_ref, av_ref, az_ref, w_ref):
    @pl.when(pl.program_id(0) == 0)
    def _():
        w_ref[...] = w32_ref[0].astype(BF16)

    u = _normalized(x_ref[...], nw_ref[...])

    def store_rope(dst_ref, scale, p):
        half = AT_HEAD_DIM // 2
        for c in range(SECTION // LANES):
            xc = p[:, c * LANES:(c + 1) * LANES]
            r = (xc * cos_ref[...] + pltpu.roll(xc, LANES - half, 1) * sina_ref[...]
                 + pltpu.roll(xc, half, 1) * sinb_ref[...])
            dst_ref[:, c * LANES:(c + 1) * LANES] = r * scale

    def store_z(p):
        az_ref[...] = _silu(p).astype(BF16)

    def store_v(p):
        av_ref[...] = p

    _project_sections(u, w_ref, [
        (3, store_z),
        (0, functools.partial(store_rope, aq_ref, AT_HEAD_DIM ** -0.5 * LOG2E)),
        (1, functools.partial(store_rope, ak_ref, 1.0)),
        (2, store_v)])


@functools.lru_cache(maxsize=None)
def _rope_tables(seq):
    half = AT_HEAD_DIM // 2
    inv_freq = 1.0 / (ROPE_THETA ** (np.arange(half, dtype=np.float64) / half))
    ang = np.arange(seq, dtype=np.float64)[:, None] * inv_freq[None, :]
    cos = np.cos(ang)
    sin = np.sin(ang)
    zero = np.zeros_like(sin)
    reps = LANES // AT_HEAD_DIM
    cos_t = np.tile(np.concatenate([cos, cos], axis=1), (1, reps))
    sina_t = np.tile(np.concatenate([-sin, zero], axis=1), (1, reps))
    sinb_t = np.tile(np.concatenate([zero, sin], axis=1), (1, reps))
    return tuple(np.asarray(t, np.float32) for t in (cos_t, sina_t, sinb_t))


def _branch_weight_spec(w_in, branch):
    assert w_in.shape[0] == 1 and w_in.shape[2] == 8 * SECTION
    return pl.BlockSpec((1, w_in.shape[1], 4 * SECTION),
                        lambda *_: (0, 0, branch), pipeline_mode=pl.Buffered(1))


def _attn_projection(x2, norm_w, w_in, seq):
    rows, d_model = x2.shape
    assert rows % PROJ_ROWS == 0 and seq % PROJ_ROWS == 0
    tables = _rope_tables(seq)
    seq_tiles = seq // PROJ_ROWS
    row_spec = lambda w: pl.BlockSpec((PROJ_ROWS, w), lambda i: (i, 0))
    tab_spec = pl.BlockSpec((PROJ_ROWS, LANES), lambda i: (i % seq_tiles, 0))
    full = lambda a: pl.BlockSpec(a.shape, lambda i: (0, 0))
    out_dtypes = (F32, F32, F32, BF16)
    return pl.pallas_call(
        _attn_proj_kernel,
        grid=(rows // PROJ_ROWS,),
        in_specs=[row_spec(d_model), full(norm_w), _branch_weight_spec(w_in, 1)]
                 + [tab_spec] * 3,
        out_specs=[row_spec(SECTION)] * len(out_dtypes),
        out_shape=[jax.ShapeDtypeStruct((rows, SECTION), dt) for dt in out_dtypes],
        scratch_shapes=[pltpu.VMEM((d_model, 4 * SECTION), BF16)],
        compiler_params=pltpu.CompilerParams(
            dimension_semantics=("arbitrary",), vmem_limit_bytes=VMEM_LIMIT),
        name="attention_projection",
    )(x2, norm_w, w_in, *tables)


def _attn_bias():
    qi = np.arange(ATT_BLOCK)[:, None]
    kj = np.arange(2 * ATT_BLOCK)[None, :]
    dist = ATT_BLOCK + qi - kj
    valid = (dist >= 0) & (dist <= ATT_BLOCK)
    b0 = np.where(valid, 0.0, NEG_BIG)
    b1 = np.where(valid & (kj >= ATT_BLOCK), 0.0, NEG_BIG)
    return jnp.asarray(np.stack([b0, b1]), F32)


def _attn_kernel(bias_ref, aq_ref, ak_ref, av_ref, az_ref, o_ref,
                 khist, vhist, kprm, vprm, qprm, m_ref, l_ref, acc_ref, onat):
    ta = ATT_TILE
    blk = ATT_BLOCK
    nph = ATT_PERM
    reg = ta // nph
    tile = pl.program_id(2)
    cur = tile % 2
    prev = 1 - cur

    @pl.when(tile == 0)
    def _():
        khist[...] = jnp.zeros_like(khist)
        vhist[...] = jnp.zeros_like(vhist)
        kprm[1] = jnp.zeros((nph, reg, LANES), F32)
        vprm[1] = jnp.zeros((nph, reg, LANES), F32)

    for r in range(nph):
        kprm[cur, r] = ak_ref[0, pl.ds(r, reg, stride=nph), :]
        vprm[cur, r] = av_ref[0, pl.ds(r, reg, stride=nph), :]
        qprm[r] = aq_ref[0, pl.ds(r, reg, stride=nph), :]

    first_head = lax.broadcasted_iota(jnp.int32, (blk, LANES), 1) < AT_HEAD_DIM
    ones = jnp.ones((2 * blk, LANES), BF16)
    bias_any = bias_ref[0]
    bias_first = bias_ref[(tile == 0).astype(jnp.int32)]

    def block_stats(q, k, v, bias):
        k = k.astype(BF16)
        v = v.astype(BF16)
        q2 = jnp.concatenate([jnp.where(first_head, q, 0.0),
                              jnp.where(first_head, 0.0, q)], axis=0).astype(BF16)
        s = _dot_nt(q2, k) + jnp.concatenate([bias, bias], axis=0)
        m = jnp.max(s, axis=-1, keepdims=True)
        p = jnp.exp2(s - m).astype(BF16)
        pv = jnp.dot(p, jnp.concatenate([v, ones], axis=1), preferred_element_type=F32)
        return (jnp.where(first_head, m[:blk], m[blk:]),
                jnp.where(first_head, pv[:blk, LANES:], pv[blk:, LANES:]),
                jnp.where(first_head, pv[:blk, :LANES], pv[blk:, :LANES]))

    def store_stats(pi, rows, stats):
        m_ref[pi, rows, :], l_ref[pi, rows, :], acc_ref[pi, rows, :] = stats

    def window(first_ref, first_rows, ref, rows):
        return jnp.concatenate([first_ref[first_rows], ref[rows]], axis=0)

    for j in range(ta // blk):
        q_rows = pl.ds(j * blk, blk)
        if j == 0:
            k = window(khist, (slice(None),) * 2, ak_ref, (0, q_rows))
            v = window(vhist, (slice(None),) * 2, av_ref, (0, q_rows))
        else:
            k_rows = pl.ds((j - 1) * blk, 2 * blk)
            k, v = ak_ref[0, k_rows, :], av_ref[0, k_rows, :]
        store_stats(0, q_rows, block_stats(aq_ref[0, q_rows, :], k, v,
                                           bias_first if j == 0 else bias_any))

    khist[...] = ak_ref[0, ta - blk:ta, :]
    vhist[...] = av_ref[0, ta - blk:ta, :]

    for r in range(nph):
        for j in range(reg // blk):
            q_rows = pl.ds(j * blk, blk)
            if j == 0:
                last = pl.ds(reg - blk, blk)
                k = window(kprm, (prev, r, last), kprm, (cur, r, q_rows))
                v = window(vprm, (prev, r, last), vprm, (cur, r, q_rows))
            else:
                k_rows = pl.ds((j - 1) * blk, 2 * blk)
                k, v = kprm[cur, r, k_rows, :], vprm[cur, r, k_rows, :]
            stats = block_stats(qprm[r, q_rows, :], k, v, bias_first if j == 0 else bias_any)
            store_stats(1, pl.ds(r * reg + j * blk, blk), stats)
        sub = reg // nph
        for r16 in range(nph):
            for j in range(sub // blk):
                q_rows = pl.ds(r16 + nph * blk * j, blk, stride=nph)
                if j == 0:
                    tail = pl.ds(r16 + nph * (sub - blk), blk, stride=nph)
                    k = window(kprm, (prev, r, tail), kprm, (cur, r, q_rows))
                    v = window(vprm, (prev, r, tail), vprm, (cur, r, q_rows))
                else:
                    k_rows = pl.ds(r16 + nph * blk * (j - 1), 2 * blk, stride=nph)
                    k, v = kprm[cur, r, k_rows, :], vprm[cur, r, k_rows, :]
                stats = block_stats(qprm[r, q_rows, :], k, v,
                                    bias_first if j == 0 else bias_any)
                store_stats(2, pl.ds(r * reg + r16 + nph * blk * j, blk, stride=nph), stats)

    def merge(ci, carry):
        pieces = reg // ATT_MERGE_ROWS
        r = ci // pieces
        i0 = (ci % pieces) * ATT_MERGE_ROWS
        nat_rows = pl.ds(r + nph * i0, ATT_MERGE_ROWS, stride=nph)
        prm_rows = pl.ds(pl.multiple_of(ci * ATT_MERGE_ROWS, ATT_MERGE_ROWS), ATT_MERGE_ROWS)
        rows = (nat_rows, prm_rows, prm_rows)
        ms = [m_ref[pi, rw, :] for pi, rw in enumerate(rows)]
        m_all = functools.reduce(jnp.maximum, ms)
        ws = [jnp.exp2(m - m_all) for m in ms]
        l_all = sum(w * l_ref[pi, rw, :] for (pi, rw), w in zip(enumerate(rows), ws))
        acc = sum(w * acc_ref[pi, rw, :] for (pi, rw), w in zip(enumerate(rows), ws))
        onat[nat_rows, :] = acc / l_all
        return carry

    lax.fori_loop(0, ta // ATT_MERGE_ROWS, merge, 0)

    def gate(ci, carry):
        rows = pl.ds(pl.multiple_of(ci * ATT_MERGE_ROWS, ATT_MERGE_ROWS), ATT_MERGE_ROWS)
        o_ref[0, rows, :] = (onat[rows, :] * az_ref[0, rows, :].astype(F32)).astype(BF16)
        return carry

    lax.fori_loop(0, ta // ATT_MERGE_ROWS, gate, 0)


def _attention(aq, ak, av, az):
    b, s, w = aq.shape
    assert s % ATT_TILE == 0 and w % LANES == 0
    assert DILATIONS == (1, ATT_PERM, ATT_PERM * ATT_PERM)
    assert (ATT_TILE // ATT_PERM) % ATT_MERGE_ROWS == 0
    bias = _attn_bias()
    reg = ATT_TILE // ATT_PERM
    pair_spec = pl.BlockSpec((1, ATT_TILE, LANES), lambda bi, h, t: (bi, t, h))
    hist = pltpu.VMEM((ATT_BLOCK, LANES), F32)
    prm = pltpu.VMEM((2, ATT_PERM, reg, LANES), F32)
    stat = pltpu.VMEM((len(DILATIONS), ATT_TILE, LANES), F32)
    return pl.pallas_call(
        _attn_kernel,
        grid=(b, w // LANES, s // ATT_TILE),
        in_specs=[pl.BlockSpec(bias.shape, lambda bi, h, t: (0, 0, 0))] + [pair_spec] * 4,
        out_specs=pair_spec,
        out_shape=jax.ShapeDtypeStruct((b, s, w), BF16),
        scratch_shapes=[hist, hist, prm, prm, pltpu.VMEM((ATT_PERM, reg, LANES), F32),
                        stat, stat, stat, pltpu.VMEM((ATT_TILE, LANES), F32)],
        compiler_params=pltpu.CompilerParams(
            dimension_semantics=("parallel", "parallel", "arbitrary"),
            vmem_limit_bytes=VMEM_LIMIT),
        name="dilated_attention",
    )(bias, aq, ak, av, az)


HG_DIAG = 8
HG_LEVELS = (8, 16, 32)
SUBLANES = 8


def _hgrn_constants():
    c = HG_CHUNK
    t = np.arange(c)
    tri = (t[:, None] >= t[None, :]).astype(np.float32)
    first = (t // HG_DIAG) * HG_DIAG
    half = np.concatenate([tri, tri - 0.5 * (tri[first] + tri[first + HG_DIAG - 1])], axis=0)
    coef = np.concatenate([half, half], axis=1)
    ts, ss = t[:, None], t[None, :]
    level = np.full((c, c), len(HG_LEVELS) + 1, np.int32)
    level[(ts // HG_DIAG == ss // HG_DIAG) & (ts >= ss)] = 0
    for i, lb in enumerate(HG_LEVELS):
        m = ((ts // (2 * lb) == ss // (2 * lb)) & ((ts // lb) % 2 == 1)
             & ((ss // lb) % 2 == 0))
        level[m] = i + 1
    return jnp.asarray(coef, BF16), jnp.asarray(level)


def _hgrn_out_kernel(coef_ref, level_ref, x_ref, nw_ref, w_in32_ref, lbl_ref, gn_ref,
                     oat_ref, w32_ref, fw_ref, o_ref,
                     w_in_ref, w_ref, state_ref, qs_ref, g_ref, kk_ref, vi_ref, zs_ref,
                     qd_ref, oi_ref, kv_ref, dec_ref, ohg_ref):
    @pl.when(pl.program_id(1) == 0)
    def _():
        state_ref[...] = jnp.zeros_like(state_ref)
        w_in_ref[...] = w_in32_ref[0].astype(BF16)
        w_ref[...] = w32_ref[0].astype(BF16)

    lg = lbl_ref[...]
    ex = jnp.exp(lg - jnp.max(lg, axis=0, keepdims=True))
    lb = jnp.clip(ex[0:1, :] / jnp.sum(ex, axis=0, keepdims=True), 1e-6, 1.0 - 1e-6)
    f_mid = 0.5 * (1.0 + lb)
    f_amp = 0.5 * (1.0 - lb)

    def store_q(p):
        qs_ref[...] = _silu(p).astype(BF16)

    def store_f(p):
        f = f_mid + f_amp * jnp.tanh(0.5 * p)
        g2 = jnp.log2(f)
        hi_bits = lax.bitcast_convert_type(g2, jnp.uint32) & jnp.uint32(0xFFFF0000)
        g2_hi = lax.bitcast_convert_type(hi_bits, F32)
        g2_lo = (g2 - g2_hi).astype(BF16)
        g2_hi = g2_hi.astype(BF16)
        for h in range(HG_HEADS):
            g_ref[:, 2 * h * HG_DK:(2 * h + 1) * HG_DK] = g2_hi[:, h * HG_DK:(h + 1) * HG_DK]
            g_ref[:, (2 * h + 1) * HG_DK:(2 * h + 2) * HG_DK] = g2_lo[:, h * HG_DK:(h + 1) * HG_DK]
        kk_ref[...] = (1.0 - f).astype(BF16)

    def store_z(p):
        zs_ref[...] = _silu(p).astype(BF16)

    def store_i(p):
        vi_ref[...] = p.astype(BF16)

    _project_sections(_normalized(x_ref[0], nw_ref[...]), w_in_ref,
                      [(0, store_q), (1, store_f), (3, store_z), (2, store_i)])

    c = HG_CHUNK
    n_chunks = HG_ROWS // c
    half = HG_HEADS * HG_DK
    coef = coef_ref[...]
    level = level_ref[...]
    units = [(h, ci) for h in range(HG_HEADS) for ci in range(n_chunks)]

    def rows_of(ci):
        return pl.ds(ci * c, c)

    def cols_of(h, width=HG_DK):
        return pl.ds(h * width, width)

    def cum_decay(h, ci):
        g2 = g_ref[rows_of(ci), cols_of(h, 2 * HG_DK)]
        return jnp.dot(coef, jnp.concatenate([g2[:, :HG_DK], g2[:, HG_DK:]], axis=0),
                       preferred_element_type=F32)

    def level_operands(h, ci, be):
        q = qs_ref[rows_of(ci), cols_of(h)]
        k = kk_ref[rows_of(ci), cols_of(h)]
        scaled = lambda t, w: t * w.astype(BF16)
        b = be[0:c]
        e_diag = be[c:2 * c]
        ops = [(scaled(q, jnp.exp2(e_diag)), scaled(k, jnp.exp2(-e_diag)))]
        for lb_rows in HG_LEVELS:
            ref = jnp.concatenate(
                [jnp.broadcast_to(b[p + lb_rows - 1:p + lb_rows, :], (2 * lb_rows, HG_DK))
                 for p in range(0, c, 2 * lb_rows)], axis=0)
            w = jnp.exp2(_neg_abs(b - ref))
            ops.append((scaled(q, w), scaled(k, w)))
        b_last = b[c - 1:c, :]
        return ops, scaled(k, jnp.exp2(b_last - b)), scaled(q, jnp.exp2(b)), jnp.exp2(b_last)

    def intra_scores(ops):
        scores = jnp.where(level == 0, _dot_nt(*ops[0]), 0.0)
        for i in range(len(HG_LEVELS)):
            scores = jnp.where(level == i + 1, _dot_nt(*ops[i + 1]), scores)
        return scores.astype(BF16)

    stage1 = [cum_decay(h, ci) for h, ci in units]
    stage2 = [level_operands(h, ci, be) for (h, ci), be in zip(units, stage1)]

    o_ref[0] = x_ref[0] + jnp.dot(oat_ref[0], w_ref[half:2 * half, :],
                                  preferred_element_type=F32)

    stage3 = [intra_scores(ops) for ops, _, _, _ in stage2]
    for (h, ci), (_, k_dec, q_dec, decay), scores in zip(units, stage2, stage3):
        v = vi_ref[rows_of(ci), cols_of(h)]
        oi_ref[rows_of(ci), cols_of(h)] = jnp.dot(scores, v, preferred_element_type=F32)
        kv_ref[h, ci] = _dot_tn(v, k_dec)
        qd_ref[rows_of(ci), cols_of(h)] = q_dec
        dec_ref[h, ci] = jnp.broadcast_to(decay, dec_ref.shape[2:])

    states = [state_ref[h] for h in range(HG_HEADS)]
    for ci in range(n_chunks):
        for h in range(HG_HEADS):
            rows, cols = rows_of(ci), cols_of(h)
            o = oi_ref[rows, cols] + _dot_nt(qd_ref[rows, cols], states[h].astype(BF16))
            states[h] = states[h] * dec_ref[h, ci, 0:1, :] + kv_ref[h, ci]
            z = zs_ref[rows, cols].astype(F32)
            ms = jnp.mean(o * o, axis=-1, keepdims=True)
            ohg_ref[rows, cols] = (o * lax.rsqrt(ms + NORM_EPS) * gn_ref[:, cols] * z
                                   ).astype(BF16)
    for h in range(HG_HEADS):
        state_ref[h] = states[h]

    hres = o_ref[0] + jnp.dot(ohg_ref[...], w_ref[0:half, :], preferred_element_type=F32)
    ms = jnp.mean(hres * hres, axis=-1, keepdims=True)
    o_ref[0] = hres * lax.rsqrt(ms + NORM_EPS) * fw_ref[...]


def _hgrn_out(x, norm_w, w_in, lb_logits, hg_norm_w, oat, w_out, final_norm_w):
    b, s, d_model = x.shape
    w = HG_HEADS * HG_DK
    assert s % HG_ROWS == 0 and w == SECTION
    assert w_out.shape == (1, w + oat.shape[-1], d_model)
    coef, level = _hgrn_constants()
    n_chunks = HG_ROWS // HG_CHUNK
    tile = lambda width: pl.BlockSpec((1, HG_ROWS, width), lambda bi, t: (bi, t, 0))
    const = lambda a: pl.BlockSpec(a.shape, lambda bi, t: (0,) * a.ndim)
    act = lambda width: pltpu.VMEM((HG_ROWS, width), BF16)
    return pl.pallas_call(
        _hgrn_out_kernel,
        grid=(b, s // HG_ROWS),
        in_specs=[const(coef), const(level), tile(d_model), const(norm_w),
                  _branch_weight_spec(w_in, 0), const(lb_logits), const(hg_norm_w),
                  tile(oat.shape[-1]), const(w_out), const(final_norm_w)],
        out_specs=tile(d_model),
        out_shape=jax.ShapeDtypeStruct((b, s, d_model), F32),
        scratch_shapes=[pltpu.VMEM((d_model, 4 * SECTION), BF16),
                        pltpu.VMEM(w_out.shape[1:], BF16),
                        pltpu.VMEM((HG_HEADS, HG_DK, HG_DK), F32),
                        act(w), act(2 * w), act(w), act(w), act(w),
                        act(w),
                        pltpu.VMEM((HG_ROWS, w), F32),
                        pltpu.VMEM((HG_HEADS, n_chunks, HG_DK, HG_DK), F32),
                        pltpu.VMEM((HG_HEADS, n_chunks, SUBLANES, HG_DK), F32),
                        act(w)],
        compiler_params=pltpu.CompilerParams(
            dimension_semantics=("parallel", "arbitrary"),
            vmem_limit_bytes=VMEM_LIMIT),
        name="hgrn2_branch_and_output",
    )(coef, level, x, norm_w, w_in, lb_logits, hg_norm_w, oat, w_out, final_norm_w)


def kernel(x, norm_w, w_in, hgrn_lb_logits, hg_norm_w, w_out, final_norm_w):
    b, s, d_model = x.shape
    assert norm_w.shape[0] == 1 and w_in.shape[0] == 1 and w_out.shape[0] == 1
    aq, ak, av, az = _attn_projection(x.reshape(b * s, d_model), norm_w, w_in, s)
    to3 = lambda a: a.reshape(b, s, a.shape[-1])
    oat = _attention(to3(aq), to3(ak), to3(av), to3(az))
    return _hgrn_out(x, norm_w, w_in, hgrn_lb_logits, hg_norm_w, oat, w_out,
                     final_norm_w.reshape(1, d_model))
```

```python
import functools

import numpy as np
import jax
import jax.numpy as jnp
from jax import lax
from jax.experimental import pallas as pl
from jax.experimental.pallas import tpu as pltpu

F32 = jnp.float32
BF16 = jnp.bfloat16

NORM_EPS = 1e-6
ROPE_THETA = 10000.0
LANES = 128

HG_HEADS = 4
HG_DK = 128
HG_CHUNK = 64
AT_HEAD_DIM = 64
SECTION = 512
DILATIONS = (1, 4, 16)
ATT_BLOCK = 128
ATT_TILE = 2 * ATT_BLOCK * max(DILATIONS)
NEG_BIG = -1e30

PROJ_ROWS = 1024
HG_ROWS = 512
ATT_PERM = 4
ATT_MERGE_ROWS = 1024
LOG2E = 1.4426950408889634
VMEM_LIMIT = 56 * 1024 * 1024


def _normalized(x, gain):
    ms = jnp.mean(x * x, axis=-1, keepdims=True)
    return (x * lax.rsqrt(ms + NORM_EPS) * gain).astype(BF16)


def _silu(p):
    return p * (0.5 * jnp.tanh(0.5 * p) + 0.5)


def _project_sections(u, w_ref, plan):
    pending = None
    for j, finish in plan:
        p = jnp.dot(u, w_ref[:, j * SECTION:(j + 1) * SECTION], preferred_element_type=F32)
        if pending is not None:
            pending()
        pending = functools.partial(finish, p)
    pending()


def _dot_nt(a, b):
    return lax.dot_general(a, b, (((1,), (1,)), ((), ())), preferred_element_type=F32)


def _dot_tn(a, b):
    return lax.dot_general(a, b, (((0,), (0,)), ((), ())), preferred_element_type=F32)


def _neg_abs(x):
    bits = lax.bitcast_convert_type(x, jnp.uint32) | jnp.uint32(0x80000000)
    return lax.bitcast_convert_type(bits, F32)


def _attn_proj_kernel(x_ref, nw_ref, w32_ref, cos_ref, sina_ref, sinb_ref,
                      aq_ref, ak_ref, av_ref, az_ref, w_ref):
    @pl.when(pl.program_id(0) == 0)
    def _():
        w_ref[...] = w32_ref[0].astype(BF16)

    u = _normalized(x_ref[...], nw_ref[...])

    def store_rope(dst_ref, scale, p):
        half = AT_HEAD_DIM // 2
        for c in range(SECTION // LANES):
            xc = p[:, c * LANES:(c + 1) * LANES]
            r = (xc * cos_ref[...] + pltpu.roll(xc, LANES - half, 1) * sina_ref[...]
                 + pltpu.roll(xc, half, 1) * sinb_ref[...])
            dst_ref[:, c * LANES:(c + 1) * LANES] = r * scale

    def store_z(p):
        az_ref[...] = _silu(p).astype(BF16)

    def store_v(p):
        av_ref[...] = p

    _project_sections(u, w_ref, [
        (3, store_z),
        (0, functools.partial(store_rope, aq_ref, AT_HEAD_DIM ** -0.5 * LOG2E)),
        (1, functools.partial(store_rope, ak_ref, 1.0)),
        (2, store_v)])


@functools.lru_cache(maxsize=None)
def _rope_tables(seq):
    half = AT_HEAD_DIM // 2
    inv_freq = 1.0 / (ROPE_THETA ** (np.arange(half, dtype=np.float64) / half))
    ang = np.arange(seq, dtype=np.float64)[:, None] * inv_freq[None, :]
    cos = np.cos(ang)
    sin = np.sin(ang)
    zero = np.zeros_like(sin)
    reps = LANES // AT_HEAD_DIM
    cos_t = np.tile(np.concatenate([cos, cos], axis=1), (1, reps))
    sina_t = np.tile(np.concatenate([-sin, zero], axis=1), (1, reps))
    sinb_t = np.tile(np.concatenate([zero, sin], axis=1), (1, reps))
    return tuple(np.asarray(t, np.float32) for t in (cos_t, sina_t, sinb_t))


def _branch_weight_spec(w_in, branch):
    assert w_in.shape[0] == 1 and w_in.shape[2] == 8 * SECTION
    return pl.BlockSpec((1, w_in.shape[1], 4 * SECTION),
                        lambda *_: (0, 0, branch), pipeline_mode=pl.Buffered(1))


def _attn_projection(x2, norm_w, w_in, seq):
    rows, d_model = x2.shape
    assert rows % PROJ_ROWS == 0 and seq % PROJ_ROWS == 0
    tables = _rope_tables(seq)
    seq_tiles = seq // PROJ_ROWS
    row_spec = lambda w: pl.BlockSpec((PROJ_ROWS, w), lambda i: (i, 0))
    tab_spec = pl.BlockSpec((PROJ_ROWS, LANES), lambda i: (i % seq_tiles, 0))
    full = lambda a: pl.BlockSpec(a.shape, lambda i: (0, 0))
    out_dtypes = (F32, F32, F32, BF16)
    return pl.pallas_call(
        _attn_proj_kernel,
        grid=(rows // PROJ_ROWS,),
        in_specs=[row_spec(d_model), full(norm_w), _branch_weight_spec(w_in, 1)]
                 + [tab_spec] * 3,
        out_specs=[row_spec(SECTION)] * len(out_dtypes),
        out_shape=[jax.ShapeDtypeStruct((rows, SECTION), dt) for dt in out_dtypes],
        scratch_shapes=[pltpu.VMEM((d_model, 4 * SECTION), BF16)],
        compiler_params=pltpu.CompilerParams(
            dimension_semantics=("arbitrary",), vmem_limit_bytes=VMEM_LIMIT),
        name="attention_projection",
    )(x2, norm_w, w_in, *tables)


def _attn_bias():
    qi = np.arange(ATT_BLOCK)[:, None]
    kj = np.arange(2 * ATT_BLOCK)[None, :]
    dist = ATT_BLOCK + qi - kj
    valid = (dist >= 0) & (dist <= ATT_BLOCK)
    b0 = np.where(valid, 0.0, NEG_BIG)
    b1 = np.where(valid & (kj >= ATT_BLOCK), 0.0, NEG_BIG)
    return jnp.asarray(np.stack([b0, b1]), F32)


def _attn_kernel(bias_ref, aq_ref, ak_ref, av_ref, az_ref, o_ref,
                 khist, vhist, kprm, vprm, qprm, m_ref, l_ref, acc_ref, onat):
    ta = ATT_TILE
    blk = ATT_BLOCK
    nph = ATT_PERM
    reg = ta // nph
    tile = pl.program_id(2)
    cur = tile % 2
    prev = 1 - cur

    @pl.when((pl.program_id(0) == 0) & (pl.program_id(1) == 0) & (tile == 0))
    def _():
        khist[...] = jnp.zeros_like(khist)
        vhist[...] = jnp.zeros_like(vhist)
        kprm[1] = jnp.zeros((nph, reg, LANES), F32)
        vprm[1] = jnp.zeros((nph, reg, LANES), F32)

    for r in range(nph):
        kprm[cur, r] = ak_ref[0, pl.ds(r, reg, stride=nph), :]
        vprm[cur, r] = av_ref[0, pl.ds(r, reg, stride=nph), :]
        qprm[r] = aq_ref[0, pl.ds(r, reg, stride=nph), :]

    first_head = lax.broadcasted_iota(jnp.int32, (blk, LANES), 1) < AT_HEAD_DIM
    ones = jnp.ones((2 * blk, LANES), BF16)
    bias_any = bias_ref[0]
    bias_first = bias_ref[(tile == 0).astype(jnp.int32)]

    def block_stats(q, k, v, bias):
        k = k.astype(BF16)
        v = v.astype(BF16)
        q2 = jnp.concatenate([jnp.where(first_head, q, 0.0),
                              jnp.where(first_head, 0.0, q)], axis=0).astype(BF16)
        s = _dot_nt(q2, k) + jnp.concatenate([bias, bias], axis=0)
        m = jnp.max(s, axis=-1, keepdims=True)
        p = jnp.exp2(s - m).astype(BF16)
        pv = jnp.dot(p, jnp.concatenate([v, ones], axis=1), preferred_element_type=F32)
        return (jnp.where(first_head, m[:blk], m[blk:]),
                jnp.where(first_head, pv[:blk, LANES:], pv[blk:, LANES:]),
                jnp.where(first_head, pv[:blk, :LANES], pv[blk:, :LANES]))

    def store_stats(pi, rows, stats):
        m_ref[pi, rows, :], l_ref[pi, rows, :], acc_ref[pi, rows, :] = stats

    def window(first_ref, first_rows, ref, rows):
        return jnp.concatenate([first_ref[first_rows], ref[rows]], axis=0)

    for j in range(ta // blk):
        q_rows = pl.ds(j * blk, blk)
        if j == 0:
            k = window(khist, (slice(None),) * 2, ak_ref, (0, q_rows))
            v = window(vhist, (slice(None),) * 2, av_ref, (0, q_rows))
        else:
            k_rows = pl.ds((j - 1) * blk, 2 * blk)
            k, v = ak_ref[0, k_rows, :], av_ref[0, k_rows, :]
        store_stats(0, q_rows, block_stats(aq_ref[0, q_rows, :], k, v,
                                           bias_first if j == 0 else bias_any))

    khist[...] = ak_ref[0, ta - blk:ta, :]
    vhist[...] = av_ref[0, ta - blk:ta, :]

    for r in range(nph):
        for j in range(reg // blk):
            q_rows = pl.ds(j * blk, blk)
            if j == 0:
                last = pl.ds(reg - blk, blk)
                k = window(kprm, (prev, r, last), kprm, (cur, r, q_rows))
                v = window(vprm, (prev, r, last), vprm, (cur, r, q_rows))
            else:
                k_rows = pl.ds((j - 1) * blk, 2 * blk)
                k, v = kprm[cur, r, k_rows, :], vprm[cur, r, k_rows, :]
            stats = block_stats(qprm[r, q_rows, :], k, v, bias_first if j == 0 else bias_any)
            store_stats(1, pl.ds(r * reg + j * blk, blk), stats)
        sub = reg // nph
        for r16 in range(nph):
            for j in range(sub // blk):
                q_rows = pl.ds(r16 + nph * blk * j, blk, stride=nph)
                if j == 0:
                    tail = pl.ds(r16 + nph * (sub - blk), blk, stride=nph)
                    k = window(kprm, (prev, r, tail), kprm, (cur, r, q_rows))
                    v = window(vprm, (prev, r, tail), vprm, (cur, r, q_rows))
                else:
                    k_rows = pl.ds(r16 + nph * blk * (j - 1), 2 * blk, stride=nph)
                    k, v = kprm[cur, r, k_rows, :], vprm[cur, r, k_rows, :]
                stats = block_stats(qprm[r, q_rows, :], k, v,
                                    bias_first if j == 0 else bias_any)
                store_stats(2, pl.ds(r * reg + r16 + nph * blk * j, blk, stride=nph), stats)

    def merge(ci, carry):
        pieces = reg // ATT_MERGE_ROWS
        r = ci // pieces
        i0 = (ci % pieces) * ATT_MERGE_ROWS
        nat_rows = pl.ds(r + nph * i0, ATT_MERGE_ROWS, stride=nph)
        prm_rows = pl.ds(pl.multiple_of(ci * ATT_MERGE_ROWS, ATT_MERGE_ROWS), ATT_MERGE_ROWS)
        rows = (nat_rows, prm_rows, prm_rows)
        ms = [m_ref[pi, rw, :] for pi, rw in enumerate(rows)]
        m_all = functools.reduce(jnp.maximum, ms)
        ws = [jnp.exp2(m - m_all) for m in ms]
        l_all = sum(w * l_ref[pi, rw, :] for (pi, rw), w in zip(enumerate(rows), ws))
        acc = sum(w * acc_ref[pi, rw, :] for (pi, rw), w in zip(enumerate(rows), ws))
        onat[nat_rows, :] = acc / l_all
        return carry

    lax.fori_loop(0, ta // ATT_MERGE_ROWS, merge, 0)

    def gate(ci, carry):
        rows = pl.ds(pl.multiple_of(ci * ATT_MERGE_ROWS, ATT_MERGE_ROWS), ATT_MERGE_ROWS)
        o_ref[0, rows, :] = (onat[rows, :] * az_ref[0, rows, :].astype(F32)).astype(BF16)
        return carry

    lax.fori_loop(0, ta // ATT_MERGE_ROWS, gate, 0)


def _attention(aq, ak, av, az):
    b, s, w = aq.shape
    assert s % ATT_TILE == 0 and w % LANES == 0
    assert DILATIONS == (1, ATT_PERM, ATT_PERM * ATT_PERM)
    assert (ATT_TILE // ATT_PERM) % ATT_MERGE_ROWS == 0
    bias = _attn_bias()
    reg = ATT_TILE // ATT_PERM
    pair_spec = pl.BlockSpec((1, ATT_TILE, LANES), lambda bi, h, t: (bi, t, h))
    hist = pltpu.VMEM((ATT_BLOCK, LANES), F32)
    prm = pltpu.VMEM((2, ATT_PERM, reg, LANES), F32)
    stat = pltpu.VMEM((len(DILATIONS), ATT_TILE, LANES), F32)
    return pl.pallas_call(
        _attn_kernel,
        grid=(b, w // LANES, s // ATT_TILE),
        in_specs=[pl.BlockSpec(bias.shape, lambda bi, h, t: (0, 0, 0))] + [pair_spec] * 4,
        out_specs=pair_spec,
        out_shape=jax.ShapeDtypeStruct((b, s, w), BF16),
        scratch_shapes=[hist, hist, prm, prm, pltpu.VMEM((ATT_PERM, reg, LANES), F32),
                        stat, stat, stat, pltpu.VMEM((ATT_TILE, LANES), F32)],
        compiler_params=pltpu.CompilerParams(
            dimension_semantics=("arbitrary", "arbitrary", "arbitrary"),
            vmem_limit_bytes=VMEM_LIMIT),
        name="dilated_attention",
    )(bias, aq, ak, av, az)


HG_DIAG = 8
HG_LEVELS = (8, 16, 32)
SUBLANES = 8


def _hgrn_constants():
    c = HG_CHUNK
    t = np.arange(c)
    tri = (t[:, None] >= t[None, :]).astype(np.float32)
    first = (t // HG_DIAG) * HG_DIAG
    half = np.concatenate([tri, tri - 0.5 * (tri[first] + tri[first + HG_DIAG - 1])], axis=0)
    coef = np.concatenate([half, half], axis=1)
    ts, ss = t[:, None], t[None, :]
    level = np.full((c, c), len(HG_LEVELS) + 1, np.int32)
    level[(ts // HG_DIAG == ss // HG_DIAG) & (ts >= ss)] = 0
    for i, lb in enumerate(HG_LEVELS):
        m = ((ts // (2 * lb) == ss // (2 * lb)) & ((ts // lb) % 2 == 1)
             & ((ss // lb) % 2 == 0))
        level[m] = i + 1
    return jnp.asarray(coef, BF16), jnp.asarray(level)


def _hgrn_out_kernel(coef_ref, level_ref, x_ref, nw_ref, w_in32_ref, lbl_ref, gn_ref,
                     oat_ref, w32_ref, fw_ref, o_ref,
                     w_in_ref, w_ref, state_ref, qs_ref, g_ref, kk_ref, vi_ref, zs_ref,
                     qd_ref, oi_ref, kv_ref, dec_ref, ohg_ref):
    @pl.when(pl.program_id(1) == 0)
    def _():
        state_ref[...] = jnp.zeros_like(state_ref)
        w_in_ref[...] = w_in32_ref[0].astype(BF16)
        w_ref[...] = w32_ref[0].astype(BF16)

    lg = lbl_ref[...]
    ex = jnp.exp(lg - jnp.max(lg, axis=0, keepdims=True))
    lb = jnp.clip(ex[0:1, :] / jnp.sum(ex, axis=0, keepdims=True), 1e-6, 1.0 - 1e-6)
    f_mid = 0.5 * (1.0 + lb)
    f_amp = 0.5 * (1.0 - lb)

    def store_q(p):
        qs_ref[...] = _silu(p).astype(BF16)

    def store_f(p):
        f = f_mid + f_amp * jnp.tanh(0.5 * p)
        g2 = jnp.log2(f)
        hi_bits = lax.bitcast_convert_type(g2, jnp.uint32) & jnp.uint32(0xFFFF0000)
        g2_hi = lax.bitcast_convert_type(hi_bits, F32)
        g2_lo = (g2 - g2_hi).astype(BF16)
        g2_hi = g2_hi.astype(BF16)
        for h in range(HG_HEADS):
            g_ref[:, 2 * h * HG_DK:(2 * h + 1) * HG_DK] = g2_hi[:, h * HG_DK:(h + 1) * HG_DK]
            g_ref[:, (2 * h + 1) * HG_DK:(2 * h + 2) * HG_DK] = g2_lo[:, h * HG_DK:(h + 1) * HG_DK]
        kk_ref[...] = (1.0 - f).astype(BF16)

    def store_z(p):
        zs_ref[...] = _silu(p).astype(BF16)

    def store_i(p):
        vi_ref[...] = p.astype(BF16)

    _project_sections(_normalized(x_ref[0], nw_ref[...]), w_in_ref,
                      [(0, store_q), (1, store_f), (3, store_z), (2, store_i)])

    c = HG_CHUNK
    n_chunks = HG_ROWS // c
    half = HG_HEADS * HG_DK
    coef = coef_ref[...]
    level = level_ref[...]
    units = [(h, ci) for h in range(HG_HEADS) for ci in range(n_chunks)]

    def rows_of(ci):
        return pl.ds(ci * c, c)

    def cols_of(h, width=HG_DK):
        return pl.ds(h * width, width)

    def cum_decay(h, ci):
        g2 = g_ref[rows_of(ci), cols_of(h, 2 * HG_DK)]
        return jnp.dot(coef, jnp.concatenate([g2[:, :HG_DK], g2[:, HG_DK:]], axis=0),
                       preferred_element_type=F32)

    def level_operands(h, ci, be):
        q = qs_ref[rows_of(ci), cols_of(h)]
        k = kk_ref[rows_of(ci), cols_of(h)]
        scaled = lambda t, w: t * w.astype(BF16)
        b = be[0:c]
        e_diag = be[c:2 * c]
        ops = [(scaled(q, jnp.exp2(e_diag)), scaled(k, jnp.exp2(-e_diag)))]
        for lb_rows in HG_LEVELS:
            ref = jnp.concatenate(
                [jnp.broadcast_to(b[p + lb_rows - 1:p + lb_rows, :], (2 * lb_rows, HG_DK))
                 for p in range(0, c, 2 * lb_rows)], axis=0)
            w = jnp.exp2(_neg_abs(b - ref))
            ops.append((scaled(q, w), scaled(k, w)))
        b_last = b[c - 1:c, :]
        return ops, scaled(k, jnp.exp2(b_last - b)), scaled(q, jnp.exp2(b)), jnp.exp2(b_last)

    def intra_scores(ops):
        scores = jnp.where(level == 0, _dot_nt(*ops[0]), 0.0)
        for i in range(len(HG_LEVELS)):
            scores = jnp.where(level == i + 1, _dot_nt(*ops[i + 1]), scores)
        return scores.astype(BF16)

    stage1 = [cum_decay(h, ci) for h, ci in units]
    stage2 = [level_operands(h, ci, be) for (h, ci), be in zip(units, stage1)]

    o_ref[0] = x_ref[0] + jnp.dot(oat_ref[0], w_ref[half:2 * half, :],
                                  preferred_element_type=F32)

    stage3 = [intra_scores(ops) for ops, _, _, _ in stage2]
    for (h, ci), (_, k_dec, q_dec, decay), scores in zip(units, stage2, stage3):
        v = vi_ref[rows_of(ci), cols_of(h)]
        oi_ref[rows_of(ci), cols_of(h)] = jnp.dot(scores, v, preferred_element_type=F32)
        kv_ref[h, ci] = _dot_tn(v, k_dec)
        qd_ref[rows_of(ci), cols_of(h)] = q_dec
        dec_ref[h, ci] = jnp.broadcast_to(decay, dec_ref.shape[2:])

    states = [state_ref[h] for h in range(HG_HEADS)]
    for ci in range(n_chunks):
        for h in range(HG_HEADS):
            rows, cols = rows_of(ci), cols_of(h)
            o = oi_ref[rows, cols] + _dot_nt(qd_ref[rows, cols], states[h].astype(BF16))
            states[h] = states[h] * dec_ref[h, ci, 0:1, :] + kv_ref[h, ci]
            z = zs_ref[rows, cols].astype(F32)
            ms = jnp.mean(o * o, axis=-1, keepdims=True)
            ohg_ref[rows, cols] = (o * lax.rsqrt(ms + NORM_EPS) * gn_ref[:, cols] * z
                                   ).astype(BF16)
    for h in range(HG_HEADS):
        state_ref[h] = states[h]

    hres = o_ref[0] + jnp.dot(ohg_ref[...], w_ref[0:half, :], preferred_element_type=F32)
    ms = jnp.mean(hres * hres, axis=-1, keepdims=True)
    o_ref[0] = hres * lax.rsqrt(ms + NORM_EPS) * fw_ref[...]


def _hgrn_out(x, norm_w, w_in, lb_logits, hg_norm_w, oat, w_out, final_norm_w):
    b, s, d_model = x.shape
    w = HG_HEADS * HG_DK
    assert s % HG_ROWS == 0 and w == SECTION
    assert w_out.shape == (1, w + oat.shape[-1], d_model)
    coef, level = _hgrn_constants()
    n_chunks = HG_ROWS // HG_CHUNK
    tile = lambda width: pl.BlockSpec((1, HG_ROWS, width), lambda bi, t: (bi, t, 0))
    const = lambda a: pl.BlockSpec(a.shape, lambda bi, t: (0,) * a.ndim)
    act = lambda width: pltpu.VMEM((HG_ROWS, width), BF16)
    return pl.pallas_call(
        _hgrn_out_kernel,
        grid=(b, s // HG_ROWS),
        in_specs=[const(coef), const(level), tile(d_model), const(norm_w),
                  _branch_weight_spec(w_in, 0), const(lb_logits), const(hg_norm_w),
                  tile(oat.shape[-1]), const(w_out), const(final_norm_w)],
        out_specs=tile(d_model),
        out_shape=jax.ShapeDtypeStruct((b, s, d_model), F32),
        scratch_shapes=[pltpu.VMEM((d_model, 4 * SECTION), BF16),
                        pltpu.VMEM(w_out.shape[1:], BF16),
                        pltpu.VMEM((HG_HEADS, HG_DK, HG_DK), F32),
                        act(w), act(2 * w), act(w), act(w), act(w),
                        act(w),
                        pltpu.VMEM((HG_ROWS, w), F32),
                        pltpu.VMEM((HG_HEADS, n_chunks, HG_DK, HG_DK), F32),
                        pltpu.VMEM((HG_HEADS, n_chunks, SUBLANES, HG_DK), F32),
                        act(w)],
        compiler_params=pltpu.CompilerParams(
            dimension_semantics=("parallel", "arbitrary"),
            vmem_limit_bytes=VMEM_LIMIT),
        name="hgrn2_branch_and_output",
    )(coef, level, x, norm_w, w_in, lb_logits, hg_norm_w, oat, w_out, final_norm_w)


def kernel(x, norm_w, w_in, hgrn_lb_logits, hg_norm_w, w_out, final_norm_w):
    b, s, d_model = x.shape
    assert norm_w.shape[0] == 1 and w_in.shape[0] == 1 and w_out.shape[0] == 1
    aq, ak, av, az = _attn_projection(x.reshape(b * s, d_model), norm_w, w_in, s)
    to3 = lambda a: a.reshape(b, s, a.shape[-1])
    oat = _attention(to3(aq), to3(ak), to3(av), to3(az))
    return _hgrn_out(x, norm_w, w_in, hgrn_lb_logits, hg_norm_w, oat, w_out,
                     final_norm_w.reshape(1, d_model))
```

```python
import functools

import numpy as np
import jax
import jax.numpy as jnp
from jax import lax
from jax.experimental import pallas as pl
from jax.experimental.pallas import tpu as pltpu

F32 = jnp.float32
BF16 = jnp.bfloat16

NORM_EPS = 1e-6
ROPE_THETA = 10000.0
LANES = 128

HG_HEADS = 4
HG_DK = 128
HG_CHUNK = 64
AT_HEAD_DIM = 64
SECTION = 512
DILATIONS = (1, 4, 16)
ATT_BLOCK = 128
ATT_TILE = 2 * ATT_BLOCK * max(DILATIONS)
NEG_BIG = -1e30

PROJ_ROWS = 1024
HG_ROWS = 512
ATT_PERM = 4
ATT_MERGE_ROWS = 1024
LOG2E = 1.4426950408889634
VMEM_LIMIT = 56 * 1024 * 1024


def _normalized(x, gain):
    ms = jnp.mean(x * x, axis=-1, keepdims=True)
    return (x * lax.rsqrt(ms + NORM_EPS) * gain).astype(BF16)


def _silu(p):
    return p * (0.5 * jnp.tanh(0.5 * p) + 0.5)


def _project_sections(u, w_ref, plan):
    pending = None
    for j, finish in plan:
        p = jnp.dot(u, w_ref[:, j * SECTION:(j + 1) * SECTION], preferred_element_type=F32)
        if pending is not None:
            pending()
        pending = functools.partial(finish, p)
    pending()


def _dot_nt(a, b):
    return lax.dot_general(a, b, (((1,), (1,)), ((), ())), preferred_element_type=F32)


def _dot_tn(a, b):
    return lax.dot_general(a, b, (((0,), (0,)), ((), ())), preferred_element_type=F32)


def _neg_abs(x):
    bits = lax.bitcast_convert_type(x, jnp.uint32) | jnp.uint32(0x80000000)
    return lax.bitcast_convert_type(bits, F32)


def _attn_proj_kernel(x_ref, nw_ref, w32_ref, cos_ref, sina_ref, sinb_ref,
                      aq_ref, ak_ref, av_ref, az_ref, w_ref):
    @pl.when(pl.program_id(0) == 0)
    def _():
        w_ref[...] = w32_ref[0].astype(BF16)

    u = _normalized(x_ref[...], nw_ref[...])

    def store_rope(dst_ref, scale, p):
        half = AT_HEAD_DIM // 2
        for c in range(SECTION // LANES):
            xc = p[:, c * LANES:(c + 1) * LANES]
            r = (xc * cos_ref[...] + pltpu.roll(xc, LANES - half, 1) * sina_ref[...]
                 + pltpu.roll(xc, half, 1) * sinb_ref[...])
            dst_ref[:, c * LANES:(c + 1) * LANES] = r * scale

    def store_z(p):
        az_ref[...] = _silu(p).astype(BF16)

    def store_v(p):
        av_ref[...] = p

    _project_sections(u, w_ref, [
        (3, store_z),
        (0, functools.partial(store_rope, aq_ref, AT_HEAD_DIM ** -0.5 * LOG2E)),
        (1, functools.partial(store_rope, ak_ref, 1.0)),
        (2, store_v)])


@functools.lru_cache(maxsize=None)
def _rope_tables(seq):
    half = AT_HEAD_DIM // 2
    inv_freq = 1.0 / (ROPE_THETA ** (np.arange(half, dtype=np.float64) / half))
    ang = np.arange(seq, dtype=np.float64)[:, None] * inv_freq[None, :]
    cos = np.cos(ang)
    sin = np.sin(ang)
    zero = np.zeros_like(sin)
    reps = LANES // AT_HEAD_DIM
    cos_t = np.tile(np.concatenate([cos, cos], axis=1), (1, reps))
    sina_t = np.tile(np.concatenate([-sin, zero], axis=1), (1, reps))
    sinb_t = np.tile(np.concatenate([zero, sin], axis=1), (1, reps))
    return tuple(np.asarray(t, np.float32) for t in (cos_t, sina_t, sinb_t))


def _branch_weight_spec(w_in, branch):
    assert w_in.shape[0] == 1 and w_in.shape[2] == 8 * SECTION
    return pl.BlockSpec((1, w_in.shape[1], 4 * SECTION),
                        lambda *_: (0, 0, branch), pipeline_mode=pl.Buffered(1))


def _attn_projection(x2, norm_w, w_in, seq):
    rows, d_model = x2.shape
    assert rows % PROJ_ROWS == 0 and seq % PROJ_ROWS == 0
    tables = _rope_tables(seq)
    seq_tiles = seq // PROJ_ROWS
    row_spec = lambda w: pl.BlockSpec((PROJ_ROWS, w), lambda i: (i, 0))
    tab_spec = pl.BlockSpec((PROJ_ROWS, LANES), lambda i: (i % seq_tiles, 0))
    full = lambda a: pl.BlockSpec(a.shape, lambda i: (0, 0))
    out_dtypes = (F32, F32, F32, BF16)
    return pl.pallas_call(
        _attn_proj_kernel,
        grid=(rows // PROJ_ROWS,),
        in_specs=[row_spec(d_model), full(norm_w), _branch_weight_spec(w_in, 1)]
                 + [tab_spec] * 3,
        out_specs=[row_spec(SECTION)] * len(out_dtypes),
        out_shape=[jax.ShapeDtypeStruct((rows, SECTION), dt) for dt in out_dtypes],
        scratch_shapes=[pltpu.VMEM((d_model, 4 * SECTION), BF16)],
        compiler_params=pltpu.CompilerParams(
            dimension_semantics=("arbitrary",), vmem_limit_bytes=VMEM_LIMIT),
        name="attention_projection",
    )(x2, norm_w, w_in, *tables)


def _attn_bias():
    qi = np.arange(ATT_BLOCK)[:, None]
    kj = np.arange(2 * ATT_BLOCK)[None, :]
    dist = ATT_BLOCK + qi - kj
    valid = (dist >= 0) & (dist <= ATT_BLOCK)
    b0 = np.where(valid, 0.0, NEG_BIG)
    b1 = np.where(valid & (kj >= ATT_BLOCK), 0.0, NEG_BIG)
    return jnp.asarray(np.stack([b0, b1]), F32)


def _attn_kernel(bias_ref, aq_ref, ak_ref, av_ref, az_ref, o_ref,
                 khist, vhist, kprm, vprm, qprm, m_ref, l_ref, acc_ref, onat):
    ta = ATT_TILE
    blk = ATT_BLOCK
    nph = ATT_PERM
    reg = ta // nph
    tile = pl.program_id(2)
    cur = tile % 2
    prev = 1 - cur

    @pl.when((pl.program_id(0) == 0) & (pl.program_id(1) == 0) & (tile == 0))
    def _():
        khist[...] = jnp.zeros_like(khist)
        vhist[...] = jnp.zeros_like(vhist)
        kprm[1] = jnp.zeros((nph, reg, LANES), F32)
        vprm[1] = jnp.zeros((nph, reg, LANES), F32)

    for r in range(nph):
        kprm[cur, r] = ak_ref[0, pl.ds(r, reg, stride=nph), :]
        vprm[cur, r] = av_ref[0, pl.ds(r, reg, stride=nph), :]
        qprm[r] = aq_ref[0, pl.ds(r, reg, stride=nph), :]

    first_head = lax.broadcasted_iota(jnp.int32, (blk, LANES), 1) < AT_HEAD_DIM
    ones = jnp.ones((2 * blk, LANES), BF16)
    bias_any = bias_ref[0]
    bias_first = bias_ref[(tile == 0).astype(jnp.int32)]

    def block_stats(q, k, v, bias):
        k = k.astype(BF16)
        v = v.astype(BF16)
        q2 = jnp.concatenate([jnp.where(first_head, q, 0.0),
                              jnp.where(first_head, 0.0, q)], axis=0).astype(BF16)
        s = _dot_nt(q2, k) + jnp.concatenate([bias, bias], axis=0)
        m = jnp.max(s, axis=-1, keepdims=True)
        p = jnp.exp2(s - m).astype(BF16)
        pv = jnp.dot(p, jnp.concatenate([v, ones], axis=1), preferred_element_type=F32)
        return (jnp.where(first_head, m[:blk], m[blk:]),
                jnp.where(first_head, pv[:blk, LANES:], pv[blk:, LANES:]),
                jnp.where(first_head, pv[:blk, :LANES], pv[blk:, :LANES]))

    def store_stats(pi, rows, stats):
        m_ref[pi, rows, :], l_ref[pi, rows, :], acc_ref[pi, rows, :] = stats

    def window(first_ref, first_rows, ref, rows):
        return jnp.concatenate([first_ref[first_rows], ref[rows]], axis=0)

    for j in range(ta // blk):
        q_rows = pl.ds(j * blk, blk)
        if j == 0:
            k = window(khist, (slice(None),) * 2, ak_ref, (0, q_rows))
            v = window(vhist, (slice(None),) * 2, av_ref, (0, q_rows))
        else:
            k_rows = pl.ds((j - 1) * blk, 2 * blk)
            k, v = ak_ref[0, k_rows, :], av_ref[0, k_rows, :]
        store_stats(0, q_rows, block_stats(aq_ref[0, q_rows, :], k, v,
                                           bias_first if j == 0 else bias_any))

    khist[...] = ak_ref[0, ta - blk:ta, :]
    vhist[...] = av_ref[0, ta - blk:ta, :]

    for r in range(nph):
        for j in range(reg // blk):
            q_rows = pl.ds(j * blk, blk)
            if j == 0:
                last = pl.ds(reg - blk, blk)
                k = window(kprm, (prev, r, last), kprm, (cur, r, q_rows))
                v = window(vprm, (prev, r, last), vprm, (cur, r, q_rows))
            else:
                k_rows = pl.ds((j - 1) * blk, 2 * blk)
                k, v = kprm[cur, r, k_rows, :], vprm[cur, r, k_rows, :]
            stats = block_stats(qprm[r, q_rows, :], k, v, bias_first if j == 0 else bias_any)
            store_stats(1, pl.ds(r * reg + j * blk, blk), stats)
        sub = reg // nph
        for r16 in range(nph):
            for j in range(sub // blk):
                q_rows = pl.ds(r16 + nph * blk * j, blk, stride=nph)
                if j == 0:
                    tail = pl.ds(r16 + nph * (sub - blk), blk, stride=nph)
                    k = window(kprm, (prev, r, tail), kprm, (cur, r, q_rows))
                    v = window(vprm, (prev, r, tail), vprm, (cur, r, q_rows))
                else:
                    k_rows = pl.ds(r16 + nph * blk * (j - 1), 2 * blk, stride=nph)
                    k, v = kprm[cur, r, k_rows, :], vprm[cur, r, k_rows, :]
                stats = block_stats(qprm[r, q_rows, :], k, v,
                                    bias_first if j == 0 else bias_any)
                store_stats(2, pl.ds(r * reg + r16 + nph * blk * j, blk, stride=nph), stats)

    def merge(ci, carry):
        pieces = reg // ATT_MERGE_ROWS
        r = ci // pieces
        i0 = (ci % pieces) * ATT_MERGE_ROWS
        nat_rows = pl.ds(r + nph * i0, ATT_MERGE_ROWS, stride=nph)
        prm_rows = pl.ds(pl.multiple_of(ci * ATT_MERGE_ROWS, ATT_MERGE_ROWS), ATT_MERGE_ROWS)
        rows = (nat_rows, prm_rows, prm_rows)
        ms = [m_ref[pi, rw, :] for pi, rw in enumerate(rows)]
        m_all = functools.reduce(jnp.maximum, ms)
        ws = [jnp.exp2(m - m_all) for m in ms]
        l_all = sum(w * l_ref[pi, rw, :] for (pi, rw), w in zip(enumerate(rows), ws))
        acc = sum(w * acc_ref[pi, rw, :] for (pi, rw), w in zip(enumerate(rows), ws))
        onat[nat_rows, :] = acc / l_all
        return carry

    lax.fori_loop(0, ta // ATT_MERGE_ROWS, merge, 0)

    def gate(ci, carry):
        rows = pl.ds(pl.multiple_of(ci * ATT_MERGE_ROWS, ATT_MERGE_ROWS), ATT_MERGE_ROWS)
        o_ref[0, rows, :] = (onat[rows, :] * az_ref[0, rows, :].astype(F32)).astype(BF16)
        return carry

    lax.fori_loop(0, ta // ATT_MERGE_ROWS, gate, 0)


def _attention(aq, ak, av, az):
    b, s, w = aq.shape
    assert s % ATT_TILE == 0 and w % LANES == 0
    assert DILATIONS == (1, ATT_PERM, ATT_PERM * ATT_PERM)
    assert (ATT_TILE // ATT_PERM) % ATT_MERGE_ROWS == 0
    bias = _attn_bias()
    reg = ATT_TILE // ATT_PERM
    pair_spec = pl.BlockSpec((1, ATT_TILE, LANES), lambda bi, h, t: (bi, t, h))
    hist = pltpu.VMEM((ATT_BLOCK, LANES), F32)
    prm = pltpu.VMEM((2, ATT_PERM, reg, LANES), F32)
    stat = pltpu.VMEM((len(DILATIONS), ATT_TILE, LANES), F32)
    return pl.pallas_call(
        _attn_kernel,
        grid=(b, w // LANES, s // ATT_TILE),
        in_specs=[pl.BlockSpec(bias.shape, lambda bi, h, t: (0, 0, 0))] + [pair_spec] * 4,
        out_specs=pair_spec,
        out_shape=jax.ShapeDtypeStruct((b, s, w), BF16),
        scratch_shapes=[hist, hist, prm, prm, pltpu.VMEM((ATT_PERM, reg, LANES), F32),
                        stat, stat, stat, pltpu.VMEM((ATT_TILE, LANES), F32)],
        compiler_params=pltpu.CompilerParams(
            dimension_semantics=("arbitrary", "arbitrary", "arbitrary"),
            vmem_limit_bytes=VMEM_LIMIT),
        name="dilated_attention",
    )(bias, aq, ak, av, az)


HG_DIAG = 8
HG_LEVELS = (8, 16, 32)
SUBLANES = 8


def _hgrn_constants():
    c = HG_CHUNK
    t = np.arange(c)
    tri = (t[:, None] >= t[None, :]).astype(np.float32)
    first = (t // HG_DIAG) * HG_DIAG
    half = np.concatenate([tri, tri - 0.5 * (tri[first] + tri[first + HG_DIAG - 1])], axis=0)
    coef = np.concatenate([half, half], axis=1)
    ts, ss = t[:, None], t[None, :]
    level = np.full((c, c), len(HG_LEVELS) + 1, np.int32)
    level[(ts // HG_DIAG == ss // HG_DIAG) & (ts >= ss)] = 0
    for i, lb in enumerate(HG_LEVELS):
        m = ((ts // (2 * lb) == ss // (2 * lb)) & ((ts // lb) % 2 == 1)
             & ((ss // lb) % 2 == 0))
        level[m] = i + 1
    return jnp.asarray(coef, BF16), jnp.asarray(level)


def _hgrn_out_kernel(coef_ref, level_ref, x_ref, nw_ref, w_in32_ref, lbl_ref, gn_ref,
                     oat_ref, w32_ref, fw_ref, o_ref,
                     w_in_ref, w_ref, state_ref, qs_ref, g_ref, kk_ref, vi_ref, zs_ref,
                     qd_ref, oi_ref, kv_ref, dec_ref, ohg_ref):
    @pl.when(pl.program_id(1) == 0)
    def _():
        state_ref[...] = jnp.zeros_like(state_ref)
        w_in_ref[...] = w_in32_ref[0].astype(BF16)
        w_ref[...] = w32_ref[0].astype(BF16)

    lg = lbl_ref[...]
    ex = jnp.exp(lg - jnp.max(lg, axis=0, keepdims=True))
    lb = jnp.clip(ex[0:1, :] / jnp.sum(ex, axis=0, keepdims=True), 1e-6, 1.0 - 1e-6)
    f_mid = 0.5 * (1.0 + lb)
    f_amp = 0.5 * (1.0 - lb)

    def store_q(p):
        qs_ref[...] = _silu(p).astype(BF16)

    def store_f(p):
        f = f_mid + f_amp * jnp.tanh(0.5 * p)
        g2 = jnp.log2(f)
        hi_bits = lax.bitcast_convert_type(g2, jnp.uint32) & jnp.uint32(0xFFFF0000)
        g2_hi = lax.bitcast_convert_type(hi_bits, F32)
        g2_lo = (g2 - g2_hi).astype(BF16)
        g2_hi = g2_hi.astype(BF16)
        for h in range(HG_HEADS):
            g_ref[:, 2 * h * HG_DK:(2 * h + 1) * HG_DK] = g2_hi[:, h * HG_DK:(h + 1) * HG_DK]
            g_ref[:, (2 * h + 1) * HG_DK:(2 * h + 2) * HG_DK] = g2_lo[:, h * HG_DK:(h + 1) * HG_DK]
        kk_ref[...] = (1.0 - f).astype(BF16)

    def store_z(p):
        zs_ref[...] = _silu(p).astype(BF16)

    def store_i(p):
        vi_ref[...] = p.astype(BF16)

    _project_sections(_normalized(x_ref[0], nw_ref[...]), w_in_ref,
                      [(0, store_q), (1, store_f), (3, store_z), (2, store_i)])

    c = HG_CHUNK
    n_chunks = HG_ROWS // c
    half = HG_HEADS * HG_DK
    coef = coef_ref[...]
    level = level_ref[...]
    units = [(h, ci) for h in range(HG_HEADS) for ci in range(n_chunks)]

    def rows_of(ci):
        return pl.ds(ci * c, c)

    def cols_of(h, width=HG_DK):
        return pl.ds(h * width, width)

    def cum_decay(h, ci):
        g2 = g_ref[rows_of(ci), cols_of(h, 2 * HG_DK)]
        return jnp.dot(coef, jnp.concatenate([g2[:, :HG_DK], g2[:, HG_DK:]], axis=0),
                       preferred_element_type=F32)

    def level_operands(h, ci, be):
        q = qs_ref[rows_of(ci), cols_of(h)].astype(F32)
        k = kk_ref[rows_of(ci), cols_of(h)].astype(F32)
        scaled = lambda t, w: (t * w).astype(BF16)
        b = be[0:c]
        e_diag = be[c:2 * c]
        ops = [(scaled(q, jnp.exp2(e_diag)), scaled(k, jnp.exp2(-e_diag)))]
        for lb_rows in HG_LEVELS:
            ref = jnp.concatenate(
                [jnp.broadcast_to(b[p + lb_rows - 1:p + lb_rows, :], (2 * lb_rows, HG_DK))
                 for p in range(0, c, 2 * lb_rows)], axis=0)
            w = jnp.exp2(_neg_abs(b - ref))
            ops.append((scaled(q, w), scaled(k, w)))
        b_last = b[c - 1:c, :]
        return ops, scaled(k, jnp.exp2(b_last - b)), scaled(q, jnp.exp2(b)), jnp.exp2(b_last)

    def intra_scores(ops):
        scores = jnp.where(level == 0, _dot_nt(*ops[0]), 0.0)
        for i in range(len(HG_LEVELS)):
            scores = jnp.where(level == i + 1, _dot_nt(*ops[i + 1]), scores)
        return scores.astype(BF16)

    stage1 = [cum_decay(h, ci) for h, ci in units]
    stage2 = [level_operands(h, ci, be) for (h, ci), be in zip(units, stage1)]

    o_ref[0] = x_ref[0] + jnp.dot(oat_ref[0], w_ref[half:2 * half, :],
                                  preferred_element_type=F32)

    stage3 = [intra_scores(ops) for ops, _, _, _ in stage2]
    for (h, ci), (_, k_dec, q_dec, decay), scores in zip(units, stage2, stage3):
        v = vi_ref[rows_of(ci), cols_of(h)]
        oi_ref[rows_of(ci), cols_of(h)] = jnp.dot(scores, v, preferred_element_type=F32)
        kv_ref[h, ci] = _dot_tn(v, k_dec)
        qd_ref[rows_of(ci), cols_of(h)] = q_dec
        dec_ref[h, ci] = jnp.broadcast_to(decay, dec_ref.shape[2:])

    states = [state_ref[h] for h in range(HG_HEADS)]
    for ci in range(n_chunks):
        for h in range(HG_HEADS):
            rows, cols = rows_of(ci), cols_of(h)
            o = oi_ref[rows, cols] + _dot_nt(qd_ref[rows, cols], states[h].astype(BF16))
            states[h] = states[h] * dec_ref[h, ci, 0:1, :] + kv_ref[h, ci]
            z = zs_ref[rows, cols].astype(F32)
            ms = jnp.mean(o * o, axis=-1, keepdims=True)
            ohg_ref[rows, cols] = (o * lax.rsqrt(ms + NORM_EPS) * gn_ref[:, cols] * z
                                   ).astype(BF16)
    for h in range(HG_HEADS):
        state_ref[h] = states[h]

    hres = o_ref[0] + jnp.dot(ohg_ref[...], w_ref[0:half, :], preferred_element_type=F32)
    ms = jnp.mean(hres * hres, axis=-1, keepdims=True)
    o_ref[0] = hres * lax.rsqrt(ms + NORM_EPS) * fw_ref[...]


def _hgrn_out(x, norm_w, w_in, lb_logits, hg_norm_w, oat, w_out, final_norm_w):
    b, s, d_model = x.shape
    w = HG_HEADS * HG_DK
    assert s % HG_ROWS == 0 and w == SECTION
    assert w_out.shape == (1, w + oat.shape[-1], d_model)
    coef, level = _hgrn_constants()
    n_chunks = HG_ROWS // HG_CHUNK
    tile = lambda width: pl.BlockSpec((1, HG_ROWS, width), lambda bi, t: (bi, t, 0))
    const = lambda a: pl.BlockSpec(a.shape, lambda bi, t: (0,) * a.ndim)
    act = lambda width: pltpu.VMEM((HG_ROWS, width), BF16)
    return pl.pallas_call(
        _hgrn_out_kernel,
        grid=(b, s // HG_ROWS),
        in_specs=[const(coef), const(level), tile(d_model), const(norm_w),
                  _branch_weight_spec(w_in, 0), const(lb_logits), const(hg_norm_w),
                  tile(oat.shape[-1]), const(w_out), const(final_norm_w)],
        out_specs=tile(d_model),
        out_shape=jax.ShapeDtypeStruct((b, s, d_model), F32),
        scratch_shapes=[pltpu.VMEM((d_model, 4 * SECTION), BF16),
                        pltpu.VMEM(w_out.shape[1:], BF16),
                        pltpu.VMEM((HG_HEADS, HG_DK, HG_DK), F32),
                        act(w), act(2 * w), act(w), act(w), act(w),
                        act(w),
                        pltpu.VMEM((HG_ROWS, w), F32),
                        pltpu.VMEM((HG_HEADS, n_chunks, HG_DK, HG_DK), F32),
                        pltpu.VMEM((HG_HEADS, n_chunks, SUBLANES, HG_DK), F32),
                        act(w)],
        compiler_params=pltpu.CompilerParams(
            dimension_semantics=("parallel", "arbitrary"),
            vmem_limit_bytes=VMEM_LIMIT),
        name="hgrn2_branch_and_output",
    )(coef, level, x, norm_w, w_in, lb_logits, hg_norm_w, oat, w_out, final_norm_w)


def kernel(x, norm_w, w_in, hgrn_lb_logits, hg_norm_w, w_out, final_norm_w):
    b, s, d_model = x.shape
    assert norm_w.shape[0] == 1 and w_in.shape[0] == 1 and w_out.shape[0] == 1
    aq, ak, av, az = _attn_projection(x.reshape(b * s, d_model), norm_w, w_in, s)
    to3 = lambda a: a.reshape(b, s, a.shape[-1])
    oat = _attention(to3(aq), to3(ak), to3(av), to3(az))
    return _hgrn_out(x, norm_w, w_in, hgrn_lb_logits, hg_norm_w, oat, w_out,
                     final_norm_w.reshape(1, d_model))
```

```python
import functools

import numpy as np
import jax
import jax.numpy as jnp
from jax import lax
from jax.experimental import pallas as pl
from jax.experimental.pallas import tpu as pltpu

F32 = jnp.float32
BF16 = jnp.bfloat16

NORM_EPS = 1e-6
ROPE_THETA = 10000.0
LANES = 128

HG_HEADS = 4
HG_DK = 128
HG_CHUNK = 64
AT_HEAD_DIM = 64
SECTION = 512
DILATIONS = (1, 4, 16)
ATT_BLOCK = 128
ATT_TILE = 2 * ATT_BLOCK * max(DILATIONS)
NEG_BIG = -1e30

PROJ_ROWS = 1024
HG_ROWS = 512
ATT_PERM = 4
ATT_MERGE_ROWS = 1024
LOG2E = 1.4426950408889634
VMEM_LIMIT = 56 * 1024 * 1024


def _normalized(x, gain):
    ms = jnp.mean(x * x, axis=-1, keepdims=True)
    return (x * lax.rsqrt(ms + NORM_EPS) * gain).astype(BF16)


def _silu(p):
    return p * (0.5 * jnp.tanh(0.5 * p) + 0.5)


def _project_sections(u, w_ref, plan):
    pending = None
    for j, finish in plan:
        p = jnp.dot(u, w_ref[:, j * SECTION:(j + 1) * SECTION], preferred_element_type=F32)
        if pending is not None:
            pending()
        pending = functools.partial(finish, p)
    pending()


def _dot_nt(a, b):
    return lax.dot_general(a, b, (((1,), (1,)), ((), ())), preferred_element_type=F32)


def _dot_tn(a, b):
    return lax.dot_general(a, b, (((0,), (0,)), ((), ())), preferred_element_type=F32)


def _neg_abs(x):
    bits = lax.bitcast_convert_type(x, jnp.uint32) | jnp.uint32(0x80000000)
    return lax.bitcast_convert_type(bits, F32)


def _attn_proj_kernel(x_ref, nw_ref, w32_ref, cos_ref, sina_ref, sinb_ref,
                      aq_ref, ak_ref, av_ref, az_ref, w_ref):
    @pl.when(pl.program_id(0) == 0)
    def _():
        w_ref[...] = w32_ref[0].astype(BF16)

    u = _normalized(x_ref[...], nw_ref[...])

    def store_rope(dst_ref, scale, p):
        half = AT_HEAD_DIM // 2
        for c in range(SECTION // LANES):
            xc = p[:, c * LANES:(c + 1) * LANES]
            r = (xc * cos_ref[...] + pltpu.roll(xc, LANES - half, 1) * sina_ref[...]
                 + pltpu.roll(xc, half, 1) * sinb_ref[...])
            dst_ref[:, c * LANES:(c + 1) * LANES] = r * scale

    def store_z(p):
        az_ref[...] = _silu(p).astype(BF16)

    def store_v(p):
        av_ref[...] = p

    _project_sections(u, w_ref, [
        (3, store_z),
        (0, functools.partial(store_rope, aq_ref, AT_HEAD_DIM ** -0.5 * LOG2E)),
        (1, functools.partial(store_rope, ak_ref, 1.0)),
        (2, store_v)])


@functools.lru_cache(maxsize=None)
def _rope_tables(seq):
    half = AT_HEAD_DIM // 2
    inv_freq = 1.0 / (ROPE_THETA ** (np.arange(half, dtype=np.float64) / half))
    ang = np.arange(seq, dtype=np.float64)[:, None] * inv_freq[None, :]
    cos = np.cos(ang)
    sin = np.sin(ang)
    zero = np.zeros_like(sin)
    reps = LANES // AT_HEAD_DIM
    cos_t = np.tile(np.concatenate([cos, cos], axis=1), (1, reps))
    sina_t = np.tile(np.concatenate([-sin, zero], axis=1), (1, reps))
    sinb_t = np.tile(np.concatenate([zero, sin], axis=1), (1, reps))
    return tuple(np.asarray(t, np.float32) for t in (cos_t, sina_t, sinb_t))


def _branch_weight_spec(w_in, branch):
    assert w_in.shape[0] == 1 and w_in.shape[2] == 8 * SECTION
    return pl.BlockSpec((1, w_in.shape[1], 4 * SECTION),
                        lambda *_: (0, 0, branch), pipeline_mode=pl.Buffered(1))


def _attn_projection(x2, norm_w, w_in, seq):
    rows, d_model = x2.shape
    assert rows % PROJ_ROWS == 0 and seq % PROJ_ROWS == 0
    tables = _rope_tables(seq)
    seq_tiles = seq // PROJ_ROWS
    row_spec = lambda w: pl.BlockSpec((PROJ_ROWS, w), lambda i: (i, 0))
    tab_spec = pl.BlockSpec((PROJ_ROWS, LANES), lambda i: (i % seq_tiles, 0))
    full = lambda a: pl.BlockSpec(a.shape, lambda i: (0, 0))
    out_dtypes = (F32, F32, F32, BF16)
    return pl.pallas_call(
        _attn_proj_kernel,
        grid=(rows // PROJ_ROWS,),
        in_specs=[row_spec(d_model), full(norm_w), _branch_weight_spec(w_in, 1)]
                 + [tab_spec] * 3,
        out_specs=[row_spec(SECTION)] * len(out_dtypes),
        out_shape=[jax.ShapeDtypeStruct((rows, SECTION), dt) for dt in out_dtypes],
        scratch_shapes=[pltpu.VMEM((d_model, 4 * SECTION), BF16)],
        compiler_params=pltpu.CompilerParams(
            dimension_semantics=("arbitrary",), vmem_limit_bytes=VMEM_LIMIT),
        name="attention_projection",
    )(x2, norm_w, w_in, *tables)


def _attn_bias():
    qi = np.arange(ATT_BLOCK)[:, None]
    kj = np.arange(2 * ATT_BLOCK)[None, :]
    dist = ATT_BLOCK + qi - kj
    valid = (dist >= 0) & (dist <= ATT_BLOCK)
    b0 = np.where(valid, 0.0, NEG_BIG)
    b1 = np.where(valid & (kj >= ATT_BLOCK), 0.0, NEG_BIG)
    return jnp.asarray(np.stack([b0, b1]), F32)


def _attn_kernel(bias_ref, aq_ref, ak_ref, av_ref, az_ref, o_ref,
                 khist, vhist, kprm, vprm, qprm, m_ref, l_ref, acc_ref, onat):
    ta = ATT_TILE
    blk = ATT_BLOCK
    nph = ATT_PERM
    reg = ta // nph
    tile = pl.program_id(2)
    cur = tile % 2
    prev = 1 - cur

    @pl.when((pl.program_id(0) == 0) & (pl.program_id(1) == 0) & (tile == 0))
    def _():
        khist[...] = jnp.zeros_like(khist)
        vhist[...] = jnp.zeros_like(vhist)
        kprm[1] = jnp.zeros((nph, reg, LANES), F32)
        vprm[1] = jnp.zeros((nph, reg, LANES), F32)

    for r in range(nph):
        kprm[cur, r] = ak_ref[0, pl.ds(r, reg, stride=nph), :]
        vprm[cur, r] = av_ref[0, pl.ds(r, reg, stride=nph), :]
        qprm[r] = aq_ref[0, pl.ds(r, reg, stride=nph), :]

    first_head = lax.broadcasted_iota(jnp.int32, (blk, LANES), 1) < AT_HEAD_DIM
    ones = jnp.ones((2 * blk, LANES), BF16)
    bias_any = bias_ref[0]
    bias_first = bias_ref[(tile == 0).astype(jnp.int32)]

    def block_stats(q, k, v, bias):
        k = k.astype(BF16)
        v = v.astype(BF16)
        q2 = jnp.concatenate([jnp.where(first_head, q, 0.0),
                              jnp.where(first_head, 0.0, q)], axis=0).astype(BF16)
        s = _dot_nt(q2, k) + jnp.concatenate([bias, bias], axis=0)
        m = jnp.max(s, axis=-1, keepdims=True)
        p = jnp.exp2(s - m).astype(BF16)
        pv = jnp.dot(p, jnp.concatenate([v, ones], axis=1), preferred_element_type=F32)
        return (jnp.where(first_head, m[:blk], m[blk:]),
                jnp.where(first_head, pv[:blk, LANES:], pv[blk:, LANES:]),
                jnp.where(first_head, pv[:blk, :LANES], pv[blk:, :LANES]))

    def store_stats(pi, rows, stats):
        m_ref[pi, rows, :], l_ref[pi, rows, :], acc_ref[pi, rows, :] = stats

    def window(first_ref, first_rows, ref, rows):
        return jnp.concatenate([first_ref[first_rows], ref[rows]], axis=0)

    for j in range(ta // blk):
        q_rows = pl.ds(j * blk, blk)
        if j == 0:
            k = window(khist, (slice(None),) * 2, ak_ref, (0, q_rows))
            v = window(vhist, (slice(None),) * 2, av_ref, (0, q_rows))
        else:
            k_rows = pl.ds((j - 1) * blk, 2 * blk)
            k, v = ak_ref[0, k_rows, :], av_ref[0, k_rows, :]
        store_stats(0, q_rows, block_stats(aq_ref[0, q_rows, :], k, v,
                                           bias_first if j == 0 else bias_any))

    khist[...] = ak_ref[0, ta - blk:ta, :]
    vhist[...] = av_ref[0, ta - blk:ta, :]

    for r in range(nph):
        for j in range(reg // blk):
            q_rows = pl.ds(j * blk, blk)
            if j == 0:
                last = pl.ds(reg - blk, blk)
                k = window(kprm, (prev, r, last), kprm, (cur, r, q_rows))
                v = window(vprm, (prev, r, last), vprm, (cur, r, q_rows))
            else:
                k_rows = pl.ds((j - 1) * blk, 2 * blk)
                k, v = kprm[cur, r, k_rows, :], vprm[cur, r, k_rows, :]
            stats = block_stats(qprm[r, q_rows, :], k, v, bias_first if j == 0 else bias_any)
            store_stats(1, pl.ds(r * reg + j * blk, blk), stats)
        sub = reg // nph
        for r16 in range(nph):
            for j in range(sub // blk):
                q_rows = pl.ds(r16 + nph * blk * j, blk, stride=nph)
                if j == 0:
                    tail = pl.ds(r16 + nph * (sub - blk), blk, stride=nph)
                    k = window(kprm, (prev, r, tail), kprm, (cur, r, q_rows))
                    v = window(vprm, (prev, r, tail), vprm, (cur, r, q_rows))
                else:
                    k_rows = pl.ds(r16 + nph * blk * (j - 1), 2 * blk, stride=nph)
                    k, v = kprm[cur, r, k_rows, :], vprm[cur, r, k_rows, :]
                stats = block_stats(qprm[r, q_rows, :], k, v,
                                    bias_first if j == 0 else bias_any)
                store_stats(2, pl.ds(r * reg + r16 + nph * blk * j, blk, stride=nph), stats)

    def merge(ci, carry):
        pieces = reg // ATT_MERGE_ROWS
        r = ci // pieces
        i0 = (ci % pieces) * ATT_MERGE_ROWS
        nat_rows = pl.ds(r + nph * i0, ATT_MERGE_ROWS, stride=nph)
        prm_rows = pl.ds(pl.multiple_of(ci * ATT_MERGE_ROWS, ATT_MERGE_ROWS), ATT_MERGE_ROWS)
        rows = (nat_rows, prm_rows, prm_rows)
        ms = [m_ref[pi, rw, :] for pi, rw in enumerate(rows)]
        m_all = functools.reduce(jnp.maximum, ms)
        ws = [jnp.exp2(m - m_all) for m in ms]
        l_all = sum(w * l_ref[pi, rw, :] for (pi, rw), w in zip(enumerate(rows), ws))
        acc = sum(w * acc_ref[pi, rw, :] for (pi, rw), w in zip(enumerate(rows), ws))
        onat[nat_rows, :] = acc / l_all
        return carry

    lax.fori_loop(0, ta // ATT_MERGE_ROWS, merge, 0)

    def gate(ci, carry):
        rows = pl.ds(pl.multiple_of(ci * ATT_MERGE_ROWS, ATT_MERGE_ROWS), ATT_MERGE_ROWS)
        o_ref[0, rows, :] = (onat[rows, :] * az_ref[0, rows, :].astype(F32)).astype(BF16)
        return carry

    lax.fori_loop(0, ta // ATT_MERGE_ROWS, gate, 0)


def _attention(aq, ak, av, az):
    b, s, w = aq.shape
    assert s % ATT_TILE == 0 and w % LANES == 0
    assert DILATIONS == (1, ATT_PERM, ATT_PERM * ATT_PERM)
    assert (ATT_TILE // ATT_PERM) % ATT_MERGE_ROWS == 0
    bias = _attn_bias()
    reg = ATT_TILE // ATT_PERM
    pair_spec = pl.BlockSpec((1, ATT_TILE, LANES), lambda bi, h, t: (bi, t, h))
    hist = pltpu.VMEM((ATT_BLOCK, LANES), F32)
    prm = pltpu.VMEM((2, ATT_PERM, reg, LANES), F32)
    stat = pltpu.VMEM((len(DILATIONS), ATT_TILE, LANES), F32)
    return pl.pallas_call(
        _attn_kernel,
        grid=(b, w // LANES, s // ATT_TILE),
        in_specs=[pl.BlockSpec(bias.shape, lambda bi, h, t: (0, 0, 0))] + [pair_spec] * 4,
        out_specs=pair_spec,
        out_shape=jax.ShapeDtypeStruct((b, s, w), BF16),
        scratch_shapes=[hist, hist, prm, prm, pltpu.VMEM((ATT_PERM, reg, LANES), F32),
                        stat, stat, stat, pltpu.VMEM((ATT_TILE, LANES), F32)],
        compiler_params=pltpu.CompilerParams(
            dimension_semantics=("arbitrary", "arbitrary", "arbitrary"),
            vmem_limit_bytes=VMEM_LIMIT),
        name="dilated_attention",
    )(bias, aq, ak, av, az)


HG_DIAG = 8
HG_LEVELS = (8, 16, 32)
SUBLANES = 8


def _hgrn_constants():
    c = HG_CHUNK
    t = np.arange(c)
    tri = (t[:, None] >= t[None, :]).astype(np.float32)
    first = (t // HG_DIAG) * HG_DIAG
    half = np.concatenate([tri, tri - 0.5 * (tri[first] + tri[first + HG_DIAG - 1])], axis=0)
    coef = np.concatenate([half, half], axis=1)
    ts, ss = t[:, None], t[None, :]
    level = np.full((c, c), len(HG_LEVELS) + 1, np.int32)
    level[(ts // HG_DIAG == ss // HG_DIAG) & (ts >= ss)] = 0
    for i, lb in enumerate(HG_LEVELS):
        m = ((ts // (2 * lb) == ss // (2 * lb)) & ((ts // lb) % 2 == 1)
             & ((ss // lb) % 2 == 0))
        level[m] = i + 1
    return jnp.asarray(coef, BF16), jnp.asarray(level)


def _hgrn_out_kernel(coef_ref, level_ref, x_ref, nw_ref, w_in32_ref, lbl_ref, gn_ref,
                     oat_ref, w32_ref, fw_ref, o_ref,
                     w_in_ref, w_ref, state_ref, qs_ref, g_ref, kk_ref, vi_ref, zs_ref,
                     qd_ref, oi_ref, kv_ref, dec_ref, ohg_ref):
    @pl.when(pl.program_id(1) == 0)
    def _():
        state_ref[...] = jnp.zeros_like(state_ref)
        w_in_ref[...] = w_in32_ref[0].astype(BF16)
        w_ref[...] = w32_ref[0].astype(BF16)

    lg = lbl_ref[...]
    ex = jnp.exp(lg - jnp.max(lg, axis=0, keepdims=True))
    lb = jnp.clip(ex[0:1, :] / jnp.sum(ex, axis=0, keepdims=True), 1e-6, 1.0 - 1e-6)
    f_mid = 0.5 * (1.0 + lb)
    f_amp = 0.5 * (1.0 - lb)

    def store_q(p):
        qs_ref[...] = _silu(p).astype(BF16)

    def store_f(p):
        f = f_mid + f_amp * jnp.tanh(0.5 * p)
        g2 = jnp.log2(f)
        hi_bits = lax.bitcast_convert_type(g2, jnp.uint32) & jnp.uint32(0xFFFF0000)
        g2_hi = lax.bitcast_convert_type(hi_bits, F32)
        g2_lo = (g2 - g2_hi).astype(BF16)
        g2_hi = g2_hi.astype(BF16)
        for h in range(HG_HEADS):
            g_ref[:, 2 * h * HG_DK:(2 * h + 1) * HG_DK] = g2_hi[:, h * HG_DK:(h + 1) * HG_DK]
            g_ref[:, (2 * h + 1) * HG_DK:(2 * h + 2) * HG_DK] = g2_lo[:, h * HG_DK:(h + 1) * HG_DK]
        kk_ref[...] = (1.0 - f).astype(BF16)

    def store_z(p):
        zs_ref[...] = _silu(p).astype(BF16)

    def store_i(p):
        vi_ref[...] = p.astype(BF16)

    u = _normalized(x_ref[0], nw_ref[...])

    def project(j):
        return jnp.dot(u, w_in_ref[:, j * SECTION:(j + 1) * SECTION],
                       preferred_element_type=F32)

    p_f = project(1)
    p_q = project(0)
    store_f(p_f)
    store_q(p_q)

    c = HG_CHUNK
    n_chunks = HG_ROWS // c
    half = HG_HEADS * HG_DK
    coef = coef_ref[...]
    level = level_ref[...]
    units = [(h, ci) for h in range(HG_HEADS) for ci in range(n_chunks)]

    def rows_of(ci):
        return pl.ds(ci * c, c)

    def cols_of(h, width=HG_DK):
        return pl.ds(h * width, width)

    def cum_decay(h, ci):
        g2 = g_ref[rows_of(ci), cols_of(h, 2 * HG_DK)]
        return jnp.dot(coef, jnp.concatenate([g2[:, :HG_DK], g2[:, HG_DK:]], axis=0),
                       preferred_element_type=F32)

    def level_operands(h, ci, be):
        q = qs_ref[rows_of(ci), cols_of(h)]
        k = kk_ref[rows_of(ci), cols_of(h)]
        scaled = lambda t, w: t * w.astype(BF16)
        b = be[0:c]
        e_diag = be[c:2 * c]
        ops = [(scaled(q, jnp.exp2(e_diag)), scaled(k, jnp.exp2(-e_diag)))]
        for lb_rows in HG_LEVELS:
            ref = jnp.concatenate(
                [jnp.broadcast_to(b[p + lb_rows - 1:p + lb_rows, :], (2 * lb_rows, HG_DK))
                 for p in range(0, c, 2 * lb_rows)], axis=0)
            w = jnp.exp2(_neg_abs(b - ref))
            ops.append((scaled(q, w), scaled(k, w)))
        b_last = b[c - 1:c, :]
        return ops, scaled(k, jnp.exp2(b_last - b)), scaled(q, jnp.exp2(b)), jnp.exp2(b_last)

    def intra_scores(ops):
        scores = jnp.where(level == 0, _dot_nt(*ops[0]), 0.0)
        for i in range(len(HG_LEVELS)):
            scores = jnp.where(level == i + 1, _dot_nt(*ops[i + 1]), scores)
        return scores.astype(BF16)

    stage1 = [cum_decay(h, ci) for h, ci in units]
    p_z = project(3)
    stage2 = [level_operands(h, ci, be) for (h, ci), be in zip(units, stage1)]
    p_i = project(2)
    store_z(p_z)
    o_ref[0] = x_ref[0] + jnp.dot(oat_ref[0], w_ref[half:2 * half, :],
                                  preferred_element_type=F32)
    store_i(p_i)

    stage3 = [intra_scores(ops) for ops, _, _, _ in stage2]
    for (h, ci), (_, k_dec, q_dec, decay), scores in zip(units, stage2, stage3):
        v = vi_ref[rows_of(ci), cols_of(h)]
        oi_ref[rows_of(ci), cols_of(h)] = jnp.dot(scores, v, preferred_element_type=F32)
        kv_ref[h, ci] = _dot_tn(v, k_dec)
        qd_ref[rows_of(ci), cols_of(h)] = q_dec
        dec_ref[h, ci] = jnp.broadcast_to(decay, dec_ref.shape[2:])

    states = [state_ref[h] for h in range(HG_HEADS)]
    for ci in range(n_chunks):
        for h in range(HG_HEADS):
            rows, cols = rows_of(ci), cols_of(h)
            o = oi_ref[rows, cols] + _dot_nt(qd_ref[rows, cols], states[h].astype(BF16))
            states[h] = states[h] * dec_ref[h, ci, 0:1, :] + kv_ref[h, ci]
            z = zs_ref[rows, cols].astype(F32)
            ms = jnp.mean(o * o, axis=-1, keepdims=True)
            ohg_ref[rows, cols] = (o * lax.rsqrt(ms + NORM_EPS) * gn_ref[:, cols] * z
                                   ).astype(BF16)
    for h in range(HG_HEADS):
        state_ref[h] = states[h]

    hres = o_ref[0] + jnp.dot(ohg_ref[...], w_ref[0:half, :], preferred_element_type=F32)
    ms = jnp.mean(hres * hres, axis=-1, keepdims=True)
    o_ref[0] = hres * lax.rsqrt(ms + NORM_EPS) * fw_ref[...]


def _hgrn_out(x, norm_w, w_in, lb_logits, hg_norm_w, oat, w_out, final_norm_w):
    b, s, d_model = x.shape
    w = HG_HEADS * HG_DK
    assert s % HG_ROWS == 0 and w == SECTION
    assert w_out.shape == (1, w + oat.shape[-1], d_model)
    coef, level = _hgrn_constants()
    n_chunks = HG_ROWS // HG_CHUNK
    tile = lambda width: pl.BlockSpec((1, HG_ROWS, width), lambda bi, t: (bi, t, 0))
    const = lambda a: pl.BlockSpec(a.shape, lambda bi, t: (0,) * a.ndim)
    act = lambda width: pltpu.VMEM((HG_ROWS, width), BF16)
    return pl.pallas_call(
        _hgrn_out_kernel,
        grid=(b, s // HG_ROWS),
        in_specs=[const(coef), const(level), tile(d_model), const(norm_w),
                  _branch_weight_spec(w_in, 0), const(lb_logits), const(hg_norm_w),
                  tile(oat.shape[-1]), const(w_out), const(final_norm_w)],
        out_specs=tile(d_model),
        out_shape=jax.ShapeDtypeStruct((b, s, d_model), F32),
        scratch_shapes=[pltpu.VMEM((d_model, 4 * SECTION), BF16),
                        pltpu.VMEM(w_out.shape[1:], BF16),
                        pltpu.VMEM((HG_HEADS, HG_DK, HG_DK), F32),
                        act(w), act(2 * w), act(w), act(w), act(w),
                        act(w),
                        pltpu.VMEM((HG_ROWS, w), F32),
                        pltpu.VMEM((HG_HEADS, n_chunks, HG_DK, HG_DK), F32),
                        pltpu.VMEM((HG_HEADS, n_chunks, SUBLANES, HG_DK), F32),
                        act(w)],
        compiler_params=pltpu.CompilerParams(
            dimension_semantics=("parallel", "arbitrary"),
            vmem_limit_bytes=VMEM_LIMIT),
        name="hgrn2_branch_and_output",
    )(coef, level, x, norm_w, w_in, lb_logits, hg_norm_w, oat, w_out, final_norm_w)


def kernel(x, norm_w, w_in, hgrn_lb_logits, hg_norm_w, w_out, final_norm_w):
    b, s, d_model = x.shape
    assert norm_w.shape[0] == 1 and w_in.shape[0] == 1 and w_out.shape[0] == 1
    aq, ak, av, az = _attn_projection(x.reshape(b * s, d_model), norm_w, w_in, s)
    to3 = lambda a: a.reshape(b, s, a.shape[-1])
    oat = _attention(to3(aq), to3(ak), to3(av), to3(az))
    return _hgrn_out(x, norm_w, w_in, hgrn_lb_logits, hg_norm_w, oat, w_out,
                     final_norm_w.reshape(1, d_model))
```

```python
import functools

import numpy as np
import jax
import jax.numpy as jnp
from jax import lax
from jax.experimental import pallas as pl
from jax.experimental.pallas import tpu as pltpu

F32 = jnp.float32
BF16 = jnp.bfloat16

NORM_EPS = 1e-6
ROPE_THETA = 10000.0
LANES = 128

HG_HEADS = 4
HG_DK = 128
HG_CHUNK = 64
AT_HEAD_DIM = 64
SECTION = 512
DILATIONS = (1, 4, 16)
ATT_BLOCK = 128
ATT_TILE = 2 * ATT_BLOCK * max(DILATIONS)
NEG_BIG = -1e30

PROJ_ROWS = 1024
HG_ROWS = 512
ATT_PERM = 4
ATT_MERGE_ROWS = 1024
LOG2E = 1.4426950408889634
VMEM_LIMIT = 56 * 1024 * 1024


def _normalized(x, gain):
    ms = jnp.mean(x * x, axis=-1, keepdims=True)
    return (x * lax.rsqrt(ms + NORM_EPS) * gain).astype(BF16)


def _silu(p):
    return p * (0.5 * jnp.tanh(0.5 * p) + 0.5)


def _project_sections(u, w_ref, plan):
    pending = None
    for j, finish in plan:
        p = jnp.dot(u, w_ref[:, j * SECTION:(j + 1) * SECTION], preferred_element_type=F32)
        if pending is not None:
            pending()
        pending = functools.partial(finish, p)
    pending()


def _dot_nt(a, b):
    return lax.dot_general(a, b, (((1,), (1,)), ((), ())), preferred_element_type=F32)


def _dot_tn(a, b):
    return lax.dot_general(a, b, (((0,), (0,)), ((), ())), preferred_element_type=F32)


def _neg_abs(x):
    bits = lax.bitcast_convert_type(x, jnp.uint32) | jnp.uint32(0x80000000)
    return lax.bitcast_convert_type(bits, F32)


def _attn_proj_kernel(x_ref, nw_ref, w32_ref, cos_ref, sina_ref, sinb_ref,
                      aq_ref, ak_ref, av_ref, az_ref, w_ref):
    @pl.when(pl.program_id(0) == 0)
    def _():
        w_ref[...] = w32_ref[0].astype(BF16)

    u = _normalized(x_ref[...], nw_ref[...])

    def store_rope(dst_ref, scale, p):
        half = AT_HEAD_DIM // 2
        for c in range(SECTION // LANES):
            xc = p[:, c * LANES:(c + 1) * LANES]
            r = (xc * cos_ref[...] + pltpu.roll(xc, LANES - half, 1) * sina_ref[...]
                 + pltpu.roll(xc, half, 1) * sinb_ref[...])
            dst_ref[:, c * LANES:(c + 1) * LANES] = r * scale

    def store_z(p):
        az_ref[...] = _silu(p).astype(BF16)

    def store_v(p):
        av_ref[...] = p

    _project_sections(u, w_ref, [
        (3, store_z),
        (0, functools.partial(store_rope, aq_ref, AT_HEAD_DIM ** -0.5 * LOG2E)),
        (1, functools.partial(store_rope, ak_ref, 1.0)),
        (2, store_v)])


@functools.lru_cache(maxsize=None)
def _rope_tables(seq):
    half = AT_HEAD_DIM // 2
    inv_freq = 1.0 / (ROPE_THETA ** (np.arange(half, dtype=np.float64) / half))
    ang = np.arange(seq, dtype=np.float64)[:, None] * inv_freq[None, :]
    cos = np.cos(ang)
    sin = np.sin(ang)
    zero = np.zeros_like(sin)
    reps = LANES // AT_HEAD_DIM
    cos_t = np.tile(np.concatenate([cos, cos], axis=1), (1, reps))
    sina_t = np.tile(np.concatenate([-sin, zero], axis=1), (1, reps))
    sinb_t = np.tile(np.concatenate([zero, sin], axis=1), (1, reps))
    return tuple(np.asarray(t, np.float32) for t in (cos_t, sina_t, sinb_t))


def _branch_weight_spec(w_in, branch):
    assert w_in.shape[0] == 1 and w_in.shape[2] == 8 * SECTION
    return pl.BlockSpec((1, w_in.shape[1], 4 * SECTION),
                        lambda *_: (0, 0, branch), pipeline_mode=pl.Buffered(1))


def _attn_projection(x2, norm_w, w_in, seq):
    rows, d_model = x2.shape
    assert rows % PROJ_ROWS == 0 and seq % PROJ_ROWS == 0
    tables = _rope_tables(seq)
    seq_tiles = seq // PROJ_ROWS
    row_spec = lambda w: pl.BlockSpec((PROJ_ROWS, w), lambda i: (i, 0))
    tab_spec = pl.BlockSpec((PROJ_ROWS, LANES), lambda i: (i % seq_tiles, 0))
    full = lambda a: pl.BlockSpec(a.shape, lambda i: (0, 0))
    out_dtypes = (F32, F32, F32, BF16)
    return pl.pallas_call(
        _attn_proj_kernel,
        grid=(rows // PROJ_ROWS,),
        in_specs=[row_spec(d_model), full(norm_w), _branch_weight_spec(w_in, 1)]
                 + [tab_spec] * 3,
        out_specs=[row_spec(SECTION)] * len(out_dtypes),
        out_shape=[jax.ShapeDtypeStruct((rows, SECTION), dt) for dt in out_dtypes],
        scratch_shapes=[pltpu.VMEM((d_model, 4 * SECTION), BF16)],
        compiler_params=pltpu.CompilerParams(
            dimension_semantics=("arbitrary",), vmem_limit_bytes=VMEM_LIMIT),
        name="attention_projection",
    )(x2, norm_w, w_in, *tables)


def _attn_bias():
    qi = np.arange(ATT_BLOCK)[:, None]
    kj = np.arange(2 * ATT_BLOCK)[None, :]
    dist = ATT_BLOCK + qi - kj
    valid = (dist >= 0) & (dist <= ATT_BLOCK)
    b0 = np.where(valid, 0.0, NEG_BIG)
    b1 = np.where(valid & (kj >= ATT_BLOCK), 0.0, NEG_BIG)
    return jnp.asarray(np.stack([b0, b1]), F32)


def _attn_kernel(bias_ref, aq_ref, ak_ref, av_ref, az_ref, o_ref,
                 khist, vhist, kprm, vprm, qprm, m_ref, l_ref, acc_ref, onat):
    ta = ATT_TILE
    blk = ATT_BLOCK
    nph = ATT_PERM
    reg = ta // nph
    tile = pl.program_id(2)
    cur = tile % 2
    prev = 1 - cur

    @pl.when((pl.program_id(0) == 0) & (pl.program_id(1) == 0) & (tile == 0))
    def _():
        khist[...] = jnp.zeros_like(khist)
        vhist[...] = jnp.zeros_like(vhist)
        kprm[1] = jnp.zeros((nph, reg, LANES), F32)
        vprm[1] = jnp.zeros((nph, reg, LANES), F32)

    for r in range(nph):
        kprm[cur, r] = ak_ref[0, pl.ds(r, reg, stride=nph), :]
        vprm[cur, r] = av_ref[0, pl.ds(r, reg, stride=nph), :]
        qprm[r] = aq_ref[0, pl.ds(r, reg, stride=nph), :]

    first_head = lax.broadcasted_iota(jnp.int32, (blk, LANES), 1) < AT_HEAD_DIM
    ones = jnp.ones((2 * blk, LANES), BF16)
    bias_any = bias_ref[0]
    bias_first = bias_ref[(tile == 0).astype(jnp.int32)]

    def block_stats(q, k, v, bias):
        k = k.astype(BF16)
        v = v.astype(BF16)
        q2 = jnp.concatenate([jnp.where(first_head, q, 0.0),
                              jnp.where(first_head, 0.0, q)], axis=0).astype(BF16)
        s = _dot_nt(q2, k) + jnp.concatenate([bias, bias], axis=0)
        m = jnp.max(s, axis=-1, keepdims=True)
        p = jnp.exp2(s - m).astype(BF16)
        pv = jnp.dot(p, jnp.concatenate([v, ones], axis=1), preferred_element_type=F32)
        return (jnp.where(first_head, m[:blk], m[blk:]),
                jnp.where(first_head, pv[:blk, LANES:], pv[blk:, LANES:]),
                jnp.where(first_head, pv[:blk, :LANES], pv[blk:, :LANES]))

    def store_stats(pi, rows, stats):
        m_ref[pi, rows, :], l_ref[pi, rows, :], acc_ref[pi, rows, :] = stats

    def window(first_ref, first_rows, ref, rows):
        return jnp.concatenate([first_ref[first_rows], ref[rows]], axis=0)

    for j in range(ta // blk):
        q_rows = pl.ds(j * blk, blk)
        if j == 0:
            k = window(khist, (slice(None),) * 2, ak_ref, (0, q_rows))
            v = window(vhist, (slice(None),) * 2, av_ref, (0, q_rows))
        else:
            k_rows = pl.ds((j - 1) * blk, 2 * blk)
            k, v = ak_ref[0, k_rows, :], av_ref[0, k_rows, :]
        store_stats(0, q_rows, block_stats(aq_ref[0, q_rows, :], k, v,
                                           bias_first if j == 0 else bias_any))

    khist[...] = ak_ref[0, ta - blk:ta, :]
    vhist[...] = av_ref[0, ta - blk:ta, :]

    for r in range(nph):
        for j in range(reg // blk):
            q_rows = pl.ds(j * blk, blk)
            if j == 0:
                last = pl.ds(reg - blk, blk)
                k = window(kprm, (prev, r, last), kprm, (cur, r, q_rows))
                v = window(vprm, (prev, r, last), vprm, (cur, r, q_rows))
            else:
                k_rows = pl.ds((j - 1) * blk, 2 * blk)
                k, v = kprm[cur, r, k_rows, :], vprm[cur, r, k_rows, :]
            stats = block_stats(qprm[r, q_rows, :], k, v, bias_first if j == 0 else bias_any)
            store_stats(1, pl.ds(r * reg + j * blk, blk), stats)
        sub = reg // nph
        for r16 in range(nph):
            for j in range(sub // blk):
                q_rows = pl.ds(r16 + nph * blk * j, blk, stride=nph)
                if j == 0:
                    tail = pl.ds(r16 + nph * (sub - blk), blk, stride=nph)
                    k = window(kprm, (prev, r, tail), kprm, (cur, r, q_rows))
                    v = window(vprm, (prev, r, tail), vprm, (cur, r, q_rows))
                else:
                    k_rows = pl.ds(r16 + nph * blk * (j - 1), 2 * blk, stride=nph)
                    k, v = kprm[cur, r, k_rows, :], vprm[cur, r, k_rows, :]
                stats = block_stats(qprm[r, q_rows, :], k, v,
                                    bias_first if j == 0 else bias_any)
                store_stats(2, pl.ds(r * reg + r16 + nph * blk * j, blk, stride=nph), stats)

    def merge(ci, carry):
        pieces = reg // ATT_MERGE_ROWS
        r = ci // pieces
        i0 = (ci % pieces) * ATT_MERGE_ROWS
        nat_rows = pl.ds(r + nph * i0, ATT_MERGE_ROWS, stride=nph)
        prm_rows = pl.ds(pl.multiple_of(ci * ATT_MERGE_ROWS, ATT_MERGE_ROWS), ATT_MERGE_ROWS)
        rows = (nat_rows, prm_rows, prm_rows)
        ms = [m_ref[pi, rw, :] for pi, rw in enumerate(rows)]
        m_all = functools.reduce(jnp.maximum, ms)
        ws = [jnp.exp2(m - m_all) for m in ms]
        l_all = sum(w * l_ref[pi, rw, :] for (pi, rw), w in zip(enumerate(rows), ws))
        acc = sum(w * acc_ref[pi, rw, :] for (pi, rw), w in zip(enumerate(rows), ws))
        onat[nat_rows, :] = acc / l_all
        return carry

    lax.fori_loop(0, ta // ATT_MERGE_ROWS, merge, 0)

    def gate(ci, carry):
        rows = pl.ds(pl.multiple_of(ci * ATT_MERGE_ROWS, ATT_MERGE_ROWS), ATT_MERGE_ROWS)
        o_ref[0, rows, :] = (onat[rows, :] * az_ref[0, rows, :].astype(F32)).astype(BF16)
        return carry

    lax.fori_loop(0, ta // ATT_MERGE_ROWS, gate, 0)


def _attention(aq, ak, av, az):
    b, s, w = aq.shape
    assert s % ATT_TILE == 0 and w % LANES == 0
    assert DILATIONS == (1, ATT_PERM, ATT_PERM * ATT_PERM)
    assert (ATT_TILE // ATT_PERM) % ATT_MERGE_ROWS == 0
    bias = _attn_bias()
    reg = ATT_TILE // ATT_PERM
    pair_spec = pl.BlockSpec((1, ATT_TILE, LANES), lambda bi, h, t: (bi, t, h))
    hist = pltpu.VMEM((ATT_BLOCK, LANES), F32)
    prm = pltpu.VMEM((2, ATT_PERM, reg, LANES), F32)
    stat = pltpu.VMEM((len(DILATIONS), ATT_TILE, LANES), F32)
    return pl.pallas_call(
        _attn_kernel,
        grid=(b, w // LANES, s // ATT_TILE),
        in_specs=[pl.BlockSpec(bias.shape, lambda bi, h, t: (0, 0, 0))] + [pair_spec] * 4,
        out_specs=pair_spec,
        out_shape=jax.ShapeDtypeStruct((b, s, w), BF16),
        scratch_shapes=[hist, hist, prm, prm, pltpu.VMEM((ATT_PERM, reg, LANES), F32),
                        stat, stat, stat, pltpu.VMEM((ATT_TILE, LANES), F32)],
        compiler_params=pltpu.CompilerParams(
            dimension_semantics=("arbitrary", "arbitrary", "arbitrary"),
            vmem_limit_bytes=VMEM_LIMIT),
        name="dilated_attention",
    )(bias, aq, ak, av, az)


HG_DIAG = 8
HG_LEVELS = (8, 16, 32)
SUBLANES = 8


def _hgrn_constants():
    c = HG_CHUNK
    t = np.arange(c)
    tri = (t[:, None] >= t[None, :]).astype(np.float32)
    first = (t // HG_DIAG) * HG_DIAG
    half = np.concatenate([tri, tri - 0.5 * (tri[first] + tri[first + HG_DIAG - 1])], axis=0)
    coef = np.concatenate([half, half], axis=1)
    ts, ss = t[:, None], t[None, :]
    level = np.full((c, c), len(HG_LEVELS) + 1, np.int32)
    level[(ts // HG_DIAG == ss // HG_DIAG) & (ts >= ss)] = 0
    for i, lb in enumerate(HG_LEVELS):
        m = ((ts // (2 * lb) == ss // (2 * lb)) & ((ts // lb) % 2 == 1)
             & ((ss // lb) % 2 == 0))
        level[m] = i + 1
    return jnp.asarray(coef, BF16), jnp.asarray(level)


def _hgrn_out_kernel(coef_ref, level_ref, x_ref, nw_ref, w_in32_ref, lbl_ref, gn_ref,
                     oat_ref, w32_ref, fw_ref, o_ref,
                     w_in_ref, w_ref, state_ref, qs_ref, g_ref, kk_ref, vi_ref, zs_ref,
                     qd_ref, oi_ref, kv_ref, dec_ref, ohg_ref):
    @pl.when(pl.program_id(1) == 0)
    def _():
        state_ref[...] = jnp.zeros_like(state_ref)
        w_in_ref[...] = w_in32_ref[0].astype(BF16)
        w_ref[...] = w32_ref[0].astype(BF16)

    lg = lbl_ref[...]
    ex = jnp.exp(lg - jnp.max(lg, axis=0, keepdims=True))
    lb = jnp.clip(ex[0:1, :] / jnp.sum(ex, axis=0, keepdims=True), 1e-6, 1.0 - 1e-6)
    f_mid = 0.5 * (1.0 + lb)
    f_amp = 0.5 * (1.0 - lb)

    def store_q(p):
        qs_ref[...] = _silu(p).astype(BF16)

    def store_f(p):
        f = f_mid + f_amp * jnp.tanh(0.5 * p)
        g2 = jnp.log2(f)
        hi_bits = lax.bitcast_convert_type(g2, jnp.uint32) & jnp.uint32(0xFFFF0000)
        g2_hi = lax.bitcast_convert_type(hi_bits, F32)
        g2_lo = (g2 - g2_hi).astype(BF16)
        g2_hi = g2_hi.astype(BF16)
        for h in range(HG_HEADS):
            g_ref[:, 2 * h * HG_DK:(2 * h + 1) * HG_DK] = g2_hi[:, h * HG_DK:(h + 1) * HG_DK]
            g_ref[:, (2 * h + 1) * HG_DK:(2 * h + 2) * HG_DK] = g2_lo[:, h * HG_DK:(h + 1) * HG_DK]
        kk_ref[...] = (1.0 - f).astype(BF16)

    def store_z(p):
        zs_ref[...] = _silu(p).astype(BF16)

    def store_i(p):
        vi_ref[...] = p.astype(BF16)

    u = _normalized(x_ref[0], nw_ref[...])

    def project(j):
        return jnp.dot(u, w_in_ref[:, j * SECTION:(j + 1) * SECTION],
                       preferred_element_type=F32)

    p_f = project(1)
    p_q = project(0)
    store_f(p_f)
    store_q(p_q)

    c = HG_CHUNK
    n_chunks = HG_ROWS // c
    half = HG_HEADS * HG_DK
    coef = coef_ref[...]
    level = level_ref[...]
    units = [(h, ci) for h in range(HG_HEADS) for ci in range(n_chunks)]

    def rows_of(ci):
        return pl.ds(ci * c, c)

    def cols_of(h, width=HG_DK):
        return pl.ds(h * width, width)

    def cum_decay(h, ci):
        g2 = g_ref[rows_of(ci), cols_of(h, 2 * HG_DK)]
        return jnp.dot(coef, jnp.concatenate([g2[:, :HG_DK], g2[:, HG_DK:]], axis=0),
                       preferred_element_type=F32)

    def level_operands(h, ci, be):
        q = qs_ref[rows_of(ci), cols_of(h)]
        k = kk_ref[rows_of(ci), cols_of(h)]
        scaled = lambda t, w: t * w.astype(BF16)
        b = be[0:c]
        e_diag = be[c:2 * c]
        ops = [(scaled(q, jnp.exp2(e_diag)), scaled(k, jnp.exp2(-e_diag)))]
        row = lax.broadcasted_iota(jnp.int32, (c, HG_DK), 0)
        for lb_rows in HG_LEVELS:
            ref = jnp.concatenate(
                [jnp.broadcast_to(b[p + lb_rows - 1:p + lb_rows, :], (2 * lb_rows, HG_DK))
                 for p in range(0, c, 2 * lb_rows)], axis=0)
            w = jnp.exp2(_neg_abs(b - ref))
            both = scaled(jnp.where((row & lb_rows) != 0, q, k), w)
            ops.append((both, both))
        b_last = b[c - 1:c, :]
        return ops, scaled(k, jnp.exp2(b_last - b)), scaled(q, jnp.exp2(b)), jnp.exp2(b_last)

    def intra_scores(ops):
        scores = jnp.where(level == 0, _dot_nt(*ops[0]), 0.0)
        for i in range(len(HG_LEVELS)):
            scores = jnp.where(level == i + 1, _dot_nt(*ops[i + 1]), scores)
        return scores.astype(BF16)

    stage1 = [cum_decay(h, ci) for h, ci in units]
    p_z = project(3)
    stage2 = [level_operands(h, ci, be) for (h, ci), be in zip(units, stage1)]
    p_i = project(2)
    store_z(p_z)
    o_ref[0] = x_ref[0] + jnp.dot(oat_ref[0], w_ref[half:2 * half, :],
                                  preferred_element_type=F32)
    store_i(p_i)

    stage3 = [intra_scores(ops) for ops, _, _, _ in stage2]
    for (h, ci), (_, k_dec, q_dec, decay), scores in zip(units, stage2, stage3):
        v = vi_ref[rows_of(ci), cols_of(h)]
        oi_ref[rows_of(ci), cols_of(h)] = jnp.dot(scores, v, preferred_element_type=F32)
        kv_ref[h, ci] = _dot_tn(v, k_dec)
        qd_ref[rows_of(ci), cols_of(h)] = q_dec
        dec_ref[h, ci] = jnp.broadcast_to(decay, dec_ref.shape[2:])

    states = [state_ref[h] for h in range(HG_HEADS)]
    for ci in range(n_chunks):
        for h in range(HG_HEADS):
            rows, cols = rows_of(ci), cols_of(h)
            o = oi_ref[rows, cols] + _dot_nt(qd_ref[rows, cols], states[h].astype(BF16))
            states[h] = states[h] * dec_ref[h, ci, 0:1, :] + kv_ref[h, ci]
            z = zs_ref[rows, cols].astype(F32)
            ms = jnp.mean(o * o, axis=-1, keepdims=True)
            ohg_ref[rows, cols] = (o * lax.rsqrt(ms + NORM_EPS) * gn_ref[:, cols] * z
                                   ).astype(BF16)
    for h in range(HG_HEADS):
        state_ref[h] = states[h]

    hres = o_ref[0] + jnp.dot(ohg_ref[...], w_ref[0:half, :], preferred_element_type=F32)
    ms = jnp.mean(hres * hres, axis=-1, keepdims=True)
    o_ref[0] = hres * lax.rsqrt(ms + NORM_EPS) * fw_ref[...]


def _hgrn_out(x, norm_w, w_in, lb_logits, hg_norm_w, oat, w_out, final_norm_w):
    b, s, d_model = x.shape
    w = HG_HEADS * HG_DK
    assert s % HG_ROWS == 0 and w == SECTION
    assert w_out.shape == (1, w + oat.shape[-1], d_model)
    coef, level = _hgrn_constants()
    n_chunks = HG_ROWS // HG_CHUNK
    tile = lambda width: pl.BlockSpec((1, HG_ROWS, width), lambda bi, t: (bi, t, 0))
    const = lambda a: pl.BlockSpec(a.shape, lambda bi, t: (0,) * a.ndim)
    act = lambda width: pltpu.VMEM((HG_ROWS, width), BF16)
    return pl.pallas_call(
        _hgrn_out_kernel,
        grid=(b, s // HG_ROWS),
        in_specs=[const(coef), const(level), tile(d_model), const(norm_w),
                  _branch_weight_spec(w_in, 0), const(lb_logits), const(hg_norm_w),
                  tile(oat.shape[-1]), const(w_out), const(final_norm_w)],
        out_specs=tile(d_model),
        out_shape=jax.ShapeDtypeStruct((b, s, d_model), F32),
        scratch_shapes=[pltpu.VMEM((d_model, 4 * SECTION), BF16),
                        pltpu.VMEM(w_out.shape[1:], BF16),
                        pltpu.VMEM((HG_HEADS, HG_DK, HG_DK), F32),
                        act(w), act(2 * w), act(w), act(w), act(w),
                        act(w),
                        pltpu.VMEM((HG_ROWS, w), F32),
                        pltpu.VMEM((HG_HEADS, n_chunks, HG_DK, HG_DK), F32),
                        pltpu.VMEM((HG_HEADS, n_chunks, SUBLANES, HG_DK), F32),
                        act(w)],
        compiler_params=pltpu.CompilerParams(
            dimension_semantics=("parallel", "arbitrary"),
            vmem_limit_bytes=VMEM_LIMIT),
        name="hgrn2_branch_and_output",
    )(coef, level, x, norm_w, w_in, lb_logits, hg_norm_w, oat, w_out, final_norm_w)


def kernel(x, norm_w, w_in, hgrn_lb_logits, hg_norm_w, w_out, final_norm_w):
    b, s, d_model = x.shape
    assert norm_w.shape[0] == 1 and w_in.shape[0] == 1 and w_out.shape[0] == 1
    aq, ak, av, az = _attn_projection(x.reshape(b * s, d_model), norm_w, w_in, s)
    to3 = lambda a: a.reshape(b, s, a.shape[-1])
    oat = _attention(to3(aq), to3(ak), to3(av), to3(az))
    return _hgrn_out(x, norm_w, w_in, hgrn_lb_logits, hg_norm_w, oat, w_out,
                     final_norm_w.reshape(1, d_model))
```

```python
import functools

import numpy as np
import jax
import jax.numpy as jnp
from jax import lax
from jax.experimental import pallas as pl
from jax.experimental.pallas import tpu as pltpu

F32 = jnp.float32
BF16 = jnp.bfloat16

NORM_EPS = 1e-6
ROPE_THETA = 10000.0
LANES = 128

HG_HEADS = 4
HG_DK = 128
HG_CHUNK = 64
AT_HEAD_DIM = 64
SECTION = 512
DILATIONS = (1, 4, 16)
ATT_BLOCK = 128
ATT_TILE = 2 * ATT_BLOCK * max(DILATIONS)
NEG_BIG = -1e30

PROJ_ROWS = 1024
HG_ROWS = 512
ATT_PERM = 4
ATT_MERGE_ROWS = 1024
LOG2E = 1.4426950408889634
VMEM_LIMIT = 56 * 1024 * 1024


def _normalized(x, gain):
    ms = jnp.mean(x * x, axis=-1, keepdims=True)
    return (x * lax.rsqrt(ms + NORM_EPS) * gain).astype(BF16)


def _silu(p):
    return p * (0.5 * jnp.tanh(0.5 * p) + 0.5)


def _project_sections(u, w_ref, plan):
    pending = None
    for j, finish in plan:
        p = jnp.dot(u, w_ref[:, j * SECTION:(j + 1) * SECTION], preferred_element_type=F32)
        if pending is not None:
            pending()
        pending = functools.partial(finish, p)
    pending()


def _dot_nt(a, b):
    return lax.dot_general(a, b, (((1,), (1,)), ((), ())), preferred_element_type=F32)


def _dot_tn(a, b):
    return lax.dot_general(a, b, (((0,), (0,)), ((), ())), preferred_element_type=F32)


def _neg_abs(x):
    bits = lax.bitcast_convert_type(x, jnp.uint32) | jnp.uint32(0x80000000)
    return lax.bitcast_convert_type(bits, F32)


def _attn_proj_kernel(x_ref, nw_ref, w32_ref, cos_ref, sina_ref, sinb_ref,
                      aq_ref, ak_ref, av_ref, az_ref, w_ref):
    @pl.when(pl.program_id(0) == 0)
    def _():
        w_ref[...] = w32_ref[0].astype(BF16)

    u = _normalized(x_ref[...], nw_ref[...])

    def store_rope(dst_ref, scale, p):
        half = AT_HEAD_DIM // 2
        for c in range(SECTION // LANES):
            xc = p[:, c * LANES:(c + 1) * LANES]
            r = (xc * cos_ref[...] + pltpu.roll(xc, LANES - half, 1) * sina_ref[...]
                 + pltpu.roll(xc, half, 1) * sinb_ref[...])
            dst_ref[:, c * LANES:(c + 1) * LANES] = r * scale

    def store_z(p):
        az_ref[...] = _silu(p).astype(BF16)

    def store_v(p):
        av_ref[...] = p

    _project_sections(u, w_ref, [
        (3, store_z),
        (0, functools.partial(store_rope, aq_ref, AT_HEAD_DIM ** -0.5 * LOG2E)),
        (1, functools.partial(store_rope, ak_ref, 1.0)),
        (2, store_v)])


@functools.lru_cache(maxsize=None)
def _rope_tables(seq):
    half = AT_HEAD_DIM // 2
    inv_freq = 1.0 / (ROPE_THETA ** (np.arange(half, dtype=np.float64) / half))
    ang = np.arange(seq, dtype=np.float64)[:, None] * inv_freq[None, :]
    cos = np.cos(ang)
    sin = np.sin(ang)
    zero = np.zeros_like(sin)
    reps = LANES // AT_HEAD_DIM
    cos_t = np.tile(np.concatenate([cos, cos], axis=1), (1, reps))
    sina_t = np.tile(np.concatenate([-sin, zero], axis=1), (1, reps))
    sinb_t = np.tile(np.concatenate([zero, sin], axis=1), (1, reps))
    return tuple(np.asarray(t, np.float32) for t in (cos_t, sina_t, sinb_t))


def _branch_weight_spec(w_in, branch):
    assert w_in.shape[0] == 1 and w_in.shape[2] == 8 * SECTION
    return pl.BlockSpec((1, w_in.shape[1], 4 * SECTION),
                        lambda *_: (0, 0, branch), pipeline_mode=pl.Buffered(1))


def _attn_projection(x2, norm_w, w_in, seq):
    rows, d_model = x2.shape
    assert rows % PROJ_ROWS == 0 and seq % PROJ_ROWS == 0
    tables = _rope_tables(seq)
    seq_tiles = seq // PROJ_ROWS
    row_spec = lambda w: pl.BlockSpec((PROJ_ROWS, w), lambda i: (i, 0))
    tab_spec = pl.BlockSpec((PROJ_ROWS, LANES), lambda i: (i % seq_tiles, 0))
    full = lambda a: pl.BlockSpec(a.shape, lambda i: (0, 0))
    out_dtypes = (F32, F32, F32, BF16)
    return pl.pallas_call(
        _attn_proj_kernel,
        grid=(rows // PROJ_ROWS,),
        in_specs=[row_spec(d_model), full(norm_w), _branch_weight_spec(w_in, 1)]
                 + [tab_spec] * 3,
        out_specs=[row_spec(SECTION)] * len(out_dtypes),
        out_shape=[jax.ShapeDtypeStruct((rows, SECTION), dt) for dt in out_dtypes],
        scratch_shapes=[pltpu.VMEM((d_model, 4 * SECTION), BF16)],
        compiler_params=pltpu.CompilerParams(
            dimension_semantics=("arbitrary",), vmem_limit_bytes=VMEM_LIMIT),
        name="attention_projection",
    )(x2, norm_w, w_in, *tables)


def _attn_bias():
    qi = np.arange(ATT_BLOCK)[:, None]
    kj = np.arange(2 * ATT_BLOCK)[None, :]
    dist = ATT_BLOCK + qi - kj
    valid = (dist >= 0) & (dist <= ATT_BLOCK)
    b0 = np.where(valid, 0.0, NEG_BIG)
    b1 = np.where(valid & (kj >= ATT_BLOCK), 0.0, NEG_BIG)
    return jnp.asarray(np.stack([b0, b1]), F32)


def _attn_kernel(bias_ref, aq_ref, ak_ref, av_ref, az_ref, o_ref,
                 khist, vhist, kprm, vprm, qprm, m_ref, l_ref, acc_ref, onat):
    ta = ATT_TILE
    blk = ATT_BLOCK
    nph = ATT_PERM
    reg = ta // nph
    tile = pl.program_id(2)
    cur = tile % 2
    prev = 1 - cur

    @pl.when((pl.program_id(0) == 0) & (pl.program_id(1) == 0) & (tile == 0))
    def _():
        khist[...] = jnp.zeros_like(khist)
        vhist[...] = jnp.zeros_like(vhist)
        kprm[1] = jnp.zeros((nph, reg, LANES), F32)
        vprm[1] = jnp.zeros((nph, reg, LANES), F32)

    for r in range(nph):
        kprm[cur, r] = ak_ref[0, pl.ds(r, reg, stride=nph), :]
        vprm[cur, r] = av_ref[0, pl.ds(r, reg, stride=nph), :]
        qprm[r] = aq_ref[0, pl.ds(r, reg, stride=nph), :]

    first_head = lax.broadcasted_iota(jnp.int32, (blk, LANES), 1) < AT_HEAD_DIM
    ones = jnp.ones((2 * blk, LANES), BF16)
    bias_any = bias_ref[0]
    bias_first = bias_ref[(tile == 0).astype(jnp.int32)]

    def block_stats(q, k, v, bias):
        k = k.astype(BF16)
        v = v.astype(BF16)
        q2 = jnp.concatenate([jnp.where(first_head, q, 0.0),
                              jnp.where(first_head, 0.0, q)], axis=0).astype(BF16)
        s = _dot_nt(q2, k) + jnp.concatenate([bias, bias], axis=0)
        m = jnp.max(s, axis=-1, keepdims=True)
        p = jnp.exp2(s - m).astype(BF16)
        pv = jnp.dot(p, jnp.concatenate([v, ones], axis=1), preferred_element_type=F32)
        return (jnp.where(first_head, m[:blk], m[blk:]),
                jnp.where(first_head, pv[:blk, LANES:], pv[blk:, LANES:]),
                jnp.where(first_head, pv[:blk, :LANES], pv[blk:, :LANES]))

    def store_stats(pi, rows, stats):
        m_ref[pi, rows, :], l_ref[pi, rows, :], acc_ref[pi, rows, :] = stats

    def window(first_ref, first_rows, ref, rows):
        return jnp.concatenate([first_ref[first_rows], ref[rows]], axis=0)

    for j in range(ta // blk):
        q_rows = pl.ds(j * blk, blk)
        if j == 0:
            k = window(khist, (slice(None),) * 2, ak_ref, (0, q_rows))
            v = window(vhist, (slice(None),) * 2, av_ref, (0, q_rows))
        else:
            k_rows = pl.ds((j - 1) * blk, 2 * blk)
            k, v = ak_ref[0, k_rows, :], av_ref[0, k_rows, :]
        store_stats(0, q_rows, block_stats(aq_ref[0, q_rows, :], k, v,
                                           bias_first if j == 0 else bias_any))

    khist[...] = ak_ref[0, ta - blk:ta, :]
    vhist[...] = av_ref[0, ta - blk:ta, :]

    for r in range(nph):
        for j in range(reg // blk):
            q_rows = pl.ds(j * blk, blk)
            if j == 0:
                last = pl.ds(reg - blk, blk)
                k = window(kprm, (prev, r, last), kprm, (cur, r, q_rows))
                v = window(vprm, (prev, r, last), vprm, (cur, r, q_rows))
            else:
                k_rows = pl.ds((j - 1) * blk, 2 * blk)
                k, v = kprm[cur, r, k_rows, :], vprm[cur, r, k_rows, :]
            stats = block_stats(qprm[r, q_rows, :], k, v, bias_first if j == 0 else bias_any)
            store_stats(1, pl.ds(r * reg + j * blk, blk), stats)
        sub = reg // nph
        for r16 in range(nph):
            for j in range(sub // blk):
                q_rows = pl.ds(r16 + nph * blk * j, blk, stride=nph)
                if j == 0:
                    tail = pl.ds(r16 + nph * (sub - blk), blk, stride=nph)
                    k = window(kprm, (prev, r, tail), kprm, (cur, r, q_rows))
                    v = window(vprm, (prev, r, tail), vprm, (cur, r, q_rows))
                else:
                    k_rows = pl.ds(r16 + nph * blk * (j - 1), 2 * blk, stride=nph)
                    k, v = kprm[cur, r, k_rows, :], vprm[cur, r, k_rows, :]
                stats = block_stats(qprm[r, q_rows, :], k, v,
                                    bias_first if j == 0 else bias_any)
                store_stats(2, pl.ds(r * reg + r16 + nph * blk * j, blk, stride=nph), stats)

    def merge(ci, carry):
        pieces = reg // ATT_MERGE_ROWS
        r = ci // pieces
        i0 = (ci % pieces) * ATT_MERGE_ROWS
        nat_rows = pl.ds(r + nph * i0, ATT_MERGE_ROWS, stride=nph)
        prm_rows = pl.ds(pl.multiple_of(ci * ATT_MERGE_ROWS, ATT_MERGE_ROWS), ATT_MERGE_ROWS)
        rows = (nat_rows, prm_rows, prm_rows)
        ms = [m_ref[pi, rw, :] for pi, rw in enumerate(rows)]
        m_all = functools.reduce(jnp.maximum, ms)
        ws = [jnp.exp2(m - m_all) for m in ms]
        l_all = sum(w * l_ref[pi, rw, :] for (pi, rw), w in zip(enumerate(rows), ws))
        acc = sum(w * acc_ref[pi, rw, :] for (pi, rw), w in zip(enumerate(rows), ws))
        onat[nat_rows, :] = acc / l_all
        return carry

    lax.fori_loop(0, ta // ATT_MERGE_ROWS, merge, 0)

    def gate(ci, carry):
        rows = pl.ds(pl.multiple_of(ci * ATT_MERGE_ROWS, ATT_MERGE_ROWS), ATT_MERGE_ROWS)
        o_ref[0, rows, :] = (onat[rows, :] * az_ref[0, rows, :].astype(F32)).astype(BF16)
        return carry

    lax.fori_loop(0, ta // ATT_MERGE_ROWS, gate, 0)


def _attention(aq, ak, av, az):
    b, s, w = aq.shape
    assert s % ATT_TILE == 0 and w % LANES == 0
    assert DILATIONS == (1, ATT_PERM, ATT_PERM * ATT_PERM)
    assert (ATT_TILE // ATT_PERM) % ATT_MERGE_ROWS == 0
    bias = _attn_bias()
    reg = ATT_TILE // ATT_PERM
    pair_spec = pl.BlockSpec((1, ATT_TILE, LANES), lambda bi, h, t: (bi, t, h))
    hist = pltpu.VMEM((ATT_BLOCK, LANES), F32)
    prm = pltpu.VMEM((2, ATT_PERM, reg, LANES), F32)
    stat = pltpu.VMEM((len(DILATIONS), ATT_TILE, LANES), F32)
    return pl.pallas_call(
        _attn_kernel,
        grid=(b, w // LANES, s // ATT_TILE),
        in_specs=[pl.BlockSpec(bias.shape, lambda bi, h, t: (0, 0, 0))] + [pair_spec] * 4,
        out_specs=pair_spec,
        out_shape=jax.ShapeDtypeStruct((b, s, w), BF16),
        scratch_shapes=[hist, hist, prm, prm, pltpu.VMEM((ATT_PERM, reg, LANES), F32),
                        stat, stat, stat, pltpu.VMEM((ATT_TILE, LANES), F32)],
        compiler_params=pltpu.CompilerParams(
            dimension_semantics=("arbitrary", "arbitrary", "arbitrary"),
            vmem_limit_bytes=VMEM_LIMIT),
        name="dilated_attention",
    )(bias, aq, ak, av, az)


HG_DIAG = 8
HG_LEVELS = (8, 16, 32)
SUBLANES = 8


def _hgrn_constants():
    c = HG_CHUNK
    t = np.arange(c)
    tri = (t[:, None] >= t[None, :]).astype(np.float32)
    first = (t // HG_DIAG) * HG_DIAG
    half = np.concatenate([tri, tri - 0.5 * (tri[first] + tri[first + HG_DIAG - 1])], axis=0)
    coef = np.concatenate([half, half], axis=1)
    ts, ss = t[:, None], t[None, :]
    level = np.full((c, c), len(HG_LEVELS) + 1, np.int32)
    level[(ts // HG_DIAG == ss // HG_DIAG) & (ts >= ss)] = 0
    for i, lb in enumerate(HG_LEVELS):
        m = ((ts // (2 * lb) == ss // (2 * lb)) & ((ts // lb) % 2 == 1)
             & ((ss // lb) % 2 == 0))
        level[m] = i + 1
    return jnp.asarray(coef, BF16), jnp.asarray(level)


def _hgrn_out_kernel(coef_ref, level_ref, x_ref, nw_ref, w_in32_ref, lbl_ref, gn_ref,
                     oat_ref, w32_ref, fw_ref, o_ref,
                     w_in_ref, w_ref, state_ref, qs_ref, g_ref, kk_ref, vi_ref, zs_ref,
                     qd_ref, oi_ref, kv_ref, dec_ref, ohg_ref):
    @pl.when(pl.program_id(1) == 0)
    def _():
        state_ref[...] = jnp.zeros_like(state_ref)
        w_in_ref[...] = w_in32_ref[0].astype(BF16)
        w_ref[...] = w32_ref[0].astype(BF16)

    lg = lbl_ref[...]
    ex = jnp.exp(lg - jnp.max(lg, axis=0, keepdims=True))
    lb = jnp.clip(ex[0:1, :] / jnp.sum(ex, axis=0, keepdims=True), 1e-6, 1.0 - 1e-6)
    f_mid = 0.5 * (1.0 + lb)
    f_amp = 0.5 * (1.0 - lb)

    def store_q(p):
        qs_ref[...] = _silu(p).astype(BF16)

    def store_f(p):
        f = f_mid + f_amp * jnp.tanh(0.5 * p)
        g2 = jnp.log2(f)
        hi_bits = lax.bitcast_convert_type(g2, jnp.uint32) & jnp.uint32(0xFFFF0000)
        g2_hi = lax.bitcast_convert_type(hi_bits, F32)
        g2_lo = (g2 - g2_hi).astype(BF16)
        g2_hi = g2_hi.astype(BF16)
        for h in range(HG_HEADS):
            g_ref[:, 2 * h * HG_DK:(2 * h + 1) * HG_DK] = g2_hi[:, h * HG_DK:(h + 1) * HG_DK]
            g_ref[:, (2 * h + 1) * HG_DK:(2 * h + 2) * HG_DK] = g2_lo[:, h * HG_DK:(h + 1) * HG_DK]
        kk_ref[...] = (1.0 - f).astype(BF16)

    def store_z(p):
        zs_ref[...] = _silu(p).astype(BF16)

    def store_i(p):
        vi_ref[...] = p.astype(BF16)

    u = _normalized(x_ref[0], nw_ref[...])

    def project(j):
        return jnp.dot(u, w_in_ref[:, j * SECTION:(j + 1) * SECTION],
                       preferred_element_type=F32)

    p_f = project(1)
    p_q = project(0)
    store_f(p_f)

    c = HG_CHUNK
    n_chunks = HG_ROWS // c
    half = HG_HEADS * HG_DK
    coef = coef_ref[...]
    level = level_ref[...]
    units = [(h, ci) for h in range(HG_HEADS) for ci in range(n_chunks)]

    def rows_of(ci):
        return pl.ds(ci * c, c)

    def cols_of(h, width=HG_DK):
        return pl.ds(h * width, width)

    def cum_decay(h, ci):
        g2 = g_ref[rows_of(ci), cols_of(h, 2 * HG_DK)]
        return jnp.dot(coef, jnp.concatenate([g2[:, :HG_DK], g2[:, HG_DK:]], axis=0),
                       preferred_element_type=F32)

    def level_operands(h, ci, be):
        q = qs_ref[rows_of(ci), cols_of(h)]
        k = kk_ref[rows_of(ci), cols_of(h)]
        scaled = lambda t, w: t * w.astype(BF16)
        b = be[0:c]
        e_diag = be[c:2 * c]
        ops = [(scaled(q, jnp.exp2(e_diag)), scaled(k, jnp.exp2(-e_diag)))]
        for lb_rows in HG_LEVELS:
            ref = jnp.concatenate(
                [jnp.broadcast_to(b[p + lb_rows - 1:p + lb_rows, :], (2 * lb_rows, HG_DK))
                 for p in range(0, c, 2 * lb_rows)], axis=0)
            w = jnp.exp2(_neg_abs(b - ref))
            ops.append((scaled(q, w), scaled(k, w)))
        b_last = b[c - 1:c, :]
        return ops, scaled(k, jnp.exp2(b_last - b)), scaled(q, jnp.exp2(b)), jnp.exp2(b_last)

    def intra_scores(ops):
        scores = jnp.where(level == 0, _dot_nt(*ops[0]), 0.0)
        for i in range(len(HG_LEVELS)):
            scores = jnp.where(level == i + 1, _dot_nt(*ops[i + 1]), scores)
        return scores.astype(BF16)

    stage1 = [cum_decay(h, ci) for h, ci in units]
    store_q(p_q)
    p_z = project(3)
    stage2 = [level_operands(h, ci, be) for (h, ci), be in zip(units, stage1)]
    p_i = project(2)
    store_z(p_z)
    store_i(p_i)

    stage3 = [intra_scores(ops) for ops, _, _, _ in stage2]
    for (h, ci), (_, k_dec, q_dec, decay), scores in zip(units, stage2, stage3):
        v = vi_ref[rows_of(ci), cols_of(h)]
        oi_ref[rows_of(ci), cols_of(h)] = jnp.dot(scores, v, preferred_element_type=F32)
        kv_ref[h, ci] = _dot_tn(v, k_dec)
        qd_ref[rows_of(ci), cols_of(h)] = q_dec
        dec_ref[h, ci] = jnp.broadcast_to(decay, dec_ref.shape[2:])

    o_ref[0] = x_ref[0] + jnp.dot(oat_ref[0], w_ref[half:2 * half, :],
                                  preferred_element_type=F32)

    states = [state_ref[h] for h in range(HG_HEADS)]
    for ci in range(n_chunks):
        for h in range(HG_HEADS):
            rows, cols = rows_of(ci), cols_of(h)
            o = oi_ref[rows, cols] + _dot_nt(qd_ref[rows, cols], states[h].astype(BF16))
            states[h] = states[h] * dec_ref[h, ci, 0:1, :] + kv_ref[h, ci]
            z = zs_ref[rows, cols].astype(F32)
            ms = jnp.mean(o * o, axis=-1, keepdims=True)
            ohg_ref[rows, cols] = (o * lax.rsqrt(ms + NORM_EPS) * gn_ref[:, cols] * z
                                   ).astype(BF16)
    for h in range(HG_HEADS):
        state_ref[h] = states[h]

    hres = o_ref[0] + jnp.dot(ohg_ref[...], w_ref[0:half, :], preferred_element_type=F32)
    ms = jnp.mean(hres * hres, axis=-1, keepdims=True)
    o_ref[0] = hres * lax.rsqrt(ms + NORM_EPS) * fw_ref[...]


def _hgrn_out(x, norm_w, w_in, lb_logits, hg_norm_w, oat, w_out, final_norm_w):
    b, s, d_model = x.shape
    w = HG_HEADS * HG_DK
    assert s % HG_ROWS == 0 and w == SECTION
    assert w_out.shape == (1, w + oat.shape[-1], d_model)
    coef, level = _hgrn_constants()
    n_chunks = HG_ROWS // HG_CHUNK
    tile = lambda width: pl.BlockSpec((1, HG_ROWS, width), lambda bi, t: (bi, t, 0))
    const = lambda a: pl.BlockSpec(a.shape, lambda bi, t: (0,) * a.ndim)
    act = lambda width: pltpu.VMEM((HG_ROWS, width), BF16)
    return pl.pallas_call(
        _hgrn_out_kernel,
        grid=(b, s // HG_ROWS),
        in_specs=[const(coef), const(level), tile(d_model), const(norm_w),
                  _branch_weight_spec(w_in, 0), const(lb_logits), const(hg_norm_w),
                  tile(oat.shape[-1]), const(w_out), const(final_norm_w)],
        out_specs=tile(d_model),
        out_shape=jax.ShapeDtypeStruct((b, s, d_model), F32),
        scratch_shapes=[pltpu.VMEM((d_model, 4 * SECTION), BF16),
                        pltpu.VMEM(w_out.shape[1:], BF16),
                        pltpu.VMEM((HG_HEADS, HG_DK, HG_DK), F32),
                        act(w), act(2 * w), act(w), act(w), act(w),
                        act(w),
                        pltpu.VMEM((HG_ROWS, w), F32),
                        pltpu.VMEM((HG_HEADS, n_chunks, HG_DK, HG_DK), F32),
                        pltpu.VMEM((HG_HEADS, n_chunks, SUBLANES, HG_DK), F32),
                        act(w)],
        compiler_params=pltpu.CompilerParams(
            dimension_semantics=("parallel", "arbitrary"),
            vmem_limit_bytes=VMEM_LIMIT),
        name="hgrn2_branch_and_output",
    )(coef, level, x, norm_w, w_in, lb_logits, hg_norm_w, oat, w_out, final_norm_w)


def kernel(x, norm_w, w_in, hgrn_lb_logits, hg_norm_w, w_out, final_norm_w):
    b, s, d_model = x.shape
    assert norm_w.shape[0] == 1 and w_in.shape[0] == 1 and w_out.shape[0] == 1
    aq, ak, av, az = _attn_projection(x.reshape(b * s, d_model), norm_w, w_in, s)
    to3 = lambda a: a.reshape(b, s, a.shape[-1])
    oat = _attention(to3(aq), to3(ak), to3(av), to3(az))
    return _hgrn_out(x, norm_w, w_in, hgrn_lb_logits, hg_norm_w, oat, w_out,
                     final_norm_w.reshape(1, d_model))
```

```python
import functools

import numpy as np
import jax
import jax.numpy as jnp
from jax import lax
from jax.experimental import pallas as pl
from jax.experimental.pallas import tpu as pltpu

F32 = jnp.float32
BF16 = jnp.bfloat16

NORM_EPS = 1e-6
ROPE_THETA = 10000.0
LANES = 128

HG_HEADS = 4
HG_DK = 128
HG_CHUNK = 64
AT_HEAD_DIM = 64
SECTION = 512
DILATIONS = (1, 4, 16)
ATT_BLOCK = 128
ATT_TILE = 2 * ATT_BLOCK * max(DILATIONS)
NEG_BIG = -1e30

PROJ_ROWS = 1024
HG_ROWS = 512
ATT_PERM = 4
ATT_MERGE_ROWS = 1024
LOG2E = 1.4426950408889634
VMEM_LIMIT = 56 * 1024 * 1024


def _normalized(x, gain):
    ms = jnp.mean(x * x, axis=-1, keepdims=True)
    return (x * lax.rsqrt(ms + NORM_EPS) * gain).astype(BF16)


def _silu(p):
    return p * (0.5 * jnp.tanh(0.5 * p) + 0.5)


def _project_sections(u, w_ref, plan):
    pending = None
    for j, finish in plan:
        p = jnp.dot(u, w_ref[:, j * SECTION:(j + 1) * SECTION], preferred_element_type=F32)
        if pending is not None:
            pending()
        pending = functools.partial(finish, p)
    pending()


def _dot_nt(a, b):
    return lax.dot_general(a, b, (((1,), (1,)), ((), ())), preferred_element_type=F32)


def _dot_tn(a, b):
    return lax.dot_general(a, b, (((0,), (0,)), ((), ())), preferred_element_type=F32)


def _neg_abs(x):
    bits = lax.bitcast_convert_type(x, jnp.uint32) | jnp.uint32(0x80000000)
    return lax.bitcast_convert_type(bits, F32)


def _attn_proj_kernel(x_ref, nw_ref, w32_ref, cos_ref, sina_ref, sinb_ref,
                      aq_ref, ak_ref, av_ref, az_ref, w_ref):
    @pl.when(pl.program_id(0) == 0)
    def _():
        w_ref[...] = w32_ref[0].astype(BF16)

    u = _normalized(x_ref[...], nw_ref[...])

    def store_rope(dst_ref, scale, p):
        half = AT_HEAD_DIM // 2
        for c in range(SECTION // LANES):
            xc = p[:, c * LANES:(c + 1) * LANES]
            r = (xc * cos_ref[...] + pltpu.roll(xc, LANES - half, 1) * sina_ref[...]
                 + pltpu.roll(xc, half, 1) * sinb_ref[...])
            dst_ref[:, c * LANES:(c + 1) * LANES] = r * scale

    def store_z(p):
        az_ref[...] = _silu(p).astype(BF16)

    def store_v(p):
        av_ref[...] = p

    _project_sections(u, w_ref, [
        (3, store_z),
        (0, functools.partial(store_rope, aq_ref, AT_HEAD_DIM ** -0.5 * LOG2E)),
        (1, functools.partial(store_rope, ak_ref, 1.0)),
        (2, store_v)])


@functools.lru_cache(maxsize=None)
def _rope_tables(seq):
    half = AT_HEAD_DIM // 2
    inv_freq = 1.0 / (ROPE_THETA ** (np.arange(half, dtype=np.float64) / half))
    ang = np.arange(seq, dtype=np.float64)[:, None] * inv_freq[None, :]
    cos = np.cos(ang)
    sin = np.sin(ang)
    zero = np.zeros_like(sin)
    reps = LANES // AT_HEAD_DIM
    cos_t = np.tile(np.concatenate([cos, cos], axis=1), (1, reps))
    sina_t = np.tile(np.concatenate([-sin, zero], axis=1), (1, reps))
    sinb_t = np.tile(np.concatenate([zero, sin], axis=1), (1, reps))
    return tuple(np.asarray(t, np.float32) for t in (cos_t, sina_t, sinb_t))


def _branch_weight_spec(w_in, branch):
    assert w_in.shape[0] == 1 and w_in.shape[2] == 8 * SECTION
    return pl.BlockSpec((1, w_in.shape[1], 4 * SECTION),
                        lambda *_: (0, 0, branch), pipeline_mode=pl.Buffered(1))


def _attn_projection(x2, norm_w, w_in, seq):
    rows, d_model = x2.shape
    assert rows % PROJ_ROWS == 0 and seq % PROJ_ROWS == 0
    tables = _rope_tables(seq)
    seq_tiles = seq // PROJ_ROWS
    row_spec = lambda w: pl.BlockSpec((PROJ_ROWS, w), lambda i: (i, 0))
    tab_spec = pl.BlockSpec((PROJ_ROWS, LANES), lambda i: (i % seq_tiles, 0))
    full = lambda a: pl.BlockSpec(a.shape, lambda i: (0, 0))
    out_dtypes = (F32, F32, F32, BF16)
    return pl.pallas_call(
        _attn_proj_kernel,
        grid=(rows // PROJ_ROWS,),
        in_specs=[row_spec(d_model), full(norm_w), _branch_weight_spec(w_in, 1)]
                 + [tab_spec] * 3,
        out_specs=[row_spec(SECTION)] * len(out_dtypes),
        out_shape=[jax.ShapeDtypeStruct((rows, SECTION), dt) for dt in out_dtypes],
        scratch_shapes=[pltpu.VMEM((d_model, 4 * SECTION), BF16)],
        compiler_params=pltpu.CompilerParams(
            dimension_semantics=("arbitrary",), vmem_limit_bytes=VMEM_LIMIT),
        name="attention_projection",
    )(x2, norm_w, w_in, *tables)


def _attn_bias():
    qi = np.arange(ATT_BLOCK)[:, None]
    kj = np.arange(2 * ATT_BLOCK)[None, :]
    dist = ATT_BLOCK + qi - kj
    valid = (dist >= 0) & (dist <= ATT_BLOCK)
    b0 = np.where(valid, 0.0, NEG_BIG)
    b1 = np.where(valid & (kj >= ATT_BLOCK), 0.0, NEG_BIG)
    return jnp.asarray(np.stack([b0, b1]), F32)


def _attn_kernel(bias_ref, aq_ref, ak_ref, av_ref, az_ref, o_ref,
                 khist, vhist, kprm, vprm, qprm, m_ref, l_ref, acc_ref, onat):
    ta = ATT_TILE
    blk = ATT_BLOCK
    nph = ATT_PERM
    reg = ta // nph
    tile = pl.program_id(2)
    cur = tile % 2
    prev = 1 - cur

    @pl.when((pl.program_id(0) == 0) & (pl.program_id(1) == 0) & (tile == 0))
    def _():
        khist[...] = jnp.zeros_like(khist)
        vhist[...] = jnp.zeros_like(vhist)
        kprm[1] = jnp.zeros((nph, reg, LANES), F32)
        vprm[1] = jnp.zeros((nph, reg, LANES), F32)

    for r in range(nph):
        kprm[cur, r] = ak_ref[0, pl.ds(r, reg, stride=nph), :]
        vprm[cur, r] = av_ref[0, pl.ds(r, reg, stride=nph), :]
        qprm[r] = aq_ref[0, pl.ds(r, reg, stride=nph), :]

    first_head = lax.broadcasted_iota(jnp.int32, (blk, LANES), 1) < AT_HEAD_DIM
    ones = jnp.ones((2 * blk, LANES), BF16)
    bias_any = bias_ref[0]
    bias_first = bias_ref[(tile == 0).astype(jnp.int32)]

    def block_stats(q, k, v, bias):
        k = k.astype(BF16)
        v = v.astype(BF16)
        q2 = jnp.concatenate([jnp.where(first_head, q, 0.0),
                              jnp.where(first_head, 0.0, q)], axis=0).astype(BF16)
        s = _dot_nt(q2, k) + jnp.concatenate([bias, bias], axis=0)
        m = jnp.max(s, axis=-1, keepdims=True)
        p = jnp.exp2(s - m).astype(BF16)
        pv = jnp.dot(p, jnp.concatenate([v, ones], axis=1), preferred_element_type=F32)
        return (jnp.where(first_head, m[:blk], m[blk:]),
                jnp.where(first_head, pv[:blk, LANES:], pv[blk:, LANES:]),
                jnp.where(first_head, pv[:blk, :LANES], pv[blk:, :LANES]))

    def store_stats(pi, rows, stats):
        m_ref[pi, rows, :], l_ref[pi, rows, :], acc_ref[pi, rows, :] = stats

    def window(first_ref, first_rows, ref, rows):
        return jnp.concatenate([first_ref[first_rows], ref[rows]], axis=0)

    for j in range(ta // blk):
        q_rows = pl.ds(j * blk, blk)
        if j == 0:
            k = window(khist, (slice(None),) * 2, ak_ref, (0, q_rows))
            v = window(vhist, (slice(None),) * 2, av_ref, (0, q_rows))
        else:
            k_rows = pl.ds((j - 1) * blk, 2 * blk)
            k, v = ak_ref[0, k_rows, :], av_ref[0, k_rows, :]
        store_stats(0, q_rows, block_stats(aq_ref[0, q_rows, :], k, v,
                                           bias_first if j == 0 else bias_any))

    khist[...] = ak_ref[0, ta - blk:ta, :]
    vhist[...] = av_ref[0, ta - blk:ta, :]

    for r in range(nph):
        for j in range(reg // blk):
            q_rows = pl.ds(j * blk, blk)
            if j == 0:
                last = pl.ds(reg - blk, blk)
                k = window(kprm, (prev, r, last), kprm, (cur, r, q_rows))
                v = window(vprm, (prev, r, last), vprm, (cur, r, q_rows))
            else:
                k_rows = pl.ds((j - 1) * blk, 2 * blk)
                k, v = kprm[cur, r, k_rows, :], vprm[cur, r, k_rows, :]
            stats = block_stats(qprm[r, q_rows, :], k, v, bias_first if j == 0 else bias_any)
            store_stats(1, pl.ds(r * reg + j * blk, blk), stats)
        sub = reg // nph
        for r16 in range(nph):
            for j in range(sub // blk):
                q_rows = pl.ds(r16 + nph * blk * j, blk, stride=nph)
                if j == 0:
                    tail = pl.ds(r16 + nph * (sub - blk), blk, stride=nph)
                    k = window(kprm, (prev, r, tail), kprm, (cur, r, q_rows))
                    v = window(vprm, (prev, r, tail), vprm, (cur, r, q_rows))
                else:
                    k_rows = pl.ds(r16 + nph * blk * (j - 1), 2 * blk, stride=nph)
                    k, v = kprm[cur, r, k_rows, :], vprm[cur, r, k_rows, :]
                stats = block_stats(qprm[r, q_rows, :], k, v,
                                    bias_first if j == 0 else bias_any)
                store_stats(2, pl.ds(r * reg + r16 + nph * blk * j, blk, stride=nph), stats)

    def merge(ci, carry):
        pieces = reg // ATT_MERGE_ROWS
        r = ci // pieces
        i0 = (ci % pieces) * ATT_MERGE_ROWS
        nat_rows = pl.ds(r + nph * i0, ATT_MERGE_ROWS, stride=nph)
        prm_rows = pl.ds(pl.multiple_of(ci * ATT_MERGE_ROWS, ATT_MERGE_ROWS), ATT_MERGE_ROWS)
        rows = (nat_rows, prm_rows, prm_rows)
        ms = [m_ref[pi, rw, :] for pi, rw in enumerate(rows)]
        m_all = functools.reduce(jnp.maximum, ms)
        ws = [jnp.exp2(m - m_all) for m in ms]
        l_all = sum(w * l_ref[pi, rw, :] for (pi, rw), w in zip(enumerate(rows), ws))
        acc = sum(w * acc_ref[pi, rw, :] for (pi, rw), w in zip(enumerate(rows), ws))
        onat[nat_rows, :] = acc / l_all
        return carry

    lax.fori_loop(0, ta // ATT_MERGE_ROWS, merge, 0)

    def gate(ci, carry):
        rows = pl.ds(pl.multiple_of(ci * ATT_MERGE_ROWS, ATT_MERGE_ROWS), ATT_MERGE_ROWS)
        o_ref[0, rows, :] = (onat[rows, :] * az_ref[0, rows, :].astype(F32)).astype(BF16)
        return carry

    lax.fori_loop(0, ta // ATT_MERGE_ROWS, gate, 0)


def _attention(aq, ak, av, az):
    b, s, w = aq.shape
    assert s % ATT_TILE == 0 and w % LANES == 0
    assert DILATIONS == (1, ATT_PERM, ATT_PERM * ATT_PERM)
    assert (ATT_TILE // ATT_PERM) % ATT_MERGE_ROWS == 0
    bias = _attn_bias()
    reg = ATT_TILE // ATT_PERM
    pair_spec = pl.BlockSpec((1, ATT_TILE, LANES), lambda bi, h, t: (bi, t, h))
    hist = pltpu.VMEM((ATT_BLOCK, LANES), F32)
    prm = pltpu.VMEM((2, ATT_PERM, reg, LANES), F32)
    stat = pltpu.VMEM((len(DILATIONS), ATT_TILE, LANES), F32)
    return pl.pallas_call(
        _attn_kernel,
        grid=(b, w // LANES, s // ATT_TILE),
        in_specs=[pl.BlockSpec(bias.shape, lambda bi, h, t: (0, 0, 0))] + [pair_spec] * 4,
        out_specs=pair_spec,
        out_shape=jax.ShapeDtypeStruct((b, s, w), BF16),
        scratch_shapes=[hist, hist, prm, prm, pltpu.VMEM((ATT_PERM, reg, LANES), F32),
                        stat, stat, stat, pltpu.VMEM((ATT_TILE, LANES), F32)],
        compiler_params=pltpu.CompilerParams(
            dimension_semantics=("arbitrary", "arbitrary", "arbitrary"),
            vmem_limit_bytes=VMEM_LIMIT),
        name="dilated_attention",
    )(bias, aq, ak, av, az)


HG_DIAG = 8
HG_LEVELS = (8, 16, 32)
SUBLANES = 8


def _hgrn_constants():
    c = HG_CHUNK
    t = np.arange(c)
    tri = (t[:, None] >= t[None, :]).astype(np.float32)
    first = (t // HG_DIAG) * HG_DIAG
    half = np.concatenate([tri, tri - 0.5 * (tri[first] + tri[first + HG_DIAG - 1])], axis=0)
    coef = np.concatenate([half, half], axis=1)
    ts, ss = t[:, None], t[None, :]
    level = np.full((c, c), len(HG_LEVELS) + 1, np.int32)
    level[(ts // HG_DIAG == ss // HG_DIAG) & (ts >= ss)] = 0
    for i, lb in enumerate(HG_LEVELS):
        m = ((ts // (2 * lb) == ss // (2 * lb)) & ((ts // lb) % 2 == 1)
             & ((ss // lb) % 2 == 0))
        level[m] = i + 1
    return jnp.asarray(coef, BF16), jnp.asarray(level)


def _hgrn_out_kernel(coef_ref, level_ref, x_ref, nw_ref, w_in32_ref, lbl_ref, gn_ref,
                     oat_ref, w32_ref, fw_ref, o_ref,
                     w_in_ref, w_ref, state_ref, qs_ref, g_ref, kk_ref, vi_ref, zs_ref,
                     qd_ref, oi_ref, kv_ref, dec_ref, ohg_ref):
    @pl.when(pl.program_id(1) == 0)
    def _():
        state_ref[...] = jnp.zeros_like(state_ref)
        w_in_ref[...] = w_in32_ref[0].astype(BF16)
        w_ref[...] = w32_ref[0].astype(BF16)

    lg = lbl_ref[...]
    ex = jnp.exp(lg - jnp.max(lg, axis=0, keepdims=True))
    lb = jnp.clip(ex[0:1, :] / jnp.sum(ex, axis=0, keepdims=True), 1e-6, 1.0 - 1e-6)
    f_mid = 0.5 * (1.0 + lb)
    f_amp = 0.5 * (1.0 - lb)

    def store_q(p):
        qs_ref[...] = _silu(p).astype(BF16)

    def store_f(p):
        f = f_mid + f_amp * jnp.tanh(0.5 * p)
        g2 = jnp.log2(f)
        hi_bits = lax.bitcast_convert_type(g2, jnp.uint32) & jnp.uint32(0xFFFF0000)
        g2_hi = lax.bitcast_convert_type(hi_bits, F32)
        g2_lo = (g2 - g2_hi).astype(BF16)
        g2_hi = g2_hi.astype(BF16)
        for h in range(HG_HEADS):
            g_ref[:, 2 * h * HG_DK:(2 * h + 1) * HG_DK] = g2_hi[:, h * HG_DK:(h + 1) * HG_DK]
            g_ref[:, (2 * h + 1) * HG_DK:(2 * h + 2) * HG_DK] = g2_lo[:, h * HG_DK:(h + 1) * HG_DK]
        kk_ref[...] = (1.0 - f).astype(BF16)

    def store_z(p):
        zs_ref[...] = _silu(p).astype(BF16)

    def store_i(p):
        vi_ref[...] = p.astype(BF16)

    u = _normalized(x_ref[0], nw_ref[...])

    def project(j):
        return jnp.dot(u, w_in_ref[:, j * SECTION:(j + 1) * SECTION],
                       preferred_element_type=F32)

    p_f = project(1)
    p_q = project(0)
    store_f(p_f)
    store_q(p_q)

    c = HG_CHUNK
    n_chunks = HG_ROWS // c
    half = HG_HEADS * HG_DK
    coef = coef_ref[...]
    level = level_ref[...]
    units = [(h, ci) for h in range(HG_HEADS) for ci in range(n_chunks)]

    def rows_of(ci):
        return pl.ds(ci * c, c)

    def cols_of(h, width=HG_DK):
        return pl.ds(h * width, width)

    def cum_decay(h, ci):
        g2 = g_ref[rows_of(ci), cols_of(h, 2 * HG_DK)]
        return jnp.dot(coef, jnp.concatenate([g2[:, :HG_DK], g2[:, HG_DK:]], axis=0),
                       preferred_element_type=F32)

    def level_operands(h, ci, be):
        q = qs_ref[rows_of(ci), cols_of(h)]
        k = kk_ref[rows_of(ci), cols_of(h)]
        scaled = lambda t, w: t * w.astype(BF16)
        b = be[0:c]
        e_diag = be[c:2 * c]
        ops = [(scaled(q, jnp.exp2(e_diag)), scaled(k, jnp.exp2(-e_diag)))]
        for lb_rows in HG_LEVELS:
            ref = jnp.concatenate(
                [jnp.broadcast_to(b[p + lb_rows - 1:p + lb_rows, :], (2 * lb_rows, HG_DK))
                 for p in range(0, c, 2 * lb_rows)], axis=0)
            w = jnp.exp2(_neg_abs(b - ref))
            ops.append((scaled(q, w), scaled(k, w)))
        b_last = b[c - 1:c, :]
        return ops, scaled(k, jnp.exp2(b_last - b)), scaled(q, jnp.exp2(b)), jnp.exp2(b_last)

    def intra_scores(ops):
        scores = jnp.where(level == 0, _dot_nt(*ops[0]), 0.0)
        for i in range(len(HG_LEVELS)):
            scores = jnp.where(level == i + 1, _dot_nt(*ops[i + 1]), scores)
        return scores.astype(BF16)

    stage1 = [cum_decay(h, ci) for h, ci in units]
    p_z = project(3)
    stage2 = [level_operands(h, ci, be) for (h, ci), be in zip(units, stage1)]
    p_i = project(2)
    store_z(p_z)
    o_ref[0] = x_ref[0] + jnp.dot(oat_ref[0], w_ref[half:2 * half, :],
                                  preferred_element_type=F32)
    store_i(p_i)

    stage3 = [intra_scores(ops) for ops, _, _, _ in stage2]
    for (h, ci), (_, k_dec, q_dec, decay), scores in zip(units, stage2, stage3):
        v = vi_ref[rows_of(ci), cols_of(h)]
        oi_ref[rows_of(ci), cols_of(h)] = jnp.dot(scores, v, preferred_element_type=F32)
        kv_ref[h, ci] = _dot_tn(v, k_dec)
        qd_ref[rows_of(ci), cols_of(h)] = q_dec
        dec_ref[h, ci] = jnp.broadcast_to(decay, dec_ref.shape[2:])

    def project_out(part, parts):
        rows = pl.ds(part * (HG_ROWS // parts), HG_ROWS // parts)
        hres = o_ref[0, rows, :] + jnp.dot(ohg_ref[rows, :], w_ref[0:half, :],
                                           preferred_element_type=F32)
        ms = jnp.mean(hres * hres, axis=-1, keepdims=True)
        o_ref[0, rows, :] = hres * lax.rsqrt(ms + NORM_EPS) * fw_ref[...]

    states = [state_ref[h] for h in range(HG_HEADS)]
    for ci in range(n_chunks):
        for h in range(HG_HEADS):
            rows, cols = rows_of(ci), cols_of(h)
            o = oi_ref[rows, cols] + _dot_nt(qd_ref[rows, cols], states[h].astype(BF16))
            states[h] = states[h] * dec_ref[h, ci, 0:1, :] + kv_ref[h, ci]
            z = zs_ref[rows, cols].astype(F32)
            ms = jnp.mean(o * o, axis=-1, keepdims=True)
            ohg_ref[rows, cols] = (o * lax.rsqrt(ms + NORM_EPS) * gn_ref[:, cols] * z
                                   ).astype(BF16)
        if ci == n_chunks // 2 - 1:
            project_out(0, 2)
    for h in range(HG_HEADS):
        state_ref[h] = states[h]
    project_out(1, 2)


def _hgrn_out(x, norm_w, w_in, lb_logits, hg_norm_w, oat, w_out, final_norm_w):
    b, s, d_model = x.shape
    w = HG_HEADS * HG_DK
    assert s % HG_ROWS == 0 and w == SECTION
    assert w_out.shape == (1, w + oat.shape[-1], d_model)
    coef, level = _hgrn_constants()
    n_chunks = HG_ROWS // HG_CHUNK
    tile = lambda width: pl.BlockSpec((1, HG_ROWS, width), lambda bi, t: (bi, t, 0))
    const = lambda a: pl.BlockSpec(a.shape, lambda bi, t: (0,) * a.ndim)
    act = lambda width: pltpu.VMEM((HG_ROWS, width), BF16)
    return pl.pallas_call(
        _hgrn_out_kernel,
        grid=(b, s // HG_ROWS),
        in_specs=[const(coef), const(level), tile(d_model), const(norm_w),
                  _branch_weight_spec(w_in, 0), const(lb_logits), const(hg_norm_w),
                  tile(oat.shape[-1]), const(w_out), const(final_norm_w)],
        out_specs=tile(d_model),
        out_shape=jax.ShapeDtypeStruct((b, s, d_model), F32),
        scratch_shapes=[pltpu.VMEM((d_model, 4 * SECTION), BF16),
                        pltpu.VMEM(w_out.shape[1:], BF16),
                        pltpu.VMEM((HG_HEADS, HG_DK, HG_DK), F32),
                        act(w), act(2 * w), act(w), act(w), act(w),
                        act(w),
                        pltpu.VMEM((HG_ROWS, w), F32),
                        pltpu.VMEM((HG_HEADS, n_chunks, HG_DK, HG_DK), F32),
                        pltpu.VMEM((HG_HEADS, n_chunks, SUBLANES, HG_DK), F32),
                        act(w)],
        compiler_params=pltpu.CompilerParams(
            dimension_semantics=("parallel", "arbitrary"),
            vmem_limit_bytes=VMEM_LIMIT),
        name="hgrn2_branch_and_output",
    )(coef, level, x, norm_w, w_in, lb_logits, hg_norm_w, oat, w_out, final_norm_w)


def kernel(x, norm_w, w_in, hgrn_lb_logits, hg_norm_w, w_out, final_norm_w):
    b, s, d_model = x.shape
    assert norm_w.shape[0] == 1 and w_in.shape[0] == 1 and w_out.shape[0] == 1
    aq, ak, av, az = _attn_projection(x.reshape(b * s, d_model), norm_w, w_in, s)
    to3 = lambda a: a.reshape(b, s, a.shape[-1])
    oat = _attention(to3(aq), to3(ak), to3(av), to3(az))
    return _hgrn_out(x, norm_w, w_in, hgrn_lb_logits, hg_norm_w, oat, w_out,
                     final_norm_w.reshape(1, d_model))
```

```python
import functools

import numpy as np
import jax
import jax.numpy as jnp
from jax import lax
from jax.experimental import pallas as pl
from jax.experimental.pallas import tpu as pltpu

F32 = jnp.float32
BF16 = jnp.bfloat16

NORM_EPS = 1e-6
ROPE_THETA = 10000.0
LANES = 128

HG_HEADS = 4
HG_DK = 128
HG_CHUNK = 64
AT_HEAD_DIM = 64
SECTION = 512
DILATIONS = (1, 4, 16)
ATT_BLOCK = 128
ATT_TILE = 2 * ATT_BLOCK * max(DILATIONS)
NEG_BIG = -1e30

PROJ_ROWS = 1024
HG_ROWS = 512
ATT_PERM = 4
ATT_MERGE_ROWS = 1024
LOG2E = 1.4426950408889634
VMEM_LIMIT = 56 * 1024 * 1024


def _normalized(x, gain):
    ms = jnp.mean(x * x, axis=-1, keepdims=True)
    return (x * lax.rsqrt(ms + NORM_EPS) * gain).astype(BF16)


def _silu(p):
    return p * (0.5 * jnp.tanh(0.5 * p) + 0.5)


def _project_sections(u, w_ref, plan):
    pending = None
    for j, finish in plan:
        p = jnp.dot(u, w_ref[:, j * SECTION:(j + 1) * SECTION], preferred_element_type=F32)
        if pending is not None:
            pending()
        pending = functools.partial(finish, p)
    pending()


def _dot_nt(a, b):
    return lax.dot_general(a, b, (((1,), (1,)), ((), ())), preferred_element_type=F32)


def _dot_tn(a, b):
    return lax.dot_general(a, b, (((0,), (0,)), ((), ())), preferred_element_type=F32)


def _neg_abs(x):
    bits = lax.bitcast_convert_type(x, jnp.uint32) | jnp.uint32(0x80000000)
    return lax.bitcast_convert_type(bits, F32)


def _attn_proj_kernel(x_ref, nw_ref, w32_ref, cos_ref, sina_ref, sinb_ref,
                      aq_ref, ak_ref, av_ref, az_ref, w_ref):
    @pl.when(pl.program_id(0) == 0)
    def _():
        w_ref[...] = w32_ref[0].astype(BF16)

    u = _normalized(x_ref[...], nw_ref[...])

    def store_rope(dst_ref, scale, p):
        half = AT_HEAD_DIM // 2
        for c in range(SECTION // LANES):
            xc = p[:, c * LANES:(c + 1) * LANES]
            r = (xc * cos_ref[...] + pltpu.roll(xc, LANES - half, 1) * sina_ref[...]
                 + pltpu.roll(xc, half, 1) * sinb_ref[...])
            dst_ref[:, c * LANES:(c + 1) * LANES] = r * scale

    def store_z(p):
        az_ref[...] = _silu(p).astype(BF16)

    def store_v(p):
        av_ref[...] = p

    _project_sections(u, w_ref, [
        (3, store_z),
        (0, functools.partial(store_rope, aq_ref, AT_HEAD_DIM ** -0.5 * LOG2E)),
        (1, functools.partial(store_rope, ak_ref, 1.0)),
        (2, store_v)])


@functools.lru_cache(maxsize=None)
def _rope_tables(seq):
    half = AT_HEAD_DIM // 2
    inv_freq = 1.0 / (ROPE_THETA ** (np.arange(half, dtype=np.float64) / half))
    ang = np.arange(seq, dtype=np.float64)[:, None] * inv_freq[None, :]
    cos = np.cos(ang)
    sin = np.sin(ang)
    zero = np.zeros_like(sin)
    reps = LANES // AT_HEAD_DIM
    cos_t = np.tile(np.concatenate([cos, cos], axis=1), (1, reps))
    sina_t = np.tile(np.concatenate([-sin, zero], axis=1), (1, reps))
    sinb_t = np.tile(np.concatenate([zero, sin], axis=1), (1, reps))
    return tuple(np.asarray(t, np.float32) for t in (cos_t, sina_t, sinb_t))


def _branch_weight_spec(w_in, branch):
    assert w_in.shape[0] == 1 and w_in.shape[2] == 8 * SECTION
    return pl.BlockSpec((1, w_in.shape[1], 4 * SECTION),
                        lambda *_: (0, 0, branch), pipeline_mode=pl.Buffered(1))


def _attn_projection(x2, norm_w, w_in, seq):
    rows, d_model = x2.shape
    assert rows % PROJ_ROWS == 0 and seq % PROJ_ROWS == 0
    tables = _rope_tables(seq)
    seq_tiles = seq // PROJ_ROWS
    row_spec = lambda w: pl.BlockSpec((PROJ_ROWS, w), lambda i: (i, 0))
    tab_spec = pl.BlockSpec((PROJ_ROWS, LANES), lambda i: (i % seq_tiles, 0))
    full = lambda a: pl.BlockSpec(a.shape, lambda i: (0, 0))
    out_dtypes = (F32, F32, F32, BF16)
    return pl.pallas_call(
        _attn_proj_kernel,
        grid=(rows // PROJ_ROWS,),
        in_specs=[row_spec(d_model), full(norm_w), _branch_weight_spec(w_in, 1)]
                 + [tab_spec] * 3,
        out_specs=[row_spec(SECTION)] * len(out_dtypes),
        out_shape=[jax.ShapeDtypeStruct((rows, SECTION), dt) for dt in out_dtypes],
        scratch_shapes=[pltpu.VMEM((d_model, 4 * SECTION), BF16)],
        compiler_params=pltpu.CompilerParams(
            dimension_semantics=("arbitrary",), vmem_limit_bytes=VMEM_LIMIT),
        name="attention_projection",
    )(x2, norm_w, w_in, *tables)


def _attn_bias():
    qi = np.arange(ATT_BLOCK)[:, None]
    kj = np.arange(2 * ATT_BLOCK)[None, :]
    dist = ATT_BLOCK + qi - kj
    valid = (dist >= 0) & (dist <= ATT_BLOCK)
    b0 = np.where(valid, 0.0, NEG_BIG)
    b1 = np.where(valid & (kj >= ATT_BLOCK), 0.0, NEG_BIG)
    return jnp.asarray(np.stack([b0, b1]), F32)


def _attn_kernel(bias_ref, aq_ref, ak_ref, av_ref, az_ref, o_ref,
                 khist, vhist, kprm, vprm, qprm, m_ref, l_ref, acc_ref, onat):
    ta = ATT_TILE
    blk = ATT_BLOCK
    nph = ATT_PERM
    reg = ta // nph
    tile = pl.program_id(2)
    cur = tile % 2
    prev = 1 - cur

    @pl.when((pl.program_id(0) == 0) & (pl.program_id(1) == 0) & (tile == 0))
    def _():
        khist[...] = jnp.zeros_like(khist)
        vhist[...] = jnp.zeros_like(vhist)
        kprm[1] = jnp.zeros((nph, reg, LANES), F32)
        vprm[1] = jnp.zeros((nph, reg, LANES), F32)

    first_head = lax.broadcasted_iota(jnp.int32, (blk, LANES), 1) < AT_HEAD_DIM
    ones = jnp.ones((2 * blk, LANES), BF16)
    bias_any = bias_ref[0]
    bias_first = bias_ref[(tile == 0).astype(jnp.int32)]

    def block_stats(q, k, v, bias):
        k = k.astype(BF16)
        v = v.astype(BF16)
        q2 = jnp.concatenate([jnp.where(first_head, q, 0.0),
                              jnp.where(first_head, 0.0, q)], axis=0).astype(BF16)
        s = _dot_nt(q2, k) + jnp.concatenate([bias, bias], axis=0)
        m = jnp.max(s, axis=-1, keepdims=True)
        p = jnp.exp2(s - m).astype(BF16)
        pv = jnp.dot(p, jnp.concatenate([v, ones], axis=1), preferred_element_type=F32)
        return (jnp.where(first_head, m[:blk], m[blk:]),
                jnp.where(first_head, pv[:blk, LANES:], pv[blk:, LANES:]),
                jnp.where(first_head, pv[:blk, :LANES], pv[blk:, :LANES]))

    def store_stats(pi, rows, stats):
        m_ref[pi, rows, :], l_ref[pi, rows, :], acc_ref[pi, rows, :] = stats

    def window(first_ref, first_rows, ref, rows):
        return jnp.concatenate([first_ref[first_rows], ref[rows]], axis=0)

    for j in range(ta // blk):
        q_rows = pl.ds(j * blk, blk)
        if j == 0:
            k = window(khist, (slice(None),) * 2, ak_ref, (0, q_rows))
            v = window(vhist, (slice(None),) * 2, av_ref, (0, q_rows))
        else:
            k_rows = pl.ds((j - 1) * blk, 2 * blk)
            k, v = ak_ref[0, k_rows, :], av_ref[0, k_rows, :]
        store_stats(0, q_rows, block_stats(aq_ref[0, q_rows, :], k, v,
                                           bias_first if j == 0 else bias_any))

    khist[...] = ak_ref[0, ta - blk:ta, :]
    vhist[...] = av_ref[0, ta - blk:ta, :]

    for r in range(nph):
        kprm[cur, r] = ak_ref[0, pl.ds(r, reg, stride=nph), :]
        vprm[cur, r] = av_ref[0, pl.ds(r, reg, stride=nph), :]
        qprm[r] = aq_ref[0, pl.ds(r, reg, stride=nph), :]

    for r in range(nph):
        for j in range(reg // blk):
            q_rows = pl.ds(j * blk, blk)
            if j == 0:
                last = pl.ds(reg - blk, blk)
                k = window(kprm, (prev, r, last), kprm, (cur, r, q_rows))
                v = window(vprm, (prev, r, last), vprm, (cur, r, q_rows))
            else:
                k_rows = pl.ds((j - 1) * blk, 2 * blk)
                k, v = kprm[cur, r, k_rows, :], vprm[cur, r, k_rows, :]
            stats = block_stats(qprm[r, q_rows, :], k, v, bias_first if j == 0 else bias_any)
            store_stats(1, pl.ds(r * reg + j * blk, blk), stats)
        sub = reg // nph
        for r16 in range(nph):
            for j in range(sub // blk):
                q_rows = pl.ds(r16 + nph * blk * j, blk, stride=nph)
                if j == 0:
                    tail = pl.ds(r16 + nph * (sub - blk), blk, stride=nph)
                    k = window(kprm, (prev, r, tail), kprm, (cur, r, q_rows))
                    v = window(vprm, (prev, r, tail), vprm, (cur, r, q_rows))
                else:
                    k_rows = pl.ds(r16 + nph * blk * (j - 1), 2 * blk, stride=nph)
                    k, v = kprm[cur, r, k_rows, :], vprm[cur, r, k_rows, :]
                stats = block_stats(qprm[r, q_rows, :], k, v,
                                    bias_first if j == 0 else bias_any)
                store_stats(2, pl.ds(r * reg + r16 + nph * blk * j, blk, stride=nph), stats)

    def merge(ci, carry):
        pieces = reg // ATT_MERGE_ROWS
        r = ci // pieces
        i0 = (ci % pieces) * ATT_MERGE_ROWS
        nat_rows = pl.ds(r + nph * i0, ATT_MERGE_ROWS, stride=nph)
        prm_rows = pl.ds(pl.multiple_of(ci * ATT_MERGE_ROWS, ATT_MERGE_ROWS), ATT_MERGE_ROWS)
        rows = (nat_rows, prm_rows, prm_rows)
        ms = [m_ref[pi, rw, :] for pi, rw in enumerate(rows)]
        m_all = functools.reduce(jnp.maximum, ms)
        ws = [jnp.exp2(m - m_all) for m in ms]
        l_all = sum(w * l_ref[pi, rw, :] for (pi, rw), w in zip(enumerate(rows), ws))
        acc = sum(w * acc_ref[pi, rw, :] for (pi, rw), w in zip(enumerate(rows), ws))
        onat[nat_rows, :] = acc / l_all
        return carry

    lax.fori_loop(0, ta // ATT_MERGE_ROWS, merge, 0)

    def gate(ci, carry):
        rows = pl.ds(pl.multiple_of(ci * ATT_MERGE_ROWS, ATT_MERGE_ROWS), ATT_MERGE_ROWS)
        o_ref[0, rows, :] = (onat[rows, :] * az_ref[0, rows, :].astype(F32)).astype(BF16)
        return carry

    lax.fori_loop(0, ta // ATT_MERGE_ROWS, gate, 0)


def _attention(aq, ak, av, az):
    b, s, w = aq.shape
    assert s % ATT_TILE == 0 and w % LANES == 0
    assert DILATIONS == (1, ATT_PERM, ATT_PERM * ATT_PERM)
    assert (ATT_TILE // ATT_PERM) % ATT_MERGE_ROWS == 0
    bias = _attn_bias()
    reg = ATT_TILE // ATT_PERM
    pair_spec = pl.BlockSpec((1, ATT_TILE, LANES), lambda bi, h, t: (bi, t, h))
    hist = pltpu.VMEM((ATT_BLOCK, LANES), F32)
    prm = pltpu.VMEM((2, ATT_PERM, reg, LANES), F32)
    stat = pltpu.VMEM((len(DILATIONS), ATT_TILE, LANES), F32)
    return pl.pallas_call(
        _attn_kernel,
        grid=(b, w // LANES, s // ATT_TILE),
        in_specs=[pl.BlockSpec(bias.shape, lambda bi, h, t: (0, 0, 0))] + [pair_spec] * 4,
        out_specs=pair_spec,
        out_shape=jax.ShapeDtypeStruct((b, s, w), BF16),
        scratch_shapes=[hist, hist, prm, prm, pltpu.VMEM((ATT_PERM, reg, LANES), F32),
                        stat, stat, stat, pltpu.VMEM((ATT_TILE, LANES), F32)],
        compiler_params=pltpu.CompilerParams(
            dimension_semantics=("arbitrary", "arbitrary", "arbitrary"),
            vmem_limit_bytes=VMEM_LIMIT),
        name="dilated_attention",
    )(bias, aq, ak, av, az)


HG_DIAG = 8
HG_LEVELS = (8, 16, 32)
SUBLANES = 8


def _hgrn_constants():
    c = HG_CHUNK
    t = np.arange(c)
    tri = (t[:, None] >= t[None, :]).astype(np.float32)
    first = (t // HG_DIAG) * HG_DIAG
    half = np.concatenate([tri, tri - 0.5 * (tri[first] + tri[first + HG_DIAG - 1])], axis=0)
    coef = np.concatenate([half, half], axis=1)
    ts, ss = t[:, None], t[None, :]
    level = np.full((c, c), len(HG_LEVELS) + 1, np.int32)
    level[(ts // HG_DIAG == ss // HG_DIAG) & (ts >= ss)] = 0
    for i, lb in enumerate(HG_LEVELS):
        m = ((ts // (2 * lb) == ss // (2 * lb)) & ((ts // lb) % 2 == 1)
             & ((ss // lb) % 2 == 0))
        level[m] = i + 1
    return jnp.asarray(coef, BF16), jnp.asarray(level)


def _hgrn_out_kernel(coef_ref, level_ref, x_ref, nw_ref, w_in32_ref, lbl_ref, gn_ref,
                     oat_ref, w32_ref, fw_ref, o_ref,
                     w_in_ref, w_ref, state_ref, qs_ref, g_ref, kk_ref, vi_ref, zs_ref,
                     qd_ref, oi_ref, kv_ref, dec_ref, ohg_ref):
    @pl.when(pl.program_id(1) == 0)
    def _():
        state_ref[...] = jnp.zeros_like(state_ref)
        w_in_ref[...] = w_in32_ref[0].astype(BF16)
        w_ref[...] = w32_ref[0].astype(BF16)

    lg = lbl_ref[...]
    ex = jnp.exp(lg - jnp.max(lg, axis=0, keepdims=True))
    lb = jnp.clip(ex[0:1, :] / jnp.sum(ex, axis=0, keepdims=True), 1e-6, 1.0 - 1e-6)
    f_mid = 0.5 * (1.0 + lb)
    f_amp = 0.5 * (1.0 - lb)

    def store_q(p):
        qs_ref[...] = _silu(p).astype(BF16)

    def store_f(p):
        f = f_mid + f_amp * jnp.tanh(0.5 * p)
        g2 = jnp.log2(f)
        hi_bits = lax.bitcast_convert_type(g2, jnp.uint32) & jnp.uint32(0xFFFF0000)
        g2_hi = lax.bitcast_convert_type(hi_bits, F32)
        g2_lo = (g2 - g2_hi).astype(BF16)
        g2_hi = g2_hi.astype(BF16)
        for h in range(HG_HEADS):
            g_ref[:, 2 * h * HG_DK:(2 * h + 1) * HG_DK] = g2_hi[:, h * HG_DK:(h + 1) * HG_DK]
            g_ref[:, (2 * h + 1) * HG_DK:(2 * h + 2) * HG_DK] = g2_lo[:, h * HG_DK:(h + 1) * HG_DK]
        kk_ref[...] = (1.0 - f).astype(BF16)

    def store_z(p):
        zs_ref[...] = _silu(p).astype(BF16)

    def store_i(p):
        vi_ref[...] = p.astype(BF16)

    u = _normalized(x_ref[0], nw_ref[...])

    def project(j):
        return jnp.dot(u, w_in_ref[:, j * SECTION:(j + 1) * SECTION],
                       preferred_element_type=F32)

    p_f = project(1)
    p_q = project(0)
    store_f(p_f)
    store_q(p_q)

    c = HG_CHUNK
    n_chunks = HG_ROWS // c
    half = HG_HEADS * HG_DK
    coef = coef_ref[...]
    level = level_ref[...]
    units = [(h, ci) for h in range(HG_HEADS) for ci in range(n_chunks)]

    def rows_of(ci):
        return pl.ds(ci * c, c)

    def cols_of(h, width=HG_DK):
        return pl.ds(h * width, width)

    def cum_decay(h, ci):
        g2 = g_ref[rows_of(ci), cols_of(h, 2 * HG_DK)]
        return jnp.dot(coef, jnp.concatenate([g2[:, :HG_DK], g2[:, HG_DK:]], axis=0),
                       preferred_element_type=F32)

    def level_operands(h, ci, be):
        q = qs_ref[rows_of(ci), cols_of(h)]
        k = kk_ref[rows_of(ci), cols_of(h)]
        scaled = lambda t, w: t * w.astype(BF16)
        b = be[0:c]
        e_diag = be[c:2 * c]
        ops = [(scaled(q, jnp.exp2(e_diag)), scaled(k, jnp.exp2(-e_diag)))]
        for lb_rows in HG_LEVELS:
            ref = jnp.concatenate(
                [jnp.broadcast_to(b[p + lb_rows - 1:p + lb_rows, :], (2 * lb_rows, HG_DK))
                 for p in range(0, c, 2 * lb_rows)], axis=0)
            w = jnp.exp2(_neg_abs(b - ref))
            ops.append((scaled(q, w), scaled(k, w)))
        b_last = b[c - 1:c, :]
        return ops, scaled(k, jnp.exp2(b_last - b)), scaled(q, jnp.exp2(b)), jnp.exp2(b_last)

    def intra_scores(ops):
        scores = jnp.where(level == 0, _dot_nt(*ops[0]), 0.0)
        for i in range(len(HG_LEVELS)):
            scores = jnp.where(level == i + 1, _dot_nt(*ops[i + 1]), scores)
        return scores.astype(BF16)

    stage1 = [cum_decay(h, ci) for h, ci in units]
    p_z = project(3)
    stage2 = [level_operands(h, ci, be) for (h, ci), be in zip(units, stage1)]
    p_i = project(2)
    store_z(p_z)
    o_ref[0] = x_ref[0] + jnp.dot(oat_ref[0], w_ref[half:2 * half, :],
                                  preferred_element_type=F32)
    store_i(p_i)

    stage3 = [intra_scores(ops) for ops, _, _, _ in stage2]
    for (h, ci), (_, k_dec, q_dec, decay), scores in zip(units, stage2, stage3):
        v = vi_ref[rows_of(ci), cols_of(h)]
        oi_ref[rows_of(ci), cols_of(h)] = jnp.dot(scores, v, preferred_element_type=F32)
        kv_ref[h, ci] = _dot_tn(v, k_dec)
        qd_ref[rows_of(ci), cols_of(h)] = q_dec
        dec_ref[h, ci] = jnp.broadcast_to(decay, dec_ref.shape[2:])

    states = [state_ref[h] for h in range(HG_HEADS)]
    for ci in range(n_chunks):
        for h in range(HG_HEADS):
            rows, cols = rows_of(ci), cols_of(h)
            o = oi_ref[rows, cols] + _dot_nt(qd_ref[rows, cols], states[h].astype(BF16))
            states[h] = states[h] * dec_ref[h, ci, 0:1, :] + kv_ref[h, ci]
            z = zs_ref[rows, cols].astype(F32)
            ms = jnp.mean(o * o, axis=-1, keepdims=True)
            ohg_ref[rows, cols] = (o * lax.rsqrt(ms + NORM_EPS) * gn_ref[:, cols] * z
                                   ).astype(BF16)
    for h in range(HG_HEADS):
        state_ref[h] = states[h]

    hres = o_ref[0] + jnp.dot(ohg_ref[...], w_ref[0:half, :], preferred_element_type=F32)
    ms = jnp.mean(hres * hres, axis=-1, keepdims=True)
    o_ref[0] = hres * lax.rsqrt(ms + NORM_EPS) * fw_ref[...]


def _hgrn_out(x, norm_w, w_in, lb_logits, hg_norm_w, oat, w_out, final_norm_w):
    b, s, d_model = x.shape
    w = HG_HEADS * HG_DK
    assert s % HG_ROWS == 0 and w == SECTION
    assert w_out.shape == (1, w + oat.shape[-1], d_model)
    coef, level = _hgrn_constants()
    n_chunks = HG_ROWS // HG_CHUNK
    tile = lambda width: pl.BlockSpec((1, HG_ROWS, width), lambda bi, t: (bi, t, 0))
    const = lambda a: pl.BlockSpec(a.shape, lambda bi, t: (0,) * a.ndim)
    act = lambda width: pltpu.VMEM((HG_ROWS, width), BF16)
    return pl.pallas_call(
        _hgrn_out_kernel,
        grid=(b, s // HG_ROWS),
        in_specs=[const(coef), const(level), tile(d_model), const(norm_w),
                  _branch_weight_spec(w_in, 0), const(lb_logits), const(hg_norm_w),
                  tile(oat.shape[-1]), const(w_out), const(final_norm_w)],
        out_specs=tile(d_model),
        out_shape=jax.ShapeDtypeStruct((b, s, d_model), F32),
        scratch_shapes=[pltpu.VMEM((d_model, 4 * SECTION), BF16),
                        pltpu.VMEM(w_out.shape[1:], BF16),
                        pltpu.VMEM((HG_HEADS, HG_DK, HG_DK), F32),
                        act(w), act(2 * w), act(w), act(w), act(w),
                        act(w),
                        pltpu.VMEM((HG_ROWS, w), F32),
                        pltpu.VMEM((HG_HEADS, n_chunks, HG_DK, HG_DK), F32),
                        pltpu.VMEM((HG_HEADS, n_chunks, SUBLANES, HG_DK), F32),
                        act(w)],
        compiler_params=pltpu.CompilerParams(
            dimension_semantics=("parallel", "arbitrary"),
            vmem_limit_bytes=VMEM_LIMIT),
        name="hgrn2_branch_and_output",
    )(coef, level, x, norm_w, w_in, lb_logits, hg_norm_w, oat, w_out, final_norm_w)


def kernel(x, norm_w, w_in, hgrn_lb_logits, hg_norm_w, w_out, final_norm_w):
    b, s, d_model = x.shape
    assert norm_w.shape[0] == 1 and w_in.shape[0] == 1 and w_out.shape[0] == 1
    aq, ak, av, az = _attn_projection(x.reshape(b * s, d_model), norm_w, w_in, s)
    to3 = lambda a: a.reshape(b, s, a.shape[-1])
    oat = _attention(to3(aq), to3(ak), to3(av), to3(az))
    return _hgrn_out(x, norm_w, w_in, hgrn_lb_logits, hg_norm_w, oat, w_out,
                     final_norm_w.reshape(1, d_model))
```

```python
import functools

import numpy as np
import jax
import jax.numpy as jnp
from jax import lax
from jax.experimental import pallas as pl
from jax.experimental.pallas import tpu as pltpu

F32 = jnp.float32
BF16 = jnp.bfloat16

NORM_EPS = 1e-6
ROPE_THETA = 10000.0
LANES = 128

HG_HEADS = 4
HG_DK = 128
HG_CHUNK = 64
AT_HEAD_DIM = 64
SECTION = 512
DILATIONS = (1, 4, 16)
ATT_BLOCK = 128
ATT_TILE = 2 * ATT_BLOCK * max(DILATIONS)
NEG_BIG = -1e30

PROJ_ROWS = 1024
HG_ROWS = 512
ATT_PERM = 4
ATT_MERGE_ROWS = 1024
LOG2E = 1.4426950408889634
VMEM_LIMIT = 56 * 1024 * 1024


def _normalized(x, gain):
    ms = jnp.mean(x * x, axis=-1, keepdims=True)
    return (x * lax.rsqrt(ms + NORM_EPS) * gain).astype(BF16)


def _silu(p):
    return p * (0.5 * jnp.tanh(0.5 * p) + 0.5)


def _project_sections(u, w_ref, plan):
    pending = None
    for j, finish in plan:
        p = jnp.dot(u, w_ref[:, j * SECTION:(j + 1) * SECTION], preferred_element_type=F32)
        if pending is not None:
            pending()
        pending = functools.partial(finish, p)
    pending()


def _dot_nt(a, b):
    return lax.dot_general(a, b, (((1,), (1,)), ((), ())), preferred_element_type=F32)


def _dot_tn(a, b):
    return lax.dot_general(a, b, (((0,), (0,)), ((), ())), preferred_element_type=F32)


def _neg_abs(x):
    bits = lax.bitcast_convert_type(x, jnp.uint32) | jnp.uint32(0x80000000)
    return lax.bitcast_convert_type(bits, F32)


def _attn_proj_kernel(x_ref, nw_ref, w32_ref, cos_ref, sina_ref, sinb_ref,
                      aq_ref, ak_ref, av_ref, az_ref, w_ref):
    @pl.when(pl.program_id(0) == 0)
    def _():
        w_ref[...] = (w32_ref[0] * nw_ref[...]).astype(BF16)

    x = x_ref[...]
    u = x.astype(BF16)
    row_scale = lax.rsqrt(jnp.mean(x * x, axis=-1, keepdims=True) + NORM_EPS)

    def store_rope(dst_ref, scale, p):
        half = AT_HEAD_DIM // 2
        row = row_scale * scale
        for c in range(SECTION // LANES):
            xc = p[:, c * LANES:(c + 1) * LANES]
            r = (xc * cos_ref[...] + pltpu.roll(xc, LANES - half, 1) * sina_ref[...]
                 + pltpu.roll(xc, half, 1) * sinb_ref[...])
            dst_ref[:, c * LANES:(c + 1) * LANES] = r * row

    def store_z(p):
        az_ref[...] = _silu(p * row_scale).astype(BF16)

    def store_v(p):
        av_ref[...] = p * row_scale

    _project_sections(u, w_ref, [
        (3, store_z),
        (0, functools.partial(store_rope, aq_ref, AT_HEAD_DIM ** -0.5 * LOG2E)),
        (1, functools.partial(store_rope, ak_ref, 1.0)),
        (2, store_v)])


@functools.lru_cache(maxsize=None)
def _rope_tables(seq):
    half = AT_HEAD_DIM // 2
    inv_freq = 1.0 / (ROPE_THETA ** (np.arange(half, dtype=np.float64) / half))
    ang = np.arange(seq, dtype=np.float64)[:, None] * inv_freq[None, :]
    cos = np.cos(ang)
    sin = np.sin(ang)
    zero = np.zeros_like(sin)
    reps = LANES // AT_HEAD_DIM
    cos_t = np.tile(np.concatenate([cos, cos], axis=1), (1, reps))
    sina_t = np.tile(np.concatenate([-sin, zero], axis=1), (1, reps))
    sinb_t = np.tile(np.concatenate([zero, sin], axis=1), (1, reps))
    return tuple(np.asarray(t, np.float32) for t in (cos_t, sina_t, sinb_t))


def _branch_weight_spec(w_in, branch):
    assert w_in.shape[0] == 1 and w_in.shape[2] == 8 * SECTION
    return pl.BlockSpec((1, w_in.shape[1], 4 * SECTION),
                        lambda *_: (0, 0, branch), pipeline_mode=pl.Buffered(1))


def _attn_projection(x2, norm_w, w_in, seq):
    rows, d_model = x2.shape
    assert rows % PROJ_ROWS == 0 and seq % PROJ_ROWS == 0
    tables = _rope_tables(seq)
    seq_tiles = seq // PROJ_ROWS
    row_spec = lambda w: pl.BlockSpec((PROJ_ROWS, w), lambda i: (i, 0))
    tab_spec = pl.BlockSpec((PROJ_ROWS, LANES), lambda i: (i % seq_tiles, 0))
    full = lambda a: pl.BlockSpec(a.shape, lambda i: (0, 0))
    out_dtypes = (F32, F32, F32, BF16)
    return pl.pallas_call(
        _attn_proj_kernel,
        grid=(rows // PROJ_ROWS,),
        in_specs=[row_spec(d_model), full(norm_w), _branch_weight_spec(w_in, 1)]
                 + [tab_spec] * 3,
        out_specs=[row_spec(SECTION)] * len(out_dtypes),
        out_shape=[jax.ShapeDtypeStruct((rows, SECTION), dt) for dt in out_dtypes],
        scratch_shapes=[pltpu.VMEM((d_model, 4 * SECTION), BF16)],
        compiler_params=pltpu.CompilerParams(
            dimension_semantics=("arbitrary",), vmem_limit_bytes=VMEM_LIMIT),
        name="attention_projection",
    )(x2, norm_w, w_in, *tables)


def _attn_bias():
    qi = np.arange(ATT_BLOCK)[:, None]
    kj = np.arange(2 * ATT_BLOCK)[None, :]
    dist = ATT_BLOCK + qi - kj
    valid = (dist >= 0) & (dist <= ATT_BLOCK)
    b0 = np.where(valid, 0.0, NEG_BIG)
    b1 = np.where(valid & (kj >= ATT_BLOCK), 0.0, NEG_BIG)
    return jnp.asarray(np.stack([b0, b1]), F32)


def _attn_kernel(bias_ref, aq_ref, ak_ref, av_ref, az_ref, o_ref,
                 khist, vhist, kprm, vprm, qprm, m_ref, l_ref, acc_ref, onat):
    ta = ATT_TILE
    blk = ATT_BLOCK
    nph = ATT_PERM
    reg = ta // nph
    tile = pl.program_id(2)
    cur = tile % 2
    prev = 1 - cur

    @pl.when((pl.program_id(0) == 0) & (pl.program_id(1) == 0) & (tile == 0))
    def _():
        khist[...] = jnp.zeros_like(khist)
        vhist[...] = jnp.zeros_like(vhist)
        kprm[1] = jnp.zeros((nph, reg, LANES), F32)
        vprm[1] = jnp.zeros((nph, reg, LANES), F32)

    for r in range(nph):
        kprm[cur, r] = ak_ref[0, pl.ds(r, reg, stride=nph), :]
        vprm[cur, r] = av_ref[0, pl.ds(r, reg, stride=nph), :]
        qprm[r] = aq_ref[0, pl.ds(r, reg, stride=nph), :]

    first_head = lax.broadcasted_iota(jnp.int32, (blk, LANES), 1) < AT_HEAD_DIM
    ones = jnp.ones((2 * blk, LANES), BF16)
    bias_any = bias_ref[0]
    bias_first = bias_ref[(tile == 0).astype(jnp.int32)]

    def block_stats(q, k, v, bias):
        k = k.astype(BF16)
        v = v.astype(BF16)
        q2 = jnp.concatenate([jnp.where(first_head, q, 0.0),
                              jnp.where(first_head, 0.0, q)], axis=0).astype(BF16)
        s = _dot_nt(q2, k) + jnp.concatenate([bias, bias], axis=0)
        m = jnp.max(s, axis=-1, keepdims=True)
        p = jnp.exp2(s - m).astype(BF16)
        pv = jnp.dot(p, jnp.concatenate([v, ones], axis=1), preferred_element_type=F32)
        return (jnp.where(first_head, m[:blk], m[blk:]),
                jnp.where(first_head, pv[:blk, LANES:], pv[blk:, LANES:]),
                jnp.where(first_head, pv[:blk, :LANES], pv[blk:, :LANES]))

    def store_stats(pi, rows, stats):
        m_ref[pi, rows, :], l_ref[pi, rows, :], acc_ref[pi, rows, :] = stats

    def window(first_ref, first_rows, ref, rows):
        return jnp.concatenate([first_ref[first_rows], ref[rows]], axis=0)

    for j in range(ta // blk):
        q_rows = pl.ds(j * blk, blk)
        if j == 0:
            k = window(khist, (slice(None),) * 2, ak_ref, (0, q_rows))
            v = window(vhist, (slice(None),) * 2, av_ref, (0, q_rows))
        else:
            k_rows = pl.ds((j - 1) * blk, 2 * blk)
            k, v = ak_ref[0, k_rows, :], av_ref[0, k_rows, :]
        store_stats(0, q_rows, block_stats(aq_ref[0, q_rows, :], k, v,
                                           bias_first if j == 0 else bias_any))

    khist[...] = ak_ref[0, ta - blk:ta, :]
    vhist[...] = av_ref[0, ta - blk:ta, :]

    for r in range(nph):
        for j in range(reg // blk):
            q_rows = pl.ds(j * blk, blk)
            if j == 0:
                last = pl.ds(reg - blk, blk)
                k = window(kprm, (prev, r, last), kprm, (cur, r, q_rows))
                v = window(vprm, (prev, r, last), vprm, (cur, r, q_rows))
            else:
                k_rows = pl.ds((j - 1) * blk, 2 * blk)
                k, v = kprm[cur, r, k_rows, :], vprm[cur, r, k_rows, :]
            stats = block_stats(qprm[r, q_rows, :], k, v, bias_first if j == 0 else bias_any)
            store_stats(1, pl.ds(r * reg + j * blk, blk), stats)
        sub = reg // nph
        for r16 in range(nph):
            for j in range(sub // blk):
                q_rows = pl.ds(r16 + nph * blk * j, blk, stride=nph)
                if j == 0:
                    tail = pl.ds(r16 + nph * (sub - blk), blk, stride=nph)
                    k = window(kprm, (prev, r, tail), kprm, (cur, r, q_rows))
                    v = window(vprm, (prev, r, tail), vprm, (cur, r, q_rows))
                else:
                    k_rows = pl.ds(r16 + nph * blk * (j - 1), 2 * blk, stride=nph)
                    k, v = kprm[cur, r, k_rows, :], vprm[cur, r, k_rows, :]
                stats = block_stats(qprm[r, q_rows, :], k, v,
                                    bias_first if j == 0 else bias_any)
                store_stats(2, pl.ds(r * reg + r16 + nph * blk * j, blk, stride=nph), stats)

    def merge(ci, carry):
        pieces = reg // ATT_MERGE_ROWS
        r = ci // pieces
        i0 = (ci % pieces) * ATT_MERGE_ROWS
        nat_rows = pl.ds(r + nph * i0, ATT_MERGE_ROWS, stride=nph)
        prm_rows = pl.ds(pl.multiple_of(ci * ATT_MERGE_ROWS, ATT_MERGE_ROWS), ATT_MERGE_ROWS)
        rows = (nat_rows, prm_rows, prm_rows)
        ms = [m_ref[pi, rw, :] for pi, rw in enumerate(rows)]
        m_all = functools.reduce(jnp.maximum, ms)
        ws = [jnp.exp2(m - m_all) for m in ms]
        l_all = sum(w * l_ref[pi, rw, :] for (pi, rw), w in zip(enumerate(rows), ws))
        acc = sum(w * acc_ref[pi, rw, :] for (pi, rw), w in zip(enumerate(rows), ws))
        onat[nat_rows, :] = acc / l_all
        return carry

    lax.fori_loop(0, ta // ATT_MERGE_ROWS, merge, 0)

    def gate(ci, carry):
        rows = pl.ds(pl.multiple_of(ci * ATT_MERGE_ROWS, ATT_MERGE_ROWS), ATT_MERGE_ROWS)
        o_ref[0, rows, :] = (onat[rows, :] * az_ref[0, rows, :].astype(F32)).astype(BF16)
        return carry

    lax.fori_loop(0, ta // ATT_MERGE_ROWS, gate, 0)


def _attention(aq, ak, av, az):
    b, s, w = aq.shape
    assert s % ATT_TILE == 0 and w % LANES == 0
    assert DILATIONS == (1, ATT_PERM, ATT_PERM * ATT_PERM)
    assert (ATT_TILE // ATT_PERM) % ATT_MERGE_ROWS == 0
    bias = _attn_bias()
    reg = ATT_TILE // ATT_PERM
    pair_spec = pl.BlockSpec((1, ATT_TILE, LANES), lambda bi, h, t: (bi, t, h))
    hist = pltpu.VMEM((ATT_BLOCK, LANES), F32)
    prm = pltpu.VMEM((2, ATT_PERM, reg, LANES), F32)
    stat = pltpu.VMEM((len(DILATIONS), ATT_TILE, LANES), F32)
    return pl.pallas_call(
        _attn_kernel,
        grid=(b, w // LANES, s // ATT_TILE),
        in_specs=[pl.BlockSpec(bias.shape, lambda bi, h, t: (0, 0, 0))] + [pair_spec] * 4,
        out_specs=pair_spec,
        out_shape=jax.ShapeDtypeStruct((b, s, w), BF16),
        scratch_shapes=[hist, hist, prm, prm, pltpu.VMEM((ATT_PERM, reg, LANES), F32),
                        stat, stat, stat, pltpu.VMEM((ATT_TILE, LANES), F32)],
        compiler_params=pltpu.CompilerParams(
            dimension_semantics=("arbitrary", "arbitrary", "arbitrary"),
            vmem_limit_bytes=VMEM_LIMIT),
        name="dilated_attention",
    )(bias, aq, ak, av, az)


HG_DIAG = 8
HG_LEVELS = (8, 16, 32)
SUBLANES = 8


def _hgrn_constants():
    c = HG_CHUNK
    t = np.arange(c)
    tri = (t[:, None] >= t[None, :]).astype(np.float32)
    first = (t // HG_DIAG) * HG_DIAG
    half = np.concatenate([tri, tri - 0.5 * (tri[first] + tri[first + HG_DIAG - 1])], axis=0)
    coef = np.concatenate([half, half], axis=1)
    ts, ss = t[:, None], t[None, :]
    level = np.full((c, c), len(HG_LEVELS) + 1, np.int32)
    level[(ts // HG_DIAG == ss // HG_DIAG) & (ts >= ss)] = 0
    for i, lb in enumerate(HG_LEVELS):
        m = ((ts // (2 * lb) == ss // (2 * lb)) & ((ts // lb) % 2 == 1)
             & ((ss // lb) % 2 == 0))
        level[m] = i + 1
    return jnp.asarray(coef, BF16), jnp.asarray(level)


def _hgrn_out_kernel(coef_ref, level_ref, x_ref, nw_ref, w_in32_ref, lbl_ref, gn_ref,
                     oat_ref, w32_ref, fw_ref, o_ref,
                     w_in_ref, w_ref, state_ref, qs_ref, g_ref, kk_ref, vi_ref, zs_ref,
                     qd_ref, oi_ref, kv_ref, dec_ref, ohg_ref):
    @pl.when(pl.program_id(1) == 0)
    def _():
        state_ref[...] = jnp.zeros_like(state_ref)
        w_in_ref[...] = w_in32_ref[0].astype(BF16)
        w_ref[...] = w32_ref[0].astype(BF16)

    lg = lbl_ref[...]
    ex = jnp.exp(lg - jnp.max(lg, axis=0, keepdims=True))
    lb = jnp.clip(ex[0:1, :] / jnp.sum(ex, axis=0, keepdims=True), 1e-6, 1.0 - 1e-6)
    f_mid = 0.5 * (1.0 + lb)
    f_amp = 0.5 * (1.0 - lb)

    def store_q(p):
        qs_ref[...] = _silu(p).astype(BF16)

    def store_f(p):
        f = f_mid + f_amp * jnp.tanh(0.5 * p)
        g2 = jnp.log2(f)
        hi_bits = lax.bitcast_convert_type(g2, jnp.uint32) & jnp.uint32(0xFFFF0000)
        g2_hi = lax.bitcast_convert_type(hi_bits, F32)
        g2_lo = (g2 - g2_hi).astype(BF16)
        g2_hi = g2_hi.astype(BF16)
        for h in range(HG_HEADS):
            g_ref[:, 2 * h * HG_DK:(2 * h + 1) * HG_DK] = g2_hi[:, h * HG_DK:(h + 1) * HG_DK]
            g_ref[:, (2 * h + 1) * HG_DK:(2 * h + 2) * HG_DK] = g2_lo[:, h * HG_DK:(h + 1) * HG_DK]
        kk_ref[...] = (1.0 - f).astype(BF16)

    def store_z(p):
        zs_ref[...] = _silu(p).astype(BF16)

    def store_i(p):
        vi_ref[...] = p.astype(BF16)

    u = _normalized(x_ref[0], nw_ref[...])

    def project(j):
        return jnp.dot(u, w_in_ref[:, j * SECTION:(j + 1) * SECTION],
                       preferred_element_type=F32)

    p_f = project(1)
    p_q = project(0)
    store_f(p_f)
    store_q(p_q)

    c = HG_CHUNK
    n_chunks = HG_ROWS // c
    half = HG_HEADS * HG_DK
    coef = coef_ref[...]
    level = level_ref[...]
    units = [(h, ci) for h in range(HG_HEADS) for ci in range(n_chunks)]

    def rows_of(ci):
        return pl.ds(ci * c, c)

    def cols_of(h, width=HG_DK):
        return pl.ds(h * width, width)

    def cum_decay(h, ci):
        g2 = g_ref[rows_of(ci), cols_of(h, 2 * HG_DK)]
        return jnp.dot(coef, jnp.concatenate([g2[:, :HG_DK], g2[:, HG_DK:]], axis=0),
                       preferred_element_type=F32)

    def level_operands(h, ci, be):
        q = qs_ref[rows_of(ci), cols_of(h)]
        k = kk_ref[rows_of(ci), cols_of(h)]
        scaled = lambda t, w: t * w.astype(BF16)
        b = be[0:c]
        e_diag = be[c:2 * c]
        ops = [(scaled(q, jnp.exp2(e_diag)), scaled(k, jnp.exp2(-e_diag)))]
        for lb_rows in HG_LEVELS:
            ref = jnp.concatenate(
                [jnp.broadcast_to(b[p + lb_rows - 1:p + lb_rows, :], (2 * lb_rows, HG_DK))
                 for p in range(0, c, 2 * lb_rows)], axis=0)
            w = jnp.exp2(_neg_abs(b - ref))
            ops.append((scaled(q, w), scaled(k, w)))
        b_last = b[c - 1:c, :]
        return ops, scaled(k, jnp.exp2(b_last - b)), scaled(q, jnp.exp2(b)), jnp.exp2(b_last)

    def intra_scores(ops):
        scores = jnp.where(level == 0, _dot_nt(*ops[0]), 0.0)
        for i in range(len(HG_LEVELS)):
            scores = jnp.where(level == i + 1, _dot_nt(*ops[i + 1]), scores)
        return scores.astype(BF16)

    stage1 = [cum_decay(h, ci) for h, ci in units]
    p_z = project(3)
    stage2 = [level_operands(h, ci, be) for (h, ci), be in zip(units, stage1)]
    p_i = project(2)
    store_z(p_z)
    o_ref[0] = x_ref[0] + jnp.dot(oat_ref[0], w_ref[half:2 * half, :],
                                  preferred_element_type=F32)
    store_i(p_i)

    stage3 = [intra_scores(ops) for ops, _, _, _ in stage2]
    for (h, ci), (_, k_dec, q_dec, decay), scores in zip(units, stage2, stage3):
        v = vi_ref[rows_of(ci), cols_of(h)]
        oi_ref[rows_of(ci), cols_of(h)] = jnp.dot(scores, v, preferred_element_type=F32)
        kv_ref[h, ci] = _dot_tn(v, k_dec)
        qd_ref[rows_of(ci), cols_of(h)] = q_dec
        dec_ref[h, ci] = jnp.broadcast_to(decay, dec_ref.shape[2:])

    states = [state_ref[h] for h in range(HG_HEADS)]
    for ci in range(n_chunks):
        for h in range(HG_HEADS):
            rows, cols = rows_of(ci), cols_of(h)
            o = oi_ref[rows, cols] + _dot_nt(qd_ref[rows, cols], states[h].astype(BF16))
            states[h] = states[h] * dec_ref[h, ci, 0:1, :] + kv_ref[h, ci]
            z = zs_ref[rows, cols].astype(F32)
            ms = jnp.mean(o * o, axis=-1, keepdims=True)
            ohg_ref[rows, cols] = (o * lax.rsqrt(ms + NORM_EPS) * gn_ref[:, cols] * z
                                   ).astype(BF16)
    for h in range(HG_HEADS):
        state_ref[h] = states[h]

    hres = o_ref[0] + jnp.dot(ohg_ref[...], w_ref[0:half, :], preferred_element_type=F32)
    ms = jnp.mean(hres * hres, axis=-1, keepdims=True)
    o_ref[0] = hres * lax.rsqrt(ms + NORM_EPS) * fw_ref[...]


def _hgrn_out(x, norm_w, w_in, lb_logits, hg_norm_w, oat, w_out, final_norm_w):
    b, s, d_model = x.shape
    w = HG_HEADS * HG_DK
    assert s % HG_ROWS == 0 and w == SECTION
    assert w_out.shape == (1, w + oat.shape[-1], d_model)
    coef, level = _hgrn_constants()
    n_chunks = HG_ROWS // HG_CHUNK
    tile = lambda width: pl.BlockSpec((1, HG_ROWS, width), lambda bi, t: (bi, t, 0))
    const = lambda a: pl.BlockSpec(a.shape, lambda bi, t: (0,) * a.ndim)
    act = lambda width: pltpu.VMEM((HG_ROWS, width), BF16)
    return pl.pallas_call(
        _hgrn_out_kernel,
        grid=(b, s // HG_ROWS),
        in_specs=[const(coef), const(level), tile(d_model), const(norm_w),
                  _branch_weight_spec(w_in, 0), const(lb_logits), const(hg_norm_w),
                  tile(oat.shape[-1]), const(w_out), const(final_norm_w)],
        out_specs=tile(d_model),
        out_shape=jax.ShapeDtypeStruct((b, s, d_model), F32),
        scratch_shapes=[pltpu.VMEM((d_model, 4 * SECTION), BF16),
                        pltpu.VMEM(w_out.shape[1:], BF16),
                        pltpu.VMEM((HG_HEADS, HG_DK, HG_DK), F32),
                        act(w), act(2 * w), act(w), act(w), act(w),
                        act(w),
                        pltpu.VMEM((HG_ROWS, w), F32),
                        pltpu.VMEM((HG_HEADS, n_chunks, HG_DK, HG_DK), F32),
                        pltpu.VMEM((HG_HEADS, n_chunks, SUBLANES, HG_DK), F32),
                        act(w)],
        compiler_params=pltpu.CompilerParams(
            dimension_semantics=("parallel", "arbitrary"),
            vmem_limit_bytes=VMEM_LIMIT),
        name="hgrn2_branch_and_output",
    )(coef, level, x, norm_w, w_in, lb_logits, hg_norm_w, oat, w_out, final_norm_w)


def kernel(x, norm_w, w_in, hgrn_lb_logits, hg_norm_w, w_out, final_norm_w):
    b, s, d_model = x.shape
    assert norm_w.shape[0] == 1 and w_in.shape[0] == 1 and w_out.shape[0] == 1
    aq, ak, av, az = _attn_projection(x.reshape(b * s, d_model),
                                      norm_w.reshape(d_model, 1), w_in, s)
    to3 = lambda a: a.reshape(b, s, a.shape[-1])
    oat = _attention(to3(aq), to3(ak), to3(av), to3(az))
    return _hgrn_out(x, norm_w, w_in, hgrn_lb_logits, hg_norm_w, oat, w_out,
                     final_norm_w.reshape(1, d_model))
```

```python
import functools

import numpy as np
import jax
import jax.numpy as jnp
from jax import lax
from jax.experimental import pallas as pl
from jax.experimental.pallas import tpu as pltpu

F32 = jnp.float32
BF16 = jnp.bfloat16

NORM_EPS = 1e-6
ROPE_THETA = 10000.0
LANES = 128

HG_HEADS = 4
HG_DK = 128
HG_CHUNK = 64
AT_HEAD_DIM = 64
SECTION = 512
DILATIONS = (1, 4, 16)
ATT_BLOCK = 128
ATT_TILE = 2 * ATT_BLOCK * max(DILATIONS)
NEG_BIG = -1e30

PROJ_ROWS = 1024
HG_ROWS = 512
ATT_PERM = 4
ATT_MERGE_ROWS = 1024
LOG2E = 1.4426950408889634
VMEM_LIMIT = 56 * 1024 * 1024


def _normalized(x, gain):
    ms = jnp.mean(x * x, axis=-1, keepdims=True)
    return (x * lax.rsqrt(ms + NORM_EPS) * gain).astype(BF16)


def _silu(p):
    return p * (0.5 * jnp.tanh(0.5 * p) + 0.5)


def _project_sections(u, w_ref, plan):
    pending = None
    for j, finish in plan:
        p = jnp.dot(u, w_ref[:, j * SECTION:(j + 1) * SECTION], preferred_element_type=F32)
        if pending is not None:
            pending()
        pending = functools.partial(finish, p)
    pending()


def _dot_nt(a, b):
    return lax.dot_general(a, b, (((1,), (1,)), ((), ())), preferred_element_type=F32)


def _dot_tn(a, b):
    return lax.dot_general(a, b, (((0,), (0,)), ((), ())), preferred_element_type=F32)


def _neg_abs(x):
    bits = lax.bitcast_convert_type(x, jnp.uint32) | jnp.uint32(0x80000000)
    return lax.bitcast_convert_type(bits, F32)


def _attn_proj_kernel(x_ref, nw_ref, w32_ref, cos_ref, sina_ref, sinb_ref,
                      aq_ref, ak_ref, av_ref, az_ref, w_ref):
    @pl.when(pl.program_id(0) == 0)
    def _():
        w_ref[...] = w32_ref[0].astype(BF16)

    u = _normalized(x_ref[...], nw_ref[...])

    def store_rope(dst_ref, scale, p):
        half = AT_HEAD_DIM // 2
        for c in range(SECTION // LANES):
            xc = p[:, c * LANES:(c + 1) * LANES]
            r = (xc * cos_ref[...] + pltpu.roll(xc, LANES - half, 1) * sina_ref[...]
                 + pltpu.roll(xc, half, 1) * sinb_ref[...])
            dst_ref[:, c * LANES:(c + 1) * LANES] = r * scale

    def store_z(p):
        az_ref[...] = _silu(p).astype(BF16)

    def store_v(p):
        av_ref[...] = p

    _project_sections(u, w_ref, [
        (3, store_z),
        (0, functools.partial(store_rope, aq_ref, AT_HEAD_DIM ** -0.5 * LOG2E)),
        (1, functools.partial(store_rope, ak_ref, 1.0)),
        (2, store_v)])


@functools.lru_cache(maxsize=None)
def _rope_tables(seq):
    half = AT_HEAD_DIM // 2
    inv_freq = 1.0 / (ROPE_THETA ** (np.arange(half, dtype=np.float64) / half))
    ang = np.arange(seq, dtype=np.float64)[:, None] * inv_freq[None, :]
    cos = np.cos(ang)
    sin = np.sin(ang)
    zero = np.zeros_like(sin)
    reps = LANES // AT_HEAD_DIM
    cos_t = np.tile(np.concatenate([cos, cos], axis=1), (1, reps))
    sina_t = np.tile(np.concatenate([-sin, zero], axis=1), (1, reps))
    sinb_t = np.tile(np.concatenate([zero, sin], axis=1), (1, reps))
    return tuple(np.asarray(t, np.float32) for t in (cos_t, sina_t, sinb_t))


def _branch_weight_spec(w_in, branch):
    assert w_in.shape[0] == 1 and w_in.shape[2] == 8 * SECTION
    return pl.BlockSpec((1, w_in.shape[1], 4 * SECTION),
                        lambda *_: (0, 0, branch), pipeline_mode=pl.Buffered(1))


def _attn_projection(x2, norm_w, w_in, seq):
    rows, d_model = x2.shape
    assert rows % PROJ_ROWS == 0 and seq % PROJ_ROWS == 0
    tables = _rope_tables(seq)
    seq_tiles = seq // PROJ_ROWS
    row_spec = lambda w: pl.BlockSpec((PROJ_ROWS, w), lambda i: (i, 0))
    tab_spec = pl.BlockSpec((PROJ_ROWS, LANES), lambda i: (i % seq_tiles, 0))
    full = lambda a: pl.BlockSpec(a.shape, lambda i: (0, 0))
    out_dtypes = (F32, F32, F32, BF16)
    return pl.pallas_call(
        _attn_proj_kernel,
        grid=(rows // PROJ_ROWS,),
        in_specs=[row_spec(d_model), full(norm_w), _branch_weight_spec(w_in, 1)]
                 + [tab_spec] * 3,
        out_specs=[row_spec(SECTION)] * len(out_dtypes),
        out_shape=[jax.ShapeDtypeStruct((rows, SECTION), dt) for dt in out_dtypes],
        scratch_shapes=[pltpu.VMEM((d_model, 4 * SECTION), BF16)],
        compiler_params=pltpu.CompilerParams(
            dimension_semantics=("arbitrary",), vmem_limit_bytes=VMEM_LIMIT),
        name="attention_projection",
    )(x2, norm_w, w_in, *tables)


def _attn_bias():
    qi = np.arange(ATT_BLOCK)[:, None]
    kj = np.arange(2 * ATT_BLOCK)[None, :]
    dist = ATT_BLOCK + qi - kj
    valid = (dist >= 0) & (dist <= ATT_BLOCK)
    b0 = np.where(valid, 0.0, NEG_BIG)
    b1 = np.where(valid & (kj >= ATT_BLOCK), 0.0, NEG_BIG)
    return jnp.asarray(np.stack([b0, b1]), F32)


def _attn_kernel(bias_ref, aq_ref, ak_ref, av_ref, az_ref, o_ref,
                 khist, vhist, kprm, vprm, qprm, m_ref, l_ref, acc_ref, onat):
    ta = ATT_TILE
    blk = ATT_BLOCK
    nph = ATT_PERM
    reg = ta // nph
    tile = pl.program_id(2)
    cur = tile % 2
    prev = 1 - cur

    @pl.when((pl.program_id(0) == 0) & (pl.program_id(1) == 0) & (tile == 0))
    def _():
        khist[...] = jnp.zeros_like(khist)
        vhist[...] = jnp.zeros_like(vhist)
        kprm[1] = jnp.zeros((nph, reg, LANES), F32)
        vprm[1] = jnp.zeros((nph, reg, LANES), F32)

    for r in range(nph):
        kprm[cur, r] = ak_ref[0, pl.ds(r, reg, stride=nph), :]
        vprm[cur, r] = av_ref[0, pl.ds(r, reg, stride=nph), :]
        qprm[r] = aq_ref[0, pl.ds(r, reg, stride=nph), :]

    first_head = lax.broadcasted_iota(jnp.int32, (blk, LANES), 1) < AT_HEAD_DIM
    ones = jnp.ones((2 * blk, LANES), BF16)
    bias_any = bias_ref[0]
    bias_first = bias_ref[(tile == 0).astype(jnp.int32)]

    def block_stats(q, k, v, bias):
        k = k.astype(BF16)
        v = v.astype(BF16)
        q2 = jnp.concatenate([jnp.where(first_head, q, 0.0),
                              jnp.where(first_head, 0.0, q)], axis=0).astype(BF16)
        s = _dot_nt(q2, k) + jnp.concatenate([bias, bias], axis=0)
        m = jnp.max(s, axis=-1, keepdims=True)
        p = jnp.exp2(s - m).astype(BF16)
        pv = jnp.dot(p, jnp.concatenate([v, ones], axis=1), preferred_element_type=F32)
        return (jnp.where(first_head, m[:blk], m[blk:]),
                jnp.where(first_head, pv[:blk, LANES:], pv[blk:, LANES:]),
                jnp.where(first_head, pv[:blk, :LANES], pv[blk:, :LANES]))

    def store_stats(pi, rows, stats):
        m_ref[pi, rows, :], l_ref[pi, rows, :], acc_ref[pi, rows, :] = stats

    def window(first_ref, first_rows, ref, rows):
        return jnp.concatenate([first_ref[first_rows], ref[rows]], axis=0)

    for j in range(ta // blk):
        q_rows = pl.ds(j * blk, blk)
        if j == 0:
            k = window(khist, (slice(None),) * 2, ak_ref, (0, q_rows))
            v = window(vhist, (slice(None),) * 2, av_ref, (0, q_rows))
        else:
            k_rows = pl.ds((j - 1) * blk, 2 * blk)
            k, v = ak_ref[0, k_rows, :], av_ref[0, k_rows, :]
        store_stats(0, q_rows, block_stats(aq_ref[0, q_rows, :], k, v,
                                           bias_first if j == 0 else bias_any))

    khist[...] = ak_ref[0, ta - blk:ta, :]
    vhist[...] = av_ref[0, ta - blk:ta, :]

    for r in range(nph):
        for j in range(reg // blk):
            q_rows = pl.ds(j * blk, blk)
            if j == 0:
                last = pl.ds(reg - blk, blk)
                k = window(kprm, (prev, r, last), kprm, (cur, r, q_rows))
                v = window(vprm, (prev, r, last), vprm, (cur, r, q_rows))
            else:
                k_rows = pl.ds((j - 1) * blk, 2 * blk)
                k, v = kprm[cur, r, k_rows, :], vprm[cur, r, k_rows, :]
            stats = block_stats(qprm[r, q_rows, :], k, v, bias_first if j == 0 else bias_any)
            store_stats(1, pl.ds(r * reg + j * blk, blk), stats)
        sub = reg // nph
        for r16 in range(nph):
            for j in range(sub // blk):
                q_rows = pl.ds(r16 + nph * blk * j, blk, stride=nph)
                if j == 0:
                    tail = pl.ds(r16 + nph * (sub - blk), blk, stride=nph)
                    k = window(kprm, (prev, r, tail), kprm, (cur, r, q_rows))
                    v = window(vprm, (prev, r, tail), vprm, (cur, r, q_rows))
                else:
                    k_rows = pl.ds(r16 + nph * blk * (j - 1), 2 * blk, stride=nph)
                    k, v = kprm[cur, r, k_rows, :], vprm[cur, r, k_rows, :]
                stats = block_stats(qprm[r, q_rows, :], k, v,
                                    bias_first if j == 0 else bias_any)
                store_stats(2, pl.ds(r * reg + r16 + nph * blk * j, blk, stride=nph), stats)

    def merge(ci, carry):
        pieces = reg // ATT_MERGE_ROWS
        r = ci // pieces
        i0 = (ci % pieces) * ATT_MERGE_ROWS
        nat_rows = pl.ds(r + nph * i0, ATT_MERGE_ROWS, stride=nph)
        prm_rows = pl.ds(pl.multiple_of(ci * ATT_MERGE_ROWS, ATT_MERGE_ROWS), ATT_MERGE_ROWS)
        rows = (nat_rows, prm_rows, prm_rows)
        ms = [m_ref[pi, rw, :] for pi, rw in enumerate(rows)]
        m_all = functools.reduce(jnp.maximum, ms)
        ws = [jnp.exp2(m - m_all) for m in ms]
        l_all = sum(w * l_ref[pi, rw, :] for (pi, rw), w in zip(enumerate(rows), ws))
        acc = sum(w * acc_ref[pi, rw, :] for (pi, rw), w in zip(enumerate(rows), ws))
        onat[nat_rows, :] = acc / l_all
        return carry

    lax.fori_loop(0, ta // ATT_MERGE_ROWS, merge, 0)

    def gate(ci, carry):
        rows = pl.ds(pl.multiple_of(ci * ATT_MERGE_ROWS, ATT_MERGE_ROWS), ATT_MERGE_ROWS)
        o_ref[0, rows, :] = (onat[rows, :] * az_ref[0, rows, :].astype(F32)).astype(BF16)
        return carry

    lax.fori_loop(0, ta // ATT_MERGE_ROWS, gate, 0)


def _attention(aq, ak, av, az):
    b, s, w = aq.shape
    assert s % ATT_TILE == 0 and w % LANES == 0
    assert DILATIONS == (1, ATT_PERM, ATT_PERM * ATT_PERM)
    assert (ATT_TILE // ATT_PERM) % ATT_MERGE_ROWS == 0
    bias = _attn_bias()
    reg = ATT_TILE // ATT_PERM
    pair_spec = pl.BlockSpec((1, ATT_TILE, LANES), lambda bi, h, t: (bi, t, h))
    hist = pltpu.VMEM((ATT_BLOCK, LANES), F32)
    prm = pltpu.VMEM((2, ATT_PERM, reg, LANES), F32)
    stat = pltpu.VMEM((len(DILATIONS), ATT_TILE, LANES), F32)
    return pl.pallas_call(
        _attn_kernel,
        grid=(b, w // LANES, s // ATT_TILE),
        in_specs=[pl.BlockSpec(bias.shape, lambda bi, h, t: (0, 0, 0))] + [pair_spec] * 4,
        out_specs=pair_spec,
        out_shape=jax.ShapeDtypeStruct((b, s, w), BF16),
        scratch_shapes=[hist, hist, prm, prm, pltpu.VMEM((ATT_PERM, reg, LANES), F32),
                        stat, stat, stat, pltpu.VMEM((ATT_TILE, LANES), F32)],
        compiler_params=pltpu.CompilerParams(
            dimension_semantics=("arbitrary", "arbitrary", "arbitrary"),
            vmem_limit_bytes=VMEM_LIMIT),
        name="dilated_attention",
    )(bias, aq, ak, av, az)


HG_DIAG = 8
HG_LEVELS = (8, 16, 32)
SUBLANES = 8


def _hgrn_constants():
    c = HG_CHUNK
    t = np.arange(c)
    tri = (t[:, None] >= t[None, :]).astype(np.float32)
    first = (t // HG_DIAG) * HG_DIAG
    half = np.concatenate([tri, tri - 0.5 * (tri[first] + tri[first + HG_DIAG - 1])], axis=0)
    coef = np.concatenate([half, half], axis=1)
    ts, ss = t[:, None], t[None, :]
    level = np.full((c, c), len(HG_LEVELS) + 1, np.int32)
    level[(ts // HG_DIAG == ss // HG_DIAG) & (ts >= ss)] = 0
    for i, lb in enumerate(HG_LEVELS):
        m = ((ts // (2 * lb) == ss // (2 * lb)) & ((ts // lb) % 2 == 1)
             & ((ss // lb) % 2 == 0))
        level[m] = i + 1
    return jnp.asarray(coef, BF16), jnp.asarray(level)


def _hgrn_out_kernel(coef_ref, level_ref, x_ref, nw_ref, w_in32_ref, lbl_ref, gn_ref,
                     oat_ref, w32_ref, fw_ref, o_ref,
                     w_in_ref, w_ref, state_ref, qs_ref, g_ref, kk_ref, vi_ref, zs_ref,
                     qd_ref, oi_ref, kv_ref, dec_ref, ohg_ref):
    @pl.when(pl.program_id(1) == 0)
    def _():
        state_ref[...] = jnp.zeros_like(state_ref)
        w_in_ref[...] = w_in32_ref[0].astype(BF16)
        w_ref[...] = w32_ref[0].astype(BF16)

    lg = lbl_ref[...]
    ex = jnp.exp(lg - jnp.max(lg, axis=0, keepdims=True))
    lb = jnp.clip(ex[0:1, :] / jnp.sum(ex, axis=0, keepdims=True), 1e-6, 1.0 - 1e-6)
    f_mid = 0.5 * (1.0 + lb)
    f_amp = 0.5 * (1.0 - lb)

    def store_q(p):
        qs_ref[...] = _silu(p).astype(BF16)

    def store_f(p):
        f = f_mid + f_amp * jnp.tanh(0.5 * p)
        g2 = jnp.log2(f)
        hi_bits = lax.bitcast_convert_type(g2, jnp.uint32) & jnp.uint32(0xFFFF0000)
        g2_hi = lax.bitcast_convert_type(hi_bits, F32)
        g2_lo = (g2 - g2_hi).astype(BF16)
        g2_hi = g2_hi.astype(BF16)
        for h in range(HG_HEADS):
            g_ref[:, 2 * h * HG_DK:(2 * h + 1) * HG_DK] = g2_hi[:, h * HG_DK:(h + 1) * HG_DK]
            g_ref[:, (2 * h + 1) * HG_DK:(2 * h + 2) * HG_DK] = g2_lo[:, h * HG_DK:(h + 1) * HG_DK]
        kk_ref[...] = (1.0 - f).astype(BF16)

    def store_z(p):
        zs_ref[...] = _silu(p).astype(BF16)

    def store_i(p):
        vi_ref[...] = p.astype(BF16)

    u = _normalized(x_ref[0], nw_ref[...])

    def project(j):
        return jnp.dot(u, w_in_ref[:, j * SECTION:(j + 1) * SECTION],
                       preferred_element_type=F32)

    p_f = project(1)
    p_q = project(0)
    store_f(p_f)
    store_q(p_q)

    c = HG_CHUNK
    n_chunks = HG_ROWS // c
    half = HG_HEADS * HG_DK
    coef = coef_ref[...]
    level = level_ref[...]
    units = [(h, ci) for h in range(HG_HEADS) for ci in range(n_chunks)]

    def rows_of(ci):
        return pl.ds(ci * c, c)

    def cols_of(h, width=HG_DK):
        return pl.ds(h * width, width)

    def cum_decay(h, ci):
        g2 = g_ref[rows_of(ci), cols_of(h, 2 * HG_DK)]
        return jnp.dot(coef, jnp.concatenate([g2[:, :HG_DK], g2[:, HG_DK:]], axis=0),
                       preferred_element_type=F32)

    def level_operands(h, ci, be):
        q = qs_ref[rows_of(ci), cols_of(h)]
        k = kk_ref[rows_of(ci), cols_of(h)]
        scaled = lambda t, w: t * w.astype(BF16)
        b = be[0:c]
        e_diag = be[c:2 * c]
        ops = [(scaled(q, jnp.exp2(e_diag)), scaled(k, jnp.exp2(-e_diag)))]
        for lb_rows in HG_LEVELS:
            ref = jnp.concatenate(
                [jnp.broadcast_to(b[p + lb_rows - 1:p + lb_rows, :], (2 * lb_rows, HG_DK))
                 for p in range(0, c, 2 * lb_rows)], axis=0)
            w = jnp.exp2(_neg_abs(b - ref))
            ops.append((scaled(q, w), scaled(k, w)))
        b_last = b[c - 1:c, :]
        return ops, scaled(k, jnp.exp2(b_last - b)), scaled(q, jnp.exp2(b)), jnp.exp2(b_last)

    def intra_scores(ops):
        scores = jnp.where(level == 0, _dot_nt(*ops[0]), 0.0)
        for i in range(len(HG_LEVELS)):
            scores = jnp.where(level == i + 1, _dot_nt(*ops[i + 1]), scores)
        return scores.astype(BF16)

    stage1 = [cum_decay(h, ci) for h, ci in units]
    p_z = project(3)
    stage2 = [level_operands(h, ci, be) for (h, ci), be in zip(units, stage1)]
    p_i = project(2)
    store_z(p_z)
    store_i(p_i)

    stage3 = [intra_scores(ops) for ops, _, _, _ in stage2]
    o_ref[0] = x_ref[0] + jnp.dot(oat_ref[0], w_ref[half:2 * half, :],
                                  preferred_element_type=F32)
    for (h, ci), (_, k_dec, q_dec, decay), scores in zip(units, stage2, stage3):
        v = vi_ref[rows_of(ci), cols_of(h)]
        oi_ref[rows_of(ci), cols_of(h)] = jnp.dot(scores, v, preferred_element_type=F32)
        kv_ref[h, ci] = _dot_tn(v, k_dec)
        qd_ref[rows_of(ci), cols_of(h)] = q_dec
        dec_ref[h, ci] = jnp.broadcast_to(decay, dec_ref.shape[2:])

    states = [state_ref[h] for h in range(HG_HEADS)]
    for ci in range(n_chunks):
        for h in range(HG_HEADS):
            rows, cols = rows_of(ci), cols_of(h)
            o = oi_ref[rows, cols] + _dot_nt(qd_ref[rows, cols], states[h].astype(BF16))
            states[h] = states[h] * dec_ref[h, ci, 0:1, :] + kv_ref[h, ci]
            z = zs_ref[rows, cols].astype(F32)
            ms = jnp.mean(o * o, axis=-1, keepdims=True)
            ohg_ref[rows, cols] = (o * lax.rsqrt(ms + NORM_EPS) * gn_ref[:, cols] * z
                                   ).astype(BF16)
    for h in range(HG_HEADS):
        state_ref[h] = states[h]

    hres = o_ref[0] + jnp.dot(ohg_ref[...], w_ref[0:half, :], preferred_element_type=F32)
    ms = jnp.mean(hres * hres, axis=-1, keepdims=True)
    o_ref[0] = hres * lax.rsqrt(ms + NORM_EPS) * fw_ref[...]


def _hgrn_out(x, norm_w, w_in, lb_logits, hg_norm_w, oat, w_out, final_norm_w):
    b, s, d_model = x.shape
    w = HG_HEADS * HG_DK
    assert s % HG_ROWS == 0 and w == SECTION
    assert w_out.shape == (1, w + oat.shape[-1], d_model)
    coef, level = _hgrn_constants()
    n_chunks = HG_ROWS // HG_CHUNK
    tile = lambda width: pl.BlockSpec((1, HG_ROWS, width), lambda bi, t: (bi, t, 0))
    const = lambda a: pl.BlockSpec(a.shape, lambda bi, t: (0,) * a.ndim)
    act = lambda width: pltpu.VMEM((HG_ROWS, width), BF16)
    return pl.pallas_call(
        _hgrn_out_kernel,
        grid=(b, s // HG_ROWS),
        in_specs=[const(coef), const(level), tile(d_model), const(norm_w),
                  _branch_weight_spec(w_in, 0), const(lb_logits), const(hg_norm_w),
                  tile(oat.shape[-1]), const(w_out), const(final_norm_w)],
        out_specs=tile(d_model),
        out_shape=jax.ShapeDtypeStruct((b, s, d_model), F32),
        scratch_shapes=[pltpu.VMEM((d_model, 4 * SECTION), BF16),
                        pltpu.VMEM(w_out.shape[1:], BF16),
                        pltpu.VMEM((HG_HEADS, HG_DK, HG_DK), F32),
                        act(w), act(2 * w), act(w), act(w), act(w),
                        act(w),
                        pltpu.VMEM((HG_ROWS, w), F32),
                        pltpu.VMEM((HG_HEADS, n_chunks, HG_DK, HG_DK), F32),
                        pltpu.VMEM((HG_HEADS, n_chunks, SUBLANES, HG_DK), F32),
                        act(w)],
        compiler_params=pltpu.CompilerParams(
            dimension_semantics=("parallel", "arbitrary"),
            vmem_limit_bytes=VMEM_LIMIT),
        name="hgrn2_branch_and_output",
    )(coef, level, x, norm_w, w_in, lb_logits, hg_norm_w, oat, w_out, final_norm_w)


def kernel(x, norm_w, w_in, hgrn_lb_logits, hg_norm_w, w_out, final_norm_w):
    b, s, d_model = x.shape
    assert norm_w.shape[0] == 1 and w_in.shape[0] == 1 and w_out.shape[0] == 1
    aq, ak, av, az = _attn_projection(x.reshape(b * s, d_model), norm_w, w_in, s)
    to3 = lambda a: a.reshape(b, s, a.shape[-1])
    oat = _attention(to3(aq), to3(ak), to3(av), to3(az))
    return _hgrn_out(x, norm_w, w_in, hgrn_lb_logits, hg_norm_w, oat, w_out,
                     final_norm_w.reshape(1, d_model))
```

```python
import functools

import numpy as np
import jax
import jax.numpy as jnp
from jax import lax
from jax.experimental import pallas as pl
from jax.experimental.pallas import tpu as pltpu

F32 = jnp.float32
BF16 = jnp.bfloat16

NORM_EPS = 1e-6
ROPE_THETA = 10000.0
LANES = 128

HG_HEADS = 4
HG_DK = 128
HG_CHUNK = 64
AT_HEAD_DIM = 64
SECTION = 512
DILATIONS = (1, 4, 16)
ATT_BLOCK = 128
ATT_TILE = 2 * ATT_BLOCK * max(DILATIONS)
NEG_BIG = -1e30

PROJ_ROWS = 1024
HG_ROWS = 512
ATT_PERM = 4
ATT_MERGE_ROWS = 1024
LOG2E = 1.4426950408889634
VMEM_LIMIT = 56 * 1024 * 1024


def _normalized(x, gain):
    ms = jnp.mean(x * x, axis=-1, keepdims=True)
    return (x * lax.rsqrt(ms + NORM_EPS) * gain).astype(BF16)


def _silu(p):
    return p * (0.5 * jnp.tanh(0.5 * p) + 0.5)


def _project_sections(u, w_ref, plan):
    pending = None
    for j, finish in plan:
        p = jnp.dot(u, w_ref[:, j * SECTION:(j + 1) * SECTION], preferred_element_type=F32)
        if pending is not None:
            pending()
        pending = functools.partial(finish, p)
    pending()


def _dot_nt(a, b):
    return lax.dot_general(a, b, (((1,), (1,)), ((), ())), preferred_element_type=F32)


def _dot_tn(a, b):
    return lax.dot_general(a, b, (((0,), (0,)), ((), ())), preferred_element_type=F32)


def _neg_abs(x):
    bits = lax.bitcast_convert_type(x, jnp.uint32) | jnp.uint32(0x80000000)
    return lax.bitcast_convert_type(bits, F32)


def _attn_proj_kernel(x_ref, nw_ref, w32_ref, cos_ref, sina_ref, sinb_ref,
                      aq_ref, ak_ref, av_ref, az_ref, w_ref):
    @pl.when(pl.program_id(0) == 0)
    def _():
        w_ref[...] = w32_ref[0].astype(BF16)

    u = _normalized(x_ref[...], nw_ref[...])

    def store_rope(dst_ref, scale, p):
        half = AT_HEAD_DIM // 2
        for c in range(SECTION // LANES):
            xc = p[:, c * LANES:(c + 1) * LANES]
            r = (xc * cos_ref[...] + pltpu.roll(xc, LANES - half, 1) * sina_ref[...]
                 + pltpu.roll(xc, half, 1) * sinb_ref[...])
            dst_ref[:, c * LANES:(c + 1) * LANES] = r * scale

    def store_z(p):
        az_ref[...] = _silu(p).astype(BF16)

    def store_v(p):
        av_ref[...] = p

    _project_sections(u, w_ref, [
        (3, store_z),
        (0, functools.partial(store_rope, aq_ref, AT_HEAD_DIM ** -0.5 * LOG2E)),
        (1, functools.partial(store_rope, ak_ref, 1.0)),
        (2, store_v)])


@functools.lru_cache(maxsize=None)
def _rope_tables(seq):
    half = AT_HEAD_DIM // 2
    inv_freq = 1.0 / (ROPE_THETA ** (np.arange(half, dtype=np.float64) / half))
    ang = np.arange(seq, dtype=np.float64)[:, None] * inv_freq[None, :]
    cos = np.cos(ang)
    sin = np.sin(ang)
    zero = np.zeros_like(sin)
    reps = LANES // AT_HEAD_DIM
    cos_t = np.tile(np.concatenate([cos, cos], axis=1), (1, reps))
    sina_t = np.tile(np.concatenate([-sin, zero], axis=1), (1, reps))
    sinb_t = np.tile(np.concatenate([zero, sin], axis=1), (1, reps))
    return tuple(np.asarray(t, np.float32) for t in (cos_t, sina_t, sinb_t))


def _branch_weight_spec(w_in, branch):
    assert w_in.shape[0] == 1 and w_in.shape[2] == 8 * SECTION
    return pl.BlockSpec((1, w_in.shape[1], 4 * SECTION),
                        lambda *_: (0, 0, branch), pipeline_mode=pl.Buffered(1))


def _attn_projection(x2, norm_w, w_in, seq):
    rows, d_model = x2.shape
    assert rows % PROJ_ROWS == 0 and seq % PROJ_ROWS == 0
    tables = _rope_tables(seq)
    seq_tiles = seq // PROJ_ROWS
    row_spec = lambda w: pl.BlockSpec((PROJ_ROWS, w), lambda i: (i, 0))
    tab_spec = pl.BlockSpec((PROJ_ROWS, LANES), lambda i: (i % seq_tiles, 0))
    full = lambda a: pl.BlockSpec(a.shape, lambda i: (0, 0))
    out_dtypes = (F32, F32, F32, BF16)
    return pl.pallas_call(
        _attn_proj_kernel,
        grid=(rows // PROJ_ROWS,),
        in_specs=[row_spec(d_model), full(norm_w), _branch_weight_spec(w_in, 1)]
                 + [tab_spec] * 3,
        out_specs=[row_spec(SECTION)] * len(out_dtypes),
        out_shape=[jax.ShapeDtypeStruct((rows, SECTION), dt) for dt in out_dtypes],
        scratch_shapes=[pltpu.VMEM((d_model, 4 * SECTION), BF16)],
        compiler_params=pltpu.CompilerParams(
            dimension_semantics=("arbitrary",), vmem_limit_bytes=VMEM_LIMIT),
        name="attention_projection",
    )(x2, norm_w, w_in, *tables)


def _attn_bias():
    qi = np.arange(ATT_BLOCK)[:, None]
    kj = np.arange(2 * ATT_BLOCK)[None, :]
    dist = ATT_BLOCK + qi - kj
    valid = (dist >= 0) & (dist <= ATT_BLOCK)
    b0 = np.where(valid, 0.0, NEG_BIG)
    b1 = np.where(valid & (kj >= ATT_BLOCK), 0.0, NEG_BIG)
    return jnp.asarray(np.stack([b0, b1]), F32)


def _attn_kernel(bias_ref, aq_ref, ak_ref, av_ref, az_ref, o_ref,
                 khist, vhist, kprm, vprm, qprm, m_ref, l_ref, acc_ref, onat):
    ta = ATT_TILE
    blk = ATT_BLOCK
    nph = ATT_PERM
    reg = ta // nph
    tile = pl.program_id(2)
    cur = tile % 2
    prev = 1 - cur

    @pl.when((pl.program_id(0) == 0) & (pl.program_id(1) == 0) & (tile == 0))
    def _():
        khist[...] = jnp.zeros_like(khist)
        vhist[...] = jnp.zeros_like(vhist)
        kprm[1] = jnp.zeros((nph, reg, LANES), F32)
        vprm[1] = jnp.zeros((nph, reg, LANES), F32)

    for r in range(nph):
        kprm[cur, r] = ak_ref[0, pl.ds(r, reg, stride=nph), :]
        vprm[cur, r] = av_ref[0, pl.ds(r, reg, stride=nph), :]
        qprm[r] = aq_ref[0, pl.ds(r, reg, stride=nph), :]

    first_head = lax.broadcasted_iota(jnp.int32, (blk, LANES), 1) < AT_HEAD_DIM
    ones = jnp.ones((2 * blk, LANES), BF16)
    bias_any = bias_ref[0]
    bias_first = bias_ref[(tile == 0).astype(jnp.int32)]

    def block_stats(q, k, v, bias):
        k = k.astype(BF16)
        v = v.astype(BF16)
        q2 = jnp.concatenate([jnp.where(first_head, q, 0.0),
                              jnp.where(first_head, 0.0, q)], axis=0).astype(BF16)
        s = _dot_nt(q2, k) + jnp.concatenate([bias, bias], axis=0)
        m = jnp.max(s, axis=-1, keepdims=True)
        p = jnp.exp2(s - m).astype(BF16)
        pv = jnp.dot(p, jnp.concatenate([v, ones], axis=1), preferred_element_type=F32)
        return (jnp.where(first_head, m[:blk], m[blk:]),
                jnp.where(first_head, pv[:blk, LANES:], pv[blk:, LANES:]),
                jnp.where(first_head, pv[:blk, :LANES], pv[blk:, :LANES]))

    def store_stats(pi, rows, stats):
        m_ref[pi, rows, :], l_ref[pi, rows, :], acc_ref[pi, rows, :] = stats

    def window(first_ref, first_rows, ref, rows):
        return jnp.concatenate([first_ref[first_rows], ref[rows]], axis=0)

    for j in range(ta // blk):
        q_rows = pl.ds(j * blk, blk)
        if j == 0:
            k = window(khist, (slice(None),) * 2, ak_ref, (0, q_rows))
            v = window(vhist, (slice(None),) * 2, av_ref, (0, q_rows))
        else:
            k_rows = pl.ds((j - 1) * blk, 2 * blk)
            k, v = ak_ref[0, k_rows, :], av_ref[0, k_rows, :]
        store_stats(0, q_rows, block_stats(aq_ref[0, q_rows, :], k, v,
                                           bias_first if j == 0 else bias_any))

    khist[...] = ak_ref[0, ta - blk:ta, :]
    vhist[...] = av_ref[0, ta - blk:ta, :]

    for r in range(nph):
        for j in range(reg // blk):
            q_rows = pl.ds(j * blk, blk)
            if j == 0:
                last = pl.ds(reg - blk, blk)
                k = window(kprm, (prev, r, last), kprm, (cur, r, q_rows))
                v = window(vprm, (prev, r, last), vprm, (cur, r, q_rows))
            else:
                k_rows = pl.ds((j - 1) * blk, 2 * blk)
                k, v = kprm[cur, r, k_rows, :], vprm[cur, r, k_rows, :]
            stats = block_stats(qprm[r, q_rows, :], k, v, bias_first if j == 0 else bias_any)
            store_stats(1, pl.ds(r * reg + j * blk, blk), stats)
        sub = reg // nph
        for r16 in range(nph):
            for j in range(sub // blk):
                q_rows = pl.ds(r16 + nph * blk * j, blk, stride=nph)
                if j == 0:
                    tail = pl.ds(r16 + nph * (sub - blk), blk, stride=nph)
                    k = window(kprm, (prev, r, tail), kprm, (cur, r, q_rows))
                    v = window(vprm, (prev, r, tail), vprm, (cur, r, q_rows))
                else:
                    k_rows = pl.ds(r16 + nph * blk * (j - 1), 2 * blk, stride=nph)
                    k, v = kprm[cur, r, k_rows, :], vprm[cur, r, k_rows, :]
                stats = block_stats(qprm[r, q_rows, :], k, v,
                                    bias_first if j == 0 else bias_any)
                store_stats(2, pl.ds(r * reg + r16 + nph * blk * j, blk, stride=nph), stats)

    def merge(ci, carry):
        pieces = reg // ATT_MERGE_ROWS
        r = ci // pieces
        i0 = (ci % pieces) * ATT_MERGE_ROWS
        nat_rows = pl.ds(r + nph * i0, ATT_MERGE_ROWS, stride=nph)
        prm_rows = pl.ds(pl.multiple_of(ci * ATT_MERGE_ROWS, ATT_MERGE_ROWS), ATT_MERGE_ROWS)
        rows = (nat_rows, prm_rows, prm_rows)
        ms = [m_ref[pi, rw, :] for pi, rw in enumerate(rows)]
        m_all = functools.reduce(jnp.maximum, ms)
        ws = [jnp.exp2(m - m_all) for m in ms]
        l_all = sum(w * l_ref[pi, rw, :] for (pi, rw), w in zip(enumerate(rows), ws))
        acc = sum(w * acc_ref[pi, rw, :] for (pi, rw), w in zip(enumerate(rows), ws))
        onat[nat_rows, :] = acc / l_all
        return carry

    lax.fori_loop(0, ta // ATT_MERGE_ROWS, merge, 0)

    def gate(ci, carry):
        rows = pl.ds(pl.multiple_of(ci * ATT_MERGE_ROWS, ATT_MERGE_ROWS), ATT_MERGE_ROWS)
        o_ref[0, rows, :] = (onat[rows, :] * az_ref[0, rows, :].astype(F32)).astype(BF16)
        return carry

    lax.fori_loop(0, ta // ATT_MERGE_ROWS, gate, 0)


def _attention(aq, ak, av, az):
    b, s, w = aq.shape
    assert s % ATT_TILE == 0 and w % LANES == 0
    assert DILATIONS == (1, ATT_PERM, ATT_PERM * ATT_PERM)
    assert (ATT_TILE // ATT_PERM) % ATT_MERGE_ROWS == 0
    bias = _attn_bias()
    reg = ATT_TILE // ATT_PERM
    pair_spec = pl.BlockSpec((1, ATT_TILE, LANES), lambda bi, h, t: (bi, t, h))
    hist = pltpu.VMEM((ATT_BLOCK, LANES), F32)
    prm = pltpu.VMEM((2, ATT_PERM, reg, LANES), F32)
    stat = pltpu.VMEM((len(DILATIONS), ATT_TILE, LANES), F32)
    return pl.pallas_call(
        _attn_kernel,
        grid=(b, w // LANES, s // ATT_TILE),
        in_specs=[pl.BlockSpec(bias.shape, lambda bi, h, t: (0, 0, 0))] + [pair_spec] * 4,
        out_specs=pair_spec,
        out_shape=jax.ShapeDtypeStruct((b, s, w), BF16),
        scratch_shapes=[hist, hist, prm, prm, pltpu.VMEM((ATT_PERM, reg, LANES), F32),
                        stat, stat, stat, pltpu.VMEM((ATT_TILE, LANES), F32)],
        compiler_params=pltpu.CompilerParams(
            dimension_semantics=("arbitrary", "arbitrary", "arbitrary"),
            vmem_limit_bytes=VMEM_LIMIT),
        name="dilated_attention",
    )(bias, aq, ak, av, az)


HG_DIAG = 8
HG_LEVELS = (8, 16, 32)
SUBLANES = 8


def _hgrn_constants():
    c = HG_CHUNK
    t = np.arange(c)
    tri = (t[:, None] >= t[None, :]).astype(np.float32)
    first = (t // HG_DIAG) * HG_DIAG
    half = np.concatenate([tri, tri - 0.5 * (tri[first] + tri[first + HG_DIAG - 1])], axis=0)
    coef = np.concatenate([half, half], axis=1)
    ts, ss = t[:, None], t[None, :]
    level = np.full((c, c), len(HG_LEVELS) + 1, np.int32)
    level[(ts // HG_DIAG == ss // HG_DIAG) & (ts >= ss)] = 0
    for i, lb in enumerate(HG_LEVELS):
        m = ((ts // (2 * lb) == ss // (2 * lb)) & ((ts // lb) % 2 == 1)
             & ((ss // lb) % 2 == 0))
        level[m] = i + 1
    return jnp.asarray(coef, BF16), jnp.asarray(level)


def _hgrn_out_kernel(coef_ref, level_ref, x_ref, nw_ref, w_in32_ref, lbl_ref, gn_ref,
                     oat_ref, w32_ref, fw_ref, o_ref,
                     w_in_ref, w_ref, state_ref, qs_ref, g_ref, kk_ref, vi_ref, zs_ref,
                     qd_ref, oi_ref, kv_ref, dec_ref, ohg_ref):
    @pl.when(pl.program_id(1) == 0)
    def _():
        state_ref[...] = jnp.zeros_like(state_ref)
        w_in_ref[...] = w_in32_ref[0].astype(BF16)
        w_ref[...] = w32_ref[0].astype(BF16)

    lg = lbl_ref[...]
    ex = jnp.exp(lg - jnp.max(lg, axis=0, keepdims=True))
    lb = jnp.clip(ex[0:1, :] / jnp.sum(ex, axis=0, keepdims=True), 1e-6, 1.0 - 1e-6)
    f_mid = 0.5 * (1.0 + lb)
    f_amp = 0.5 * (1.0 - lb)

    def store_q(p):
        qs_ref[...] = _silu(p).astype(BF16)

    def store_f(p):
        f = f_mid + f_amp * jnp.tanh(0.5 * p)
        g2 = jnp.log2(f)
        hi_bits = lax.bitcast_convert_type(g2, jnp.uint32) & jnp.uint32(0xFFFF0000)
        g2_hi = lax.bitcast_convert_type(hi_bits, F32)
        g2_lo = (g2 - g2_hi).astype(BF16)
        g2_hi = g2_hi.astype(BF16)
        for h in range(HG_HEADS):
            g_ref[:, 2 * h * HG_DK:(2 * h + 1) * HG_DK] = g2_hi[:, h * HG_DK:(h + 1) * HG_DK]
            g_ref[:, (2 * h + 1) * HG_DK:(2 * h + 2) * HG_DK] = g2_lo[:, h * HG_DK:(h + 1) * HG_DK]
        kk_ref[...] = (1.0 - f).astype(BF16)

    def store_z(p):
        zs_ref[...] = _silu(p).astype(BF16)

    def store_i(p):
        vi_ref[...] = p.astype(BF16)

    u = _normalized(x_ref[0], nw_ref[...])

    def project(j):
        return jnp.dot(u, w_in_ref[:, j * SECTION:(j + 1) * SECTION],
                       preferred_element_type=F32)

    p_f = project(1)
    p_q = project(0)
    store_f(p_f)
    store_q(p_q)

    c = HG_CHUNK
    n_chunks = HG_ROWS // c
    half = HG_HEADS * HG_DK
    coef = coef_ref[...]
    level = level_ref[...]
    units = [(h, ci) for h in range(HG_HEADS) for ci in range(n_chunks)]

    def rows_of(ci):
        return pl.ds(ci * c, c)

    def cols_of(h, width=HG_DK):
        return pl.ds(h * width, width)

    def cum_decay(h, ci):
        g2 = g_ref[rows_of(ci), cols_of(h, 2 * HG_DK)]
        return jnp.dot(coef, jnp.concatenate([g2[:, :HG_DK], g2[:, HG_DK:]], axis=0),
                       preferred_element_type=F32)

    def level_operands(h, ci, be):
        q = qs_ref[rows_of(ci), cols_of(h)]
        k = kk_ref[rows_of(ci), cols_of(h)]
        scaled = lambda t, w: t * w.astype(BF16)
        b = be[0:c]
        e_diag = be[c:2 * c]
        ops = [(scaled(q, jnp.exp2(e_diag)), scaled(k, jnp.exp2(-e_diag)))]
        for lb_rows in HG_LEVELS:
            ref = jnp.concatenate(
                [jnp.broadcast_to(b[p + lb_rows - 1:p + lb_rows, :], (2 * lb_rows, HG_DK))
                 for p in range(0, c, 2 * lb_rows)], axis=0)
            w = jnp.exp2(_neg_abs(b - ref))
            ops.append((scaled(q, w), scaled(k, w)))
        b_last = b[c - 1:c, :]
        return ops, scaled(k, jnp.exp2(b_last - b)), scaled(q, jnp.exp2(b)), jnp.exp2(b_last)

    def intra_scores(ops):
        scores = jnp.where(level == 0, _dot_nt(*ops[0]), 0.0)
        for i in range(len(HG_LEVELS)):
            scores = jnp.where(level == i + 1, _dot_nt(*ops[i + 1]), scores)
        return scores.astype(BF16)

    stage1 = [cum_decay(h, ci) for h, ci in units]
    p_z = project(3)
    stage2 = [level_operands(h, ci, be) for (h, ci), be in zip(units, stage1)]
    p_i = project(2)
    o_ref[0] = x_ref[0] + jnp.dot(oat_ref[0], w_ref[half:2 * half, :],
                                  preferred_element_type=F32)
    store_i(p_i)

    stage3 = [intra_scores(ops) for ops, _, _, _ in stage2]
    for (h, ci), (_, k_dec, q_dec, decay), scores in zip(units, stage2, stage3):
        v = vi_ref[rows_of(ci), cols_of(h)]
        oi_ref[rows_of(ci), cols_of(h)] = jnp.dot(scores, v, preferred_element_type=F32)
        kv_ref[h, ci] = _dot_tn(v, k_dec)
        qd_ref[rows_of(ci), cols_of(h)] = q_dec
        dec_ref[h, ci] = jnp.broadcast_to(decay, dec_ref.shape[2:])

    store_z(p_z)

    states = [state_ref[h] for h in range(HG_HEADS)]
    for ci in range(n_chunks):
        for h in range(HG_HEADS):
            rows, cols = rows_of(ci), cols_of(h)
            o = oi_ref[rows, cols] + _dot_nt(qd_ref[rows, cols], states[h].astype(BF16))
            states[h] = states[h] * dec_ref[h, ci, 0:1, :] + kv_ref[h, ci]
            z = zs_ref[rows, cols].astype(F32)
            ms = jnp.mean(o * o, axis=-1, keepdims=True)
            ohg_ref[rows, cols] = (o * lax.rsqrt(ms + NORM_EPS) * gn_ref[:, cols] * z
                                   ).astype(BF16)
    for h in range(HG_HEADS):
        state_ref[h] = states[h]

    hres = o_ref[0] + jnp.dot(ohg_ref[...], w_ref[0:half, :], preferred_element_type=F32)
    ms = jnp.mean(hres * hres, axis=-1, keepdims=True)
    o_ref[0] = hres * lax.rsqrt(ms + NORM_EPS) * fw_ref[...]


def _hgrn_out(x, norm_w, w_in, lb_logits, hg_norm_w, oat, w_out, final_norm_w):
    b, s, d_model = x.shape
    w = HG_HEADS * HG_DK
    assert s % HG_ROWS == 0 and w == SECTION
    assert w_out.shape == (1, w + oat.shape[-1], d_model)
    coef, level = _hgrn_constants()
    n_chunks = HG_ROWS // HG_CHUNK
    tile = lambda width: pl.BlockSpec((1, HG_ROWS, width), lambda bi, t: (bi, t, 0))
    const = lambda a: pl.BlockSpec(a.shape, lambda bi, t: (0,) * a.ndim)
    act = lambda width: pltpu.VMEM((HG_ROWS, width), BF16)
    return pl.pallas_call(
        _hgrn_out_kernel,
        grid=(b, s // HG_ROWS),
        in_specs=[const(coef), const(level), tile(d_model), const(norm_w),
                  _branch_weight_spec(w_in, 0), const(lb_logits), const(hg_norm_w),
                  tile(oat.shape[-1]), const(w_out), const(final_norm_w)],
        out_specs=tile(d_model),
        out_shape=jax.ShapeDtypeStruct((b, s, d_model), F32),
        scratch_shapes=[pltpu.VMEM((d_model, 4 * SECTION), BF16),
                        pltpu.VMEM(w_out.shape[1:], BF16),
                        pltpu.VMEM((HG_HEADS, HG_DK, HG_DK), F32),
                        act(w), act(2 * w), act(w), act(w), act(w),
                        act(w),
                        pltpu.VMEM((HG_ROWS, w), F32),
                        pltpu.VMEM((HG_HEADS, n_chunks, HG_DK, HG_DK), F32),
                        pltpu.VMEM((HG_HEADS, n_chunks, SUBLANES, HG_DK), F32),
                        act(w)],
        compiler_params=pltpu.CompilerParams(
            dimension_semantics=("parallel", "arbitrary"),
            vmem_limit_bytes=VMEM_LIMIT),
        name="hgrn2_branch_and_output",
    )(coef, level, x, norm_w, w_in, lb_logits, hg_norm_w, oat, w_out, final_norm_w)


def kernel(x, norm_w, w_in, hgrn_lb_logits, hg_norm_w, w_out, final_norm_w):
    b, s, d_model = x.shape
    assert norm_w.shape[0] == 1 and w_in.shape[0] == 1 and w_out.shape[0] == 1
    aq, ak, av, az = _attn_projection(x.reshape(b * s, d_model), norm_w, w_in, s)
    to3 = lambda a: a.reshape(b, s, a.shape[-1])
    oat = _attention(to3(aq), to3(ak), to3(av), to3(az))
    return _hgrn_out(x, norm_w, w_in, hgrn_lb_logits, hg_norm_w, oat, w_out,
                     final_norm_w.reshape(1, d_model))
```

```python
import functools

import numpy as np
import jax
import jax.numpy as jnp
from jax import lax
from jax.experimental import pallas as pl
from jax.experimental.pallas import tpu as pltpu

F32 = jnp.float32
BF16 = jnp.bfloat16

NORM_EPS = 1e-6
ROPE_THETA = 10000.0
LANES = 128

HG_HEADS = 4
HG_DK = 128
HG_CHUNK = 64
AT_HEAD_DIM = 64
SECTION = 512
DILATIONS = (1, 4, 16)
ATT_BLOCK = 128
ATT_TILE = 2 * ATT_BLOCK * max(DILATIONS)
NEG_BIG = -1e30

PROJ_ROWS = 1024
HG_ROWS = 512
ATT_PERM = 4
ATT_MERGE_ROWS = 1024
LOG2E = 1.4426950408889634
VMEM_LIMIT = 56 * 1024 * 1024


def _normalized(x, gain):
    ms = jnp.mean(x * x, axis=-1, keepdims=True)
    return (x * lax.rsqrt(ms + NORM_EPS) * gain).astype(BF16)


def _silu(p):
    return p * (0.5 * jnp.tanh(0.5 * p) + 0.5)


def _project_sections(u, w_ref, plan):
    pending = None
    for j, finish in plan:
        p = jnp.dot(u, w_ref[:, j * SECTION:(j + 1) * SECTION], preferred_element_type=F32)
        if pending is not None:
            pending()
        pending = functools.partial(finish, p)
    pending()


def _dot_nt(a, b):
    return lax.dot_general(a, b, (((1,), (1,)), ((), ())), preferred_element_type=F32)


def _dot_tn(a, b):
    return lax.dot_general(a, b, (((0,), (0,)), ((), ())), preferred_element_type=F32)


def _neg_abs(x):
    bits = lax.bitcast_convert_type(x, jnp.uint32) | jnp.uint32(0x80000000)
    return lax.bitcast_convert_type(bits, F32)


def _attn_proj_kernel(x_ref, nw_ref, w32_ref, cos_ref, sina_ref, sinb_ref,
                      aq_ref, ak_ref, av_ref, az_ref, w_ref):
    @pl.when(pl.program_id(0) == 0)
    def _():
        w_ref[...] = w32_ref[0].astype(BF16)

    u = _normalized(x_ref[...], nw_ref[...])

    def store_rope(dst_ref, scale, p):
        half = AT_HEAD_DIM // 2
        for c in range(SECTION // LANES):
            xc = p[:, c * LANES:(c + 1) * LANES]
            r = (xc * cos_ref[...] + pltpu.roll(xc, LANES - half, 1) * sina_ref[...]
                 + pltpu.roll(xc, half, 1) * sinb_ref[...])
            dst_ref[:, c * LANES:(c + 1) * LANES] = r * scale

    def store_z(p):
        az_ref[...] = _silu(p).astype(BF16)

    def store_v(p):
        av_ref[...] = p

    _project_sections(u, w_ref, [
        (3, store_z),
        (0, functools.partial(store_rope, aq_ref, AT_HEAD_DIM ** -0.5 * LOG2E)),
        (1, functools.partial(store_rope, ak_ref, 1.0)),
        (2, store_v)])


@functools.lru_cache(maxsize=None)
def _rope_tables(seq):
    half = AT_HEAD_DIM // 2
    inv_freq = 1.0 / (ROPE_THETA ** (np.arange(half, dtype=np.float64) / half))
    ang = np.arange(seq, dtype=np.float64)[:, None] * inv_freq[None, :]
    cos = np.cos(ang)
    sin = np.sin(ang)
    zero = np.zeros_like(sin)
    reps = LANES // AT_HEAD_DIM
    cos_t = np.tile(np.concatenate([cos, cos], axis=1), (1, reps))
    sina_t = np.tile(np.concatenate([-sin, zero], axis=1), (1, reps))
    sinb_t = np.tile(np.concatenate([zero, sin], axis=1), (1, reps))
    return tuple(np.asarray(t, np.float32) for t in (cos_t, sina_t, sinb_t))


def _branch_weight_spec(w_in, branch):
    assert w_in.shape[0] == 1 and w_in.shape[2] == 8 * SECTION
    return pl.BlockSpec((1, w_in.shape[1], 4 * SECTION),
                        lambda *_: (0, 0, branch), pipeline_mode=pl.Buffered(1))


def _attn_projection(x2, norm_w, w_in, seq):
    rows, d_model = x2.shape
    assert rows % PROJ_ROWS == 0 and seq % PROJ_ROWS == 0
    tables = _rope_tables(seq)
    seq_tiles = seq // PROJ_ROWS
    row_spec = lambda w: pl.BlockSpec((PROJ_ROWS, w), lambda i: (i, 0))
    tab_spec = pl.BlockSpec((PROJ_ROWS, LANES), lambda i: (i % seq_tiles, 0))
    full = lambda a: pl.BlockSpec(a.shape, lambda i: (0, 0))
    out_dtypes = (F32, F32, F32, BF16)
    return pl.pallas_call(
        _attn_proj_kernel,
        grid=(rows // PROJ_ROWS,),
        in_specs=[row_spec(d_model), full(norm_w), _branch_weight_spec(w_in, 1)]
                 + [tab_spec] * 3,
        out_specs=[row_spec(SECTION)] * len(out_dtypes),
        out_shape=[jax.ShapeDtypeStruct((rows, SECTION), dt) for dt in out_dtypes],
        scratch_shapes=[pltpu.VMEM((d_model, 4 * SECTION), BF16)],
        compiler_params=pltpu.CompilerParams(
            dimension_semantics=("arbitrary",), vmem_limit_bytes=VMEM_LIMIT),
        name="attention_projection",
    )(x2, norm_w, w_in, *tables)


def _attn_bias():
    qi = np.arange(ATT_BLOCK)[:, None]
    kj = np.arange(2 * ATT_BLOCK)[None, :]
    dist = ATT_BLOCK + qi - kj
    valid = (dist >= 0) & (dist <= ATT_BLOCK)
    b0 = np.where(valid, 0.0, NEG_BIG)
    b1 = np.where(valid & (kj >= ATT_BLOCK), 0.0, NEG_BIG)
    return jnp.asarray(np.stack([b0, b1]), F32)


def _attn_kernel(bias_ref, aq_ref, ak_ref, av_ref, az_ref, o_ref,
                 khist, vhist, kprm, vprm, qprm, m_ref, l_ref, acc_ref, onat):
    ta = ATT_TILE
    blk = ATT_BLOCK
    nph = ATT_PERM
    reg = ta // nph
    tile = pl.program_id(2)
    cur = tile % 2
    prev = 1 - cur

    @pl.when((pl.program_id(0) == 0) & (pl.program_id(1) == 0) & (tile == 0))
    def _():
        khist[...] = jnp.zeros_like(khist)
        vhist[...] = jnp.zeros_like(vhist)
        kprm[1] = jnp.zeros((nph, reg, LANES), F32)
        vprm[1] = jnp.zeros((nph, reg, LANES), F32)

    for r in range(nph):
        kprm[cur, r] = ak_ref[0, pl.ds(r, reg, stride=nph), :]
        vprm[cur, r] = av_ref[0, pl.ds(r, reg, stride=nph), :]
        qprm[r] = aq_ref[0, pl.ds(r, reg, stride=nph), :]

    first_head = lax.broadcasted_iota(jnp.int32, (blk, LANES), 1) < AT_HEAD_DIM
    ones = jnp.ones((2 * blk, LANES), BF16)
    bias_any = bias_ref[0]
    bias_first = bias_ref[(tile == 0).astype(jnp.int32)]

    def block_stats(q, k, v, bias):
        k = k.astype(BF16)
        v = v.astype(BF16)
        q2 = jnp.concatenate([jnp.where(first_head, q, 0.0),
                              jnp.where(first_head, 0.0, q)], axis=0).astype(BF16)
        s = _dot_nt(q2, k) + jnp.concatenate([bias, bias], axis=0)
        m = jnp.max(s, axis=-1, keepdims=True)
        p = jnp.exp2(s - m).astype(BF16)
        pv = jnp.dot(p, jnp.concatenate([v, ones], axis=1), preferred_element_type=F32)
        return (jnp.where(first_head, m[:blk], m[blk:]),
                jnp.where(first_head, pv[:blk, LANES:], pv[blk:, LANES:]),
                jnp.where(first_head, pv[:blk, :LANES], pv[blk:, :LANES]))

    def store_stats(pi, rows, stats):
        m_ref[pi, rows, :], l_ref[pi, rows, :], acc_ref[pi, rows, :] = stats

    def window(first_ref, first_rows, ref, rows):
        return jnp.concatenate([first_ref[first_rows], ref[rows]], axis=0)

    for j in range(ta // blk):
        q_rows = pl.ds(j * blk, blk)
        if j == 0:
            k = window(khist, (slice(None),) * 2, ak_ref, (0, q_rows))
            v = window(vhist, (slice(None),) * 2, av_ref, (0, q_rows))
        else:
            k_rows = pl.ds((j - 1) * blk, 2 * blk)
            k, v = ak_ref[0, k_rows, :], av_ref[0, k_rows, :]
        store_stats(0, q_rows, block_stats(aq_ref[0, q_rows, :], k, v,
                                           bias_first if j == 0 else bias_any))

    khist[...] = ak_ref[0, ta - blk:ta, :]
    vhist[...] = av_ref[0, ta - blk:ta, :]

    for r in range(nph):
        for j in range(reg // blk):
            q_rows = pl.ds(j * blk, blk)
            if j == 0:
                last = pl.ds(reg - blk, blk)
                k = window(kprm, (prev, r, last), kprm, (cur, r, q_rows))
                v = window(vprm, (prev, r, last), vprm, (cur, r, q_rows))
            else:
                k_rows = pl.ds((j - 1) * blk, 2 * blk)
                k, v = kprm[cur, r, k_rows, :], vprm[cur, r, k_rows, :]
            stats = block_stats(qprm[r, q_rows, :], k, v, bias_first if j == 0 else bias_any)
            store_stats(1, pl.ds(r * reg + j * blk, blk), stats)
        sub = reg // nph
        for r16 in range(nph):
            for j in range(sub // blk):
                q_rows = pl.ds(r16 + nph * blk * j, blk, stride=nph)
                if j == 0:
                    tail = pl.ds(r16 + nph * (sub - blk), blk, stride=nph)
                    k = window(kprm, (prev, r, tail), kprm, (cur, r, q_rows))
                    v = window(vprm, (prev, r, tail), vprm, (cur, r, q_rows))
                else:
                    k_rows = pl.ds(r16 + nph * blk * (j - 1), 2 * blk, stride=nph)
                    k, v = kprm[cur, r, k_rows, :], vprm[cur, r, k_rows, :]
                stats = block_stats(qprm[r, q_rows, :], k, v,
                                    bias_first if j == 0 else bias_any)
                store_stats(2, pl.ds(r * reg + r16 + nph * blk * j, blk, stride=nph), stats)

    def merge(ci, carry):
        pieces = reg // ATT_MERGE_ROWS
        r = ci // pieces
        i0 = (ci % pieces) * ATT_MERGE_ROWS
        nat_rows = pl.ds(r + nph * i0, ATT_MERGE_ROWS, stride=nph)
        prm_rows = pl.ds(pl.multiple_of(ci * ATT_MERGE_ROWS, ATT_MERGE_ROWS), ATT_MERGE_ROWS)
        rows = (nat_rows, prm_rows, prm_rows)
        ms = [m_ref[pi, rw, :] for pi, rw in enumerate(rows)]
        m_all = functools.reduce(jnp.maximum, ms)
        ws = [jnp.exp2(m - m_all) for m in ms]
        l_all = sum(w * l_ref[pi, rw, :] for (pi, rw), w in zip(enumerate(rows), ws))
        acc = sum(w * acc_ref[pi, rw, :] for (pi, rw), w in zip(enumerate(rows), ws))
        onat[nat_rows, :] = acc / l_all
        return carry

    lax.fori_loop(0, ta // ATT_MERGE_ROWS, merge, 0)

    def gate(ci, carry):
        rows = pl.ds(pl.multiple_of(ci * ATT_MERGE_ROWS, ATT_MERGE_ROWS), ATT_MERGE_ROWS)
        o_ref[0, rows, :] = (onat[rows, :] * az_ref[0, rows, :].astype(F32)).astype(BF16)
        return carry

    lax.fori_loop(0, ta // ATT_MERGE_ROWS, gate, 0)


def _attention(aq, ak, av, az):
    b, s, w = aq.shape
    assert s % ATT_TILE == 0 and w % LANES == 0
    assert DILATIONS == (1, ATT_PERM, ATT_PERM * ATT_PERM)
    assert (ATT_TILE // ATT_PERM) % ATT_MERGE_ROWS == 0
    bias = _attn_bias()
    reg = ATT_TILE // ATT_PERM
    pair_spec = pl.BlockSpec((1, ATT_TILE, LANES), lambda bi, h, t: (bi, t, h))
    hist = pltpu.VMEM((ATT_BLOCK, LANES), F32)
    prm = pltpu.VMEM((2, ATT_PERM, reg, LANES), F32)
    stat = pltpu.VMEM((len(DILATIONS), ATT_TILE, LANES), F32)
    return pl.pallas_call(
        _attn_kernel,
        grid=(b, w // LANES, s // ATT_TILE),
        in_specs=[pl.BlockSpec(bias.shape, lambda bi, h, t: (0, 0, 0))] + [pair_spec] * 4,
        out_specs=pair_spec,
        out_shape=jax.ShapeDtypeStruct((b, s, w), BF16),
        scratch_shapes=[hist, hist, prm, prm, pltpu.VMEM((ATT_PERM, reg, LANES), F32),
                        stat, stat, stat, pltpu.VMEM((ATT_TILE, LANES), F32)],
        compiler_params=pltpu.CompilerParams(
            dimension_semantics=("arbitrary", "arbitrary", "arbitrary"),
            vmem_limit_bytes=VMEM_LIMIT),
        name="dilated_attention",
    )(bias, aq, ak, av, az)


HG_DIAG = 8
HG_LEVELS = (8, 16, 32)
SUBLANES = 8


def _hgrn_constants():
    c = HG_CHUNK
    t = np.arange(c)
    tri = (t[:, None] >= t[None, :]).astype(np.float32)
    first = (t // HG_DIAG) * HG_DIAG
    half = np.concatenate([tri, tri - 0.5 * (tri[first] + tri[first + HG_DIAG - 1])], axis=0)
    coef = np.concatenate([half, half], axis=1)
    ts, ss = t[:, None], t[None, :]
    level = np.full((c, c), len(HG_LEVELS) + 1, np.int32)
    level[(ts // HG_DIAG == ss // HG_DIAG) & (ts >= ss)] = 0
    for i, lb in enumerate(HG_LEVELS):
        m = ((ts // (2 * lb) == ss // (2 * lb)) & ((ts // lb) % 2 == 1)
             & ((ss // lb) % 2 == 0))
        level[m] = i + 1
    return jnp.asarray(coef, BF16), jnp.asarray(level)


def _hgrn_out_kernel(coef_ref, level_ref, x_ref, nw_ref, w_in32_ref, lbl_ref, gn_ref,
                     oat_ref, w32_ref, fw_ref, o_ref,
                     w_in_ref, w_ref, state_ref, qs_ref, g_ref, kk_ref, vi_ref, zs_ref,
                     qd_ref, oi_ref, kv_ref, dec_ref, ohg_ref):
    @pl.when(pl.program_id(1) == 0)
    def _():
        state_ref[...] = jnp.zeros_like(state_ref)
        w_in_ref[...] = w_in32_ref[0].astype(BF16)
        w_ref[...] = w32_ref[0].astype(BF16)

    lg = lbl_ref[...]
    ex = jnp.exp(lg - jnp.max(lg, axis=0, keepdims=True))
    lb = jnp.clip(ex[0:1, :] / jnp.sum(ex, axis=0, keepdims=True), 1e-6, 1.0 - 1e-6)
    f_mid = 0.5 * (1.0 + lb)
    f_amp = 0.5 * (1.0 - lb)

    def store_q(p):
        qs_ref[...] = _silu(p).astype(BF16)

    def store_f(p):
        f = f_mid + f_amp * jnp.tanh(0.5 * p)
        g2 = jnp.log2(f)
        hi_bits = lax.bitcast_convert_type(g2, jnp.uint32) & jnp.uint32(0xFFFF0000)
        g2_hi = lax.bitcast_convert_type(hi_bits, F32)
        g2_lo = (g2 - g2_hi).astype(BF16)
        g2_hi = g2_hi.astype(BF16)
        for h in range(HG_HEADS):
            g_ref[:, 2 * h * HG_DK:(2 * h + 1) * HG_DK] = g2_hi[:, h * HG_DK:(h + 1) * HG_DK]
            g_ref[:, (2 * h + 1) * HG_DK:(2 * h + 2) * HG_DK] = g2_lo[:, h * HG_DK:(h + 1) * HG_DK]
        kk_ref[...] = (1.0 - f).astype(BF16)

    def store_z(p):
        zs_ref[...] = _silu(p).astype(BF16)

    def store_i(p):
        vi_ref[...] = p.astype(BF16)

    u = _normalized(x_ref[0], nw_ref[...])

    def project(j):
        return jnp.dot(u, w_in_ref[:, j * SECTION:(j + 1) * SECTION],
                       preferred_element_type=F32)

    p_f = project(1)
    p_q = project(0)
    store_f(p_f)
    store_q(p_q)

    c = HG_CHUNK
    n_chunks = HG_ROWS // c
    half = HG_HEADS * HG_DK
    coef = coef_ref[...]
    level = level_ref[...]
    units = [(h, ci) for h in range(HG_HEADS) for ci in range(n_chunks)]

    def rows_of(ci):
        return pl.ds(ci * c, c)

    def cols_of(h, width=HG_DK):
        return pl.ds(h * width, width)

    def cum_decay(h, ci):
        g2 = g_ref[rows_of(ci), cols_of(h, 2 * HG_DK)]
        return jnp.dot(coef, jnp.concatenate([g2[:, :HG_DK], g2[:, HG_DK:]], axis=0),
                       preferred_element_type=F32)

    def level_operands(h, ci, be):
        q = qs_ref[rows_of(ci), cols_of(h)]
        k = kk_ref[rows_of(ci), cols_of(h)]
        scaled = lambda t, w: t * w.astype(BF16)
        b = be[0:c]
        e_diag = be[c:2 * c]
        ops = [(scaled(q, jnp.exp2(e_diag)), scaled(k, jnp.exp2(-e_diag)))]
        for lb_rows in HG_LEVELS:
            ref = jnp.concatenate(
                [jnp.broadcast_to(b[p + lb_rows - 1:p + lb_rows, :], (2 * lb_rows, HG_DK))
                 for p in range(0, c, 2 * lb_rows)], axis=0)
            w = jnp.exp2(_neg_abs(b - ref))
            ops.append((scaled(q, w), scaled(k, w)))
        b_last = b[c - 1:c, :]
        return ops, scaled(k, jnp.exp2(b_last - b)), scaled(q, jnp.exp2(b)), jnp.exp2(b_last)

    def intra_scores(ops):
        scores = jnp.where(level == 0, _dot_nt(*ops[0]), 0.0)
        for i in range(len(HG_LEVELS)):
            scores = jnp.where(level == i + 1, _dot_nt(*ops[i + 1]), scores)
        return scores.astype(BF16)

    stage1 = [cum_decay(h, ci) for h, ci in units]
    p_z = project(3)
    p_i = project(2)
    stage2 = [level_operands(h, ci, be) for (h, ci), be in zip(units, stage1)]
    store_z(p_z)
    o_ref[0] = x_ref[0] + jnp.dot(oat_ref[0], w_ref[half:2 * half, :],
                                  preferred_element_type=F32)
    store_i(p_i)

    stage3 = [intra_scores(ops) for ops, _, _, _ in stage2]
    for (h, ci), (_, k_dec, q_dec, decay), scores in zip(units, stage2, stage3):
        v = vi_ref[rows_of(ci), cols_of(h)]
        oi_ref[rows_of(ci), cols_of(h)] = jnp.dot(scores, v, preferred_element_type=F32)
        kv_ref[h, ci] = _dot_tn(v, k_dec)
        qd_ref[rows_of(ci), cols_of(h)] = q_dec
        dec_ref[h, ci] = jnp.broadcast_to(decay, dec_ref.shape[2:])

    states = [state_ref[h] for h in range(HG_HEADS)]
    for ci in range(n_chunks):
        for h in range(HG_HEADS):
            rows, cols = rows_of(ci), cols_of(h)
            o = oi_ref[rows, cols] + _dot_nt(qd_ref[rows, cols], states[h].astype(BF16))
            states[h] = states[h] * dec_ref[h, ci, 0:1, :] + kv_ref[h, ci]
            z = zs_ref[rows, cols].astype(F32)
            ms = jnp.mean(o * o, axis=-1, keepdims=True)
            ohg_ref[rows, cols] = (o * lax.rsqrt(ms + NORM_EPS) * gn_ref[:, cols] * z
                                   ).astype(BF16)
    for h in range(HG_HEADS):
        state_ref[h] = states[h]

    hres = o_ref[0] + jnp.dot(ohg_ref[...], w_ref[0:half, :], preferred_element_type=F32)
    ms = jnp.mean(hres * hres, axis=-1, keepdims=True)
    o_ref[0] = hres * lax.rsqrt(ms + NORM_EPS) * fw_ref[...]


def _hgrn_out(x, norm_w, w_in, lb_logits, hg_norm_w, oat, w_out, final_norm_w):
    b, s, d_model = x.shape
    w = HG_HEADS * HG_DK
    assert s % HG_ROWS == 0 and w == SECTION
    assert w_out.shape == (1, w + oat.shape[-1], d_model)
    coef, level = _hgrn_constants()
    n_chunks = HG_ROWS // HG_CHUNK
    tile = lambda width: pl.BlockSpec((1, HG_ROWS, width), lambda bi, t: (bi, t, 0))
    const = lambda a: pl.BlockSpec(a.shape, lambda bi, t: (0,) * a.ndim)
    act = lambda width: pltpu.VMEM((HG_ROWS, width), BF16)
    return pl.pallas_call(
        _hgrn_out_kernel,
        grid=(b, s // HG_ROWS),
        in_specs=[const(coef), const(level), tile(d_model), const(norm_w),
                  _branch_weight_spec(w_in, 0), const(lb_logits), const(hg_norm_w),
                  tile(oat.shape[-1]), const(w_out), const(final_norm_w)],
        out_specs=tile(d_model),
        out_shape=jax.ShapeDtypeStruct((b, s, d_model), F32),
        scratch_shapes=[pltpu.VMEM((d_model, 4 * SECTION), BF16),
                        pltpu.VMEM(w_out.shape[1:], BF16),
                        pltpu.VMEM((HG_HEADS, HG_DK, HG_DK), F32),
                        act(w), act(2 * w), act(w), act(w), act(w),
                        act(w),
                        pltpu.VMEM((HG_ROWS, w), F32),
                        pltpu.VMEM((HG_HEADS, n_chunks, HG_DK, HG_DK), F32),
                        pltpu.VMEM((HG_HEADS, n_chunks, SUBLANES, HG_DK), F32),
                        act(w)],
        compiler_params=pltpu.CompilerParams(
            dimension_semantics=("parallel", "arbitrary"),
            vmem_limit_bytes=VMEM_LIMIT),
        name="hgrn2_branch_and_output",
    )(coef, level, x, norm_w, w_in, lb_logits, hg_norm_w, oat, w_out, final_norm_w)


def kernel(x, norm_w, w_in, hgrn_lb_logits, hg_norm_w, w_out, final_norm_w):
    b, s, d_model = x.shape
    assert norm_w.shape[0] == 1 and w_in.shape[0] == 1 and w_out.shape[0] == 1
    aq, ak, av, az = _attn_projection(x.reshape(b * s, d_model), norm_w, w_in, s)
    to3 = lambda a: a.reshape(b, s, a.shape[-1])
    oat = _attention(to3(aq), to3(ak), to3(av), to3(az))
    return _hgrn_out(x, norm_w, w_in, hgrn_lb_logits, hg_norm_w, oat, w_out,
                     final_norm_w.reshape(1, d_model))
```

```python
import functools

import numpy as np
import jax
import jax.numpy as jnp
from jax import lax
from jax.experimental import pallas as pl
from jax.experimental.pallas import tpu as pltpu

F32 = jnp.float32
BF16 = jnp.bfloat16

NORM_EPS = 1e-6
ROPE_THETA = 10000.0
LANES = 128

HG_HEADS = 4
HG_DK = 128
HG_CHUNK = 64
AT_HEAD_DIM = 64
SECTION = 512
DILATIONS = (1, 4, 16)
ATT_BLOCK = 128
ATT_TILE = 2 * ATT_BLOCK * max(DILATIONS)
NEG_BIG = -1e30

PROJ_ROWS = 1024
HG_ROWS = 512
ATT_PERM = 4
ATT_MERGE_ROWS = 1024
LOG2E = 1.4426950408889634
VMEM_LIMIT = 56 * 1024 * 1024


def _normalized(x, gain):
    ms = jnp.mean(x * x, axis=-1, keepdims=True)
    return (x * lax.rsqrt(ms + NORM_EPS) * gain).astype(BF16)


def _silu(p):
    return p * (0.5 * jnp.tanh(0.5 * p) + 0.5)


def _project_sections(u, w_ref, plan):
    pending = None
    for j, finish in plan:
        p = jnp.dot(u, w_ref[:, j * SECTION:(j + 1) * SECTION], preferred_element_type=F32)
        if pending is not None:
            pending()
        pending = functools.partial(finish, p)
    pending()


def _dot_nt(a, b):
    return lax.dot_general(a, b, (((1,), (1,)), ((), ())), preferred_element_type=F32)


def _dot_tn(a, b):
    return lax.dot_general(a, b, (((0,), (0,)), ((), ())), preferred_element_type=F32)


def _neg_abs(x):
    bits = lax.bitcast_convert_type(x, jnp.uint32) | jnp.uint32(0x80000000)
    return lax.bitcast_convert_type(bits, F32)


def _attn_proj_kernel(x_ref, nw_ref, w32_ref, cos_ref, sina_ref, sinb_ref,
                      aq_ref, ak_ref, av_ref, az_ref, w_ref):
    @pl.when(pl.program_id(0) == 0)
    def _():
        w_ref[...] = w32_ref[0].astype(BF16)

    u = _normalized(x_ref[...], nw_ref[...])

    def store_rope(dst_ref, scale, p):
        half = AT_HEAD_DIM // 2
        for c in range(SECTION // LANES):
            xc = p[:, c * LANES:(c + 1) * LANES]
            r = (xc * cos_ref[...] + pltpu.roll(xc, LANES - half, 1) * sina_ref[...]
                 + pltpu.roll(xc, half, 1) * sinb_ref[...])
            dst_ref[:, c * LANES:(c + 1) * LANES] = r * scale

    def store_z(p):
        az_ref[...] = _silu(p).astype(BF16)

    def store_v(p):
        av_ref[...] = p

    _project_sections(u, w_ref, [
        (3, store_z),
        (0, functools.partial(store_rope, aq_ref, AT_HEAD_DIM ** -0.5 * LOG2E)),
        (1, functools.partial(store_rope, ak_ref, 1.0)),
        (2, store_v)])


@functools.lru_cache(maxsize=None)
def _rope_tables(seq):
    half = AT_HEAD_DIM // 2
    inv_freq = 1.0 / (ROPE_THETA ** (np.arange(half, dtype=np.float64) / half))
    ang = np.arange(seq, dtype=np.float64)[:, None] * inv_freq[None, :]
    cos = np.cos(ang)
    sin = np.sin(ang)
    zero = np.zeros_like(sin)
    reps = LANES // AT_HEAD_DIM
    cos_t = np.tile(np.concatenate([cos, cos], axis=1), (1, reps))
    sina_t = np.tile(np.concatenate([-sin, zero], axis=1), (1, reps))
    sinb_t = np.tile(np.concatenate([zero, sin], axis=1), (1, reps))
    return tuple(np.asarray(t, np.float32) for t in (cos_t, sina_t, sinb_t))


def _branch_weight_spec(w_in, branch):
    assert w_in.shape[0] == 1 and w_in.shape[2] == 8 * SECTION
    return pl.BlockSpec((1, w_in.shape[1], 4 * SECTION),
                        lambda *_: (0, 0, branch), pipeline_mode=pl.Buffered(1))


def _attn_projection(x2, norm_w, w_in, seq):
    rows, d_model = x2.shape
    assert rows % PROJ_ROWS == 0 and seq % PROJ_ROWS == 0
    tables = _rope_tables(seq)
    seq_tiles = seq // PROJ_ROWS
    row_spec = lambda w: pl.BlockSpec((PROJ_ROWS, w), lambda i: (i, 0))
    tab_spec = pl.BlockSpec((PROJ_ROWS, LANES), lambda i: (i % seq_tiles, 0))
    full = lambda a: pl.BlockSpec(a.shape, lambda i: (0, 0))
    out_dtypes = (F32, F32, F32, BF16)
    return pl.pallas_call(
        _attn_proj_kernel,
        grid=(rows // PROJ_ROWS,),
        in_specs=[row_spec(d_model), full(norm_w), _branch_weight_spec(w_in, 1)]
                 + [tab_spec] * 3,
        out_specs=[row_spec(SECTION)] * len(out_dtypes),
        out_shape=[jax.ShapeDtypeStruct((rows, SECTION), dt) for dt in out_dtypes],
        scratch_shapes=[pltpu.VMEM((d_model, 4 * SECTION), BF16)],
        compiler_params=pltpu.CompilerParams(
            dimension_semantics=("arbitrary",), vmem_limit_bytes=VMEM_LIMIT),
        name="attention_projection",
    )(x2, norm_w, w_in, *tables)


def _attn_mask_operands():
    qi = np.arange(ATT_BLOCK)[None, :]
    kj = np.arange(2 * ATT_BLOCK)[:, None]
    dist = ATT_BLOCK + qi - kj
    valid = (dist >= 0) & (dist <= ATT_BLOCK)
    m0 = np.where(valid, 0.0, NEG_BIG)
    m1 = np.where(valid & (kj >= ATT_BLOCK), 0.0, NEG_BIG)
    eye = np.tile(np.eye(ATT_BLOCK), (2, 1))
    return jnp.asarray(np.stack([m0, m1]), BF16), jnp.asarray(eye, BF16)


def _attn_kernel(mask_ref, eye_ref, aq_ref, ak_ref, av_ref, az_ref, o_ref,
                 khist, vhist, kprm, vprm, qprm, m_ref, l_ref, acc_ref, onat):
    ta = ATT_TILE
    blk = ATT_BLOCK
    nph = ATT_PERM
    reg = ta // nph
    tile = pl.program_id(2)
    cur = tile % 2
    prev = 1 - cur

    @pl.when((pl.program_id(0) == 0) & (pl.program_id(1) == 0) & (tile == 0))
    def _():
        khist[...] = jnp.zeros_like(khist)
        vhist[...] = jnp.zeros_like(vhist)
        kprm[1] = jnp.zeros((nph, reg, LANES), F32)
        vprm[1] = jnp.zeros((nph, reg, LANES), F32)

    for r in range(nph):
        kprm[cur, r] = ak_ref[0, pl.ds(r, reg, stride=nph), :]
        vprm[cur, r] = av_ref[0, pl.ds(r, reg, stride=nph), :]
        qprm[r] = aq_ref[0, pl.ds(r, reg, stride=nph), :]

    first_head = lax.broadcasted_iota(jnp.int32, (blk, LANES), 1) < AT_HEAD_DIM
    ones = jnp.ones((2 * blk, LANES), BF16)
    mask_any = 0
    mask_first = (tile == 0).astype(jnp.int32)

    def block_stats(q, k, v, mask_index):
        k = jnp.concatenate([k.astype(BF16), mask_ref[mask_index]], axis=1)
        v = v.astype(BF16)
        q2 = jnp.concatenate([jnp.where(first_head, q, 0.0),
                              jnp.where(first_head, 0.0, q)], axis=0).astype(BF16)
        s = _dot_nt(jnp.concatenate([q2, eye_ref[...]], axis=1), k)
        m = jnp.max(s, axis=-1, keepdims=True)
        p = jnp.exp2(s - m).astype(BF16)
        pv = jnp.dot(p, jnp.concatenate([v, ones], axis=1), preferred_element_type=F32)
        return (jnp.where(first_head, m[:blk], m[blk:]),
                jnp.where(first_head, pv[:blk, LANES:], pv[blk:, LANES:]),
                jnp.where(first_head, pv[:blk, :LANES], pv[blk:, :LANES]))

    def store_stats(pi, rows, stats):
        m_ref[pi, rows, :], l_ref[pi, rows, :], acc_ref[pi, rows, :] = stats

    def window(first_ref, first_rows, ref, rows):
        return jnp.concatenate([first_ref[first_rows], ref[rows]], axis=0)

    for j in range(ta // blk):
        q_rows = pl.ds(j * blk, blk)
        if j == 0:
            k = window(khist, (slice(None),) * 2, ak_ref, (0, q_rows))
            v = window(vhist, (slice(None),) * 2, av_ref, (0, q_rows))
        else:
            k_rows = pl.ds((j - 1) * blk, 2 * blk)
            k, v = ak_ref[0, k_rows, :], av_ref[0, k_rows, :]
        store_stats(0, q_rows, block_stats(aq_ref[0, q_rows, :], k, v,
                                           mask_first if j == 0 else mask_any))

    khist[...] = ak_ref[0, ta - blk:ta, :]
    vhist[...] = av_ref[0, ta - blk:ta, :]

    for r in range(nph):
        for j in range(reg // blk):
            q_rows = pl.ds(j * blk, blk)
            if j == 0:
                last = pl.ds(reg - blk, blk)
                k = window(kprm, (prev, r, last), kprm, (cur, r, q_rows))
                v = window(vprm, (prev, r, last), vprm, (cur, r, q_rows))
            else:
                k_rows = pl.ds((j - 1) * blk, 2 * blk)
                k, v = kprm[cur, r, k_rows, :], vprm[cur, r, k_rows, :]
            stats = block_stats(qprm[r, q_rows, :], k, v, mask_first if j == 0 else mask_any)
            store_stats(1, pl.ds(r * reg + j * blk, blk), stats)
        sub = reg // nph
        for r16 in range(nph):
            for j in range(sub // blk):
                q_rows = pl.ds(r16 + nph * blk * j, blk, stride=nph)
                if j == 0:
                    tail = pl.ds(r16 + nph * (sub - blk), blk, stride=nph)
                    k = window(kprm, (prev, r, tail), kprm, (cur, r, q_rows))
                    v = window(vprm, (prev, r, tail), vprm, (cur, r, q_rows))
                else:
                    k_rows = pl.ds(r16 + nph * blk * (j - 1), 2 * blk, stride=nph)
                    k, v = kprm[cur, r, k_rows, :], vprm[cur, r, k_rows, :]
                stats = block_stats(qprm[r, q_rows, :], k, v,
                                    mask_first if j == 0 else mask_any)
                store_stats(2, pl.ds(r * reg + r16 + nph * blk * j, blk, stride=nph), stats)

    def merge(ci, carry):
        pieces = reg // ATT_MERGE_ROWS
        r = ci // pieces
        i0 = (ci % pieces) * ATT_MERGE_ROWS
        nat_rows = pl.ds(r + nph * i0, ATT_MERGE_ROWS, stride=nph)
        prm_rows = pl.ds(pl.multiple_of(ci * ATT_MERGE_ROWS, ATT_MERGE_ROWS), ATT_MERGE_ROWS)
        rows = (nat_rows, prm_rows, prm_rows)
        ms = [m_ref[pi, rw, :] for pi, rw in enumerate(rows)]
        m_all = functools.reduce(jnp.maximum, ms)
        ws = [jnp.exp2(m - m_all) for m in ms]
        l_all = sum(w * l_ref[pi, rw, :] for (pi, rw), w in zip(enumerate(rows), ws))
        acc = sum(w * acc_ref[pi, rw, :] for (pi, rw), w in zip(enumerate(rows), ws))
        onat[nat_rows, :] = acc / l_all
        return carry

    lax.fori_loop(0, ta // ATT_MERGE_ROWS, merge, 0)

    def gate(ci, carry):
        rows = pl.ds(pl.multiple_of(ci * ATT_MERGE_ROWS, ATT_MERGE_ROWS), ATT_MERGE_ROWS)
        o_ref[0, rows, :] = (onat[rows, :] * az_ref[0, rows, :].astype(F32)).astype(BF16)
        return carry

    lax.fori_loop(0, ta // ATT_MERGE_ROWS, gate, 0)


def _attention(aq, ak, av, az):
    b, s, w = aq.shape
    assert s % ATT_TILE == 0 and w % LANES == 0
    assert DILATIONS == (1, ATT_PERM, ATT_PERM * ATT_PERM)
    assert (ATT_TILE // ATT_PERM) % ATT_MERGE_ROWS == 0
    mask, eye = _attn_mask_operands()
    reg = ATT_TILE // ATT_PERM
    pair_spec = pl.BlockSpec((1, ATT_TILE, LANES), lambda bi, h, t: (bi, t, h))
    hist = pltpu.VMEM((ATT_BLOCK, LANES), F32)
    prm = pltpu.VMEM((2, ATT_PERM, reg, LANES), F32)
    stat = pltpu.VMEM((len(DILATIONS), ATT_TILE, LANES), F32)
    return pl.pallas_call(
        _attn_kernel,
        grid=(b, w // LANES, s // ATT_TILE),
        in_specs=[pl.BlockSpec(mask.shape, lambda bi, h, t: (0, 0, 0)),
                  pl.BlockSpec(eye.shape, lambda bi, h, t: (0, 0))] + [pair_spec] * 4,
        out_specs=pair_spec,
        out_shape=jax.ShapeDtypeStruct((b, s, w), BF16),
        scratch_shapes=[hist, hist, prm, prm, pltpu.VMEM((ATT_PERM, reg, LANES), F32),
                        stat, stat, stat, pltpu.VMEM((ATT_TILE, LANES), F32)],
        compiler_params=pltpu.CompilerParams(
            dimension_semantics=("arbitrary", "arbitrary", "arbitrary"),
            vmem_limit_bytes=VMEM_LIMIT),
        name="dilated_attention",
    )(mask, eye, aq, ak, av, az)


HG_DIAG = 8
HG_LEVELS = (8, 16, 32)
SUBLANES = 8


def _hgrn_constants():
    c = HG_CHUNK
    t = np.arange(c)
    tri = (t[:, None] >= t[None, :]).astype(np.float32)
    first = (t // HG_DIAG) * HG_DIAG
    half = np.concatenate([tri, tri - 0.5 * (tri[first] + tri[first + HG_DIAG - 1])], axis=0)
    coef = np.concatenate([half, half], axis=1)
    ts, ss = t[:, None], t[None, :]
    level = np.full((c, c), len(HG_LEVELS) + 1, np.int32)
    level[(ts // HG_DIAG == ss // HG_DIAG) & (ts >= ss)] = 0
    for i, lb in enumerate(HG_LEVELS):
        m = ((ts // (2 * lb) == ss // (2 * lb)) & ((ts // lb) % 2 == 1)
             & ((ss // lb) % 2 == 0))
        level[m] = i + 1
    return jnp.asarray(coef, BF16), jnp.asarray(level)


def _hgrn_out_kernel(coef_ref, level_ref, x_ref, nw_ref, w_in32_ref, lbl_ref, gn_ref,
                     oat_ref, w32_ref, fw_ref, o_ref,
                     w_in_ref, w_ref, state_ref, qs_ref, g_ref, kk_ref, vi_ref, zs_ref,
                     qd_ref, oi_ref, kv_ref, dec_ref, ohg_ref):
    @pl.when(pl.program_id(1) == 0)
    def _():
        state_ref[...] = jnp.zeros_like(state_ref)
        w_in_ref[...] = w_in32_ref[0].astype(BF16)
        w_ref[...] = w32_ref[0].astype(BF16)

    lg = lbl_ref[...]
    ex = jnp.exp(lg - jnp.max(lg, axis=0, keepdims=True))
    lb = jnp.clip(ex[0:1, :] / jnp.sum(ex, axis=0, keepdims=True), 1e-6, 1.0 - 1e-6)
    f_mid = 0.5 * (1.0 + lb)
    f_amp = 0.5 * (1.0 - lb)

    def store_q(p):
        qs_ref[...] = _silu(p).astype(BF16)

    def store_f(p):
        f = f_mid + f_amp * jnp.tanh(0.5 * p)
        g2 = jnp.log2(f)
        hi_bits = lax.bitcast_convert_type(g2, jnp.uint32) & jnp.uint32(0xFFFF0000)
        g2_hi = lax.bitcast_convert_type(hi_bits, F32)
        g2_lo = (g2 - g2_hi).astype(BF16)
        g2_hi = g2_hi.astype(BF16)
        for h in range(HG_HEADS):
            g_ref[:, 2 * h * HG_DK:(2 * h + 1) * HG_DK] = g2_hi[:, h * HG_DK:(h + 1) * HG_DK]
            g_ref[:, (2 * h + 1) * HG_DK:(2 * h + 2) * HG_DK] = g2_lo[:, h * HG_DK:(h + 1) * HG_DK]
        kk_ref[...] = (1.0 - f).astype(BF16)

    def store_z(p):
        zs_ref[...] = _silu(p).astype(BF16)

    def store_i(p):
        vi_ref[...] = p.astype(BF16)

    u = _normalized(x_ref[0], nw_ref[...])

    def project(j):
        return jnp.dot(u, w_in_ref[:, j * SECTION:(j + 1) * SECTION],
                       preferred_element_type=F32)

    p_f = project(1)
    p_q = project(0)
    store_f(p_f)
    store_q(p_q)

    c = HG_CHUNK
    n_chunks = HG_ROWS // c
    half = HG_HEADS * HG_DK
    coef = coef_ref[...]
    level = level_ref[...]
    units = [(h, ci) for h in range(HG_HEADS) for ci in range(n_chunks)]

    def rows_of(ci):
        return pl.ds(ci * c, c)

    def cols_of(h, width=HG_DK):
        return pl.ds(h * width, width)

    def cum_decay(h, ci):
        g2 = g_ref[rows_of(ci), cols_of(h, 2 * HG_DK)]
        return jnp.dot(coef, jnp.concatenate([g2[:, :HG_DK], g2[:, HG_DK:]], axis=0),
                       preferred_element_type=F32)

    def level_operands(h, ci, be):
        q = qs_ref[rows_of(ci), cols_of(h)]
        k = kk_ref[rows_of(ci), cols_of(h)]
        scaled = lambda t, w: t * w.astype(BF16)
        b = be[0:c]
        e_diag = be[c:2 * c]
        ops = [(scaled(q, jnp.exp2(e_diag)), scaled(k, jnp.exp2(-e_diag)))]
        for lb_rows in HG_LEVELS:
            ref = jnp.concatenate(
                [jnp.broadcast_to(b[p + lb_rows - 1:p + lb_rows, :], (2 * lb_rows, HG_DK))
                 for p in range(0, c, 2 * lb_rows)], axis=0)
            w = jnp.exp2(_neg_abs(b - ref))
            ops.append((scaled(q, w), scaled(k, w)))
        b_last = b[c - 1:c, :]
        return ops, scaled(k, jnp.exp2(b_last - b)), scaled(q, jnp.exp2(b)), jnp.exp2(b_last)

    def intra_scores(ops):
        scores = jnp.where(level == 0, _dot_nt(*ops[0]), 0.0)
        for i in range(len(HG_LEVELS)):
            scores = jnp.where(level == i + 1, _dot_nt(*ops[i + 1]), scores)
        return scores.astype(BF16)

    stage1 = [cum_decay(h, ci) for h, ci in units]
    p_z = project(3)
    stage2 = [level_operands(h, ci, be) for (h, ci), be in zip(units, stage1)]
    p_i = project(2)
    store_z(p_z)
    o_ref[0] = x_ref[0] + jnp.dot(oat_ref[0], w_ref[half:2 * half, :],
                                  preferred_element_type=F32)
    store_i(p_i)

    stage3 = [intra_scores(ops) for ops, _, _, _ in stage2]
    for (h, ci), (_, k_dec, q_dec, decay), scores in zip(units, stage2, stage3):
        v = vi_ref[rows_of(ci), cols_of(h)]
        oi_ref[rows_of(ci), cols_of(h)] = jnp.dot(scores, v, preferred_element_type=F32)
        kv_ref[h, ci] = _dot_tn(v, k_dec)
        qd_ref[rows_of(ci), cols_of(h)] = q_dec
        dec_ref[h, ci] = jnp.broadcast_to(decay, dec_ref.shape[2:])

    states = [state_ref[h] for h in range(HG_HEADS)]
    for ci in range(n_chunks):
        for h in range(HG_HEADS):
            rows, cols = rows_of(ci), cols_of(h)
            o = oi_ref[rows, cols] + _dot_nt(qd_ref[rows, cols], states[h].astype(BF16))
            states[h] = states[h] * dec_ref[h, ci, 0:1, :] + kv_ref[h, ci]
            z = zs_ref[rows, cols].astype(F32)
            ms = jnp.mean(o * o, axis=-1, keepdims=True)
            ohg_ref[rows, cols] = (o * lax.rsqrt(ms + NORM_EPS) * gn_ref[:, cols] * z
                                   ).astype(BF16)
    for h in range(HG_HEADS):
        state_ref[h] = states[h]

    hres = o_ref[0] + jnp.dot(ohg_ref[...], w_ref[0:half, :], preferred_element_type=F32)
    ms = jnp.mean(hres * hres, axis=-1, keepdims=True)
    o_ref[0] = hres * lax.rsqrt(ms + NORM_EPS) * fw_ref[...]


def _hgrn_out(x, norm_w, w_in, lb_logits, hg_norm_w, oat, w_out, final_norm_w):
    b, s, d_model = x.shape
    w = HG_HEADS * HG_DK
    assert s % HG_ROWS == 0 and w == SECTION
    assert w_out.shape == (1, w + oat.shape[-1], d_model)
    coef, level = _hgrn_constants()
    n_chunks = HG_ROWS // HG_CHUNK
    tile = lambda width: pl.BlockSpec((1, HG_ROWS, width), lambda bi, t: (bi, t, 0))
    const = lambda a: pl.BlockSpec(a.shape, lambda bi, t: (0,) * a.ndim)
    act = lambda width: pltpu.VMEM((HG_ROWS, width), BF16)
    return pl.pallas_call(
        _hgrn_out_kernel,
        grid=(b, s // HG_ROWS),
        in_specs=[const(coef), const(level), tile(d_model), const(norm_w),
                  _branch_weight_spec(w_in, 0), const(lb_logits), const(hg_norm_w),
                  tile(oat.shape[-1]), const(w_out), const(final_norm_w)],
        out_specs=tile(d_model),
        out_shape=jax.ShapeDtypeStruct((b, s, d_model), F32),
        scratch_shapes=[pltpu.VMEM((d_model, 4 * SECTION), BF16),
                        pltpu.VMEM(w_out.shape[1:], BF16),
                        pltpu.VMEM((HG_HEADS, HG_DK, HG_DK), F32),
                        act(w), act(2 * w), act(w), act(w), act(w),
                        act(w),
                        pltpu.VMEM((HG_ROWS, w), F32),
                        pltpu.VMEM((HG_HEADS, n_chunks, HG_DK, HG_DK), F32),
                        pltpu.VMEM((HG_HEADS, n_chunks, SUBLANES, HG_DK), F32),
                        act(w)],
        compiler_params=pltpu.CompilerParams(
            dimension_semantics=("parallel", "arbitrary"),
            vmem_limit_bytes=VMEM_LIMIT),
        name="hgrn2_branch_and_output",
    )(coef, level, x, norm_w, w_in, lb_logits, hg_norm_w, oat, w_out, final_norm_w)


def kernel(x, norm_w, w_in, hgrn_lb_logits, hg_norm_w, w_out, final_norm_w):
    b, s, d_model = x.shape
    assert norm_w.shape[0] == 1 and w_in.shape[0] == 1 and w_out.shape[0] == 1
    aq, ak, av, az = _attn_projection(x.reshape(b * s, d_model), norm_w, w_in, s)
    to3 = lambda a: a.reshape(b, s, a.shape[-1])
    oat = _attention(to3(aq), to3(ak), to3(av), to3(az))
    return _hgrn_out(x, norm_w, w_in, hgrn_lb_logits, hg_norm_w, oat, w_out,
                     final_norm_w.reshape(1, d_model))
```

```python
import functools

import numpy as np
import jax
import jax.numpy as jnp
from jax import lax
from jax.experimental import pallas as pl
from jax.experimental.pallas import tpu as pltpu

F32 = jnp.float32
BF16 = jnp.bfloat16

NORM_EPS = 1e-6
ROPE_THETA = 10000.0
LANES = 128

HG_HEADS = 4
HG_DK = 128
HG_CHUNK = 64
AT_HEAD_DIM = 64
SECTION = 512
DILATIONS = (1, 4, 16)
ATT_BLOCK = 128
ATT_TILE = 2 * ATT_BLOCK * max(DILATIONS)
NEG_BIG = -1e30

PROJ_ROWS = 1024
HG_ROWS = 512
ATT_PERM = 4
ATT_MERGE_ROWS = 1024
LOG2E = 1.4426950408889634
VMEM_LIMIT = 56 * 1024 * 1024


def _normalized(x, gain):
    ms = jnp.mean(x * x, axis=-1, keepdims=True)
    return (x * lax.rsqrt(ms + NORM_EPS) * gain).astype(BF16)


def _silu(p):
    return p * (0.5 * jnp.tanh(0.5 * p) + 0.5)


def _project_sections(u, w_ref, plan):
    pending = None
    for j, finish in plan:
        p = jnp.dot(u, w_ref[:, j * SECTION:(j + 1) * SECTION], preferred_element_type=F32)
        if pending is not None:
            pending()
        pending = functools.partial(finish, p)
    pending()


def _dot_nt(a, b):
    return lax.dot_general(a, b, (((1,), (1,)), ((), ())), preferred_element_type=F32)


def _dot_tn(a, b):
    return lax.dot_general(a, b, (((0,), (0,)), ((), ())), preferred_element_type=F32)


def _neg_abs(x):
    bits = lax.bitcast_convert_type(x, jnp.uint32) | jnp.uint32(0x80000000)
    return lax.bitcast_convert_type(bits, F32)


def _attn_proj_kernel(x_ref, nw_ref, w32_ref, cos_ref, sina_ref, sinb_ref,
                      aq_ref, ak_ref, av_ref, az_ref, w_ref):
    @pl.when(pl.program_id(0) == 0)
    def _():
        w_ref[...] = w32_ref[0].astype(BF16)

    u = _normalized(x_ref[...], nw_ref[...])

    def store_rope(dst_ref, scale, p):
        half = AT_HEAD_DIM // 2
        for c in range(SECTION // LANES):
            xc = p[:, c * LANES:(c + 1) * LANES]
            r = (xc * cos_ref[...] + pltpu.roll(xc, LANES - half, 1) * sina_ref[...]
                 + pltpu.roll(xc, half, 1) * sinb_ref[...])
            dst_ref[:, c * LANES:(c + 1) * LANES] = r * scale

    def store_z(p):
        az_ref[...] = _silu(p).astype(BF16)

    def store_v(p):
        av_ref[...] = p

    _project_sections(u, w_ref, [
        (3, store_z),
        (0, functools.partial(store_rope, aq_ref, AT_HEAD_DIM ** -0.5 * LOG2E)),
        (1, functools.partial(store_rope, ak_ref, 1.0)),
        (2, store_v)])


@functools.lru_cache(maxsize=None)
def _rope_tables(seq):
    half = AT_HEAD_DIM // 2
    inv_freq = 1.0 / (ROPE_THETA ** (np.arange(half, dtype=np.float64) / half))
    ang = np.arange(seq, dtype=np.float64)[:, None] * inv_freq[None, :]
    cos = np.cos(ang)
    sin = np.sin(ang)
    zero = np.zeros_like(sin)
    reps = LANES // AT_HEAD_DIM
    cos_t = np.tile(np.concatenate([cos, cos], axis=1), (1, reps))
    sina_t = np.tile(np.concatenate([-sin, zero], axis=1), (1, reps))
    sinb_t = np.tile(np.concatenate([zero, sin], axis=1), (1, reps))
    return tuple(np.asarray(t, np.float32) for t in (cos_t, sina_t, sinb_t))


def _branch_weight_spec(w_in, branch):
    assert w_in.shape[0] == 1 and w_in.shape[2] == 8 * SECTION
    return pl.BlockSpec((1, w_in.shape[1], 4 * SECTION),
                        lambda *_: (0, 0, branch), pipeline_mode=pl.Buffered(1))


def _attn_projection(x2, norm_w, w_in, seq):
    rows, d_model = x2.shape
    assert rows % PROJ_ROWS == 0 and seq % PROJ_ROWS == 0
    tables = _rope_tables(seq)
    seq_tiles = seq // PROJ_ROWS
    row_spec = lambda w: pl.BlockSpec((PROJ_ROWS, w), lambda i: (i, 0))
    tab_spec = pl.BlockSpec((PROJ_ROWS, LANES), lambda i: (i % seq_tiles, 0))
    full = lambda a: pl.BlockSpec(a.shape, lambda i: (0, 0))
    out_dtypes = (F32, F32, F32, BF16)
    return pl.pallas_call(
        _attn_proj_kernel,
        grid=(rows // PROJ_ROWS,),
        in_specs=[row_spec(d_model), full(norm_w), _branch_weight_spec(w_in, 1)]
                 + [tab_spec] * 3,
        out_specs=[row_spec(SECTION)] * len(out_dtypes),
        out_shape=[jax.ShapeDtypeStruct((rows, SECTION), dt) for dt in out_dtypes],
        scratch_shapes=[pltpu.VMEM((d_model, 4 * SECTION), BF16)],
        compiler_params=pltpu.CompilerParams(
            dimension_semantics=("arbitrary",), vmem_limit_bytes=VMEM_LIMIT),
        name="attention_projection",
    )(x2, norm_w, w_in, *tables)


def _attn_bias():
    qi = np.arange(ATT_BLOCK)[:, None]
    kj = np.arange(2 * ATT_BLOCK)[None, :]
    dist = ATT_BLOCK + qi - kj
    valid = (dist >= 0) & (dist <= ATT_BLOCK)
    b0 = np.where(valid, 0.0, NEG_BIG)
    b1 = np.where(valid & (kj >= ATT_BLOCK), 0.0, NEG_BIG)
    return jnp.asarray(np.stack([b0, b1]), F32)


def _attn_kernel(bias_ref, aq_ref, ak_ref, av_ref, az_ref, o_ref,
                 khist, vhist, kprm, vprm, qprm, m_ref, l_ref, acc_ref, onat):
    ta = ATT_TILE
    blk = ATT_BLOCK
    nph = ATT_PERM
    reg = ta // nph
    tile = pl.program_id(2)
    cur = tile % 2
    prev = 1 - cur

    @pl.when((pl.program_id(0) == 0) & (pl.program_id(1) == 0) & (tile == 0))
    def _():
        khist[...] = jnp.zeros_like(khist)
        vhist[...] = jnp.zeros_like(vhist)
        kprm[1] = jnp.zeros((nph, reg, LANES), F32)
        vprm[1] = jnp.zeros((nph, reg, LANES), F32)

    for r in range(nph):
        kprm[cur, r] = ak_ref[0, pl.ds(r, reg, stride=nph), :]
        vprm[cur, r] = av_ref[0, pl.ds(r, reg, stride=nph), :]
        qprm[r] = aq_ref[0, pl.ds(r, reg, stride=nph), :]

    first_head = lax.broadcasted_iota(jnp.int32, (blk, LANES), 1) < AT_HEAD_DIM
    ones = jnp.ones((2 * blk, LANES), BF16)
    bias_any = 0
    bias_first = (tile == 0).astype(jnp.int32)

    def block_stats(q, k, v, bias_index):
        bias = bias_ref[bias_index]
        k = k.astype(BF16)
        v = v.astype(BF16)
        q2 = jnp.concatenate([jnp.where(first_head, q, 0.0),
                              jnp.where(first_head, 0.0, q)], axis=0).astype(BF16)
        s = _dot_nt(q2, k) + jnp.concatenate([bias, bias], axis=0)
        m = jnp.max(s, axis=-1, keepdims=True)
        p = jnp.exp2(s - m).astype(BF16)
        pv = jnp.dot(p, jnp.concatenate([v, ones], axis=1), preferred_element_type=F32)
        return (jnp.where(first_head, m[:blk], m[blk:]),
                jnp.where(first_head, pv[:blk, LANES:], pv[blk:, LANES:]),
                jnp.where(first_head, pv[:blk, :LANES], pv[blk:, :LANES]))

    def store_stats(pi, rows, stats):
        m_ref[pi, rows, :], l_ref[pi, rows, :], acc_ref[pi, rows, :] = stats

    def window(first_ref, first_rows, ref, rows):
        return jnp.concatenate([first_ref[first_rows], ref[rows]], axis=0)

    for j in range(ta // blk):
        q_rows = pl.ds(j * blk, blk)
        if j == 0:
            k = window(khist, (slice(None),) * 2, ak_ref, (0, q_rows))
            v = window(vhist, (slice(None),) * 2, av_ref, (0, q_rows))
        else:
            k_rows = pl.ds((j - 1) * blk, 2 * blk)
            k, v = ak_ref[0, k_rows, :], av_ref[0, k_rows, :]
        store_stats(0, q_rows, block_stats(aq_ref[0, q_rows, :], k, v,
                                           bias_first if j == 0 else bias_any))

    khist[...] = ak_ref[0, ta - blk:ta, :]
    vhist[...] = av_ref[0, ta - blk:ta, :]

    for r in range(nph):
        for j in range(reg // blk):
            q_rows = pl.ds(j * blk, blk)
            if j == 0:
                last = pl.ds(reg - blk, blk)
                k = window(kprm, (prev, r, last), kprm, (cur, r, q_rows))
                v = window(vprm, (prev, r, last), vprm, (cur, r, q_rows))
            else:
                k_rows = pl.ds((j - 1) * blk, 2 * blk)
                k, v = kprm[cur, r, k_rows, :], vprm[cur, r, k_rows, :]
            stats = block_stats(qprm[r, q_rows, :], k, v, bias_first if j == 0 else bias_any)
            store_stats(1, pl.ds(r * reg + j * blk, blk), stats)
        sub = reg // nph
        for r16 in range(nph):
            for j in range(sub // blk):
                q_rows = pl.ds(r16 + nph * blk * j, blk, stride=nph)
                if j == 0:
                    tail = pl.ds(r16 + nph * (sub - blk), blk, stride=nph)
                    k = window(kprm, (prev, r, tail), kprm, (cur, r, q_rows))
                    v = window(vprm, (prev, r, tail), vprm, (cur, r, q_rows))
                else:
                    k_rows = pl.ds(r16 + nph * blk * (j - 1), 2 * blk, stride=nph)
                    k, v = kprm[cur, r, k_rows, :], vprm[cur, r, k_rows, :]
                stats = block_stats(qprm[r, q_rows, :], k, v,
                                    bias_first if j == 0 else bias_any)
                store_stats(2, pl.ds(r * reg + r16 + nph * blk * j, blk, stride=nph), stats)

    def merge(ci, carry):
        pieces = reg // ATT_MERGE_ROWS
        r = ci // pieces
        i0 = (ci % pieces) * ATT_MERGE_ROWS
        nat_rows = pl.ds(r + nph * i0, ATT_MERGE_ROWS, stride=nph)
        prm_rows = pl.ds(pl.multiple_of(ci * ATT_MERGE_ROWS, ATT_MERGE_ROWS), ATT_MERGE_ROWS)
        rows = (nat_rows, prm_rows, prm_rows)
        ms = [m_ref[pi, rw, :] for pi, rw in enumerate(rows)]
        m_all = functools.reduce(jnp.maximum, ms)
        ws = [jnp.exp2(m - m_all) for m in ms]
        l_all = sum(w * l_ref[pi, rw, :] for (pi, rw), w in zip(enumerate(rows), ws))
        acc = sum(w * acc_ref[pi, rw, :] for (pi, rw), w in zip(enumerate(rows), ws))
        onat[nat_rows, :] = acc / l_all
        return carry

    lax.fori_loop(0, ta // ATT_MERGE_ROWS, merge, 0)

    def gate(ci, carry):
        rows = pl.ds(pl.multiple_of(ci * ATT_MERGE_ROWS, ATT_MERGE_ROWS), ATT_MERGE_ROWS)
        o_ref[0, rows, :] = (onat[rows, :] * az_ref[0, rows, :].astype(F32)).astype(BF16)
        return carry

    lax.fori_loop(0, ta // ATT_MERGE_ROWS, gate, 0)


def _attention(aq, ak, av, az):
    b, s, w = aq.shape
    assert s % ATT_TILE == 0 and w % LANES == 0
    assert DILATIONS == (1, ATT_PERM, ATT_PERM * ATT_PERM)
    assert (ATT_TILE // ATT_PERM) % ATT_MERGE_ROWS == 0
    bias = _attn_bias()
    reg = ATT_TILE // ATT_PERM
    pair_spec = pl.BlockSpec((1, ATT_TILE, LANES), lambda bi, h, t: (bi, t, h))
    hist = pltpu.VMEM((ATT_BLOCK, LANES), F32)
    prm = pltpu.VMEM((2, ATT_PERM, reg, LANES), F32)
    stat = pltpu.VMEM((len(DILATIONS), ATT_TILE, LANES), F32)
    return pl.pallas_call(
        _attn_kernel,
        grid=(b, w // LANES, s // ATT_TILE),
        in_specs=[pl.BlockSpec(bias.shape, lambda bi, h, t: (0, 0, 0))] + [pair_spec] * 4,
        out_specs=pair_spec,
        out_shape=jax.ShapeDtypeStruct((b, s, w), BF16),
        scratch_shapes=[hist, hist, prm, prm, pltpu.VMEM((ATT_PERM, reg, LANES), F32),
                        stat, stat, stat, pltpu.VMEM((ATT_TILE, LANES), F32)],
        compiler_params=pltpu.CompilerParams(
            dimension_semantics=("arbitrary", "arbitrary", "arbitrary"),
            vmem_limit_bytes=VMEM_LIMIT),
        name="dilated_attention",
    )(bias, aq, ak, av, az)


HG_DIAG = 8
HG_LEVELS = (8, 16, 32)
SUBLANES = 8


def _hgrn_constants():
    c = HG_CHUNK
    t = np.arange(c)
    tri = (t[:, None] >= t[None, :]).astype(np.float32)
    first = (t // HG_DIAG) * HG_DIAG
    half = np.concatenate([tri, tri - 0.5 * (tri[first] + tri[first + HG_DIAG - 1])], axis=0)
    coef = np.concatenate([half, half], axis=1)
    ts, ss = t[:, None], t[None, :]
    level = np.full((c, c), len(HG_LEVELS) + 1, np.int32)
    level[(ts // HG_DIAG == ss // HG_DIAG) & (ts >= ss)] = 0
    for i, lb in enumerate(HG_LEVELS):
        m = ((ts // (2 * lb) == ss // (2 * lb)) & ((ts // lb) % 2 == 1)
             & ((ss // lb) % 2 == 0))
        level[m] = i + 1
    return jnp.asarray(coef, BF16), jnp.asarray(level)


def _hgrn_out_kernel(coef_ref, level_ref, x_ref, nw_ref, w_in32_ref, lbl_ref, gn_ref,
                     oat_ref, w32_ref, fw_ref, o_ref,
                     w_in_ref, w_ref, state_ref, qs_ref, g_ref, kk_ref, vi_ref, zs_ref,
                     qd_ref, oi_ref, kv_ref, dec_ref, ohg_ref):
    @pl.when(pl.program_id(1) == 0)
    def _():
        state_ref[...] = jnp.zeros_like(state_ref)
        w_in_ref[...] = w_in32_ref[0].astype(BF16)
        w_ref[...] = w32_ref[0].astype(BF16)

    lg = lbl_ref[...]
    ex = jnp.exp(lg - jnp.max(lg, axis=0, keepdims=True))
    lb = jnp.clip(ex[0:1, :] / jnp.sum(ex, axis=0, keepdims=True), 1e-6, 1.0 - 1e-6)
    f_mid = 0.5 * (1.0 + lb)
    f_amp = 0.5 * (1.0 - lb)

    def store_q(p):
        qs_ref[...] = _silu(p).astype(BF16)

    def store_f(p):
        f = f_mid + f_amp * jnp.tanh(0.5 * p)
        g2 = jnp.log2(f)
        hi_bits = lax.bitcast_convert_type(g2, jnp.uint32) & jnp.uint32(0xFFFF0000)
        g2_hi = lax.bitcast_convert_type(hi_bits, F32)
        g2_lo = (g2 - g2_hi).astype(BF16)
        g2_hi = g2_hi.astype(BF16)
        for h in range(HG_HEADS):
            g_ref[:, 2 * h * HG_DK:(2 * h + 1) * HG_DK] = g2_hi[:, h * HG_DK:(h + 1) * HG_DK]
            g_ref[:, (2 * h + 1) * HG_DK:(2 * h + 2) * HG_DK] = g2_lo[:, h * HG_DK:(h + 1) * HG_DK]
        kk_ref[...] = (1.0 - f).astype(BF16)

    def store_z(p):
        zs_ref[...] = _silu(p).astype(BF16)

    def store_i(p):
        vi_ref[...] = p.astype(BF16)

    u = _normalized(x_ref[0], nw_ref[...])

    def project(j):
        return jnp.dot(u, w_in_ref[:, j * SECTION:(j + 1) * SECTION],
                       preferred_element_type=F32)

    p_f = project(1)
    p_q = project(0)
    store_f(p_f)
    store_q(p_q)

    c = HG_CHUNK
    n_chunks = HG_ROWS // c
    half = HG_HEADS * HG_DK
    coef = coef_ref[...]
    level = level_ref[...]
    units = [(h, ci) for h in range(HG_HEADS) for ci in range(n_chunks)]

    def rows_of(ci):
        return pl.ds(ci * c, c)

    def cols_of(h, width=HG_DK):
        return pl.ds(h * width, width)

    def cum_decay(h, ci):
        g2 = g_ref[rows_of(ci), cols_of(h, 2 * HG_DK)]
        return jnp.dot(coef, jnp.concatenate([g2[:, :HG_DK], g2[:, HG_DK:]], axis=0),
                       preferred_element_type=F32)

    def level_operands(h, ci, be):
        q = qs_ref[rows_of(ci), cols_of(h)]
        k = kk_ref[rows_of(ci), cols_of(h)]
        scaled = lambda t, w: t * w.astype(BF16)
        b = be[0:c]
        e_diag = be[c:2 * c]
        ops = [(scaled(q, jnp.exp2(e_diag)), scaled(k, jnp.exp2(-e_diag)))]
        for lb_rows in HG_LEVELS:
            ref = jnp.concatenate(
                [jnp.broadcast_to(b[p + lb_rows - 1:p + lb_rows, :], (2 * lb_rows, HG_DK))
                 for p in range(0, c, 2 * lb_rows)], axis=0)
            w = jnp.exp2(_neg_abs(b - ref))
            ops.append((scaled(q, w), scaled(k, w)))
        b_last = b[c - 1:c, :]
        return ops, scaled(k, jnp.exp2(b_last - b)), scaled(q, jnp.exp2(b)), jnp.exp2(b_last)

    def intra_scores(ops):
        scores = jnp.where(level == 0, _dot_nt(*ops[0]), 0.0)
        for i in range(len(HG_LEVELS)):
            scores = jnp.where(level == i + 1, _dot_nt(*ops[i + 1]), scores)
        return scores.astype(BF16)

    stage1 = [cum_decay(h, ci) for h, ci in units]
    p_z = project(3)
    stage2 = [level_operands(h, ci, be) for (h, ci), be in zip(units, stage1)]
    p_i = project(2)
    store_z(p_z)
    o_ref[0] = x_ref[0] + jnp.dot(oat_ref[0], w_ref[half:2 * half, :],
                                  preferred_element_type=F32)
    store_i(p_i)

    stage3 = [intra_scores(ops) for ops, _, _, _ in stage2]
    for (h, ci), (_, k_dec, q_dec, decay), scores in zip(units, stage2, stage3):
        v = vi_ref[rows_of(ci), cols_of(h)]
        oi_ref[rows_of(ci), cols_of(h)] = jnp.dot(scores, v, preferred_element_type=F32)
        kv_ref[h, ci] = _dot_tn(v, k_dec)
        qd_ref[rows_of(ci), cols_of(h)] = q_dec
        dec_ref[h, ci] = jnp.broadcast_to(decay, dec_ref.shape[2:])

    states = [state_ref[h] for h in range(HG_HEADS)]
    for ci in range(n_chunks):
        for h in range(HG_HEADS):
            rows, cols = rows_of(ci), cols_of(h)
            o = oi_ref[rows, cols] + _dot_nt(qd_ref[rows, cols], states[h].astype(BF16))
            states[h] = states[h] * dec_ref[h, ci, 0:1, :] + kv_ref[h, ci]
            z = zs_ref[rows, cols].astype(F32)
            ms = jnp.mean(o * o, axis=-1, keepdims=True)
            ohg_ref[rows, cols] = (o * lax.rsqrt(ms + NORM_EPS) * gn_ref[:, cols] * z
                                   ).astype(BF16)
    for h in range(HG_HEADS):
        state_ref[h] = states[h]

    hres = o_ref[0] + jnp.dot(ohg_ref[...], w_ref[0:half, :], preferred_element_type=F32)
    ms = jnp.mean(hres * hres, axis=-1, keepdims=True)
    o_ref[0] = hres * lax.rsqrt(ms + NORM_EPS) * fw_ref[...]


def _hgrn_out(x, norm_w, w_in, lb_logits, hg_norm_w, oat, w_out, final_norm_w):
    b, s, d_model = x.shape
    w = HG_HEADS * HG_DK
    assert s % HG_ROWS == 0 and w == SECTION
    assert w_out.shape == (1, w + oat.shape[-1], d_model)
    coef, level = _hgrn_constants()
    n_chunks = HG_ROWS // HG_CHUNK
    tile = lambda width: pl.BlockSpec((1, HG_ROWS, width), lambda bi, t: (bi, t, 0))
    const = lambda a: pl.BlockSpec(a.shape, lambda bi, t: (0,) * a.ndim)
    act = lambda width: pltpu.VMEM((HG_ROWS, width), BF16)
    return pl.pallas_call(
        _hgrn_out_kernel,
        grid=(b, s // HG_ROWS),
        in_specs=[const(coef), const(level), tile(d_model), const(norm_w),
                  _branch_weight_spec(w_in, 0), const(lb_logits), const(hg_norm_w),
                  tile(oat.shape[-1]), const(w_out), const(final_norm_w)],
        out_specs=tile(d_model),
        out_shape=jax.ShapeDtypeStruct((b, s, d_model), F32),
        scratch_shapes=[pltpu.VMEM((d_model, 4 * SECTION), BF16),
                        pltpu.VMEM(w_out.shape[1:], BF16),
                        pltpu.VMEM((HG_HEADS, HG_DK, HG_DK), F32),
                        act(w), act(2 * w), act(w), act(w), act(w),
                        act(w),
                        pltpu.VMEM((HG_ROWS, w), F32),
                        pltpu.VMEM((HG_HEADS, n_chunks, HG_DK, HG_DK), F32),
                        pltpu.VMEM((HG_HEADS, n_chunks, SUBLANES, HG_DK), F32),
                        act(w)],
        compiler_params=pltpu.CompilerParams(
            dimension_semantics=("parallel", "arbitrary"),
            vmem_limit_bytes=VMEM_LIMIT),
        name="hgrn2_branch_and_output",
    )(coef, level, x, norm_w, w_in, lb_logits, hg_norm_w, oat, w_out, final_norm_w)


def kernel(x, norm_w, w_in, hgrn_lb_logits, hg_norm_w, w_out, final_norm_w):
    b, s, d_model = x.shape
    assert norm_w.shape[0] == 1 and w_in.shape[0] == 1 and w_out.shape[0] == 1
    aq, ak, av, az = _attn_projection(x.reshape(b * s, d_model), norm_w, w_in, s)
    to3 = lambda a: a.reshape(b, s, a.shape[-1])
    oat = _attention(to3(aq), to3(ak), to3(av), to3(az))
    return _hgrn_out(x, norm_w, w_in, hgrn_lb_logits, hg_norm_w, oat, w_out,
                     final_norm_w.reshape(1, d_model))
```

```python
import functools

import numpy as np
import jax
import jax.numpy as jnp
from jax import lax
from jax.experimental import pallas as pl
from jax.experimental.pallas import tpu as pltpu

F32 = jnp.float32
BF16 = jnp.bfloat16

NORM_EPS = 1e-6
ROPE_THETA = 10000.0
LANES = 128

HG_HEADS = 4
HG_DK = 128
HG_CHUNK = 64
AT_HEAD_DIM = 64
SECTION = 512
DILATIONS = (1, 4, 16)
ATT_BLOCK = 128
ATT_TILE = 2 * ATT_BLOCK * max(DILATIONS)
NEG_BIG = -1e30

PROJ_ROWS = 1024
HG_ROWS = 512
ATT_PERM = 4
ATT_MERGE_ROWS = 1024
LOG2E = 1.4426950408889634
VMEM_LIMIT = 56 * 1024 * 1024


def _normalized(x, gain):
    ms = jnp.mean(x * x, axis=-1, keepdims=True)
    return (x * lax.rsqrt(ms + NORM_EPS) * gain).astype(BF16)


def _silu(p):
    return p * (0.5 * jnp.tanh(0.5 * p) + 0.5)


def _project_sections(u, w_ref, plan):
    pending = None
    for j, finish in plan:
        p = jnp.dot(u, w_ref[:, j * SECTION:(j + 1) * SECTION], preferred_element_type=F32)
        if pending is not None:
            pending()
        pending = functools.partial(finish, p)
    pending()


def _dot_nt(a, b):
    return lax.dot_general(a, b, (((1,), (1,)), ((), ())), preferred_element_type=F32)


def _dot_tn(a, b):
    return lax.dot_general(a, b, (((0,), (0,)), ((), ())), preferred_element_type=F32)


def _neg_abs(x):
    bits = lax.bitcast_convert_type(x, jnp.uint32) | jnp.uint32(0x80000000)
    return lax.bitcast_convert_type(bits, F32)


def _attn_proj_kernel(x_ref, nw_ref, w32_ref, cos_ref, sina_ref, sinb_ref,
                      aq_ref, ak_ref, av_ref, az_ref, w_ref):
    @pl.when(pl.program_id(0) == 0)
    def _():
        w_ref[...] = w32_ref[0].astype(BF16)

    u = _normalized(x_ref[...], nw_ref[...])

    def store_rope(dst_ref, scale, p):
        half = AT_HEAD_DIM // 2
        for c in range(SECTION // LANES):
            xc = p[:, c * LANES:(c + 1) * LANES]
            r = (xc * cos_ref[...] + pltpu.roll(xc, LANES - half, 1) * sina_ref[...]
                 + pltpu.roll(xc, half, 1) * sinb_ref[...])
            dst_ref[:, c * LANES:(c + 1) * LANES] = r * scale

    def store_z(p):
        az_ref[...] = _silu(p).astype(BF16)

    def store_v(p):
        av_ref[...] = p

    _project_sections(u, w_ref, [
        (3, store_z),
        (0, functools.partial(store_rope, aq_ref, AT_HEAD_DIM ** -0.5 * LOG2E)),
        (1, functools.partial(store_rope, ak_ref, 1.0)),
        (2, store_v)])


@functools.lru_cache(maxsize=None)
def _rope_tables(seq):
    half = AT_HEAD_DIM // 2
    inv_freq = 1.0 / (ROPE_THETA ** (np.arange(half, dtype=np.float64) / half))
    ang = np.arange(seq, dtype=np.float64)[:, None] * inv_freq[None, :]
    cos = np.cos(ang)
    sin = np.sin(ang)
    zero = np.zeros_like(sin)
    reps = LANES // AT_HEAD_DIM
    cos_t = np.tile(np.concatenate([cos, cos], axis=1), (1, reps))
    sina_t = np.tile(np.concatenate([-sin, zero], axis=1), (1, reps))
    sinb_t = np.tile(np.concatenate([zero, sin], axis=1), (1, reps))
    return tuple(np.asarray(t, np.float32) for t in (cos_t, sina_t, sinb_t))


def _branch_weight_spec(w_in, branch):
    assert w_in.shape[0] == 1 and w_in.shape[2] == 8 * SECTION
    return pl.BlockSpec((1, w_in.shape[1], 4 * SECTION),
                        lambda *_: (0, 0, branch), pipeline_mode=pl.Buffered(1))


def _attn_projection(x2, norm_w, w_in, seq):
    rows, d_model = x2.shape
    assert rows % PROJ_ROWS == 0 and seq % PROJ_ROWS == 0
    tables = _rope_tables(seq)
    seq_tiles = seq // PROJ_ROWS
    row_spec = lambda w: pl.BlockSpec((PROJ_ROWS, w), lambda i: (i, 0))
    tab_spec = pl.BlockSpec((PROJ_ROWS, LANES), lambda i: (i % seq_tiles, 0))
    full = lambda a: pl.BlockSpec(a.shape, lambda i: (0, 0))
    out_dtypes = (F32, F32, F32, BF16)
    return pl.pallas_call(
        _attn_proj_kernel,
        grid=(rows // PROJ_ROWS,),
        in_specs=[row_spec(d_model), full(norm_w), _branch_weight_spec(w_in, 1)]
                 + [tab_spec] * 3,
        out_specs=[row_spec(SECTION)] * len(out_dtypes),
        out_shape=[jax.ShapeDtypeStruct((rows, SECTION), dt) for dt in out_dtypes],
        scratch_shapes=[pltpu.VMEM((d_model, 4 * SECTION), BF16)],
        compiler_params=pltpu.CompilerParams(
            dimension_semantics=("arbitrary",), vmem_limit_bytes=VMEM_LIMIT),
        name="attention_projection",
    )(x2, norm_w, w_in, *tables)


def _attn_bias():
    qi = np.arange(ATT_BLOCK)[:, None]
    kj = np.arange(2 * ATT_BLOCK)[None, :]
    dist = ATT_BLOCK + qi - kj
    valid = (dist >= 0) & (dist <= ATT_BLOCK)
    b0 = np.where(valid, 0.0, NEG_BIG)
    b1 = np.where(valid & (kj >= ATT_BLOCK), 0.0, NEG_BIG)
    return jnp.asarray(np.stack([b0, b1]), F32)


def _attn_kernel(bias_ref, aq_ref, ak_ref, av_ref, az_ref, o_ref,
                 khist, vhist, kprm, vprm, qprm, m_ref, l_ref, acc_ref, onat):
    ta = ATT_TILE
    blk = ATT_BLOCK
    nph = ATT_PERM
    reg = ta // nph
    tile = pl.program_id(2)
    cur = tile % 2
    prev = 1 - cur

    @pl.when((pl.program_id(0) == 0) & (pl.program_id(1) == 0) & (tile == 0))
    def _():
        khist[...] = jnp.zeros_like(khist)
        vhist[...] = jnp.zeros_like(vhist)
        kprm[1] = jnp.zeros((nph, reg, LANES), F32)
        vprm[1] = jnp.zeros((nph, reg, LANES), F32)

    for r in range(nph):
        kprm[cur, r] = ak_ref[0, pl.ds(r, reg, stride=nph), :]
        vprm[cur, r] = av_ref[0, pl.ds(r, reg, stride=nph), :]
        qprm[r] = aq_ref[0, pl.ds(r, reg, stride=nph), :]

    first_head = lax.broadcasted_iota(jnp.int32, (blk, LANES), 1) < AT_HEAD_DIM
    ones = jnp.ones((2 * blk, LANES), BF16)
    bias_any = 0
    bias_first = (tile == 0).astype(jnp.int32)

    def block_stats(q, k, v, bias_index):
        bias = bias_ref[bias_index]
        k = k.astype(BF16)
        v = v.astype(BF16)
        q2 = jnp.concatenate([jnp.where(first_head, q, 0.0),
                              jnp.where(first_head, 0.0, q)], axis=0).astype(BF16)
        s = _dot_nt(q2, k) + jnp.concatenate([bias, bias], axis=0)
        m = jnp.max(s, axis=-1, keepdims=True)
        p = jnp.exp2(s - m).astype(BF16)
        pv = jnp.dot(p, jnp.concatenate([v, ones], axis=1), preferred_element_type=F32)
        return (jnp.where(first_head, m[:blk], m[blk:]),
                jnp.where(first_head, pv[:blk, LANES:], pv[blk:, LANES:]),
                jnp.where(first_head, pv[:blk, :LANES], pv[blk:, :LANES]))

    def store_stats(pi, rows, stats):
        m_ref[pi, rows, :], l_ref[pi, rows, :], acc_ref[pi, rows, :] = stats

    def window(first_ref, first_rows, ref, rows):
        return jnp.concatenate([first_ref[first_rows], ref[rows]], axis=0)

    for j in range(ta // blk):
        q_rows = pl.ds(j * blk, blk)
        if j == 0:
            k = window(khist, (slice(None),) * 2, ak_ref, (0, q_rows))
            v = window(vhist, (slice(None),) * 2, av_ref, (0, q_rows))
        else:
            k_rows = pl.ds((j - 1) * blk, 2 * blk)
            k, v = ak_ref[0, k_rows, :], av_ref[0, k_rows, :]
        store_stats(0, q_rows, block_stats(aq_ref[0, q_rows, :], k, v,
                                           bias_first if j == 0 else bias_any))

    khist[...] = ak_ref[0, ta - blk:ta, :]
    vhist[...] = av_ref[0, ta - blk:ta, :]

    for r in range(nph):
        for j in range(reg // blk):
            q_rows = pl.ds(j * blk, blk)
            if j == 0:
                last = pl.ds(reg - blk, blk)
                k = window(kprm, (prev, r, last), kprm, (cur, r, q_rows))
                v = window(vprm, (prev, r, last), vprm, (cur, r, q_rows))
            else:
                k_rows = pl.ds((j - 1) * blk, 2 * blk)
                k, v = kprm[cur, r, k_rows, :], vprm[cur, r, k_rows, :]
            stats = block_stats(qprm[r, q_rows, :], k, v, bias_first if j == 0 else bias_any)
            store_stats(1, pl.ds(r * reg + j * blk, blk), stats)
        sub = reg // nph
        for r16 in range(nph):
            for j in range(sub // blk):
                q_rows = pl.ds(r16 + nph * blk * j, blk, stride=nph)
                if j == 0:
                    tail = pl.ds(r16 + nph * (sub - blk), blk, stride=nph)
                    k = window(kprm, (prev, r, tail), kprm, (cur, r, q_rows))
                    v = window(vprm, (prev, r, tail), vprm, (cur, r, q_rows))
                else:
                    k_rows = pl.ds(r16 + nph * blk * (j - 1), 2 * blk, stride=nph)
                    k, v = kprm[cur, r, k_rows, :], vprm[cur, r, k_rows, :]
                stats = block_stats(qprm[r, q_rows, :], k, v,
                                    bias_first if j == 0 else bias_any)
                store_stats(2, pl.ds(r * reg + r16 + nph * blk * j, blk, stride=nph), stats)

    def merge(ci, carry):
        pieces = reg // ATT_MERGE_ROWS
        r = ci // pieces
        i0 = (ci % pieces) * ATT_MERGE_ROWS
        nat_rows = pl.ds(r + nph * i0, ATT_MERGE_ROWS, stride=nph)
        prm_rows = pl.ds(pl.multiple_of(ci * ATT_MERGE_ROWS, ATT_MERGE_ROWS), ATT_MERGE_ROWS)
        rows = (nat_rows, prm_rows, prm_rows)
        ms = [m_ref[pi, rw, :] for pi, rw in enumerate(rows)]
        m_all = functools.reduce(jnp.maximum, ms)
        ws = [jnp.exp2(m - m_all) for m in ms]
        l_all = sum(w * l_ref[pi, rw, :] for (pi, rw), w in zip(enumerate(rows), ws))
        acc = sum(w * acc_ref[pi, rw, :] for (pi, rw), w in zip(enumerate(rows), ws))
        onat[nat_rows, :] = acc / l_all
        return carry

    lax.fori_loop(0, ta // ATT_MERGE_ROWS, merge, 0)

    def gate(ci, carry):
        rows = pl.ds(pl.multiple_of(ci * ATT_MERGE_ROWS, ATT_MERGE_ROWS), ATT_MERGE_ROWS)
        o_ref[0, rows, :] = (onat[rows, :] * az_ref[0, rows, :].astype(F32)).astype(BF16)
        return carry

    lax.fori_loop(0, ta // ATT_MERGE_ROWS, gate, 0)


def _attention(aq, ak, av, az):
    b, s, w = aq.shape
    assert s % ATT_TILE == 0 and w % LANES == 0
    assert DILATIONS == (1, ATT_PERM, ATT_PERM * ATT_PERM)
    assert (ATT_TILE // ATT_PERM) % ATT_MERGE_ROWS == 0
    bias = _attn_bias()
    reg = ATT_TILE // ATT_PERM
    pair_spec = pl.BlockSpec((1, ATT_TILE, LANES), lambda bi, h, t: (bi, t, h))
    hist = pltpu.VMEM((ATT_BLOCK, LANES), F32)
    prm = pltpu.VMEM((2, ATT_PERM, reg, LANES), F32)
    stat = pltpu.VMEM((len(DILATIONS), ATT_TILE, LANES), F32)
    return pl.pallas_call(
        _attn_kernel,
        grid=(b, w // LANES, s // ATT_TILE),
        in_specs=[pl.BlockSpec(bias.shape, lambda bi, h, t: (0, 0, 0))] + [pair_spec] * 4,
        out_specs=pair_spec,
        out_shape=jax.ShapeDtypeStruct((b, s, w), BF16),
        scratch_shapes=[hist, hist, prm, prm, pltpu.VMEM((ATT_PERM, reg, LANES), F32),
                        stat, stat, stat, pltpu.VMEM((ATT_TILE, LANES), F32)],
        compiler_params=pltpu.CompilerParams(
            dimension_semantics=("arbitrary", "arbitrary", "arbitrary"),
            vmem_limit_bytes=VMEM_LIMIT),
        name="dilated_attention",
    )(bias, aq, ak, av, az)


HG_DIAG = 8
HG_LEVELS = (8, 16, 32)
SUBLANES = 8


def _hgrn_constants():
    c = HG_CHUNK
    t = np.arange(c)
    tri = (t[:, None] >= t[None, :]).astype(np.float32)
    first = (t // HG_DIAG) * HG_DIAG
    half = np.concatenate([tri, tri - 0.5 * (tri[first] + tri[first + HG_DIAG - 1])], axis=0)
    coef = np.concatenate([half, half], axis=1)
    ts, ss = t[:, None], t[None, :]
    level = np.full((c, c), len(HG_LEVELS) + 1, np.int32)
    level[(ts // HG_DIAG == ss // HG_DIAG) & (ts >= ss)] = 0
    for i, lb in enumerate(HG_LEVELS):
        m = ((ts // (2 * lb) == ss // (2 * lb)) & ((ts // lb) % 2 == 1)
             & ((ss // lb) % 2 == 0))
        level[m] = i + 1
    return jnp.asarray(coef, BF16), jnp.asarray(level)


def _hgrn_out_kernel(coef_ref, level_ref, x_ref, nw_ref, w_in32_ref, lbl_ref, gn_ref,
                     oat_ref, w32_ref, fw_ref, o_ref,
                     w_in_ref, w_ref, state_ref, qs_ref, g_ref, kk_ref, vi_ref, zs_ref,
                     qd_ref, oi_ref, kv_ref, dec_ref, ohg_ref):
    @pl.when(pl.program_id(1) == 0)
    def _():
        state_ref[...] = jnp.zeros_like(state_ref)
        w_in_ref[...] = w_in32_ref[0].astype(BF16)
        w_ref[...] = w32_ref[0].astype(BF16)

    lg = lbl_ref[...]
    ex = jnp.exp(lg - jnp.max(lg, axis=0, keepdims=True))
    lb = jnp.clip(ex[0:1, :] / jnp.sum(ex, axis=0, keepdims=True), 1e-6, 1.0 - 1e-6)
    f_mid = 0.5 * (1.0 + lb)
    f_amp = 0.5 * (1.0 - lb)

    def store_q(p):
        qs_ref[...] = _silu(p).astype(BF16)

    def store_f(p):
        f = f_mid + f_amp * jnp.tanh(0.5 * p)
        g2 = jnp.log2(f)
        hi_bits = lax.bitcast_convert_type(g2, jnp.uint32) & jnp.uint32(0xFFFF0000)
        g2_hi = lax.bitcast_convert_type(hi_bits, F32)
        g2_lo = (g2 - g2_hi).astype(BF16)
        g2_hi = g2_hi.astype(BF16)
        for h in range(HG_HEADS):
            g_ref[:, 2 * h * HG_DK:(2 * h + 1) * HG_DK] = g2_hi[:, h * HG_DK:(h + 1) * HG_DK]
            g_ref[:, (2 * h + 1) * HG_DK:(2 * h + 2) * HG_DK] = g2_lo[:, h * HG_DK:(h + 1) * HG_DK]
        kk_ref[...] = (1.0 - f).astype(BF16)

    def store_z(p):
        zs_ref[...] = _silu(p).astype(BF16)

    def store_i(p):
        vi_ref[...] = p.astype(BF16)

    u = _normalized(x_ref[0], nw_ref[...])

    def project(j):
        return jnp.dot(u, w_in_ref[:, j * SECTION:(j + 1) * SECTION],
                       preferred_element_type=F32)

    p_f = project(1)
    p_q = project(0)
    store_f(p_f)
    store_q(p_q)

    c = HG_CHUNK
    n_chunks = HG_ROWS // c
    half = HG_HEADS * HG_DK
    units = [(h, ci) for h in range(HG_HEADS) for ci in range(n_chunks)]

    def rows_of(ci):
        return pl.ds(ci * c, c)

    def cols_of(h, width=HG_DK):
        return pl.ds(h * width, width)

    def cum_decay(h, ci):
        g2 = g_ref[rows_of(ci), cols_of(h, 2 * HG_DK)]
        return jnp.dot(coef_ref[...], jnp.concatenate([g2[:, :HG_DK], g2[:, HG_DK:]], axis=0),
                       preferred_element_type=F32)

    def level_operands(h, ci, be):
        q = qs_ref[rows_of(ci), cols_of(h)]
        k = kk_ref[rows_of(ci), cols_of(h)]
        scaled = lambda t, w: t * w.astype(BF16)
        b = be[0:c]
        e_diag = be[c:2 * c]
        ops = [(scaled(q, jnp.exp2(e_diag)), scaled(k, jnp.exp2(-e_diag)))]
        for lb_rows in HG_LEVELS:
            ref = jnp.concatenate(
                [jnp.broadcast_to(b[p + lb_rows - 1:p + lb_rows, :], (2 * lb_rows, HG_DK))
                 for p in range(0, c, 2 * lb_rows)], axis=0)
            w = jnp.exp2(_neg_abs(b - ref))
            ops.append((scaled(q, w), scaled(k, w)))
        b_last = b[c - 1:c, :]
        return ops, scaled(k, jnp.exp2(b_last - b)), scaled(q, jnp.exp2(b)), jnp.exp2(b_last)

    def intra_scores(ops):
        level = level_ref[...]
        scores = jnp.where(level == 0, _dot_nt(*ops[0]), 0.0)
        for i in range(len(HG_LEVELS)):
            scores = jnp.where(level == i + 1, _dot_nt(*ops[i + 1]), scores)
        return scores.astype(BF16)

    stage1 = [cum_decay(h, ci) for h, ci in units]
    p_z = project(3)
    stage2 = [level_operands(h, ci, be) for (h, ci), be in zip(units, stage1)]
    p_i = project(2)
    store_z(p_z)
    o_ref[0] = x_ref[0] + jnp.dot(oat_ref[0], w_ref[half:2 * half, :],
                                  preferred_element_type=F32)
    store_i(p_i)

    stage3 = [intra_scores(ops) for ops, _, _, _ in stage2]
    for (h, ci), (_, k_dec, q_dec, decay), scores in zip(units, stage2, stage3):
        v = vi_ref[rows_of(ci), cols_of(h)]
        oi_ref[rows_of(ci), cols_of(h)] = jnp.dot(scores, v, preferred_element_type=F32)
        kv_ref[h, ci] = _dot_tn(v, k_dec)
        qd_ref[rows_of(ci), cols_of(h)] = q_dec
        dec_ref[h, ci] = jnp.broadcast_to(decay, dec_ref.shape[2:])

    states = [state_ref[h] for h in range(HG_HEADS)]
    for ci in range(n_chunks):
        for h in range(HG_HEADS):
            rows, cols = rows_of(ci), cols_of(h)
            o = oi_ref[rows, cols] + _dot_nt(qd_ref[rows, cols], states[h].astype(BF16))
            states[h] = states[h] * dec_ref[h, ci, 0:1, :] + kv_ref[h, ci]
            z = zs_ref[rows, cols].astype(F32)
            ms = jnp.mean(o * o, axis=-1, keepdims=True)
            ohg_ref[rows, cols] = (o * lax.rsqrt(ms + NORM_EPS) * gn_ref[:, cols] * z
                                   ).astype(BF16)
    for h in range(HG_HEADS):
        state_ref[h] = states[h]

    hres = o_ref[0] + jnp.dot(ohg_ref[...], w_ref[0:half, :], preferred_element_type=F32)
    ms = jnp.mean(hres * hres, axis=-1, keepdims=True)
    o_ref[0] = hres * lax.rsqrt(ms + NORM_EPS) * fw_ref[...]


def _hgrn_out(x, norm_w, w_in, lb_logits, hg_norm_w, oat, w_out, final_norm_w):
    b, s, d_model = x.shape
    w = HG_HEADS * HG_DK
    assert s % HG_ROWS == 0 and w == SECTION
    assert w_out.shape == (1, w + oat.shape[-1], d_model)
    coef, level = _hgrn_constants()
    n_chunks = HG_ROWS // HG_CHUNK
    tile = lambda width: pl.BlockSpec((1, HG_ROWS, width), lambda bi, t: (bi, t, 0))
    const = lambda a: pl.BlockSpec(a.shape, lambda bi, t: (0,) * a.ndim)
    act = lambda width: pltpu.VMEM((HG_ROWS, width), BF16)
    return pl.pallas_call(
        _hgrn_out_kernel,
        grid=(b, s // HG_ROWS),
        in_specs=[const(coef), const(level), tile(d_model), const(norm_w),
                  _branch_weight_spec(w_in, 0), const(lb_logits), const(hg_norm_w),
                  tile(oat.shape[-1]), const(w_out), const(final_norm_w)],
        out_specs=tile(d_model),
        out_shape=jax.ShapeDtypeStruct((b, s, d_model), F32),
        scratch_shapes=[pltpu.VMEM((d_model, 4 * SECTION), BF16),
                        pltpu.VMEM(w_out.shape[1:], BF16),
                        pltpu.VMEM((HG_HEADS, HG_DK, HG_DK), F32),
                        act(w), act(2 * w), act(w), act(w), act(w),
                        act(w),
                        pltpu.VMEM((HG_ROWS, w), F32),
                        pltpu.VMEM((HG_HEADS, n_chunks, HG_DK, HG_DK), F32),
                        pltpu.VMEM((HG_HEADS, n_chunks, SUBLANES, HG_DK), F32),
                        act(w)],
        compiler_params=pltpu.CompilerParams(
            dimension_semantics=("parallel", "arbitrary"),
            vmem_limit_bytes=VMEM_LIMIT),
        name="hgrn2_branch_and_output",
    )(coef, level, x, norm_w, w_in, lb_logits, hg_norm_w, oat, w_out, final_norm_w)


def kernel(x, norm_w, w_in, hgrn_lb_logits, hg_norm_w, w_out, final_norm_w):
    b, s, d_model = x.shape
    assert norm_w.shape[0] == 1 and w_in.shape[0] == 1 and w_out.shape[0] == 1
    aq, ak, av, az = _attn_projection(x.reshape(b * s, d_model), norm_w, w_in, s)
    to3 = lambda a: a.reshape(b, s, a.shape[-1])
    oat = _attention(to3(aq), to3(ak), to3(av), to3(az))
    return _hgrn_out(x, norm_w, w_in, hgrn_lb_logits, hg_norm_w, oat, w_out,
                     final_norm_w.reshape(1, d_model))
```
